```python
import math
import jax, jax.numpy as jnp
from jax import lax
import numpy as np

D_MODEL = 1024
BATCH = 4
SEQ = 8192
DEPTH = 4

N_MIXERS = 3
N_HEADS = 16
HEAD_DIM = 64
BLOCK = 128
SWA_KV_HEADS = 2
SWA_WINDOW = 128
DIL_PATTERNS = ((128, 1), (512, 4), (2048, 16))
MLA_Q_RANK = 256
MLA_KV_RANK = 128
MLA_NOPE = 64
MLA_ROPE = 32
MLA_V = 64
ROPE_THETA = 10000.0
REL_BUCKETS = 32
REL_MAX_DIST = 2048
PEER_HEADS = 8
PEER_KEYS = 128
PEER_EXPERTS = PEER_KEYS * PEER_KEYS
PEER_DKEY = 256
PEER_TOPK = 16
PEER_CHUNK = 128
DN_ALPHA = (2 * DEPTH) ** 0.25
DN_BETA = (8 * DEPTH) ** -0.25
LN_EPS = 1e-5
RMS_EPS = 1e-6
NEG = -1e30

N_SWA = (DEPTH + 2) // 3
N_DIL = (DEPTH + 1) // 3
N_MLA = DEPTH // 3

kernel_name = "hybrid_swa_dilated_mla_peer_deepnorm"

f32 = jnp.float32


def layer_norm(x, g, b):
    xf = x.astype(f32)
    mu = xf.mean(-1, keepdims=True)
    var = jnp.square(xf - mu).mean(-1, keepdims=True)
    return ((xf - mu) * lax.rsqrt(var + LN_EPS) * g + b).astype(x.dtype)


def rms_norm(x, g):
    xf = x.astype(f32)
    return (xf * lax.rsqrt(jnp.square(xf).mean(-1, keepdims=True) + RMS_EPS) * g).astype(x.dtype)


def rel_bucket(dist):
    max_exact = REL_BUCKETS // 2
    n = jnp.maximum(dist, 0)
    nf = jnp.maximum(n, 1).astype(f32)
    large = max_exact + (jnp.log(nf / max_exact) / math.log(REL_MAX_DIST / max_exact)
                         * (REL_BUCKETS - max_exact)).astype(jnp.int32)
    large = jnp.minimum(large, REL_BUCKETS - 1)
    return jnp.where(n < max_exact, n, large)


def banded_attention(q, k, v, rel_bias, max_dist, dilation):
    b, hk, g, L, dh = q.shape
    nb = L // BLOCK
    qb = q.reshape(b, hk, g, nb, BLOCK, dh)

    def band(t):
        tb = t.reshape(b, hk, nb, BLOCK, t.shape[-1])
        prev = jnp.concatenate([jnp.zeros_like(tb[:, :, :1]), tb[:, :, :-1]], axis=2)
        return jnp.concatenate([prev, tb], axis=3)

    kb, vb = band(k), band(v)
    s = jnp.einsum('bhgnqd,bhnkd->bhgnqk', qb, kb).astype(f32) * (dh ** -0.5)
    qi = jnp.arange(BLOCK)[:, None]
    kj = jnp.arange(2 * BLOCK)[None, :]
    dist = BLOCK + qi - kj
    bias = rel_bias[rel_bucket(dist * dilation)].astype(f32)
    bias = jnp.moveaxis(bias, -1, 0).reshape(hk, g, BLOCK, 2 * BLOCK)
    valid = ((dist >= 0) & (dist <= max_dist))[None] & \
        ((jnp.arange(nb)[:, None, None] > 0) | (kj >= BLOCK)[None])
    s = jnp.where(valid, s + bias[:, :, None], NEG)
    m = s.max(-1, keepdims=True)
    p = jnp.exp(s - m)
    l = p.sum(-1, keepdims=True)
    o = jnp.einsum('bhgnqk,bhnkd->bhgnqd', p, vb) / l
    lse = (m + jnp.log(l))[..., 0]
    return o.reshape(b, hk, g, L, dh), lse.reshape(b, hk, g, L)


def swa_mixer(x, w_in, sinks, w_out, rel_bias):
    b, s, _ = x.shape
    grp = N_HEADS // SWA_KV_HEADS
    q, k, v = jnp.split(x @ w_in, [N_HEADS * HEAD_DIM, (N_HEADS + SWA_KV_HEADS) * HEAD_DIM], axis=-1)
    q = q.reshape(b, s, SWA_KV_HEADS, grp, HEAD_DIM).transpose(0, 2, 3, 1, 4)
    k = k.reshape(b, s, SWA_KV_HEADS, HEAD_DIM).transpose(0, 2, 1, 3)
    v = v.reshape(b, s, SWA_KV_HEADS, HEAD_DIM).transpose(0, 2, 1, 3)
    o, lse = banded_attention(q, k, v, rel_bias, SWA_WINDOW - 1, 1)
    sink = sinks.astype(f32).reshape(SWA_KV_HEADS, grp)[None, :, :, None]
    o = o * jax.nn.sigmoid(lse - sink)[..., None]
    o = o.transpose(0, 3, 1, 2, 4).reshape(b, s, N_HEADS * HEAD_DIM).astype(x.dtype)
    return o @ w_out


def dilated_mixer(x, w_in, w_out, rel_bias):
    b, s, _ = x.shape
    n_pat = len(DIL_PATTERNS)
    proj = (x @ w_in).reshape(b, s, n_pat, 3, N_HEADS, HEAD_DIM)
    outs, lses = [], []
    for gi, (window, dil) in enumerate(DIL_PATTERNS):
        seg = dil * BLOCK
        sp = -(-s // seg) * seg
        t = jnp.pad(proj[:, :, gi], ((0, 0), (0, sp - s), (0, 0), (0, 0), (0, 0)))
        m_len = sp // dil
        t = t.reshape(b, m_len, dil, 3, N_HEADS, HEAD_DIM).transpose(3, 0, 2, 4, 1, 5)
        t = t.reshape(3, b * dil, N_HEADS, m_len, HEAD_DIM)
        o, lse = banded_attention(t[0][:, :, None], t[1], t[2], rel_bias, window // dil, dil)
        o = o[:, :, 0].reshape(b, dil, N_HEADS, m_len, HEAD_DIM).transpose(0, 3, 1, 2, 4)
        o = o.reshape(b, sp, N_HEADS, HEAD_DIM)[:, :s]
        lse = lse[:, :, 0].reshape(b, dil, N_HEADS, m_len).transpose(0, 3, 1, 2).reshape(b, sp, N_HEADS)[:, :s]
        outs.append(o)
        lses.append(lse)
    o = jnp.stack(outs, 0)
    wts = jax.nn.softmax(jnp.stack(lses, 0), axis=0)
    o = (wts[..., None] * o).sum(0).reshape(b, s, N_HEADS * HEAD_DIM).astype(x.dtype)
    return o @ w_out


def rope(t, pos):
    half = t.shape[-1] // 2
    freq = ROPE_THETA ** (-jnp.arange(half, dtype=f32) / half)
    ang = pos[:, None].astype(f32) * freq[None, :]
    cos, sin = jnp.cos(ang), jnp.sin(ang)
    tf = t.astype(f32)
    t1, t2 = tf[..., :half], tf[..., half:]
    return jnp.concatenate([t1 * cos - t2 * sin, t1 * sin + t2 * cos], -1).astype(t.dtype)


def mla_mixer(x, w_in, q_norm, w_uq, kv_norm, w_ukv, w_out):
    b, s, _ = x.shape
    c_q, c_kv, k_r = jnp.split(x @ w_in, [MLA_Q_RANK, MLA_Q_RANK + MLA_KV_RANK], axis=-1)
    q = (rms_norm(c_q, q_norm) @ w_uq).reshape(b, s, N_HEADS, MLA_NOPE + MLA_ROPE).transpose(0, 2, 1, 3)
    kv = (rms_norm(c_kv, kv_norm) @ w_ukv).reshape(b, s, N_HEADS, MLA_NOPE + MLA_V).transpose(0, 2, 1, 3)
    pos = jnp.arange(s)
    q = jnp.concatenate([q[..., :MLA_NOPE], rope(q[..., MLA_NOPE:], pos)], -1)
    k_r = rope(k_r[:, None], pos)
    k = jnp.concatenate([kv[..., :MLA_NOPE], jnp.broadcast_to(k_r, (b, N_HEADS, s, MLA_ROPE))], -1)
    v = kv[..., MLA_NOPE:]
    scale = (MLA_NOPE + MLA_ROPE) ** -0.5
    nb = s // BLOCK
    qb = q.reshape(b, N_HEADS, nb, BLOCK, MLA_NOPE + MLA_ROPE).transpose(2, 0, 1, 3, 4)

    def query_block(args):
        qblk, n = args
        sc = jnp.einsum('bhqd,bhkd->bhqk', qblk, k).astype(f32) * scale
        qpos = n * BLOCK + jnp.arange(BLOCK)
        sc = jnp.where(pos[None, :] <= qpos[:, None], sc, NEG)
        return jnp.einsum('bhqk,bhkd->bhqd', jax.nn.softmax(sc, axis=-1), v)

    o = lax.map(query_block, (qb, jnp.arange(nb)))
    o = o.transpose(1, 0, 3, 2, 4).reshape(b, s, N_HEADS * MLA_V).astype(x.dtype)
    return o @ w_out


def peer(x, w_q, keys, u, v):
    b, s, d = x.shape
    t = x.reshape(b * s, d)
    n_tok = b * s
    q = (t @ w_q).reshape(n_tok, PEER_HEADS, 2, PEER_DKEY // 2)
    sc = jnp.einsum('thpd,hpkd->thpk', q, keys).astype(f32)
    s_top, i_top = lax.top_k(sc, PEER_TOPK)
    cand = s_top[:, :, 0, :, None] + s_top[:, :, 1, None, :]
    cand_idx = i_top[:, :, 0, :, None] * PEER_KEYS + i_top[:, :, 1, None, :]
    best, sel = lax.top_k(cand.reshape(n_tok, PEER_HEADS, PEER_TOPK * PEER_TOPK), PEER_TOPK)
    idx = jnp.take_along_axis(cand_idx.reshape(n_tok, PEER_HEADS, -1), sel, axis=-1)
    gate = jax.nn.softmax(best, axis=-1)
    nc = n_tok // PEER_CHUNK
    hk = PEER_HEADS * PEER_TOPK

    def token_chunk(args):
        tc, ic, gc = args
        ue = jnp.take(u, ic, axis=0)
        ve = jnp.take(v, ic, axis=0)
        h = jax.nn.gelu(jnp.einsum('cd,ced->ce', tc, ue).astype(f32), approximate=False)
        return jnp.einsum('ce,ced->cd', gc * h, ve)

    y = lax.map(token_chunk, (t.reshape(nc, PEER_CHUNK, d), idx.reshape(nc, PEER_CHUNK, hk),
                              gate.reshape(nc, PEER_CHUNK, hk)))
    return y.reshape(b, s, d).astype(x.dtype)


def setup_inputs(seed: int = 0) -> dict:
    key = jax.random.key(seed)
    ks = iter(jax.random.split(key, 32))
    D = D_MODEL

    def nrm(shape, scale):
        return jax.random.normal(next(ks), shape, f32) * scale

    x = nrm((BATCH, SEQ, D), 1.0)
    rel_bias = nrm((REL_BUCKETS, N_HEADS), 0.5)
    ln_g = 1.0 + nrm((DEPTH, 2, D), 0.02)
    ln_b = nrm((DEPTH, 2, D), 0.02)
    swa_w_in = nrm((N_SWA, D, (N_HEADS + 2 * SWA_KV_HEADS) * HEAD_DIM), D ** -0.5)
    swa_sinks = nrm((N_SWA, N_HEADS), 1.0)
    swa_w_out = nrm((N_SWA, N_HEADS * HEAD_DIM, D), DN_BETA * (N_HEADS * HEAD_DIM) ** -0.5)
    dil_w_in = nrm((N_DIL, D, len(DIL_PATTERNS) * 3 * N_HEADS * HEAD_DIM), D ** -0.5)
    dil_w_out = nrm((N_DIL, N_HEADS * HEAD_DIM, D), DN_BETA * (N_HEADS * HEAD_DIM) ** -0.5)
    mla_w_in = nrm((N_MLA, D, MLA_Q_RANK + MLA_KV_RANK + MLA_ROPE), D ** -0.5)
    mla_q_norm = 1.0 + nrm((N_MLA, MLA_Q_RANK), 0.02)
    mla_w_uq = nrm((N_MLA, MLA_Q_RANK, N_HEADS * (MLA_NOPE + MLA_ROPE)), MLA_Q_RANK ** -0.5)
    mla_kv_norm = 1.0 + nrm((N_MLA, MLA_KV_RANK), 0.02)
    mla_w_ukv = nrm((N_MLA, MLA_KV_RANK, N_HEADS * (MLA_NOPE + MLA_V)), MLA_KV_RANK ** -0.5)
    mla_w_out = nrm((N_MLA, N_HEADS * MLA_V, D), DN_BETA * (N_HEADS * MLA_V) ** -0.5)
    peer_w_q = nrm((DEPTH, D, PEER_HEADS * PEER_DKEY), D ** -0.5)
    peer_keys = nrm((DEPTH, PEER_HEADS, 2, PEER_KEYS, PEER_DKEY // 2), (PEER_DKEY // 2) ** -0.5)
    peer_u = nrm((DEPTH, PEER_EXPERTS, D), D ** -0.5)
    peer_v = nrm((DEPTH, PEER_EXPERTS, D), DN_BETA * PEER_HEADS ** -0.5)
    return {"x": x, "rel_bias": rel_bias, "ln_g": ln_g, "ln_b": ln_b,
            "swa_w_in": swa_w_in, "swa_sinks": swa_sinks, "swa_w_out": swa_w_out,
            "dil_w_in": dil_w_in, "dil_w_out": dil_w_out,
            "mla_w_in": mla_w_in, "mla_q_norm": mla_q_norm, "mla_w_uq": mla_w_uq,
            "mla_kv_norm": mla_kv_norm, "mla_w_ukv": mla_w_ukv, "mla_w_out": mla_w_out,
            "peer_w_q": peer_w_q, "peer_keys": peer_keys, "peer_u": peer_u, "peer_v": peer_v}


def reference(x, rel_bias, ln_g, ln_b, swa_w_in, swa_sinks, swa_w_out, dil_w_in, dil_w_out,
              mla_w_in, mla_q_norm, mla_w_uq, mla_kv_norm, mla_w_ukv, mla_w_out,
              peer_w_q, peer_keys, peer_u, peer_v):
    for i in range(DEPTH):
        kind, j = i % N_MIXERS, i // N_MIXERS
        if kind == 0:
            y = swa_mixer(x, swa_w_in[j], swa_sinks[j], swa_w_out[j], rel_bias)
        elif kind == 1:
            y = dilated_mixer(x, dil_w_in[j], dil_w_out[j], rel_bias)
        else:
            y = mla_mixer(x, mla_w_in[j], mla_q_norm[j], mla_w_uq[j], mla_kv_norm[j],
                          mla_w_ukv[j], mla_w_out[j])
        x = layer_norm(DN_ALPHA * x + y, ln_g[i, 0], ln_b[i, 0])
        y = peer(x, peer_w_q[i], peer_keys[i], peer_u[i], peer_v[i])
        x = layer_norm(DN_ALPHA * x + y, ln_g[i, 1], ln_b[i, 1])
    return x
```

```python
import functools
import math

import jax
import jax.numpy as jnp
from jax import lax
from jax.experimental import pallas as pl
from jax.experimental.pallas import tpu as pltpu

f32 = jnp.float32
bf16 = jnp.bfloat16

N_HEADS = 16
HEAD_DIM = 64
BLOCK = 128
SWA_KV_HEADS = 2
SWA_WINDOW = 128
DIL_PATTERNS = ((128, 1), (512, 4), (2048, 16))
MLA_Q_RANK = 256
MLA_KV_RANK = 128
MLA_NOPE = 64
MLA_ROPE = 32
MLA_V = 64
ROPE_THETA = 10000.0
REL_BUCKETS = 32
REL_MAX_DIST = 2048
PEER_HEADS = 8
PEER_KEYS = 128
PEER_DKEY = 256
PEER_TOPK = 16
LN_EPS = 1e-5
RMS_EPS = 1e-6
NEG = -1e30

LANE = 128
VMEM_LIMIT = 56 * 1024 * 1024

_NT = (((1,), (1,)), ((), ()))


def _params(*sem):
    return pltpu.CompilerParams(dimension_semantics=sem, vmem_limit_bytes=VMEM_LIMIT)


def _mm_kernel(x_ref, w_ref, o_ref):
    o_ref[...] = jnp.dot(x_ref[...].astype(bf16), w_ref[...],
                         preferred_element_type=f32).astype(o_ref.dtype)


def _mm(x, w, out_dtype, tm, tn):
    m, k = x.shape
    n = w.shape[1]
    tm, tn = min(tm, m), min(tn, n)
    return pl.pallas_call(
        _mm_kernel, grid=(m // tm, n // tn),
        in_specs=[pl.BlockSpec((tm, k), lambda i, j: (i, 0)),
                  pl.BlockSpec((k, tn), lambda i, j: (0, j))],
        out_specs=pl.BlockSpec((tm, tn), lambda i, j: (i, j)),
        out_shape=jax.ShapeDtypeStruct((m, n), out_dtype),
        compiler_params=_params("parallel", "parallel"))(x, w)


def _rel_bucket(dist):
    max_exact = REL_BUCKETS // 2
    n = jnp.maximum(dist, 0)
    nf = jnp.maximum(n, 1).astype(f32)
    large = max_exact + (jnp.log(nf / max_exact) / math.log(REL_MAX_DIST / max_exact)
                         * (REL_BUCKETS - max_exact)).astype(jnp.int32)
    large = jnp.minimum(large, REL_BUCKETS - 1)
    return jnp.where(n < max_exact, n, large)


def _bias_kernel(rel_ref, bucket_ref, valid_ref, o_ref):
    bucket = bucket_ref[...]
    for h in range(N_HEADS):
        acc = jnp.zeros(bucket.shape, f32)
        for b in range(REL_BUCKETS):
            acc = jnp.where(bucket == b, rel_ref[b, h], acc)
        for variant in range(2):
            o_ref[variant, h] = jnp.where(valid_ref[variant] != 0, acc, NEG)


def _band_bias(rel_bias, max_dist, dilation):
    qi = jnp.arange(BLOCK)[:, None]
    kj = jnp.arange(2 * BLOCK)[None, :]
    dist = BLOCK + qi - kj
    bucket = _rel_bucket(dist * dilation).astype(jnp.int32)
    valid = (dist >= 0) & (dist <= max_dist)
    valid = jnp.stack([valid & (kj >= BLOCK), valid]).astype(jnp.int32)
    return pl.pallas_call(
        _bias_kernel,
        in_specs=[pl.BlockSpec(memory_space=pltpu.SMEM),
                  pl.BlockSpec(memory_space=pltpu.VMEM),
                  pl.BlockSpec(memory_space=pltpu.VMEM)],
        out_specs=pl.BlockSpec(memory_space=pltpu.VMEM),
        out_shape=jax.ShapeDtypeStruct((2, N_HEADS, BLOCK, 2 * BLOCK), f32))(rel_bias, bucket, valid)


def _band_kernel(q_ref, kp_ref, kc_ref, vp_ref, vc_ref, bias_ref, *rest, group, with_sink):
    if with_sink:
        sink_ref, o_ref = rest
    else:
        o_ref, lse_ref = rest
    scale = HEAD_DIM ** -0.5
    for h in range(N_HEADS):
        hk = h // group
        qs = slice(h * HEAD_DIM, (h + 1) * HEAD_DIM)
        ks = slice(hk * HEAD_DIM, (hk + 1) * HEAD_DIM)
        q = q_ref[0, :, qs]
        k = jnp.concatenate([kp_ref[0, :, ks], kc_ref[0, :, ks]], axis=0)
        v = jnp.concatenate([vp_ref[0, :, ks], vc_ref[0, :, ks]], axis=0)
        s = lax.dot_general(q, k, _NT, preferred_element_type=f32) * scale + bias_ref[0, h]
        m = s.max(-1, keepdims=True)
        p = jnp.exp(s - m)
        l = p.sum(-1, keepdims=True)
        o = jnp.dot(p.astype(bf16), v, preferred_element_type=f32) / l
        lse = m + jnp.log(l)
        if with_sink:
            o = o * jax.nn.sigmoid(lse - sink_ref[:, qs])
        else:
            lse_ref[0, :, qs] = jnp.broadcast_to(lse, (BLOCK, HEAD_DIM))
        o_ref[0, :, qs] = o.astype(o_ref.dtype)


def _band_attention(proj, bias, dil, feat, q_col, k_col, v_col, kv_width, group, sink=None):
    b, s, _ = proj.shape
    d_q = N_HEADS * HEAD_DIM
    m_len = s // dil
    nb = m_len // BLOCK
    view = proj.reshape(b, m_len, dil * feat)
    qpr, kpr = feat // d_q, feat // kv_width

    def cur(col, per_row):
        return lambda bi, r, n: (bi, n, r * per_row + col)

    def prev(col, per_row):
        return lambda bi, r, n: (bi, jnp.maximum(n - 1, 0), r * per_row + col)

    in_specs = [pl.BlockSpec((1, BLOCK, d_q), cur(q_col, qpr)),
                pl.BlockSpec((1, BLOCK, kv_width), prev(k_col, kpr)),
                pl.BlockSpec((1, BLOCK, kv_width), cur(k_col, kpr)),
                pl.BlockSpec((1, BLOCK, kv_width), prev(v_col, kpr)),
                pl.BlockSpec((1, BLOCK, kv_width), cur(v_col, kpr)),
                pl.BlockSpec((1, N_HEADS, BLOCK, 2 * BLOCK),
                             lambda bi, r, n: (jnp.minimum(n, 1), 0, 0, 0))]
    args = [view, view, view, view, view, bias]
    o_spec = pl.BlockSpec((1, BLOCK, d_q), lambda bi, r, n: (bi, n, r))
    o_shape = jax.ShapeDtypeStruct((b, m_len, dil * d_q), bf16)
    if sink is not None:
        in_specs.append(pl.BlockSpec((1, d_q), lambda bi, r, n: (0, 0)))
        args.append(sink)
        out_specs, out_shape = o_spec, o_shape
    else:
        out_specs = [o_spec, o_spec]
        out_shape = [o_shape, jax.ShapeDtypeStruct((b, m_len, dil * d_q), f32)]
    out = pl.pallas_call(
        functools.partial(_band_kernel, group=group, with_sink=sink is not None),
        grid=(b, dil, nb), in_specs=in_specs, out_specs=out_specs, out_shape=out_shape,
        compiler_params=_params("parallel", "parallel", "arbitrary"))(*args)
    if sink is not None:
        return out.reshape(b * s, d_q)
    return out[0].reshape(b * s, d_q), out[1].reshape(b * s, d_q)


def _layer_norm(z, g, b):
    mu = z.mean(-1, keepdims=True)
    zc = z - mu
    var = jnp.square(zc).mean(-1, keepdims=True)
    return zc * lax.rsqrt(var + LN_EPS) * g + b


def _proj_ln_kernel(*refs, n_pat, alpha):
    o_refs = refs[:n_pat]
    lse_refs = refs[n_pat:2 * n_pat] if n_pat > 1 else ()
    w_ref, x_ref, g_ref, b_ref, out_ref = refs[len(o_refs) + len(lse_refs):]
    if n_pat == 1:
        o = o_refs[0][...]
    else:
        lses = [r[...] for r in lse_refs]
        mx = functools.reduce(jnp.maximum, lses)
        es = [jnp.exp(l - mx) for l in lses]
        num = sum(e * r[...].astype(f32) for e, r in zip(es, o_refs))
        o = (num / sum(es)).astype(bf16)
    y = jnp.dot(o, w_ref[...], preferred_element_type=f32)
    out_ref[...] = _layer_norm(alpha * x_ref[...] + y, g_ref[...], b_ref[...])


def _proj_ln(os_, lses, w_out, x, g, b, alpha, tm=512):
    t, d = x.shape
    tm = min(tm, t)
    kw = w_out.shape[0]
    n_pat = len(os_)
    row = lambda width: pl.BlockSpec((tm, width), lambda i: (i, 0))
    fixed = lambda shape: pl.BlockSpec(shape, lambda i: (0, 0))
    in_specs = ([row(kw)] * n_pat + [row(kw)] * len(lses)
                + [fixed((kw, d)), row(d), fixed((1, d)), fixed((1, d))])
    return pl.pallas_call(
        functools.partial(_proj_ln_kernel, n_pat=n_pat, alpha=alpha),
        grid=(t // tm,), in_specs=in_specs, out_specs=row(d),
        out_shape=jax.ShapeDtypeStruct((t, d), f32),
        compiler_params=_params("parallel"))(*os_, *lses, w_out, x, g.reshape(1, d), b.reshape(1, d))


def _rms(c, g):
    return c * lax.rsqrt(jnp.square(c).mean(-1, keepdims=True) + RMS_EPS) * g


def _mla_prep_kernel(x_ref, win_ref, qn_ref, wq_ref, kvn_ref, wkv_ref, cos_ref, sin_ref,
                     q_ref, k_ref, v_ref):
    hw = N_HEADS * LANE
    cos, sin = cos_ref[...], sin_ref[...]
    xw = jnp.dot(x_ref[...].astype(bf16), win_ref[...], preferred_element_type=f32)
    cq = _rms(xw[:, :MLA_Q_RANK], qn_ref[...]).astype(bf16)
    ckv = _rms(xw[:, MLA_Q_RANK:MLA_Q_RANK + MLA_KV_RANK], kvn_ref[...]).astype(bf16)
    off = MLA_Q_RANK + MLA_KV_RANK
    kr = xw[:, off:off + LANE] * cos + xw[:, off + LANE:off + 2 * LANE] * sin
    qq = jnp.dot(cq, wq_ref[...], preferred_element_type=f32)
    kv = jnp.dot(ckv, wkv_ref[...], preferred_element_type=f32)
    for h in range(N_HEADS):
        blk = slice(h * LANE, (h + 1) * LANE)
        swp = slice(hw + h * LANE, hw + (h + 1) * LANE)
        q_ref[:, blk] = (qq[:, blk] * cos + qq[:, swp] * sin).astype(bf16)
        k_ref[:, blk] = (kv[:, blk] + kr).astype(bf16)
    v_ref[...] = kv[:, hw:].astype(bf16)


def _mla_prep(x, w_in, q_norm, w_uq, kv_norm, w_ukv, seq, tm=512):
    t, d = x.shape
    tm = min(tm, seq)
    hw = N_HEADS * LANE
    half = MLA_ROPE // 2
    dq = MLA_NOPE + MLA_ROPE
    kr_w = w_in[:, MLA_Q_RANK + MLA_KV_RANK:]
    zeros = lambda n: jnp.zeros((d, n), f32)
    kr_a = jnp.concatenate([zeros(MLA_NOPE), kr_w, zeros(LANE - dq)], 1)
    kr_b = jnp.concatenate([zeros(MLA_NOPE), kr_w[:, half:], kr_w[:, :half], zeros(LANE - dq)], 1)
    win = jnp.concatenate([w_in[:, :MLA_Q_RANK + MLA_KV_RANK], kr_a, kr_b], 1).astype(bf16)
    wq3 = w_uq.reshape(MLA_Q_RANK, N_HEADS, dq)
    zq = lambda n: jnp.zeros((MLA_Q_RANK, N_HEADS, n), f32)
    wq_a = jnp.concatenate([wq3, zq(LANE - dq)], 2)
    wq_b = jnp.concatenate([zq(MLA_NOPE), wq3[:, :, MLA_NOPE + half:], wq3[:, :, MLA_NOPE:MLA_NOPE + half],
                            zq(LANE - dq)], 2)
    wq = jnp.concatenate([wq_a.reshape(MLA_Q_RANK, hw), wq_b.reshape(MLA_Q_RANK, hw)], 1).astype(bf16)
    wkv3 = w_ukv.reshape(MLA_KV_RANK, N_HEADS, MLA_NOPE + MLA_V)
    zk = lambda n: jnp.zeros((MLA_KV_RANK, N_HEADS, n), f32)
    wk = jnp.concatenate([wkv3[:, :, :MLA_NOPE], zk(LANE - MLA_NOPE)], 2)
    wv = jnp.concatenate([wkv3[:, :, MLA_NOPE:], zk(LANE - MLA_V)], 2)
    wkv = jnp.concatenate([wk.reshape(MLA_KV_RANK, hw), wv.reshape(MLA_KV_RANK, hw)], 1).astype(bf16)
    freq = ROPE_THETA ** (-jnp.arange(half, dtype=f32) / half)
    ang = jnp.arange(seq)[:, None].astype(f32) * freq[None, :]
    c, s_ = jnp.cos(ang), jnp.sin(ang)
    cos_t = jnp.concatenate([jnp.ones((seq, MLA_NOPE), f32), c, c, jnp.ones((seq, LANE - dq), f32)], 1)
    sin_t = jnp.concatenate([jnp.zeros((seq, MLA_NOPE), f32), -s_, s_, jnp.zeros((seq, LANE - dq), f32)], 1)

    n_pos = seq // tm
    row = lambda width: pl.BlockSpec((tm, width), lambda i: (i, 0))
    fixed = lambda a: pl.BlockSpec(a.shape, lambda i: (0, 0))
    pos = pl.BlockSpec((tm, LANE), lambda i: (i % n_pos, 0))
    qn, kvn = q_norm.reshape(1, -1), kv_norm.reshape(1, -1)
    out_sd = jax.ShapeDtypeStruct((t, hw), bf16)
    return pl.pallas_call(
        _mla_prep_kernel, grid=(t // tm,),
        in_specs=[row(d), fixed(win), fixed(qn), fixed(wq), fixed(kvn), fixed(wkv), pos, pos],
        out_specs=[row(hw)] * 3, out_shape=[out_sd] * 3,
        compiler_params=_params("parallel"))(x, win, qn, wq, kvn, wkv, cos_t, sin_t)


def _mla_attn_kernel(q_ref, k_ref, v_ref, o_ref, *, tq, scale):
    qi = pl.program_id(2)
    q = q_ref[0]

    def update(carry, kblk, vblk, mask):
        m, l, acc = carry
        s = lax.dot_general(q, kblk, _NT, preferred_element_type=f32) * scale
        if mask is not None:
            s = jnp.where(mask, s, NEG)
        m_new = jnp.maximum(m, s.max(-1, keepdims=True))
        a = jnp.exp(m - m_new)
        p = jnp.exp(s - m_new)
        l = a * l + p.sum(-1, keepdims=True)
        acc = a * acc + jnp.dot(p.astype(bf16), vblk, preferred_element_type=f32)
        return m_new, l, acc

    def body(ki, carry):
        start = pl.multiple_of(ki * tq, tq)
        return update(carry, k_ref[0, pl.ds(start, tq), :], v_ref[0, pl.ds(start, tq), :], None)

    init = (jnp.full((tq, 1), NEG, f32), jnp.zeros((tq, 1), f32), jnp.zeros((tq, LANE), f32))
    carry = lax.fori_loop(0, qi, body, init)
    start = pl.multiple_of(qi * tq, tq)
    causal = lax.broadcasted_iota(jnp.int32, (tq, tq), 1) <= lax.broadcasted_iota(jnp.int32, (tq, tq), 0)
    _, l, acc = update(carry, k_ref[0, pl.ds(start, tq), :], v_ref[0, pl.ds(start, tq), :], causal)
    o_ref[0] = (acc / l).astype(o_ref.dtype)


def _mla_attention(q, k, v, batch, seq, tq=512):
    hw = N_HEADS * LANE
    tq = min(tq, seq)
    q, k, v = (a.reshape(batch, seq, hw) for a in (q, k, v))
    qspec = pl.BlockSpec((1, tq, LANE), lambda b, h, i: (b, i, h))
    kspec = pl.BlockSpec((1, seq, LANE), lambda b, h, i: (b, 0, h))
    out = pl.pallas_call(
        functools.partial(_mla_attn_kernel, tq=tq, scale=(MLA_NOPE + MLA_ROPE) ** -0.5),
        grid=(batch, N_HEADS, seq // tq),
        in_specs=[qspec, kspec, kspec], out_specs=qspec,
        out_shape=jax.ShapeDtypeStruct((batch, seq, hw), bf16),
        compiler_params=_params("parallel", "parallel", "arbitrary"))(q, k, v)
    return out.reshape(batch * seq, hw)


def _top16(s, want_rank):
    work = s
    rank = jnp.full(s.shape, 99.0, f32) if want_rank else None
    vals = []
    for k in range(PEER_TOPK):
        mx = work.max(axis=0, keepdims=True)
        hit = work == mx
        if want_rank:
            rank = jnp.where(hit, float(k), rank)
        work = jnp.where(hit, -jnp.inf, work)
        vals.append(mx)
    return vals, rank


def _stack_rows(rows):
    n = len(rows)
    rid = lax.broadcasted_iota(jnp.int32, (n, rows[0].shape[1]), 0)
    out = jnp.broadcast_to(rows[0], (n, rows[0].shape[1]))
    for i in range(1, n):
        out = jnp.where(rid == i, rows[i], out)
    return out


def _route_kernel(x_ref, wq_ref, keys_ref, rank2_ref, crow_ref, e1z_ref, e2_ref, *, tm):
    q = jnp.dot(x_ref[...].astype(bf16), wq_ref[...], preferred_element_type=f32).astype(bf16)
    half = PEER_DKEY // 2
    s1_all = lax.dot_general(keys_ref[0, 0], q[:, :half], _NT, preferred_element_type=f32)
    s2_all = lax.dot_general(keys_ref[0, 1], q[:, half:], _NT, preferred_element_type=f32)
    rid8 = lax.broadcasted_iota(jnp.int32, (8, LANE), 0)
    for c in range(tm // LANE):
        lanes = slice(c * LANE, (c + 1) * LANE)
        s1, s2 = s1_all[:, lanes], s2_all[:, lanes]
        v1, _ = _top16(s1, False)
        v2, rank2 = _top16(s2, True)
        v2_all = _stack_rows(v2)
        cands = [v1[0] + v2_all]
        for a in range(1, 8):
            cands.append(jnp.where(rid8 < PEER_TOPK // (a + 1), v1[a] + v2_all[:8], -jnp.inf))
        cands.append(_stack_rows(v1[8:]) + v2[0])
        cand = jnp.concatenate(cands, axis=0)
        work = cand
        for _ in range(PEER_TOPK):
            tau = work.max(axis=0, keepdims=True)
            work = jnp.where(work == tau, -jnp.inf, work)
        cmax = v1[0] + v2[0]
        z = jnp.where(cand >= tau, jnp.exp(cand - cmax), 0.0).sum(axis=0, keepdims=True)
        crow = jnp.zeros(s1.shape, f32)
        for b in range(PEER_TOPK):
            crow = crow + jnp.where(s1 + v2[b] >= tau, 1.0, 0.0)
        rank2_ref[0, :, lanes] = rank2
        crow_ref[0, :, lanes] = crow
        e1z_ref[0, :, lanes] = jnp.exp(s1 - v1[0]) / z
        e2_ref[0, :, lanes] = jnp.exp(s2 - v2[0])


def _peer_route(x, w_q, keys, tm=512):
    t, d = x.shape
    tm = min(tm, t)
    spec = pl.BlockSpec((1, PEER_KEYS, tm), lambda i, h: (h, 0, i))
    sd = jax.ShapeDtypeStruct((PEER_HEADS, PEER_KEYS, t), f32)
    return pl.pallas_call(
        functools.partial(_route_kernel, tm=tm), grid=(t // tm, PEER_HEADS),
        in_specs=[pl.BlockSpec((tm, d), lambda i, h: (i, 0)),
                  pl.BlockSpec((d, PEER_DKEY), lambda i, h: (0, h)),
                  pl.BlockSpec((1, 2, PEER_KEYS, PEER_DKEY // 2), lambda i, h: (h, 0, 0, 0))],
        out_specs=[spec] * 4, out_shape=[sd] * 4,
        compiler_params=_params("parallel", "arbitrary"))(x, w_q, keys)


def _peer_dense_kernel(x_ref, u_ref, vt_ref, rank2_ref, crow_ref, e1z_ref, e2_ref, g_ref, b_ref,
                       o_ref, acc_ref, xb_ref, ht_ref, gt_ref, *, rows, tb, lc, alpha):
    j = pl.program_id(1)

    @pl.when(j == 0)
    def _():
        acc_ref[...] = jnp.zeros_like(acc_ref)
        xb_ref[...] = x_ref[...].astype(bf16)

    ht_ref[...] = lax.dot_general(u_ref[...], xb_ref[...], _NT, preferred_element_type=f32)

    def row(r, carry):
        i1 = j * rows + r
        base = pl.multiple_of(r * PEER_KEYS, PEER_KEYS)
        for c in range(tb // lc):
            lanes = slice(c * lc, (c + 1) * lc)
            w = jnp.zeros((PEER_KEYS, lc), f32)
            for h in range(PEER_HEADS):
                mask = rank2_ref[h, :, lanes] < crow_ref[h, pl.ds(i1, 1), lanes]
                w = w + jnp.where(mask, e2_ref[h, :, lanes] * e1z_ref[h, pl.ds(i1, 1), lanes], 0.0)
            hv = ht_ref[pl.ds(base, PEER_KEYS), lanes]
            gelu = 0.5 * hv * (1.0 + lax.erf(hv * (2.0 ** -0.5)))
            gt_ref[pl.ds(base, PEER_KEYS), lanes] = (gelu * w).astype(bf16)
        return carry

    lax.fori_loop(0, rows, row, 0)
    acc_ref[...] += jnp.dot(vt_ref[...], gt_ref[...], preferred_element_type=f32)

    @pl.when(j == pl.num_programs(1) - 1)
    def _():
        z = alpha * x_ref[...] + acc_ref[...].T
        o_ref[...] = _layer_norm(z, g_ref[...], b_ref[...])


def _peer_dense(x, u, vt, routing, g, b, alpha, tb=512, ec=1024, lc=256):
    t, d = x.shape
    e = u.shape[0]
    tb = min(tb, t)
    rows = ec // PEER_KEYS
    rspec = pl.BlockSpec((PEER_HEADS, PEER_KEYS, tb), lambda i, j: (0, 0, i))
    fixed = pl.BlockSpec((1, d), lambda i, j: (0, 0))
    xspec = pl.BlockSpec((tb, d), lambda i, j: (i, 0))
    return pl.pallas_call(
        functools.partial(_peer_dense_kernel, rows=rows, tb=tb, lc=min(lc, tb), alpha=alpha),
        grid=(t // tb, e // ec),
        in_specs=[xspec, pl.BlockSpec((ec, d), lambda i, j: (j, 0)),
                  pl.BlockSpec((d, ec), lambda i, j: (0, j)),
                  rspec, rspec, rspec, rspec, fixed, fixed],
        out_specs=xspec, out_shape=jax.ShapeDtypeStruct((t, d), f32),
        scratch_shapes=[pltpu.VMEM((d, tb), f32), pltpu.VMEM((tb, d), bf16),
                        pltpu.VMEM((ec, tb), f32), pltpu.VMEM((ec, tb), bf16)],
        compiler_params=_params("parallel", "arbitrary"))(
            x, u, vt, *routing, g.reshape(1, d), b.reshape(1, d))


def _swa_layer(x, batch, seq, w_in, sinks, w_out, bias, g, b, alpha):
    feat = (N_HEADS + 2 * SWA_KV_HEADS) * HEAD_DIM
    d_q = N_HEADS * HEAD_DIM
    kvw = SWA_KV_HEADS * HEAD_DIM
    proj = _mm(x, w_in.astype(bf16), bf16, 512, feat).reshape(batch, seq, feat)
    sink = jnp.repeat(sinks.astype(f32), HEAD_DIM).reshape(1, d_q)
    o = _band_attention(proj, bias, 1, feat, 0, d_q // kvw, d_q // kvw + 1, kvw,
                        N_HEADS // SWA_KV_HEADS, sink)
    return _proj_ln([o], [], w_out.astype(bf16), x, g, b, alpha)


def _dil_layer(x, batch, seq, w_in, w_out, biases, g, b, alpha):
    d_q = N_HEADS * HEAD_DIM
    feat = len(DIL_PATTERNS) * 3 * d_q
    proj = _mm(x, w_in.astype(bf16), bf16, 512, 1536).reshape(batch, seq, feat)
    outs, lses = [], []
    for gi, (window, dil) in enumerate(DIL_PATTERNS):
        o, lse = _band_attention(proj, biases[gi], dil, feat, gi * 3, gi * 3 + 1, gi * 3 + 2, d_q, 1)
        outs.append(o)
        lses.append(lse)
    return _proj_ln(outs, lses, w_out.astype(bf16), x, g, b, alpha)


def _mla_layer(x, batch, seq, w_in, q_norm, w_uq, kv_norm, w_ukv, w_out, g, b, alpha):
    q, k, v = _mla_prep(x, w_in, q_norm, w_uq, kv_norm, w_ukv, seq)
    o = _mla_attention(q, k, v, batch, seq)
    d = w_out.shape[1]
    w3 = w_out.reshape(N_HEADS, MLA_V, d)
    w_pad = jnp.concatenate([w3, jnp.zeros((N_HEADS, LANE - MLA_V, d), f32)], 1)
    return _proj_ln([o], [], w_pad.reshape(N_HEADS * LANE, d).astype(bf16), x, g, b, alpha)


def _peer_layer(x, w_q, keys, u, v, g, b, alpha):
    routing = _peer_route(x, w_q.astype(bf16), keys.astype(bf16))
    return _peer_dense(x, u.astype(bf16), v.T.astype(bf16), routing, g, b, alpha)


def kernel(x, rel_bias, ln_g, ln_b, swa_w_in, swa_sinks, swa_w_out, dil_w_in, dil_w_out,
           mla_w_in, mla_q_norm, mla_w_uq, mla_kv_norm, mla_w_ukv, mla_w_out,
           peer_w_q, peer_keys, peer_u, peer_v):
    batch, seq, d = x.shape
    depth = ln_g.shape[0]
    alpha = (2 * depth) ** 0.25
    assert seq % (DIL_PATTERNS[-1][1] * BLOCK) == 0, "sequence must be a whole number of dilation segments"
    swa_bias = _band_bias(rel_bias, SWA_WINDOW - 1, 1)
    dil_bias = [_band_bias(rel_bias, window // dil, dil) for window, dil in DIL_PATTERNS]
    h = x.reshape(batch * seq, d)
    for i in range(depth):
        kind, j = i % 3, i // 3
        if kind == 0:
            h = _swa_layer(h, batch, seq, swa_w_in[j], swa_sinks[j], swa_w_out[j], swa_bias,
                           ln_g[i, 0], ln_b[i, 0], alpha)
        elif kind == 1:
            h = _dil_layer(h, batch, seq, dil_w_in[j], dil_w_out[j], dil_bias,
                           ln_g[i, 0], ln_b[i, 0], alpha)
        else:
            h = _mla_layer(h, batch, seq, mla_w_in[j], mla_q_norm[j], mla_w_uq[j], mla_kv_norm[j],
                           mla_w_ukv[j], mla_w_out[j], ln_g[i, 0], ln_b[i, 0], alpha)
        h = _peer_layer(h, peer_w_q[i], peer_keys[i], peer_u[i], peer_v[i], ln_g[i, 1], ln_b[i, 1], alpha)
    return h.reshape(batch, seq, d)
```

```python
import functools
import math

import jax
import jax.numpy as jnp
from jax import lax
from jax.experimental import pallas as pl
from jax.experimental.pallas import tpu as pltpu

f32 = jnp.float32
bf16 = jnp.bfloat16

N_HEADS = 16
HEAD_DIM = 64
BLOCK = 128
SWA_KV_HEADS = 2
SWA_WINDOW = 128
DIL_PATTERNS = ((128, 1), (512, 4), (2048, 16))
MLA_Q_RANK = 256
MLA_KV_RANK = 128
MLA_NOPE = 64
MLA_ROPE = 32
MLA_V = 64
ROPE_THETA = 10000.0
REL_BUCKETS = 32
REL_MAX_DIST = 2048
PEER_HEADS = 8
PEER_KEYS = 128
PEER_DKEY = 256
PEER_TOPK = 16
LN_EPS = 1e-5
RMS_EPS = 1e-6
NEG = -1e30

LANE = 128
BF16_ROWS = 16
F32_ROWS = 8
MXU_WIDTH = 256
VMEM_LIMIT = 56 * 1024 * 1024

_NT = (((1,), (1,)), ((), ()))


def _params(*sem):
    return pltpu.CompilerParams(dimension_semantics=sem, vmem_limit_bytes=VMEM_LIMIT)


def _mm_kernel(x_ref, w_ref, o_ref):
    o_ref[...] = jnp.dot(x_ref[...].astype(bf16), w_ref[...],
                         preferred_element_type=f32).astype(o_ref.dtype)


def _mm(x, w, out_dtype, tm, tn):
    m, k = x.shape
    n = w.shape[1]
    tm, tn = min(tm, m), min(tn, n)
    return pl.pallas_call(
        _mm_kernel, grid=(m // tm, n // tn),
        in_specs=[pl.BlockSpec((tm, k), lambda i, j: (i, 0)),
                  pl.BlockSpec((k, tn), lambda i, j: (0, j))],
        out_specs=pl.BlockSpec((tm, tn), lambda i, j: (i, j)),
        out_shape=jax.ShapeDtypeStruct((m, n), out_dtype),
        compiler_params=_params("parallel", "parallel"))(x, w)


def _rel_bucket(dist):
    max_exact = REL_BUCKETS // 2
    n = jnp.maximum(dist, 0)
    nf = jnp.maximum(n, 1).astype(f32)
    large = max_exact + (jnp.log(nf / max_exact) / math.log(REL_MAX_DIST / max_exact)
                         * (REL_BUCKETS - max_exact)).astype(jnp.int32)
    large = jnp.minimum(large, REL_BUCKETS - 1)
    return jnp.where(n < max_exact, n, large)


def _bias_kernel(rel_ref, bucket_ref, valid_ref, o_ref):
    bucket = bucket_ref[...]
    for h in range(N_HEADS):
        acc = jnp.zeros(bucket.shape, f32)
        for b in range(REL_BUCKETS):
            acc = jnp.where(bucket == b, rel_ref[b, h], acc)
        for variant in range(2):
            o_ref[variant, h] = jnp.where(valid_ref[variant] != 0, acc, NEG)


def _band_bias(rel_bias, max_dist, dilation):
    qi = jnp.arange(BLOCK)[:, None]
    kj = jnp.arange(2 * BLOCK)[None, :]
    dist = BLOCK + qi - kj
    bucket = _rel_bucket(dist * dilation).astype(jnp.int32)
    valid = (dist >= 0) & (dist <= max_dist)
    valid = jnp.stack([valid & (kj >= BLOCK), valid]).astype(jnp.int32)
    return pl.pallas_call(
        _bias_kernel,
        in_specs=[pl.BlockSpec(memory_space=pltpu.SMEM),
                  pl.BlockSpec(memory_space=pltpu.VMEM),
                  pl.BlockSpec(memory_space=pltpu.VMEM)],
        out_specs=pl.BlockSpec(memory_space=pltpu.VMEM),
        out_shape=jax.ShapeDtypeStruct((2, N_HEADS, BLOCK, 2 * BLOCK), f32))(rel_bias, bucket, valid)


def _band_kernel(q_ref, kp_ref, kc_ref, vp_ref, vc_ref, bias_ref, *rest, group, with_sink):
    if with_sink:
        sink_ref, o_ref = rest
    else:
        o_ref, lse_ref = rest
    scale = HEAD_DIM ** -0.5
    for h in range(N_HEADS):
        hk = h // group
        qs = slice(h * HEAD_DIM, (h + 1) * HEAD_DIM)
        ks = slice(hk * HEAD_DIM, (hk + 1) * HEAD_DIM)
        q = q_ref[0, :, qs]
        k = jnp.concatenate([kp_ref[0, :, ks], kc_ref[0, :, ks]], axis=0)
        v = jnp.concatenate([vp_ref[0, :, ks], vc_ref[0, :, ks]], axis=0)
        s = lax.dot_general(q, k, _NT, preferred_element_type=f32) * scale + bias_ref[0, h]
        m = s.max(-1, keepdims=True)
        p = jnp.exp(s - m)
        l = p.sum(-1, keepdims=True)
        o = jnp.dot(p.astype(bf16), v, preferred_element_type=f32) / l
        lse = m + jnp.log(l)
        if with_sink:
            o = o * jax.nn.sigmoid(lse - sink_ref[:, qs])
        else:
            lse_ref[0, :, qs] = jnp.broadcast_to(lse, (BLOCK, HEAD_DIM))
        o_ref[0, :, qs] = o.astype(o_ref.dtype)


def _band_attention(view, bias, dil, feat, q_col, k_col, v_col, kv_width, group, sink=None):
    b, m_len, _ = view.shape
    d_q = N_HEADS * HEAD_DIM
    nb = m_len // BLOCK
    qpr, kpr = feat // d_q, feat // kv_width

    def cur(col, per_row):
        return lambda bi, r, n: (bi, n, r * per_row + col)

    def prev(col, per_row):
        return lambda bi, r, n: (bi, jnp.maximum(n - 1, 0), r * per_row + col)

    in_specs = [pl.BlockSpec((1, BLOCK, d_q), cur(q_col, qpr)),
                pl.BlockSpec((1, BLOCK, kv_width), prev(k_col, kpr)),
                pl.BlockSpec((1, BLOCK, kv_width), cur(k_col, kpr)),
                pl.BlockSpec((1, BLOCK, kv_width), prev(v_col, kpr)),
                pl.BlockSpec((1, BLOCK, kv_width), cur(v_col, kpr)),
                pl.BlockSpec((1, N_HEADS, BLOCK, 2 * BLOCK),
                             lambda bi, r, n: (jnp.minimum(n, 1), 0, 0, 0))]
    args = [view, view, view, view, view, bias]
    o_spec = pl.BlockSpec((1, BLOCK, d_q), lambda bi, r, n: (bi, n, r))
    o_shape = jax.ShapeDtypeStruct((b, m_len, dil * d_q), bf16)
    if sink is not None:
        in_specs.append(pl.BlockSpec((1, d_q), lambda bi, r, n: (0, 0)))
        args.append(sink)
        out_specs, out_shape = o_spec, o_shape
    else:
        out_specs = [o_spec, o_spec]
        out_shape = [o_shape, jax.ShapeDtypeStruct((b, m_len, dil * d_q), f32)]
    out = pl.pallas_call(
        functools.partial(_band_kernel, group=group, with_sink=sink is not None),
        grid=(b, dil, nb), in_specs=in_specs, out_specs=out_specs, out_shape=out_shape,
        compiler_params=_params("parallel", "parallel", "arbitrary"))(*args)
    if sink is not None:
        return out.reshape(b * m_len, dil * d_q)
    return out[0].reshape(b * m_len, dil * d_q), out[1].reshape(b * m_len, dil * d_q)


def _layer_norm(z, g, b):
    mu = z.mean(-1, keepdims=True)
    zc = z - mu
    var = jnp.square(zc).mean(-1, keepdims=True)
    return zc * lax.rsqrt(var + LN_EPS) * g + b


def _proj_ln_kernel(*refs, dils, alpha):
    n_pat = len(dils)
    o_refs = refs[:n_pat]
    lse_refs = refs[n_pat:2 * n_pat] if n_pat > 1 else ()
    n_in = len(o_refs) + len(lse_refs)
    w_ref, x_ref, g_ref, b_ref, out_ref = refs[n_in:n_in + 5]
    scratch = refs[n_in + 5:]
    if n_pat == 1:
        o = o_refs[0][...]
    else:
        kw = w_ref.shape[0]
        mix_ref = scratch[-1]

        def natural(ref, dil, scr, c):
            if dil == 1:
                return ref[:, c * LANE:(c + 1) * LANE].astype(f32)
            rows = ref.shape[0]
            for r in range(dil):
                scr[c, pl.ds(r, rows, stride=dil), :] = (
                    ref[:, r * kw + c * LANE:r * kw + (c + 1) * LANE].astype(f32))
            return scr[c]

        for c in range(kw // LANE):
            lses = [natural(r, dl, scratch[2 * i], c) for i, (r, dl) in enumerate(zip(lse_refs, dils))]
            mx = functools.reduce(jnp.maximum, lses)
            es = [jnp.exp(l - mx) for l in lses]
            num = sum(e * natural(r, dl, scratch[2 * i + 1], c)
                      for i, (e, r, dl) in enumerate(zip(es, o_refs, dils)))
            mix_ref[:, c * LANE:(c + 1) * LANE] = (num / sum(es)).astype(bf16)
        o = mix_ref[...]
    y = jnp.dot(o, w_ref[...], preferred_element_type=f32)
    out_ref[...] = _layer_norm(alpha * x_ref[...] + y, g_ref[...], b_ref[...])


def _proj_ln(os_, lses, dils, w_out, x, g, b, alpha, tm=512):
    t, d = x.shape
    tm = min(tm, t)
    kw = w_out.shape[0]
    row = lambda width: pl.BlockSpec((tm, width), lambda i: (i, 0))
    dilated = lambda dl: pl.BlockSpec((tm // dl, dl * kw), lambda i: (i, 0))
    fixed = lambda shape: pl.BlockSpec(shape, lambda i: (0, 0))
    in_specs = ([dilated(dl) for dl in dils] + [dilated(dl) for dl in dils[:len(lses)]]
                + [fixed((kw, d)), row(d), fixed((1, d)), fixed((1, d))])
    scratch = [pltpu.VMEM((kw // LANE, tm, LANE), f32)] * (2 * len(lses))
    if lses:
        scratch.append(pltpu.VMEM((tm, kw), bf16))
    return pl.pallas_call(
        functools.partial(_proj_ln_kernel, dils=tuple(dils), alpha=alpha),
        grid=(t // tm,), in_specs=in_specs, out_specs=row(d),
        out_shape=jax.ShapeDtypeStruct((t, d), f32), scratch_shapes=scratch,
        compiler_params=_params("parallel"))(*os_, *lses, w_out, x, g.reshape(1, d), b.reshape(1, d))


def _mm_dil_kernel(x_ref, w_ref, o_ref, res_ref, *, dil):
    n = w_ref.shape[1]
    res = jnp.dot(x_ref[...].astype(bf16), w_ref[...], preferred_element_type=f32)
    if dil == 1:
        o_ref[...] = res.astype(o_ref.dtype)
        return
    rows = o_ref.shape[0]
    for c in range(n // LANE):
        res_ref[c] = res[:, c * LANE:(c + 1) * LANE]
        for r in range(dil):
            o_ref[:, r * n + c * LANE:r * n + (c + 1) * LANE] = (
                res_ref[c, pl.ds(r, rows, stride=dil), :].astype(o_ref.dtype))


def _mm_dil(x, w, dil, tm=512):
    t, k = x.shape
    n = w.shape[1]
    tm = min(tm, t)
    return pl.pallas_call(
        functools.partial(_mm_dil_kernel, dil=dil), grid=(t // tm,),
        in_specs=[pl.BlockSpec((tm, k), lambda i: (i, 0)), pl.BlockSpec((k, n), lambda i: (0, 0))],
        out_specs=pl.BlockSpec((tm // dil, dil * n), lambda i: (i, 0)),
        out_shape=jax.ShapeDtypeStruct((t // dil, dil * n), bf16),
        scratch_shapes=[pltpu.VMEM((n // LANE, tm, LANE), f32)],
        compiler_params=_params("parallel"))(x, w)


def _rms(c, g):
    return c * lax.rsqrt(jnp.square(c).mean(-1, keepdims=True) + RMS_EPS) * g


def _mla_prep_kernel(x_ref, win_ref, qn_ref, wq_ref, kvn_ref, wkv_ref, cos_ref, sin_ref,
                     q_ref, k_ref, v_ref):
    hw = N_HEADS * LANE
    scale = (MLA_NOPE + MLA_ROPE) ** -0.5
    cos, sin = cos_ref[...], sin_ref[...]
    xw = jnp.dot(x_ref[...].astype(bf16), win_ref[...], preferred_element_type=f32)
    cq = _rms(xw[:, :MLA_Q_RANK], qn_ref[...]).astype(bf16)
    ckv = _rms(xw[:, MLA_Q_RANK:MLA_Q_RANK + MLA_KV_RANK], kvn_ref[...]).astype(bf16)
    off = MLA_Q_RANK + MLA_KV_RANK
    kr = xw[:, off:off + LANE] * cos + xw[:, off + LANE:off + 2 * LANE] * sin
    qq = jnp.dot(cq, wq_ref[...], preferred_element_type=f32)
    kv = jnp.dot(ckv, wkv_ref[...], preferred_element_type=f32)
    for h in range(N_HEADS):
        blk = slice(h * LANE, (h + 1) * LANE)
        swp = slice(hw + h * LANE, hw + (h + 1) * LANE)
        q_ref[:, blk] = ((qq[:, blk] * cos + qq[:, swp] * sin) * scale).astype(bf16)
        k_ref[:, blk] = (kv[:, blk] + kr).astype(bf16)
    ones_lane = lax.broadcasted_iota(jnp.int32, (1, hw), 1) % LANE == MLA_V
    v_ref[...] = jnp.where(ones_lane, 1.0, kv[:, hw:]).astype(bf16)


def _mla_prep(x, w_in, q_norm, w_uq, kv_norm, w_ukv, seq, tm=512):
    t, d = x.shape
    tm = min(tm, seq)
    hw = N_HEADS * LANE
    half = MLA_ROPE // 2
    dq = MLA_NOPE + MLA_ROPE
    kr_w = w_in[:, MLA_Q_RANK + MLA_KV_RANK:]
    zeros = lambda n: jnp.zeros((d, n), f32)
    kr_a = jnp.concatenate([zeros(MLA_NOPE), kr_w, zeros(LANE - dq)], 1)
    kr_b = jnp.concatenate([zeros(MLA_NOPE), kr_w[:, half:], kr_w[:, :half], zeros(LANE - dq)], 1)
    win = jnp.concatenate([w_in[:, :MLA_Q_RANK + MLA_KV_RANK], kr_a, kr_b], 1).astype(bf16)
    wq3 = w_uq.reshape(MLA_Q_RANK, N_HEADS, dq)
    zq = lambda n: jnp.zeros((MLA_Q_RANK, N_HEADS, n), f32)
    wq_a = jnp.concatenate([wq3, zq(LANE - dq)], 2)
    wq_b = jnp.concatenate([zq(MLA_NOPE), wq3[:, :, MLA_NOPE + half:], wq3[:, :, MLA_NOPE:MLA_NOPE + half],
                            zq(LANE - dq)], 2)
    wq = jnp.concatenate([wq_a.reshape(MLA_Q_RANK, hw), wq_b.reshape(MLA_Q_RANK, hw)], 1).astype(bf16)
    wkv3 = w_ukv.reshape(MLA_KV_RANK, N_HEADS, MLA_NOPE + MLA_V)
    zk = lambda n: jnp.zeros((MLA_KV_RANK, N_HEADS, n), f32)
    wk = jnp.concatenate([wkv3[:, :, :MLA_NOPE], zk(LANE - MLA_NOPE)], 2)
    wv = jnp.concatenate([wkv3[:, :, MLA_NOPE:], zk(LANE - MLA_V)], 2)
    wkv = jnp.concatenate([wk.reshape(MLA_KV_RANK, hw), wv.reshape(MLA_KV_RANK, hw)], 1).astype(bf16)
    freq = ROPE_THETA ** (-jnp.arange(half, dtype=f32) / half)
    ang = jnp.arange(seq)[:, None].astype(f32) * freq[None, :]
    c, s_ = jnp.cos(ang), jnp.sin(ang)
    cos_t = jnp.concatenate([jnp.ones((seq, MLA_NOPE), f32), c, c, jnp.ones((seq, LANE - dq), f32)], 1)
    sin_t = jnp.concatenate([jnp.zeros((seq, MLA_NOPE), f32), -s_, s_, jnp.zeros((seq, LANE - dq), f32)], 1)

    n_pos = seq // tm
    row = lambda width: pl.BlockSpec((tm, width), lambda i: (i, 0))
    fixed = lambda a: pl.BlockSpec(a.shape, lambda i: (0, 0))
    pos = pl.BlockSpec((tm, LANE), lambda i: (i % n_pos, 0))
    qn, kvn = q_norm.reshape(1, -1), kv_norm.reshape(1, -1)
    out_sd = jax.ShapeDtypeStruct((t, hw), bf16)
    return pl.pallas_call(
        _mla_prep_kernel, grid=(t // tm,),
        in_specs=[row(d), fixed(win), fixed(qn), fixed(wq), fixed(kvn), fixed(wkv), pos, pos],
        out_specs=[row(hw)] * 3, out_shape=[out_sd] * 3,
        compiler_params=_params("parallel"))(x, win, qn, wq, kvn, wkv, cos_t, sin_t)


def _mla_attn_kernel(q_ref, k_ref, v_ref, o_ref, *, tq, heads):
    qi = pl.program_id(2)
    lanes = [slice(j * LANE, (j + 1) * LANE) for j in range(heads)]
    qs = [q_ref[0, :, ln] for ln in lanes]

    def update(carry, start, mask):
        out = []
        for j in range(heads):
            m, acc = carry[j]
            s = lax.dot_general(qs[j], k_ref[0, pl.ds(start, tq), lanes[j]], _NT, preferred_element_type=f32)
            if mask is not None:
                s = jnp.where(mask, s, NEG)
            m_new = jnp.maximum(m, s.max(-1, keepdims=True))
            p = jnp.exp(s - m_new).astype(bf16)
            pv = jnp.dot(p, v_ref[0, pl.ds(start, tq), lanes[j]], preferred_element_type=f32)
            out.append((m_new, jnp.exp(m - m_new) * acc + pv))
        return tuple(out)

    init = tuple((jnp.full((tq, 1), NEG, f32), jnp.zeros((tq, LANE), f32)) for _ in range(heads))
    carry = lax.fori_loop(0, qi, lambda ki, c: update(c, pl.multiple_of(ki * tq, tq), None), init)
    causal = lax.broadcasted_iota(jnp.int32, (tq, tq), 1) <= lax.broadcasted_iota(jnp.int32, (tq, tq), 0)
    carry = update(carry, pl.multiple_of(qi * tq, tq), causal)
    for j in range(heads):
        acc = carry[j][1]
        o_ref[0, :, lanes[j]] = (acc / acc[:, MLA_V:MLA_V + 1]).astype(o_ref.dtype)


def _mla_attention(q, k, v, batch, seq, tq=512, heads=2):
    hw = N_HEADS * LANE
    tq = min(tq, seq)
    q, k, v = (a.reshape(batch, seq, hw) for a in (q, k, v))
    qspec = pl.BlockSpec((1, tq, heads * LANE), lambda b, h, i: (b, i, h))
    kspec = pl.BlockSpec((1, seq, heads * LANE), lambda b, h, i: (b, 0, h))
    out = pl.pallas_call(
        functools.partial(_mla_attn_kernel, tq=tq, heads=heads),
        grid=(batch, N_HEADS // heads, seq // tq),
        in_specs=[qspec, kspec, kspec], out_specs=qspec,
        out_shape=jax.ShapeDtypeStruct((batch, seq, hw), bf16),
        compiler_params=_params("parallel", "parallel", "arbitrary"))(q, k, v)
    return out.reshape(batch * seq, hw)


def _top16(s, want_rank):
    work = s
    rank = jnp.full(s.shape, 99.0, f32) if want_rank else None
    vals = []
    for k in range(PEER_TOPK):
        mx = work.max(axis=0, keepdims=True)
        hit = work == mx
        if want_rank:
            rank = jnp.where(hit, float(k), rank)
        work = jnp.where(hit, -jnp.inf, work)
        vals.append(mx)
    return vals, rank


def _stack_rows(rows):
    n = len(rows)
    rid = lax.broadcasted_iota(jnp.int32, (n, rows[0].shape[1]), 0)
    out = jnp.broadcast_to(rows[0], (n, rows[0].shape[1]))
    for i in range(1, n):
        out = jnp.where(rid == i, rows[i], out)
    return out


def _route_kernel(x_ref, wq_ref, keys_ref, rank2_ref, crow_ref, e1z_ref, e2_ref, *, tm):
    q = jnp.dot(x_ref[...].astype(bf16), wq_ref[...], preferred_element_type=f32).astype(bf16)
    half = PEER_DKEY // 2
    s1_all = lax.dot_general(keys_ref[0, 0], q[:, :half], _NT, preferred_element_type=f32)
    s2_all = lax.dot_general(keys_ref[0, 1], q[:, half:], _NT, preferred_element_type=f32)
    rid8 = lax.broadcasted_iota(jnp.int32, (8, LANE), 0)
    for c in range(tm // LANE):
        lanes = slice(c * LANE, (c + 1) * LANE)
        s1, s2 = s1_all[:, lanes], s2_all[:, lanes]
        v1, _ = _top16(s1, False)
        v2, rank2 = _top16(s2, True)
        v2_all = _stack_rows(v2)
        cands = [v1[0] + v2_all]
        for a in range(1, 8):
            cands.append(jnp.where(rid8 < PEER_TOPK // (a + 1), v1[a] + v2_all[:8], -jnp.inf))
        cands.append(_stack_rows(v1[8:]) + v2[0])
        cand = jnp.concatenate(cands, axis=0)
        work = cand
        for _ in range(PEER_TOPK):
            tau = work.max(axis=0, keepdims=True)
            work = jnp.where(work == tau, -jnp.inf, work)
        cmax = v1[0] + v2[0]
        z = jnp.where(cand >= tau, jnp.exp(cand - cmax), 0.0).sum(axis=0, keepdims=True)
        crow = jnp.zeros(s1.shape, f32)
        for b in range(PEER_TOPK):
            crow = crow + jnp.where(s1 + v2[b] >= tau, 1.0, 0.0)
        rank2_ref[0, :, lanes] = rank2.astype(bf16)
        crow_ref[0, :, lanes] = crow
        e1z_ref[0, :, lanes] = jnp.exp(s1 - v1[0]) / z
        e2_ref[0, :, lanes] = jnp.exp(s2 - v2[0]).astype(bf16)


def _peer_route(x, w_q, keys, tm=512):
    t, d = x.shape
    tm = min(tm, t)
    spec = pl.BlockSpec((1, PEER_KEYS, tm), lambda i, h: (h, 0, i))
    sd = lambda dt: jax.ShapeDtypeStruct((PEER_HEADS, PEER_KEYS, t), dt)
    return pl.pallas_call(
        functools.partial(_route_kernel, tm=tm), grid=(t // tm, PEER_HEADS),
        in_specs=[pl.BlockSpec((tm, d), lambda i, h: (i, 0)),
                  pl.BlockSpec((d, PEER_DKEY), lambda i, h: (0, h)),
                  pl.BlockSpec((1, 2, PEER_KEYS, PEER_DKEY // 2), lambda i, h: (h, 0, 0, 0))],
        out_specs=[spec] * 4, out_shape=[sd(bf16), sd(f32), sd(f32), sd(bf16)],
        compiler_params=_params("parallel", "arbitrary"))(x, w_q, keys)


def _peer_dense_kernel(x_ref, u_ref, vt_ref, rank2_in, crow_ref, e1z_ref, e2_in, g_ref, b_ref,
                       o_ref, acc_ref, xb_ref, ht0, ht1, gt0, gt1, rank2_ref, e2_ref,
                       *, rows, tb, lc, mw, alpha):
    k = pl.program_id(1)
    ec = rows * PEER_KEYS

    @pl.when(k == 0)
    def _():
        acc_ref[...] = jnp.zeros_like(acc_ref)
        ht1[...] = jnp.zeros_like(ht1)
        gt0[...] = jnp.zeros_like(gt0)
        xb_ref[...] = x_ref[...].astype(bf16)
        rank2_ref[...] = rank2_in[...]
        e2_ref[...] = e2_in[...]

    zero = jnp.zeros((), bf16)
    kt = PEER_KEYS // BF16_ROWS

    def up(half, ht, cw):
        ht[:, cw * mw:(cw + 1) * mw] = lax.dot_general(
            u_ref[half * ec:(half + 1) * ec, :], xb_ref[cw * mw:(cw + 1) * mw, :], _NT,
            preferred_element_type=f32)

    n_tiles = pl.num_programs(1) - 1

    def gate(ht, tile, half, gt, cw):
        base = pl.multiple_of(jnp.clip(tile, 0, n_tiles - 1) * F32_ROWS, F32_ROWS)
        for r in range(rows):
            keys = slice(r * PEER_KEYS, (r + 1) * PEER_KEYS)
            row = slice(half * rows + r, half * rows + r + 1)
            for c in range(cw * mw // lc, (cw + 1) * mw // lc):
                lanes = slice(c * lc, (c + 1) * lc)
                w = None
                for h in range(PEER_HEADS):
                    cr = crow_ref[h, pl.ds(base, F32_ROWS), lanes][row]
                    ez = e1z_ref[h, pl.ds(base, F32_ROWS), lanes][row]
                    cr = jnp.broadcast_to(cr, (BF16_ROWS, lc)).astype(bf16)[None]
                    ez = jnp.broadcast_to(ez, (BF16_ROWS, lc)).astype(bf16)[None]
                    term = jnp.where(rank2_ref[h, :, :, lanes] < cr, e2_ref[h, :, :, lanes] * ez, zero)
                    w = term if w is None else w + term
                hv = ht[keys, lanes]
                gelu = 0.5 * hv * (1.0 + lax.erf(hv * (2.0 ** -0.5)))
                gt[r * kt:(r + 1) * kt, :, lanes] = gelu.astype(bf16).reshape(kt, BF16_ROWS, lc) * w

    def down(half, gt, cw):
        lanes = slice(cw * mw, (cw + 1) * mw)
        acc_ref[:, lanes] += jnp.dot(vt_ref[:, half * ec:(half + 1) * ec], gt[:, :, lanes].reshape(ec, mw),
                                     preferred_element_type=f32)

    for cw in range(tb // mw):
        up(0, ht0, cw)
        gate(ht1, k - 1, 1, gt1, cw)
        down(0, gt0, cw)
    for cw in range(tb // mw):
        up(1, ht1, cw)
        gate(ht0, k, 0, gt0, cw)
        down(1, gt1, cw)

    @pl.when(k == pl.num_programs(1) - 1)
    def _():
        z = alpha * x_ref[...] + acc_ref[...].T
        o_ref[...] = _layer_norm(z, g_ref[...], b_ref[...])


def _peer_dense(x, u, vt, routing, g, b, alpha, tb=512, ec=512, lc=128):
    t, d = x.shape
    e = u.shape[0]
    tb = min(tb, t)
    rows = ec // PEER_KEYS
    assert 2 * rows == F32_ROWS, "a pair of expert blocks must span one 8-row f32 tile of sub-key-1 rows"
    nk = e // (2 * ec)
    kt = PEER_KEYS // BF16_ROWS
    rank2, crow, e1z, e2 = routing
    rank2, e2 = (a.reshape(PEER_HEADS, kt, BF16_ROWS, t) for a in (rank2, e2))
    full = pl.BlockSpec((PEER_HEADS, kt, BF16_ROWS, tb), lambda i, k: (0, 0, 0, i))
    rowtab = pl.BlockSpec((PEER_HEADS, PEER_KEYS, tb), lambda i, k: (0, 0, i))
    fixed = pl.BlockSpec((1, d), lambda i, k: (0, 0))
    xspec = pl.BlockSpec((tb, d), lambda i, k: (i, 0))
    gt_scr = pltpu.VMEM((ec // BF16_ROWS, BF16_ROWS, tb), bf16)
    return pl.pallas_call(
        functools.partial(_peer_dense_kernel, rows=rows, tb=tb, lc=lc, mw=MXU_WIDTH, alpha=alpha),
        grid=(t // tb, nk + 1),
        in_specs=[xspec, pl.BlockSpec((2 * ec, d), lambda i, k: (jnp.minimum(k, nk - 1), 0)),
                  pl.BlockSpec((d, 2 * ec), lambda i, k: (0, jnp.maximum(k - 1, 0))),
                  full, rowtab, rowtab, full, fixed, fixed],
        out_specs=xspec, out_shape=jax.ShapeDtypeStruct((t, d), f32),
        scratch_shapes=[pltpu.VMEM((d, tb), f32), pltpu.VMEM((tb, d), bf16),
                        pltpu.VMEM((ec, tb), f32), pltpu.VMEM((ec, tb), f32), gt_scr, gt_scr,
                        pltpu.VMEM((PEER_HEADS, kt, BF16_ROWS, tb), bf16),
                        pltpu.VMEM((PEER_HEADS, kt, BF16_ROWS, tb), bf16)],
        compiler_params=_params("parallel", "arbitrary"))(
            x, u, vt, rank2, crow, e1z, e2, g.reshape(1, d), b.reshape(1, d))


def _swa_layer(x, batch, seq, w_in, sinks, w_out, bias, g, b, alpha):
    feat = (N_HEADS + 2 * SWA_KV_HEADS) * HEAD_DIM
    d_q = N_HEADS * HEAD_DIM
    kvw = SWA_KV_HEADS * HEAD_DIM
    proj = _mm(x, w_in.astype(bf16), bf16, 512, feat).reshape(batch, seq, feat)
    sink = jnp.repeat(sinks.astype(f32), HEAD_DIM).reshape(1, d_q)
    o = _band_attention(proj, bias, 1, feat, 0, d_q // kvw, d_q // kvw + 1, kvw,
                        N_HEADS // SWA_KV_HEADS, sink)
    return _proj_ln([o], [], [1], w_out.astype(bf16), x, g, b, alpha)


def _dil_layer(x, batch, seq, w_in, w_out, biases, g, b, alpha):
    d_q = N_HEADS * HEAD_DIM
    feat = 3 * d_q
    w_in = w_in.astype(bf16)
    outs, lses, dils = [], [], []
    for gi, (window, dil) in enumerate(DIL_PATTERNS):
        proj = _mm_dil(x, w_in[:, gi * feat:(gi + 1) * feat], dil).reshape(batch, seq // dil, dil * feat)
        o, lse = _band_attention(proj, biases[gi], dil, feat, 0, 1, 2, d_q, 1)
        outs.append(o)
        lses.append(lse)
        dils.append(dil)
    return _proj_ln(outs, lses, dils, w_out.astype(bf16), x, g, b, alpha)


def _mla_layer(x, batch, seq, w_in, q_norm, w_uq, kv_norm, w_ukv, w_out, g, b, alpha):
    q, k, v = _mla_prep(x, w_in, q_norm, w_uq, kv_norm, w_ukv, seq)
    o = _mla_attention(q, k, v, batch, seq)
    d = w_out.shape[1]
    w3 = w_out.reshape(N_HEADS, MLA_V, d)
    w_pad = jnp.concatenate([w3, jnp.zeros((N_HEADS, LANE - MLA_V, d), f32)], 1)
    return _proj_ln([o], [], [1], w_pad.reshape(N_HEADS * LANE, d).astype(bf16), x, g, b, alpha)


def _peer_layer(x, w_q, keys, u, v, g, b, alpha):
    routing = _peer_route(x, w_q.astype(bf16), keys.astype(bf16))
    return _peer_dense(x, u.astype(bf16), v.T.astype(bf16), routing, g, b, alpha)


def kernel(x, rel_bias, ln_g, ln_b, swa_w_in, swa_sinks, swa_w_out, dil_w_in, dil_w_out,
           mla_w_in, mla_q_norm, mla_w_uq, mla_kv_norm, mla_w_ukv, mla_w_out,
           peer_w_q, peer_keys, peer_u, peer_v):
    batch, seq, d = x.shape
    depth = ln_g.shape[0]
    alpha = (2 * depth) ** 0.25
    assert seq % (DIL_PATTERNS[-1][1] * BLOCK) == 0, "sequence must be a whole number of dilation segments"
    swa_bias = _band_bias(rel_bias, SWA_WINDOW - 1, 1)
    dil_bias = [_band_bias(rel_bias, window // dil, dil) for window, dil in DIL_PATTERNS]
    h = x.reshape(batch * seq, d)
    for i in range(depth):
        kind, j = i % 3, i // 3
        if kind == 0:
            h = _swa_layer(h, batch, seq, swa_w_in[j], swa_sinks[j], swa_w_out[j], swa_bias,
                           ln_g[i, 0], ln_b[i, 0], alpha)
        elif kind == 1:
            h = _dil_layer(h, batch, seq, dil_w_in[j], dil_w_out[j], dil_bias,
                           ln_g[i, 0], ln_b[i, 0], alpha)
        else:
            h = _mla_layer(h, batch, seq, mla_w_in[j], mla_q_norm[j], mla_w_uq[j], mla_kv_norm[j],
                           mla_w_ukv[j], mla_w_out[j], ln_g[i, 0], ln_b[i, 0], alpha)
        h = _peer_layer(h, peer_w_q[i], peer_keys[i], peer_u[i], peer_v[i], ln_g[i, 1], ln_b[i, 1], alpha)
    return h.reshape(batch, seq, d)
```

```python
import functools
import math

import jax
import jax.numpy as jnp
from jax import lax
from jax.experimental import pallas as pl
from jax.experimental.pallas import tpu as pltpu

f32 = jnp.float32
bf16 = jnp.bfloat16

N_HEADS = 16
HEAD_DIM = 64
BLOCK = 128
SWA_KV_HEADS = 2
SWA_WINDOW = 128
DIL_PATTERNS = ((128, 1), (512, 4), (2048, 16))
MLA_Q_RANK = 256
MLA_KV_RANK = 128
MLA_NOPE = 64
MLA_ROPE = 32
MLA_V = 64
ROPE_THETA = 10000.0
REL_BUCKETS = 32
REL_MAX_DIST = 2048
PEER_HEADS = 8
PEER_KEYS = 128
PEER_DKEY = 256
PEER_TOPK = 16
LN_EPS = 1e-5
RMS_EPS = 1e-6
NEG = -1e30

LANE = 128
BF16_ROWS = 16
F32_ROWS = 8
MXU_WIDTH = 256
HEAD_BATCH = 16
DENSE_EC = 4 * PEER_KEYS
VMEM_LIMIT = 56 * 1024 * 1024

_NT = (((1,), (1,)), ((), ()))


def _params(*sem):
    return pltpu.CompilerParams(dimension_semantics=sem, vmem_limit_bytes=VMEM_LIMIT)


def _mm_kernel(x_ref, w_ref, o_ref):
    o_ref[...] = jnp.dot(x_ref[...].astype(bf16), w_ref[...],
                         preferred_element_type=f32).astype(o_ref.dtype)


def _mm(x, w, out_dtype, tm, tn):
    m, k = x.shape
    n = w.shape[1]
    tm, tn = min(tm, m), min(tn, n)
    return pl.pallas_call(
        _mm_kernel, grid=(m // tm, n // tn),
        in_specs=[pl.BlockSpec((tm, k), lambda i, j: (i, 0)),
                  pl.BlockSpec((k, tn), lambda i, j: (0, j))],
        out_specs=pl.BlockSpec((tm, tn), lambda i, j: (i, j)),
        out_shape=jax.ShapeDtypeStruct((m, n), out_dtype),
        compiler_params=_params("parallel", "parallel"))(x, w)


def _rel_bucket(dist):
    max_exact = REL_BUCKETS // 2
    n = jnp.maximum(dist, 0)
    nf = jnp.maximum(n, 1).astype(f32)
    large = max_exact + (jnp.log(nf / max_exact) / math.log(REL_MAX_DIST / max_exact)
                         * (REL_BUCKETS - max_exact)).astype(jnp.int32)
    large = jnp.minimum(large, REL_BUCKETS - 1)
    return jnp.where(n < max_exact, n, large)


def _bias_kernel(rel_ref, bucket_ref, valid_ref, o_ref):
    bucket = bucket_ref[...]
    for h in range(N_HEADS):
        acc = jnp.zeros(bucket.shape, f32)
        for b in range(REL_BUCKETS):
            acc = jnp.where(bucket == b, rel_ref[b, h], acc)
        for variant in range(2):
            o_ref[variant, h] = jnp.where(valid_ref[variant] != 0, acc, NEG)


def _band_bias(rel_bias, max_dist, dilation):
    qi = jnp.arange(BLOCK)[:, None]
    kj = jnp.arange(2 * BLOCK)[None, :]
    dist = BLOCK + qi - kj
    bucket = _rel_bucket(dist * dilation).astype(jnp.int32)
    valid = (dist >= 0) & (dist <= max_dist)
    valid = jnp.stack([valid & (kj >= BLOCK), valid]).astype(jnp.int32)
    return pl.pallas_call(
        _bias_kernel,
        in_specs=[pl.BlockSpec(memory_space=pltpu.SMEM),
                  pl.BlockSpec(memory_space=pltpu.VMEM),
                  pl.BlockSpec(memory_space=pltpu.VMEM)],
        out_specs=pl.BlockSpec(memory_space=pltpu.VMEM),
        out_shape=jax.ShapeDtypeStruct((2, N_HEADS, BLOCK, 2 * BLOCK), f32))(rel_bias, bucket, valid)


def _band_kernel(q_ref, kp_ref, kc_ref, vp_ref, vc_ref, bias_ref, *rest, group, with_sink):
    if with_sink:
        sink_ref, o_ref = rest
    else:
        o_ref, lse_ref = rest
    scale = HEAD_DIM ** -0.5
    for h0 in range(0, N_HEADS, HEAD_BATCH):
        heads = range(h0, h0 + HEAD_BATCH)
        qs = [slice(h * HEAD_DIM, (h + 1) * HEAD_DIM) for h in heads]
        ks = [slice((h // group) * HEAD_DIM, (h // group + 1) * HEAD_DIM) for h in heads]
        kk = [jnp.concatenate([kp_ref[0, :, c], kc_ref[0, :, c]], axis=0) for c in ks]
        vv = [jnp.concatenate([vp_ref[0, :, c], vc_ref[0, :, c]], axis=0) for c in ks]
        ss = [lax.dot_general(q_ref[0, :, c], k, _NT, preferred_element_type=f32) * scale + bias_ref[0, h]
              for h, c, k in zip(heads, qs, kk)]
        ms = [s.max(-1, keepdims=True) for s in ss]
        ps = [jnp.exp(s - m) for s, m in zip(ss, ms)]
        ls = [p.sum(-1, keepdims=True) for p in ps]
        os_ = [jnp.dot(p.astype(bf16), v, preferred_element_type=f32) / l for p, v, l in zip(ps, vv, ls)]
        lses = [m + jnp.log(l) for m, l in zip(ms, ls)]
        for c, o, lse in zip(qs, os_, lses):
            if with_sink:
                o = o * jax.nn.sigmoid(lse - sink_ref[:, c])
            else:
                lse_ref[0, :, c] = jnp.broadcast_to(lse, (BLOCK, HEAD_DIM))
            o_ref[0, :, c] = o.astype(o_ref.dtype)


def _band_attention(view, bias, dil, feat, q_col, k_col, v_col, kv_width, group, sink=None):
    b, m_len, _ = view.shape
    d_q = N_HEADS * HEAD_DIM
    nb = m_len // BLOCK
    qpr, kpr = feat // d_q, feat // kv_width

    def cur(col, per_row):
        return lambda bi, r, n: (bi, n, r * per_row + col)

    def prev(col, per_row):
        return lambda bi, r, n: (bi, jnp.maximum(n - 1, 0), r * per_row + col)

    in_specs = [pl.BlockSpec((1, BLOCK, d_q), cur(q_col, qpr)),
                pl.BlockSpec((1, BLOCK, kv_width), prev(k_col, kpr)),
                pl.BlockSpec((1, BLOCK, kv_width), cur(k_col, kpr)),
                pl.BlockSpec((1, BLOCK, kv_width), prev(v_col, kpr)),
                pl.BlockSpec((1, BLOCK, kv_width), cur(v_col, kpr)),
                pl.BlockSpec((1, N_HEADS, BLOCK, 2 * BLOCK),
                             lambda bi, r, n: (jnp.minimum(n, 1), 0, 0, 0))]
    args = [view, view, view, view, view, bias]
    o_spec = pl.BlockSpec((1, BLOCK, d_q), lambda bi, r, n: (bi, n, r))
    o_shape = jax.ShapeDtypeStruct((b, m_len, dil * d_q), bf16)
    if sink is not None:
        in_specs.append(pl.BlockSpec((1, d_q), lambda bi, r, n: (0, 0)))
        args.append(sink)
        out_specs, out_shape = o_spec, o_shape
    else:
        out_specs = [o_spec, o_spec]
        out_shape = [o_shape, jax.ShapeDtypeStruct((b, m_len, dil * d_q), f32)]
    out = pl.pallas_call(
        functools.partial(_band_kernel, group=group, with_sink=sink is not None),
        grid=(b, dil, nb), in_specs=in_specs, out_specs=out_specs, out_shape=out_shape,
        compiler_params=_params("parallel", "parallel", "arbitrary"))(*args)
    if sink is not None:
        return out.reshape(b * m_len, dil * d_q)
    return out[0].reshape(b * m_len, dil * d_q), out[1].reshape(b * m_len, dil * d_q)


def _layer_norm(z, g, b):
    mu = z.mean(-1, keepdims=True)
    zc = z - mu
    var = jnp.square(zc).mean(-1, keepdims=True)
    return zc * lax.rsqrt(var + LN_EPS) * g + b


def _proj_ln_kernel(*refs, dils, alpha):
    n_pat = len(dils)
    o_refs = refs[:n_pat]
    lse_refs = refs[n_pat:2 * n_pat] if n_pat > 1 else ()
    n_in = len(o_refs) + len(lse_refs)
    w_ref, x_ref, g_ref, b_ref, out_ref = refs[n_in:n_in + 5]
    scratch = refs[n_in + 5:]
    if n_pat == 1:
        o = o_refs[0][...]
    else:
        kw = w_ref.shape[0]
        mix_ref = scratch[-1]

        def natural(ref, dil, scr, c):
            if dil == 1:
                return ref[:, c * LANE:(c + 1) * LANE].astype(f32)
            rows = ref.shape[0]
            for r in range(dil):
                scr[c, pl.ds(r, rows, stride=dil), :] = (
                    ref[:, r * kw + c * LANE:r * kw + (c + 1) * LANE].astype(f32))
            return scr[c]

        for c in range(kw // LANE):
            lses = [natural(r, dl, scratch[2 * i], c) for i, (r, dl) in enumerate(zip(lse_refs, dils))]
            mx = functools.reduce(jnp.maximum, lses)
            es = [jnp.exp(l - mx) for l in lses]
            num = sum(e * natural(r, dl, scratch[2 * i + 1], c)
                      for i, (e, r, dl) in enumerate(zip(es, o_refs, dils)))
            mix_ref[:, c * LANE:(c + 1) * LANE] = (num / sum(es)).astype(bf16)
        o = mix_ref[...]
    y = jnp.dot(o, w_ref[...], preferred_element_type=f32)
    out_ref[...] = _layer_norm(alpha * x_ref[...] + y, g_ref[...], b_ref[...])


def _proj_ln(os_, lses, dils, w_out, x, g, b, alpha, tm=512):
    t, d = x.shape
    tm = min(tm, t)
    kw = w_out.shape[0]
    row = lambda width: pl.BlockSpec((tm, width), lambda i: (i, 0))
    dilated = lambda dl: pl.BlockSpec((tm // dl, dl * kw), lambda i: (i, 0))
    fixed = lambda shape: pl.BlockSpec(shape, lambda i: (0, 0))
    in_specs = ([dilated(dl) for dl in dils] + [dilated(dl) for dl in dils[:len(lses)]]
                + [fixed((kw, d)), row(d), fixed((1, d)), fixed((1, d))])
    scratch = [pltpu.VMEM((kw // LANE, tm, LANE), f32)] * (2 * len(lses))
    if lses:
        scratch.append(pltpu.VMEM((tm, kw), bf16))
    return pl.pallas_call(
        functools.partial(_proj_ln_kernel, dils=tuple(dils), alpha=alpha),
        grid=(t // tm,), in_specs=in_specs, out_specs=row(d),
        out_shape=jax.ShapeDtypeStruct((t, d), f32), scratch_shapes=scratch,
        compiler_params=_params("parallel"))(*os_, *lses, w_out, x, g.reshape(1, d), b.reshape(1, d))


def _mm_dil_kernel(x_ref, w_ref, o_ref, res_ref, *, dil):
    n = w_ref.shape[1]
    res = jnp.dot(x_ref[...].astype(bf16), w_ref[...], preferred_element_type=f32)
    if dil == 1:
        o_ref[...] = res.astype(o_ref.dtype)
        return
    rows = o_ref.shape[0]
    for c in range(n // LANE):
        res_ref[c] = res[:, c * LANE:(c + 1) * LANE]
        for r in range(dil):
            o_ref[:, r * n + c * LANE:r * n + (c + 1) * LANE] = (
                res_ref[c, pl.ds(r, rows, stride=dil), :].astype(o_ref.dtype))


def _mm_dil(x, w, dil, tm=512):
    t, k = x.shape
    n = w.shape[1]
    tm = min(tm, t)
    return pl.pallas_call(
        functools.partial(_mm_dil_kernel, dil=dil), grid=(t // tm,),
        in_specs=[pl.BlockSpec((tm, k), lambda i: (i, 0)), pl.BlockSpec((k, n), lambda i: (0, 0))],
        out_specs=pl.BlockSpec((tm // dil, dil * n), lambda i: (i, 0)),
        out_shape=jax.ShapeDtypeStruct((t // dil, dil * n), bf16),
        scratch_shapes=[pltpu.VMEM((n // LANE, tm, LANE), f32)],
        compiler_params=_params("parallel"))(x, w)


def _rms(c, g):
    return c * lax.rsqrt(jnp.square(c).mean(-1, keepdims=True) + RMS_EPS) * g


def _mla_prep_kernel(x_ref, win_ref, qn_ref, wq_ref, kvn_ref, wkv_ref, cos_ref, sin_ref,
                     q_ref, k_ref, v_ref):
    hw = N_HEADS * LANE
    scale = (MLA_NOPE + MLA_ROPE) ** -0.5
    cos, sin = cos_ref[...], sin_ref[...]
    xw = jnp.dot(x_ref[...].astype(bf16), win_ref[...], preferred_element_type=f32)
    cq = _rms(xw[:, :MLA_Q_RANK], qn_ref[...]).astype(bf16)
    ckv = _rms(xw[:, MLA_Q_RANK:MLA_Q_RANK + MLA_KV_RANK], kvn_ref[...]).astype(bf16)
    off = MLA_Q_RANK + MLA_KV_RANK
    kr = xw[:, off:off + LANE] * cos + xw[:, off + LANE:off + 2 * LANE] * sin
    qq = jnp.dot(cq, wq_ref[...], preferred_element_type=f32)
    kv = jnp.dot(ckv, wkv_ref[...], preferred_element_type=f32)
    for h in range(N_HEADS):
        blk = slice(h * LANE, (h + 1) * LANE)
        swp = slice(hw + h * LANE, hw + (h + 1) * LANE)
        q_ref[:, blk] = ((qq[:, blk] * cos + qq[:, swp] * sin) * scale).astype(bf16)
        k_ref[:, blk] = (kv[:, blk] + kr).astype(bf16)
    ones_lane = lax.broadcasted_iota(jnp.int32, (1, hw), 1) % LANE == MLA_V
    v_ref[...] = jnp.where(ones_lane, 1.0, kv[:, hw:]).astype(bf16)


def _mla_prep(x, w_in, q_norm, w_uq, kv_norm, w_ukv, seq, tm=512):
    t, d = x.shape
    tm = min(tm, seq)
    hw = N_HEADS * LANE
    half = MLA_ROPE // 2
    dq = MLA_NOPE + MLA_ROPE
    kr_w = w_in[:, MLA_Q_RANK + MLA_KV_RANK:]
    zeros = lambda n: jnp.zeros((d, n), f32)
    kr_a = jnp.concatenate([zeros(MLA_NOPE), kr_w, zeros(LANE - dq)], 1)
    kr_b = jnp.concatenate([zeros(MLA_NOPE), kr_w[:, half:], kr_w[:, :half], zeros(LANE - dq)], 1)
    win = jnp.concatenate([w_in[:, :MLA_Q_RANK + MLA_KV_RANK], kr_a, kr_b], 1).astype(bf16)
    wq3 = w_uq.reshape(MLA_Q_RANK, N_HEADS, dq)
    zq = lambda n: jnp.zeros((MLA_Q_RANK, N_HEADS, n), f32)
    wq_a = jnp.concatenate([wq3, zq(LANE - dq)], 2)
    wq_b = jnp.concatenate([zq(MLA_NOPE), wq3[:, :, MLA_NOPE + half:], wq3[:, :, MLA_NOPE:MLA_NOPE + half],
                            zq(LANE - dq)], 2)
    wq = jnp.concatenate([wq_a.reshape(MLA_Q_RANK, hw), wq_b.reshape(MLA_Q_RANK, hw)], 1).astype(bf16)
    wkv3 = w_ukv.reshape(MLA_KV_RANK, N_HEADS, MLA_NOPE + MLA_V)
    zk = lambda n: jnp.zeros((MLA_KV_RANK, N_HEADS, n), f32)
    wk = jnp.concatenate([wkv3[:, :, :MLA_NOPE], zk(LANE - MLA_NOPE)], 2)
    wv = jnp.concatenate([wkv3[:, :, MLA_NOPE:], zk(LANE - MLA_V)], 2)
    wkv = jnp.concatenate([wk.reshape(MLA_KV_RANK, hw), wv.reshape(MLA_KV_RANK, hw)], 1).astype(bf16)
    freq = ROPE_THETA ** (-jnp.arange(half, dtype=f32) / half)
    ang = jnp.arange(seq)[:, None].astype(f32) * freq[None, :]
    c, s_ = jnp.cos(ang), jnp.sin(ang)
    cos_t = jnp.concatenate([jnp.ones((seq, MLA_NOPE), f32), c, c, jnp.ones((seq, LANE - dq), f32)], 1)
    sin_t = jnp.concatenate([jnp.zeros((seq, MLA_NOPE), f32), -s_, s_, jnp.zeros((seq, LANE - dq), f32)], 1)

    n_pos = seq // tm
    row = lambda width: pl.BlockSpec((tm, width), lambda i: (i, 0))
    fixed = lambda a: pl.BlockSpec(a.shape, lambda i: (0, 0))
    pos = pl.BlockSpec((tm, LANE), lambda i: (i % n_pos, 0))
    qn, kvn = q_norm.reshape(1, -1), kv_norm.reshape(1, -1)
    out_sd = jax.ShapeDtypeStruct((t, hw), bf16)
    return pl.pallas_call(
        _mla_prep_kernel, grid=(t // tm,),
        in_specs=[row(d), fixed(win), fixed(qn), fixed(wq), fixed(kvn), fixed(wkv), pos, pos],
        out_specs=[row(hw)] * 3, out_shape=[out_sd] * 3,
        compiler_params=_params("parallel"))(x, win, qn, wq, kvn, wkv, cos_t, sin_t)


def _mla_attn_kernel(q_ref, k_ref, v_ref, o_ref, *, tq, heads):
    qi = pl.program_id(2)
    lanes = [slice(j * LANE, (j + 1) * LANE) for j in range(heads)]
    qs = [q_ref[0, :, ln] for ln in lanes]

    def update(carry, start, mask):
        ss = [lax.dot_general(q, k_ref[0, pl.ds(start, tq), ln], _NT, preferred_element_type=f32)
              for q, ln in zip(qs, lanes)]
        if mask is not None:
            ss = [jnp.where(mask, s, NEG) for s in ss]
        ms = [jnp.maximum(m, s.max(-1, keepdims=True)) for (m, _), s in zip(carry, ss)]
        ps = [jnp.exp(s - m).astype(bf16) for s, m in zip(ss, ms)]
        pvs = [jnp.dot(p, v_ref[0, pl.ds(start, tq), ln], preferred_element_type=f32)
               for p, ln in zip(ps, lanes)]
        return tuple((m_new, jnp.exp(m - m_new) * acc + pv)
                     for (m, acc), m_new, pv in zip(carry, ms, pvs))

    init = tuple((jnp.full((tq, 1), NEG, f32), jnp.zeros((tq, LANE), f32)) for _ in range(heads))
    carry = lax.fori_loop(0, qi, lambda ki, c: update(c, pl.multiple_of(ki * tq, tq), None), init)
    causal = lax.broadcasted_iota(jnp.int32, (tq, tq), 1) <= lax.broadcasted_iota(jnp.int32, (tq, tq), 0)
    carry = update(carry, pl.multiple_of(qi * tq, tq), causal)
    for j in range(heads):
        acc = carry[j][1]
        o_ref[0, :, lanes[j]] = (acc / acc[:, MLA_V:MLA_V + 1]).astype(o_ref.dtype)


def _mla_attention(q, k, v, batch, seq, tq=512, heads=2):
    hw = N_HEADS * LANE
    tq = min(tq, seq)
    q, k, v = (a.reshape(batch, seq, hw) for a in (q, k, v))
    qspec = pl.BlockSpec((1, tq, heads * LANE), lambda b, h, i: (b, i, h))
    kspec = pl.BlockSpec((1, seq, heads * LANE), lambda b, h, i: (b, 0, h))
    out = pl.pallas_call(
        functools.partial(_mla_attn_kernel, tq=tq, heads=heads),
        grid=(batch, N_HEADS // heads, seq // tq),
        in_specs=[qspec, kspec, kspec], out_specs=qspec,
        out_shape=jax.ShapeDtypeStruct((batch, seq, hw), bf16),
        compiler_params=_params("parallel", "parallel", "arbitrary"))(q, k, v)
    return out.reshape(batch * seq, hw)


def _top16(s, want_rank):
    work = s
    rank = jnp.full(s.shape, 99.0, f32) if want_rank else None
    vals = []
    for k in range(PEER_TOPK):
        mx = work.max(axis=0, keepdims=True)
        hit = work == mx
        if want_rank:
            rank = jnp.where(hit, float(k), rank)
        work = jnp.where(hit, -jnp.inf, work)
        vals.append(mx)
    return vals, rank


def _stack_rows(rows):
    n = len(rows)
    rid = lax.broadcasted_iota(jnp.int32, (n, rows[0].shape[1]), 0)
    out = jnp.broadcast_to(rows[0], (n, rows[0].shape[1]))
    for i in range(1, n):
        out = jnp.where(rid == i, rows[i], out)
    return out


def _route_kernel(x_ref, wq_ref, keys_ref, rank2_ref, crow_ref, e1z_ref, e2_ref, *, tm):
    q = jnp.dot(x_ref[...].astype(bf16), wq_ref[...], preferred_element_type=f32).astype(bf16)
    half = PEER_DKEY // 2
    s1_all = lax.dot_general(keys_ref[0, 0], q[:, :half], _NT, preferred_element_type=f32)
    s2_all = lax.dot_general(keys_ref[0, 1], q[:, half:], _NT, preferred_element_type=f32)
    rid8 = lax.broadcasted_iota(jnp.int32, (8, LANE), 0)
    for c in range(tm // LANE):
        lanes = slice(c * LANE, (c + 1) * LANE)
        s1, s2 = s1_all[:, lanes], s2_all[:, lanes]
        v1, _ = _top16(s1, False)
        v2, rank2 = _top16(s2, True)
        v2_all = _stack_rows(v2)
        cands = [v1[0] + v2_all]
        for a in range(1, 8):
            cands.append(jnp.where(rid8 < PEER_TOPK // (a + 1), v1[a] + v2_all[:8], -jnp.inf))
        cands.append(_stack_rows(v1[8:]) + v2[0])
        cand = jnp.concatenate(cands, axis=0)
        work = cand
        for _ in range(PEER_TOPK):
            tau = work.max(axis=0, keepdims=True)
            work = jnp.where(work == tau, -jnp.inf, work)
        cmax = v1[0] + v2[0]
        z = jnp.where(cand >= tau, jnp.exp(cand - cmax), 0.0).sum(axis=0, keepdims=True)
        crow = jnp.zeros(s1.shape, f32)
        for b in range(PEER_TOPK):
            crow = crow + jnp.where(s1 + v2[b] >= tau, 1.0, 0.0)
        rank2_ref[0, :, lanes] = rank2.astype(bf16)
        crow_ref[0, :, lanes] = crow
        e1z_ref[0, :, lanes] = jnp.exp(s1 - v1[0]) / z
        e2_ref[0, :, lanes] = jnp.exp(s2 - v2[0]).astype(bf16)


def _peer_route(x, w_q, keys, tm=512):
    t, d = x.shape
    tm = min(tm, t)
    spec = pl.BlockSpec((1, PEER_KEYS, tm), lambda i, h: (h, 0, i))
    sd = lambda dt: jax.ShapeDtypeStruct((PEER_HEADS, PEER_KEYS, t), dt)
    return pl.pallas_call(
        functools.partial(_route_kernel, tm=tm), grid=(t // tm, PEER_HEADS),
        in_specs=[pl.BlockSpec((tm, d), lambda i, h: (i, 0)),
                  pl.BlockSpec((d, PEER_DKEY), lambda i, h: (0, h)),
                  pl.BlockSpec((1, 2, PEER_KEYS, PEER_DKEY // 2), lambda i, h: (h, 0, 0, 0))],
        out_specs=[spec] * 4, out_shape=[sd(bf16), sd(f32), sd(f32), sd(bf16)],
        compiler_params=_params("parallel", "arbitrary"))(x, w_q, keys)


def _peer_dense_kernel(x_ref, u0_ref, u1_ref, vt0_ref, vt1_ref, rank2_in, crow_ref, e1z_ref, e2_in,
                       g_ref, b_ref, o_ref, acc_ref, xt_ref, ht0, ht1, gt0, gt1, rank2_ref, e2_ref,
                       *, rows, tb, lc, mw, alpha):
    k = pl.program_id(1)
    ec = rows * PEER_KEYS

    @pl.when(k == 0)
    def _():
        acc_ref[...] = jnp.zeros_like(acc_ref)
        ht1[...] = jnp.zeros_like(ht1)
        gt0[...] = jnp.zeros_like(gt0)
        xt_ref[...] = x_ref[...].T.astype(bf16)
        rank2_ref[...] = rank2_in[...]
        e2_ref[...] = e2_in[...]

    zero = jnp.zeros((), bf16)
    kt = PEER_KEYS // BF16_ROWS

    def up(half, ht, cw):
        lanes = slice(cw * mw, (cw + 1) * mw)
        ht[:, lanes] = jnp.dot((u0_ref, u1_ref)[half][...], xt_ref[:, lanes], preferred_element_type=f32)

    n_tiles = pl.num_programs(1) - 1

    def gate(ht, tile, half, gt, cw):
        base = pl.multiple_of(jnp.clip(tile, 0, n_tiles - 1) * F32_ROWS, F32_ROWS)
        for r in range(rows):
            keys = slice(r * PEER_KEYS, (r + 1) * PEER_KEYS)
            row = slice(half * rows + r, half * rows + r + 1)
            for c in range(cw * mw // lc, (cw + 1) * mw // lc):
                lanes = slice(c * lc, (c + 1) * lc)
                w = None
                for h in range(PEER_HEADS):
                    cr = crow_ref[h, pl.ds(base, F32_ROWS), lanes][row]
                    ez = e1z_ref[h, pl.ds(base, F32_ROWS), lanes][row]
                    cr = jnp.broadcast_to(cr, (BF16_ROWS, lc)).astype(bf16)[None]
                    ez = jnp.broadcast_to(ez, (BF16_ROWS, lc)).astype(bf16)[None]
                    term = jnp.where(rank2_ref[h, :, :, lanes] < cr, e2_ref[h, :, :, lanes] * ez, zero)
                    w = term if w is None else w + term
                hv = ht[keys, lanes]
                gelu = 0.5 * hv * (1.0 + lax.erf(hv * (2.0 ** -0.5)))
                gt[r * kt:(r + 1) * kt, :, lanes] = gelu.astype(bf16).reshape(kt, BF16_ROWS, lc) * w

    def down(half, gt, cw):
        lanes = slice(cw * mw, (cw + 1) * mw)
        acc_ref[:, lanes] += jnp.dot((vt0_ref, vt1_ref)[half][0], gt[:, :, lanes].reshape(ec, mw),
                                     preferred_element_type=f32)

    for cw in range(tb // mw):
        up(0, ht0, cw)
        gate(ht1, k - 1, 1, gt1, cw)
        down(0, gt0, cw)
    for cw in range(tb // mw):
        up(1, ht1, cw)
        gate(ht0, k, 0, gt0, cw)
        down(1, gt1, cw)

    @pl.when(k == pl.num_programs(1) - 1)
    def _():
        z = alpha * x_ref[...] + acc_ref[...].T
        o_ref[...] = _layer_norm(z, g_ref[...], b_ref[...])


def _peer_dense(x, u, vt, routing, g, b, alpha, tb=512, lc=128):
    t, d = x.shape
    e = u.shape[0]
    ec = DENSE_EC
    tb = min(tb, t)
    rows = ec // PEER_KEYS
    assert 2 * rows == F32_ROWS, "a pair of expert blocks must span one 8-row f32 tile of sub-key-1 rows"
    nk = e // (2 * ec)
    kt = PEER_KEYS // BF16_ROWS
    rank2, crow, e1z, e2 = routing
    rank2, e2 = (a.reshape(PEER_HEADS, kt, BF16_ROWS, t) for a in (rank2, e2))
    full = pl.BlockSpec((PEER_HEADS, kt, BF16_ROWS, tb), lambda i, k: (0, 0, 0, i))
    rowtab = pl.BlockSpec((PEER_HEADS, PEER_KEYS, tb), lambda i, k: (0, 0, i))
    fixed = pl.BlockSpec((1, d), lambda i, k: (0, 0))
    xspec = pl.BlockSpec((tb, d), lambda i, k: (i, 0))
    gt_scr = pltpu.VMEM((ec // BF16_ROWS, BF16_ROWS, tb), bf16)
    u_blk = lambda half: pl.BlockSpec((ec, d), lambda i, k: (2 * jnp.minimum(k, nk - 1) + half, 0))
    vt_blk = lambda half: pl.BlockSpec((1, d, ec), lambda i, k: (2 * jnp.maximum(k - 1, 0) + half, 0, 0))
    return pl.pallas_call(
        functools.partial(_peer_dense_kernel, rows=rows, tb=tb, lc=lc, mw=MXU_WIDTH, alpha=alpha),
        grid=(t // tb, nk + 1),
        in_specs=[xspec, u_blk(0), u_blk(1), vt_blk(0), vt_blk(1),
                  full, rowtab, rowtab, full, fixed, fixed],
        out_specs=xspec, out_shape=jax.ShapeDtypeStruct((t, d), f32),
        scratch_shapes=[pltpu.VMEM((d, tb), f32), pltpu.VMEM((d, tb), bf16),
                        pltpu.VMEM((ec, tb), f32), pltpu.VMEM((ec, tb), f32), gt_scr, gt_scr,
                        pltpu.VMEM((PEER_HEADS, kt, BF16_ROWS, tb), bf16),
                        pltpu.VMEM((PEER_HEADS, kt, BF16_ROWS, tb), bf16)],
        compiler_params=_params("parallel", "arbitrary"))(
            x, u, u, vt, vt, rank2, crow, e1z, e2, g.reshape(1, d), b.reshape(1, d))


def _swa_layer(x, batch, seq, w_in, sinks, w_out, bias, g, b, alpha):
    feat = (N_HEADS + 2 * SWA_KV_HEADS) * HEAD_DIM
    d_q = N_HEADS * HEAD_DIM
    kvw = SWA_KV_HEADS * HEAD_DIM
    proj = _mm(x, w_in.astype(bf16), bf16, 512, feat).reshape(batch, seq, feat)
    sink = jnp.repeat(sinks.astype(f32), HEAD_DIM).reshape(1, d_q)
    o = _band_attention(proj, bias, 1, feat, 0, d_q // kvw, d_q // kvw + 1, kvw,
                        N_HEADS // SWA_KV_HEADS, sink)
    return _proj_ln([o], [], [1], w_out.astype(bf16), x, g, b, alpha)


def _dil_layer(x, batch, seq, w_in, w_out, biases, g, b, alpha):
    d_q = N_HEADS * HEAD_DIM
    feat = 3 * d_q
    w_in = w_in.astype(bf16)
    outs, lses, dils = [], [], []
    for gi, (window, dil) in enumerate(DIL_PATTERNS):
        proj = _mm_dil(x, w_in[:, gi * feat:(gi + 1) * feat], dil).reshape(batch, seq // dil, dil * feat)
        o, lse = _band_attention(proj, biases[gi], dil, feat, 0, 1, 2, d_q, 1)
        outs.append(o)
        lses.append(lse)
        dils.append(dil)
    return _proj_ln(outs, lses, dils, w_out.astype(bf16), x, g, b, alpha)


def _mla_layer(x, batch, seq, w_in, q_norm, w_uq, kv_norm, w_ukv, w_out, g, b, alpha):
    q, k, v = _mla_prep(x, w_in, q_norm, w_uq, kv_norm, w_ukv, seq)
    o = _mla_attention(q, k, v, batch, seq)
    d = w_out.shape[1]
    w3 = w_out.reshape(N_HEADS, MLA_V, d)
    w_pad = jnp.concatenate([w3, jnp.zeros((N_HEADS, LANE - MLA_V, d), f32)], 1)
    return _proj_ln([o], [], [1], w_pad.reshape(N_HEADS * LANE, d).astype(bf16), x, g, b, alpha)


def _peer_layer(x, w_q, keys, u, v, g, b, alpha):
    routing = _peer_route(x, w_q.astype(bf16), keys.astype(bf16))
    e, d = v.shape
    vt = v.astype(bf16).reshape(e // DENSE_EC, DENSE_EC, d).transpose(0, 2, 1)
    return _peer_dense(x, u.astype(bf16), vt, routing, g, b, alpha)


def kernel(x, rel_bias, ln_g, ln_b, swa_w_in, swa_sinks, swa_w_out, dil_w_in, dil_w_out,
           mla_w_in, mla_q_norm, mla_w_uq, mla_kv_norm, mla_w_ukv, mla_w_out,
           peer_w_q, peer_keys, peer_u, peer_v):
    batch, seq, d = x.shape
    depth = ln_g.shape[0]
    alpha = (2 * depth) ** 0.25
    assert seq % (DIL_PATTERNS[-1][1] * BLOCK) == 0, "sequence must be a whole number of dilation segments"
    swa_bias = _band_bias(rel_bias, SWA_WINDOW - 1, 1)
    dil_bias = [_band_bias(rel_bias, window // dil, dil) for window, dil in DIL_PATTERNS]
    h = x.reshape(batch * seq, d)
    for i in range(depth):
        kind, j = i % 3, i // 3
        if kind == 0:
            h = _swa_layer(h, batch, seq, swa_w_in[j], swa_sinks[j], swa_w_out[j], swa_bias,
                           ln_g[i, 0], ln_b[i, 0], alpha)
        elif kind == 1:
            h = _dil_layer(h, batch, seq, dil_w_in[j], dil_w_out[j], dil_bias,
                           ln_g[i, 0], ln_b[i, 0], alpha)
        else:
            h = _mla_layer(h, batch, seq, mla_w_in[j], mla_q_norm[j], mla_w_uq[j], mla_kv_norm[j],
                           mla_w_ukv[j], mla_w_out[j], ln_g[i, 0], ln_b[i, 0], alpha)
        h = _peer_layer(h, peer_w_q[i], peer_keys[i], peer_u[i], peer_v[i], ln_g[i, 1], ln_b[i, 1], alpha)
    return h.reshape(batch, seq, d)
```

```python
import functools
import math

import jax
import jax.numpy as jnp
from jax import lax
from jax.experimental import pallas as pl
from jax.experimental.pallas import tpu as pltpu

f32 = jnp.float32
bf16 = jnp.bfloat16

N_HEADS = 16
HEAD_DIM = 64
BLOCK = 128
SWA_KV_HEADS = 2
SWA_WINDOW = 128
DIL_PATTERNS = ((128, 1), (512, 4), (2048, 16))
MLA_Q_RANK = 256
MLA_KV_RANK = 128
MLA_NOPE = 64
MLA_ROPE = 32
MLA_V = 64
ROPE_THETA = 10000.0
REL_BUCKETS = 32
REL_MAX_DIST = 2048
PEER_HEADS = 8
PEER_KEYS = 128
PEER_DKEY = 256
PEER_TOPK = 16
LN_EPS = 1e-5
RMS_EPS = 1e-6
NEG = -1e30

LANE = 128
BF16_ROWS = 16
F32_ROWS = 8
MXU_WIDTH = 256
HEAD_BATCH = 16
DENSE_EC = 4 * PEER_KEYS
VMEM_LIMIT = 56 * 1024 * 1024

_NT = (((1,), (1,)), ((), ()))


def _params(*sem):
    return pltpu.CompilerParams(dimension_semantics=sem, vmem_limit_bytes=VMEM_LIMIT)


def _mm_kernel(x_ref, w_ref, o_ref):
    o_ref[...] = jnp.dot(x_ref[...].astype(bf16), w_ref[...],
                         preferred_element_type=f32).astype(o_ref.dtype)


def _mm(x, w, out_dtype, tm, tn):
    m, k = x.shape
    n = w.shape[1]
    tm, tn = min(tm, m), min(tn, n)
    return pl.pallas_call(
        _mm_kernel, grid=(m // tm, n // tn),
        in_specs=[pl.BlockSpec((tm, k), lambda i, j: (i, 0)),
                  pl.BlockSpec((k, tn), lambda i, j: (0, j))],
        out_specs=pl.BlockSpec((tm, tn), lambda i, j: (i, j)),
        out_shape=jax.ShapeDtypeStruct((m, n), out_dtype),
        compiler_params=_params("parallel", "parallel"))(x, w)


def _rel_bucket(dist):
    max_exact = REL_BUCKETS // 2
    n = jnp.maximum(dist, 0)
    nf = jnp.maximum(n, 1).astype(f32)
    large = max_exact + (jnp.log(nf / max_exact) / math.log(REL_MAX_DIST / max_exact)
                         * (REL_BUCKETS - max_exact)).astype(jnp.int32)
    large = jnp.minimum(large, REL_BUCKETS - 1)
    return jnp.where(n < max_exact, n, large)


def _bias_kernel(rel_ref, bucket_ref, valid_ref, o_ref):
    bucket = bucket_ref[...]
    for h in range(N_HEADS):
        acc = jnp.zeros(bucket.shape, f32)
        for b in range(REL_BUCKETS):
            acc = jnp.where(bucket == b, rel_ref[b, h], acc)
        for variant in range(2):
            o_ref[variant, h] = jnp.where(valid_ref[variant] != 0, acc, NEG)


def _band_bias(rel_bias, max_dist, dilation):
    qi = jnp.arange(BLOCK)[:, None]
    kj = jnp.arange(2 * BLOCK)[None, :]
    dist = BLOCK + qi - kj
    bucket = _rel_bucket(dist * dilation).astype(jnp.int32)
    valid = (dist >= 0) & (dist <= max_dist)
    valid = jnp.stack([valid & (kj >= BLOCK), valid]).astype(jnp.int32)
    return pl.pallas_call(
        _bias_kernel,
        in_specs=[pl.BlockSpec(memory_space=pltpu.SMEM),
                  pl.BlockSpec(memory_space=pltpu.VMEM),
                  pl.BlockSpec(memory_space=pltpu.VMEM)],
        out_specs=pl.BlockSpec(memory_space=pltpu.VMEM),
        out_shape=jax.ShapeDtypeStruct((2, N_HEADS, BLOCK, 2 * BLOCK), f32))(rel_bias, bucket, valid)


def _band_kernel(q_ref, kp_ref, kc_ref, vp_ref, vc_ref, bias_ref, *rest, group, with_sink):
    if with_sink:
        sink_ref, o_ref = rest
    else:
        o_ref, lse_ref = rest
    scale = HEAD_DIM ** -0.5
    for h0 in range(0, N_HEADS, HEAD_BATCH):
        heads = range(h0, h0 + HEAD_BATCH)
        qs = [slice(h * HEAD_DIM, (h + 1) * HEAD_DIM) for h in heads]
        ks = [slice((h // group) * HEAD_DIM, (h // group + 1) * HEAD_DIM) for h in heads]
        kk = [jnp.concatenate([kp_ref[0, :, c], kc_ref[0, :, c]], axis=0) for c in ks]
        vv = [jnp.concatenate([vp_ref[0, :, c], vc_ref[0, :, c]], axis=0) for c in ks]
        ss = [lax.dot_general(q_ref[0, :, c], k, _NT, preferred_element_type=f32) * scale + bias_ref[0, h]
              for h, c, k in zip(heads, qs, kk)]
        ms = [s.max(-1, keepdims=True) for s in ss]
        ps = [jnp.exp(s - m) for s, m in zip(ss, ms)]
        ls = [p.sum(-1, keepdims=True) for p in ps]
        os_ = [jnp.dot(p.astype(bf16), v, preferred_element_type=f32) / l for p, v, l in zip(ps, vv, ls)]
        lses = [m + jnp.log(l) for m, l in zip(ms, ls)]
        for c, o, lse in zip(qs, os_, lses):
            if with_sink:
                o = o * jax.nn.sigmoid(lse - sink_ref[:, c])
            else:
                lse_ref[0, :, c] = jnp.broadcast_to(lse, (BLOCK, HEAD_DIM))
            o_ref[0, :, c] = o.astype(o_ref.dtype)


def _band_attention(view, bias, dil, feat, q_col, k_col, v_col, kv_width, group, sink=None):
    b, m_len, _ = view.shape
    d_q = N_HEADS * HEAD_DIM
    nb = m_len // BLOCK
    qpr, kpr = feat // d_q, feat // kv_width

    def cur(col, per_row):
        return lambda bi, r, n: (bi, n, r * per_row + col)

    def prev(col, per_row):
        return lambda bi, r, n: (bi, jnp.maximum(n - 1, 0), r * per_row + col)

    in_specs = [pl.BlockSpec((1, BLOCK, d_q), cur(q_col, qpr)),
                pl.BlockSpec((1, BLOCK, kv_width), prev(k_col, kpr)),
                pl.BlockSpec((1, BLOCK, kv_width), cur(k_col, kpr)),
                pl.BlockSpec((1, BLOCK, kv_width), prev(v_col, kpr)),
                pl.BlockSpec((1, BLOCK, kv_width), cur(v_col, kpr)),
                pl.BlockSpec((1, N_HEADS, BLOCK, 2 * BLOCK),
                             lambda bi, r, n: (jnp.minimum(n, 1), 0, 0, 0))]
    args = [view, view, view, view, view, bias]
    o_spec = pl.BlockSpec((1, BLOCK, d_q), lambda bi, r, n: (bi, n, r))
    o_shape = jax.ShapeDtypeStruct((b, m_len, dil * d_q), bf16)
    if sink is not None:
        in_specs.append(pl.BlockSpec((1, d_q), lambda bi, r, n: (0, 0)))
        args.append(sink)
        out_specs, out_shape = o_spec, o_shape
    else:
        out_specs = [o_spec, o_spec]
        out_shape = [o_shape, jax.ShapeDtypeStruct((b, m_len, dil * d_q), f32)]
    out = pl.pallas_call(
        functools.partial(_band_kernel, group=group, with_sink=sink is not None),
        grid=(b, dil, nb), in_specs=in_specs, out_specs=out_specs, out_shape=out_shape,
        compiler_params=_params("parallel", "parallel", "arbitrary"))(*args)
    if sink is not None:
        return out.reshape(b * m_len, dil * d_q)
    return out[0].reshape(b * m_len, dil * d_q), out[1].reshape(b * m_len, dil * d_q)


def _layer_norm(z, g, b):
    mu = z.mean(-1, keepdims=True)
    zc = z - mu
    var = jnp.square(zc).mean(-1, keepdims=True)
    return zc * lax.rsqrt(var + LN_EPS) * g + b


def _proj_ln_kernel(*refs, dils, alpha):
    n_pat = len(dils)
    o_refs = refs[:n_pat]
    lse_refs = refs[n_pat:2 * n_pat] if n_pat > 1 else ()
    n_in = len(o_refs) + len(lse_refs)
    w_ref, x_ref, g_ref, b_ref, out_ref = refs[n_in:n_in + 5]
    scratch = refs[n_in + 5:]
    if n_pat == 1:
        o = o_refs[0][...]
    else:
        kw = w_ref.shape[0]
        mix_ref = scratch[-1]

        def natural(ref, dil, scr, c):
            if dil == 1:
                return ref[:, c * LANE:(c + 1) * LANE].astype(f32)
            rows = ref.shape[0]
            for r in range(dil):
                scr[c, pl.ds(r, rows, stride=dil), :] = (
                    ref[:, r * kw + c * LANE:r * kw + (c + 1) * LANE].astype(f32))
            return scr[c]

        for c in range(kw // LANE):
            lses = [natural(r, dl, scratch[2 * i], c) for i, (r, dl) in enumerate(zip(lse_refs, dils))]
            mx = functools.reduce(jnp.maximum, lses)
            es = [jnp.exp(l - mx) for l in lses]
            num = sum(e * natural(r, dl, scratch[2 * i + 1], c)
                      for i, (e, r, dl) in enumerate(zip(es, o_refs, dils)))
            mix_ref[:, c * LANE:(c + 1) * LANE] = (num / sum(es)).astype(bf16)
        o = mix_ref[...]
    y = jnp.dot(o, w_ref[...], preferred_element_type=f32)
    out_ref[...] = _layer_norm(alpha * x_ref[...] + y, g_ref[...], b_ref[...])


def _proj_ln(os_, lses, dils, w_out, x, g, b, alpha, tm=512):
    t, d = x.shape
    tm = min(tm, t)
    kw = w_out.shape[0]
    row = lambda width: pl.BlockSpec((tm, width), lambda i: (i, 0))
    dilated = lambda dl: pl.BlockSpec((tm // dl, dl * kw), lambda i: (i, 0))
    fixed = lambda shape: pl.BlockSpec(shape, lambda i: (0, 0))
    in_specs = ([dilated(dl) for dl in dils] + [dilated(dl) for dl in dils[:len(lses)]]
                + [fixed((kw, d)), row(d), fixed((1, d)), fixed((1, d))])
    scratch = [pltpu.VMEM((kw // LANE, tm, LANE), f32)] * (2 * len(lses))
    if lses:
        scratch.append(pltpu.VMEM((tm, kw), bf16))
    return pl.pallas_call(
        functools.partial(_proj_ln_kernel, dils=tuple(dils), alpha=alpha),
        grid=(t // tm,), in_specs=in_specs, out_specs=row(d),
        out_shape=jax.ShapeDtypeStruct((t, d), f32), scratch_shapes=scratch,
        compiler_params=_params("parallel"))(*os_, *lses, w_out, x, g.reshape(1, d), b.reshape(1, d))


def _mm_dil_kernel(x_ref, w_ref, o_ref, res_ref, *, dil):
    n = w_ref.shape[1]
    res = jnp.dot(x_ref[...].astype(bf16), w_ref[...], preferred_element_type=f32)
    if dil == 1:
        o_ref[...] = res.astype(o_ref.dtype)
        return
    rows = o_ref.shape[0]
    for c in range(n // LANE):
        res_ref[c] = res[:, c * LANE:(c + 1) * LANE]
        for r in range(dil):
            o_ref[:, r * n + c * LANE:r * n + (c + 1) * LANE] = (
                res_ref[c, pl.ds(r, rows, stride=dil), :].astype(o_ref.dtype))


def _mm_dil(x, w, dil, tm=512):
    t, k = x.shape
    n = w.shape[1]
    tm = min(tm, t)
    return pl.pallas_call(
        functools.partial(_mm_dil_kernel, dil=dil), grid=(t // tm,),
        in_specs=[pl.BlockSpec((tm, k), lambda i: (i, 0)), pl.BlockSpec((k, n), lambda i: (0, 0))],
        out_specs=pl.BlockSpec((tm // dil, dil * n), lambda i: (i, 0)),
        out_shape=jax.ShapeDtypeStruct((t // dil, dil * n), bf16),
        scratch_shapes=[pltpu.VMEM((n // LANE, tm, LANE), f32)],
        compiler_params=_params("parallel"))(x, w)


def _rms(c, g):
    return c * lax.rsqrt(jnp.square(c).mean(-1, keepdims=True) + RMS_EPS) * g


def _mla_prep_kernel(x_ref, win_ref, qn_ref, wq_ref, kvn_ref, wkv_ref, cos_ref, sin_ref,
                     q_ref, k_ref, v_ref):
    hw = N_HEADS * LANE
    scale = (MLA_NOPE + MLA_ROPE) ** -0.5
    cos, sin = cos_ref[...], sin_ref[...]
    xw = jnp.dot(x_ref[...].astype(bf16), win_ref[...], preferred_element_type=f32)
    cq = _rms(xw[:, :MLA_Q_RANK], qn_ref[...]).astype(bf16)
    ckv = _rms(xw[:, MLA_Q_RANK:MLA_Q_RANK + MLA_KV_RANK], kvn_ref[...]).astype(bf16)
    off = MLA_Q_RANK + MLA_KV_RANK
    kr = xw[:, off:off + LANE] * cos + xw[:, off + LANE:off + 2 * LANE] * sin
    qq = jnp.dot(cq, wq_ref[...], preferred_element_type=f32)
    kv = jnp.dot(ckv, wkv_ref[...], preferred_element_type=f32)
    for h in range(N_HEADS):
        blk = slice(h * LANE, (h + 1) * LANE)
        swp = slice(hw + h * LANE, hw + (h + 1) * LANE)
        q_ref[:, blk] = ((qq[:, blk] * cos + qq[:, swp] * sin) * scale).astype(bf16)
        k_ref[:, blk] = (kv[:, blk] + kr).astype(bf16)
    ones_lane = lax.broadcasted_iota(jnp.int32, (1, hw), 1) % LANE == MLA_V
    v_ref[...] = jnp.where(ones_lane, 1.0, kv[:, hw:]).astype(bf16)


def _mla_prep(x, w_in, q_norm, w_uq, kv_norm, w_ukv, seq, tm=512):
    t, d = x.shape
    tm = min(tm, seq)
    hw = N_HEADS * LANE
    half = MLA_ROPE // 2
    dq = MLA_NOPE + MLA_ROPE
    kr_w = w_in[:, MLA_Q_RANK + MLA_KV_RANK:]
    zeros = lambda n: jnp.zeros((d, n), f32)
    kr_a = jnp.concatenate([zeros(MLA_NOPE), kr_w, zeros(LANE - dq)], 1)
    kr_b = jnp.concatenate([zeros(MLA_NOPE), kr_w[:, half:], kr_w[:, :half], zeros(LANE - dq)], 1)
    win = jnp.concatenate([w_in[:, :MLA_Q_RANK + MLA_KV_RANK], kr_a, kr_b], 1).astype(bf16)
    wq3 = w_uq.reshape(MLA_Q_RANK, N_HEADS, dq)
    zq = lambda n: jnp.zeros((MLA_Q_RANK, N_HEADS, n), f32)
    wq_a = jnp.concatenate([wq3, zq(LANE - dq)], 2)
    wq_b = jnp.concatenate([zq(MLA_NOPE), wq3[:, :, MLA_NOPE + half:], wq3[:, :, MLA_NOPE:MLA_NOPE + half],
                            zq(LANE - dq)], 2)
    wq = jnp.concatenate([wq_a.reshape(MLA_Q_RANK, hw), wq_b.reshape(MLA_Q_RANK, hw)], 1).astype(bf16)
    wkv3 = w_ukv.reshape(MLA_KV_RANK, N_HEADS, MLA_NOPE + MLA_V)
    zk = lambda n: jnp.zeros((MLA_KV_RANK, N_HEADS, n), f32)
    wk = jnp.concatenate([wkv3[:, :, :MLA_NOPE], zk(LANE - MLA_NOPE)], 2)
    wv = jnp.concatenate([wkv3[:, :, MLA_NOPE:], zk(LANE - MLA_V)], 2)
    wkv = jnp.concatenate([wk.reshape(MLA_KV_RANK, hw), wv.reshape(MLA_KV_RANK, hw)], 1).astype(bf16)
    freq = ROPE_THETA ** (-jnp.arange(half, dtype=f32) / half)
    ang = jnp.arange(seq)[:, None].astype(f32) * freq[None, :]
    c, s_ = jnp.cos(ang), jnp.sin(ang)
    cos_t = jnp.concatenate([jnp.ones((seq, MLA_NOPE), f32), c, c, jnp.ones((seq, LANE - dq), f32)], 1)
    sin_t = jnp.concatenate([jnp.zeros((seq, MLA_NOPE), f32), -s_, s_, jnp.zeros((seq, LANE - dq), f32)], 1)

    n_pos = seq // tm
    row = lambda width: pl.BlockSpec((tm, width), lambda i: (i, 0))
    fixed = lambda a: pl.BlockSpec(a.shape, lambda i: (0, 0))
    pos = pl.BlockSpec((tm, LANE), lambda i: (i % n_pos, 0))
    qn, kvn = q_norm.reshape(1, -1), kv_norm.reshape(1, -1)
    out_sd = jax.ShapeDtypeStruct((t, hw), bf16)
    return pl.pallas_call(
        _mla_prep_kernel, grid=(t // tm,),
        in_specs=[row(d), fixed(win), fixed(qn), fixed(wq), fixed(kvn), fixed(wkv), pos, pos],
        out_specs=[row(hw)] * 3, out_shape=[out_sd] * 3,
        compiler_params=_params("parallel"))(x, win, qn, wq, kvn, wkv, cos_t, sin_t)


def _mla_attn_kernel(q_ref, k_ref, v_ref, o_ref, *, tq, heads):
    qi = pl.program_id(2)
    lanes = [slice(j * LANE, (j + 1) * LANE) for j in range(heads)]
    qs = [q_ref[0, :, ln] for ln in lanes]

    def update(carry, start, mask):
        ss = [lax.dot_general(q, k_ref[0, pl.ds(start, tq), ln], _NT, preferred_element_type=f32)
              for q, ln in zip(qs, lanes)]
        if mask is not None:
            ss = [jnp.where(mask, s, NEG) for s in ss]
        ms = [jnp.maximum(m, s.max(-1, keepdims=True)) for (m, _), s in zip(carry, ss)]
        ps = [jnp.exp(s - m).astype(bf16) for s, m in zip(ss, ms)]
        pvs = [jnp.dot(p, v_ref[0, pl.ds(start, tq), ln], preferred_element_type=f32)
               for p, ln in zip(ps, lanes)]
        return tuple((m_new, jnp.exp(m - m_new) * acc + pv)
                     for (m, acc), m_new, pv in zip(carry, ms, pvs))

    init = tuple((jnp.full((tq, 1), NEG, f32), jnp.zeros((tq, LANE), f32)) for _ in range(heads))
    carry = lax.fori_loop(0, qi, lambda ki, c: update(c, pl.multiple_of(ki * tq, tq), None), init)
    causal = lax.broadcasted_iota(jnp.int32, (tq, tq), 1) <= lax.broadcasted_iota(jnp.int32, (tq, tq), 0)
    carry = update(carry, pl.multiple_of(qi * tq, tq), causal)
    for j in range(heads):
        acc = carry[j][1]
        o_ref[0, :, lanes[j]] = (acc / acc[:, MLA_V:MLA_V + 1]).astype(o_ref.dtype)


def _mla_attention(q, k, v, batch, seq, tq=512, heads=2):
    hw = N_HEADS * LANE
    tq = min(tq, seq)
    q, k, v = (a.reshape(batch, seq, hw) for a in (q, k, v))
    qspec = pl.BlockSpec((1, tq, heads * LANE), lambda b, h, i: (b, i, h))
    kspec = pl.BlockSpec((1, seq, heads * LANE), lambda b, h, i: (b, 0, h))
    out = pl.pallas_call(
        functools.partial(_mla_attn_kernel, tq=tq, heads=heads),
        grid=(batch, N_HEADS // heads, seq // tq),
        in_specs=[qspec, kspec, kspec], out_specs=qspec,
        out_shape=jax.ShapeDtypeStruct((batch, seq, hw), bf16),
        compiler_params=_params("parallel", "parallel", "arbitrary"))(q, k, v)
    return out.reshape(batch * seq, hw)


def _top16_pair(s1, s2):
    w1, w2 = s1, s2
    rank2 = jnp.full(s2.shape, 99.0, f32)
    v1, v2 = [], []
    for k in range(PEER_TOPK):
        m1 = w1.max(axis=0, keepdims=True)
        m2 = w2.max(axis=0, keepdims=True)
        h1 = w1 == m1
        h2 = w2 == m2
        rank2 = jnp.where(h2, float(k), rank2)
        w1 = jnp.where(h1, -jnp.inf, w1)
        w2 = jnp.where(h2, -jnp.inf, w2)
        v1.append(m1)
        v2.append(m2)
    return v1, v2, rank2


def _stack_rows(rows):
    n = len(rows)
    rid = lax.broadcasted_iota(jnp.int32, (n, rows[0].shape[1]), 0)
    out = jnp.broadcast_to(rows[0], (n, rows[0].shape[1]))
    for i in range(1, n):
        out = jnp.where(rid == i, rows[i], out)
    return out


def _route_kernel(x_ref, wq_ref, keys_ref, rank2_ref, crow_ref, e1z_ref, e2_ref, *, tm):
    q = jnp.dot(x_ref[...].astype(bf16), wq_ref[...], preferred_element_type=f32).astype(bf16)
    half = PEER_DKEY // 2
    s1_all = lax.dot_general(keys_ref[0, 0], q[:, :half], _NT, preferred_element_type=f32)
    s2_all = lax.dot_general(keys_ref[0, 1], q[:, half:], _NT, preferred_element_type=f32)
    rid8 = lax.broadcasted_iota(jnp.int32, (8, LANE), 0)
    for c in range(tm // LANE):
        lanes = slice(c * LANE, (c + 1) * LANE)
        s1, s2 = s1_all[:, lanes], s2_all[:, lanes]
        v1, v2, rank2 = _top16_pair(s1, s2)
        v2_all = _stack_rows(v2)
        cands = [v1[0] + v2_all]
        for a in range(1, 8):
            cands.append(jnp.where(rid8 < PEER_TOPK // (a + 1), v1[a] + v2_all[:8], -jnp.inf))
        cands.append(_stack_rows(v1[8:]) + v2[0])
        cand = jnp.concatenate(cands, axis=0)
        work = cand
        for _ in range(PEER_TOPK):
            tau = work.max(axis=0, keepdims=True)
            work = jnp.where(work == tau, -jnp.inf, work)
        cmax = v1[0] + v2[0]
        z = jnp.where(cand >= tau, jnp.exp(cand - cmax), 0.0).sum(axis=0, keepdims=True)
        crow = jnp.zeros(s1.shape, f32)
        for b in range(PEER_TOPK):
            crow = crow + jnp.where(s1 + v2[b] >= tau, 1.0, 0.0)
        rank2_ref[0, :, lanes] = rank2.astype(bf16)
        crow_ref[0, :, lanes] = crow
        e1z_ref[0, :, lanes] = jnp.exp(s1 - v1[0]) / z
        e2_ref[0, :, lanes] = jnp.exp(s2 - v2[0]).astype(bf16)


def _peer_route(x, w_q, keys, tm=512):
    t, d = x.shape
    tm = min(tm, t)
    spec = pl.BlockSpec((1, PEER_KEYS, tm), lambda i, h: (h, 0, i))
    sd = lambda dt: jax.ShapeDtypeStruct((PEER_HEADS, PEER_KEYS, t), dt)
    return pl.pallas_call(
        functools.partial(_route_kernel, tm=tm), grid=(t // tm, PEER_HEADS),
        in_specs=[pl.BlockSpec((tm, d), lambda i, h: (i, 0)),
                  pl.BlockSpec((d, PEER_DKEY), lambda i, h: (0, h)),
                  pl.BlockSpec((1, 2, PEER_KEYS, PEER_DKEY // 2), lambda i, h: (h, 0, 0, 0))],
        out_specs=[spec] * 4, out_shape=[sd(bf16), sd(f32), sd(f32), sd(bf16)],
        compiler_params=_params("parallel", "arbitrary"))(x, w_q, keys)


def _peer_dense_kernel(x_ref, u0_ref, u1_ref, vt0_ref, vt1_ref, rank2_in, crow_ref, e1z_ref, e2_in,
                       g_ref, b_ref, o_ref, acc_ref, xt_ref, ht0, ht1, gt0, gt1, rank2_ref, e2_ref,
                       *, rows, tb, lc, mw, alpha):
    k = pl.program_id(1)
    ec = rows * PEER_KEYS

    @pl.when(k == 0)
    def _():
        acc_ref[...] = jnp.zeros_like(acc_ref)
        ht1[...] = jnp.zeros_like(ht1)
        gt0[...] = jnp.zeros_like(gt0)
        xt_ref[...] = x_ref[...].T.astype(bf16)
        rank2_ref[...] = rank2_in[...]
        e2_ref[...] = e2_in[...]

    zero = jnp.zeros((), bf16)
    kt = PEER_KEYS // BF16_ROWS

    def up(half, ht, cw):
        lanes = slice(cw * mw, (cw + 1) * mw)
        ht[:, lanes] = jnp.dot((u0_ref, u1_ref)[half][...], xt_ref[:, lanes], preferred_element_type=f32)

    n_tiles = pl.num_programs(1) - 1

    def gate(ht, tile, half, gt, cw):
        base = pl.multiple_of(jnp.clip(tile, 0, n_tiles - 1) * F32_ROWS, F32_ROWS)
        for r in range(rows):
            keys = slice(r * PEER_KEYS, (r + 1) * PEER_KEYS)
            row = slice(half * rows + r, half * rows + r + 1)
            for c in range(cw * mw // lc, (cw + 1) * mw // lc):
                lanes = slice(c * lc, (c + 1) * lc)
                w = None
                for h in range(PEER_HEADS):
                    cr = crow_ref[h, pl.ds(base, F32_ROWS), lanes][row]
                    ez = e1z_ref[h, pl.ds(base, F32_ROWS), lanes][row]
                    cr = jnp.broadcast_to(cr, (BF16_ROWS, lc)).astype(bf16)[None]
                    ez = jnp.broadcast_to(ez, (BF16_ROWS, lc)).astype(bf16)[None]
                    term = jnp.where(rank2_ref[h, :, :, lanes] < cr, e2_ref[h, :, :, lanes] * ez, zero)
                    w = term if w is None else w + term
                hv = ht[keys, lanes]
                gelu = 0.5 * hv * (1.0 + lax.erf(hv * (2.0 ** -0.5)))
                gt[r * kt:(r + 1) * kt, :, lanes] = gelu.astype(bf16).reshape(kt, BF16_ROWS, lc) * w

    def down(half, gt, cw):
        lanes = slice(cw * mw, (cw + 1) * mw)
        acc_ref[:, lanes] += jnp.dot((vt0_ref, vt1_ref)[half][0], gt[:, :, lanes].reshape(ec, mw),
                                     preferred_element_type=f32)

    for cw in range(tb // mw):
        up(0, ht0, cw)
        gate(ht1, k - 1, 1, gt1, cw)
        down(0, gt0, cw)
    for cw in range(tb // mw):
        up(1, ht1, cw)
        gate(ht0, k, 0, gt0, cw)
        down(1, gt1, cw)

    @pl.when(k == pl.num_programs(1) - 1)
    def _():
        z = alpha * x_ref[...] + acc_ref[...].T
        o_ref[...] = _layer_norm(z, g_ref[...], b_ref[...])


def _peer_dense(x, u, vt, routing, g, b, alpha, tb=1024, lc=128):
    t, d = x.shape
    e = u.shape[0]
    ec = DENSE_EC
    tb = min(tb, t)
    rows = ec // PEER_KEYS
    assert 2 * rows == F32_ROWS, "a pair of expert blocks must span one 8-row f32 tile of sub-key-1 rows"
    nk = e // (2 * ec)
    kt = PEER_KEYS // BF16_ROWS
    rank2, crow, e1z, e2 = routing
    rank2, e2 = (a.reshape(PEER_HEADS, kt, BF16_ROWS, t) for a in (rank2, e2))
    once = pl.Buffered(1)
    full = pl.BlockSpec((PEER_HEADS, kt, BF16_ROWS, tb), lambda i, k: (0, 0, 0, i), pipeline_mode=once)
    rowtab = pl.BlockSpec((PEER_HEADS, PEER_KEYS, tb), lambda i, k: (0, 0, i), pipeline_mode=once)
    fixed = pl.BlockSpec((1, d), lambda i, k: (0, 0))
    xin = pl.BlockSpec((tb, d), lambda i, k: (i, 0), pipeline_mode=once)
    xspec = pl.BlockSpec((tb, d), lambda i, k: (i, 0))
    gt_scr = pltpu.VMEM((ec // BF16_ROWS, BF16_ROWS, tb), bf16)
    u_blk = lambda half: pl.BlockSpec((ec, d), lambda i, k: (2 * jnp.minimum(k, nk - 1) + half, 0))
    vt_blk = lambda half: pl.BlockSpec((1, d, ec), lambda i, k: (2 * jnp.maximum(k - 1, 0) + half, 0, 0))
    return pl.pallas_call(
        functools.partial(_peer_dense_kernel, rows=rows, tb=tb, lc=lc, mw=MXU_WIDTH, alpha=alpha),
        grid=(t // tb, nk + 1),
        in_specs=[xin, u_blk(0), u_blk(1), vt_blk(0), vt_blk(1),
                  full, rowtab, rowtab, full, fixed, fixed],
        out_specs=xspec, out_shape=jax.ShapeDtypeStruct((t, d), f32),
        scratch_shapes=[pltpu.VMEM((d, tb), f32), pltpu.VMEM((d, tb), bf16),
                        pltpu.VMEM((ec, tb), f32), pltpu.VMEM((ec, tb), f32), gt_scr, gt_scr,
                        pltpu.VMEM((PEER_HEADS, kt, BF16_ROWS, tb), bf16),
                        pltpu.VMEM((PEER_HEADS, kt, BF16_ROWS, tb), bf16)],
        compiler_params=_params("parallel", "arbitrary"))(
            x, u, u, vt, vt, rank2, crow, e1z, e2, g.reshape(1, d), b.reshape(1, d))


def _swa_layer(x, batch, seq, w_in, sinks, w_out, bias, g, b, alpha):
    feat = (N_HEADS + 2 * SWA_KV_HEADS) * HEAD_DIM
    d_q = N_HEADS * HEAD_DIM
    kvw = SWA_KV_HEADS * HEAD_DIM
    proj = _mm(x, w_in.astype(bf16), bf16, 512, feat).reshape(batch, seq, feat)
    sink = jnp.repeat(sinks.astype(f32), HEAD_DIM).reshape(1, d_q)
    o = _band_attention(proj, bias, 1, feat, 0, d_q // kvw, d_q // kvw + 1, kvw,
                        N_HEADS // SWA_KV_HEADS, sink)
    return _proj_ln([o], [], [1], w_out.astype(bf16), x, g, b, alpha)


def _dil_layer(x, batch, seq, w_in, w_out, biases, g, b, alpha):
    d_q = N_HEADS * HEAD_DIM
    feat = 3 * d_q
    w_in = w_in.astype(bf16)
    outs, lses, dils = [], [], []
    for gi, (window, dil) in enumerate(DIL_PATTERNS):
        proj = _mm_dil(x, w_in[:, gi * feat:(gi + 1) * feat], dil).reshape(batch, seq // dil, dil * feat)
        o, lse = _band_attention(proj, biases[gi], dil, feat, 0, 1, 2, d_q, 1)
        outs.append(o)
        lses.append(lse)
        dils.append(dil)
    return _proj_ln(outs, lses, dils, w_out.astype(bf16), x, g, b, alpha)


def _mla_layer(x, batch, seq, w_in, q_norm, w_uq, kv_norm, w_ukv, w_out, g, b, alpha):
    q, k, v = _mla_prep(x, w_in, q_norm, w_uq, kv_norm, w_ukv, seq)
    o = _mla_attention(q, k, v, batch, seq)
    d = w_out.shape[1]
    w3 = w_out.reshape(N_HEADS, MLA_V, d)
    w_pad = jnp.concatenate([w3, jnp.zeros((N_HEADS, LANE - MLA_V, d), f32)], 1)
    return _proj_ln([o], [], [1], w_pad.reshape(N_HEADS * LANE, d).astype(bf16), x, g, b, alpha)


def _peer_layer(x, w_q, keys, u, v, g, b, alpha):
    routing = _peer_route(x, w_q.astype(bf16), keys.astype(bf16))
    e, d = v.shape
    vt = v.astype(bf16).reshape(e // DENSE_EC, DENSE_EC, d).transpose(0, 2, 1)
    return _peer_dense(x, u.astype(bf16), vt, routing, g, b, alpha)


def kernel(x, rel_bias, ln_g, ln_b, swa_w_in, swa_sinks, swa_w_out, dil_w_in, dil_w_out,
           mla_w_in, mla_q_norm, mla_w_uq, mla_kv_norm, mla_w_ukv, mla_w_out,
           peer_w_q, peer_keys, peer_u, peer_v):
    batch, seq, d = x.shape
    depth = ln_g.shape[0]
    alpha = (2 * depth) ** 0.25
    assert seq % (DIL_PATTERNS[-1][1] * BLOCK) == 0, "sequence must be a whole number of dilation segments"
    swa_bias = _band_bias(rel_bias, SWA_WINDOW - 1, 1)
    dil_bias = [_band_bias(rel_bias, window // dil, dil) for window, dil in DIL_PATTERNS]
    h = x.reshape(batch * seq, d)
    for i in range(depth):
        kind, j = i % 3, i // 3
        if kind == 0:
            h = _swa_layer(h, batch, seq, swa_w_in[j], swa_sinks[j], swa_w_out[j], swa_bias,
                           ln_g[i, 0], ln_b[i, 0], alpha)
        elif kind == 1:
            h = _dil_layer(h, batch, seq, dil_w_in[j], dil_w_out[j], dil_bias,
                           ln_g[i, 0], ln_b[i, 0], alpha)
        else:
            h = _mla_layer(h, batch, seq, mla_w_in[j], mla_q_norm[j], mla_w_uq[j], mla_kv_norm[j],
                           mla_w_ukv[j], mla_w_out[j], ln_g[i, 0], ln_b[i, 0], alpha)
        h = _peer_layer(h, peer_w_q[i], peer_keys[i], peer_u[i], peer_v[i], ln_g[i, 1], ln_b[i, 1], alpha)
    return h.reshape(batch, seq, d)
```

```python
import functools
import math

import jax
import jax.numpy as jnp
from jax import lax
from jax.experimental import pallas as pl
from jax.experimental.pallas import tpu as pltpu

f32 = jnp.float32
bf16 = jnp.bfloat16

N_HEADS = 16
HEAD_DIM = 64
BLOCK = 128
SWA_KV_HEADS = 2
SWA_WINDOW = 128
DIL_PATTERNS = ((128, 1), (512, 4), (2048, 16))
MLA_Q_RANK = 256
MLA_KV_RANK = 128
MLA_NOPE = 64
MLA_ROPE = 32
MLA_V = 64
ROPE_THETA = 10000.0
REL_BUCKETS = 32
REL_MAX_DIST = 2048
PEER_HEADS = 8
PEER_KEYS = 128
PEER_DKEY = 256
PEER_TOPK = 16
LN_EPS = 1e-5
RMS_EPS = 1e-6
NEG = -1e30

LANE = 128
BF16_ROWS = 16
F32_ROWS = 8
MXU_WIDTH = 256
MLA_GROUP = 2
ROUTE_TOKENS = 512
HEAD_BATCH = 16
DENSE_EC = 4 * PEER_KEYS
VMEM_LIMIT = 56 * 1024 * 1024

_NT = (((1,), (1,)), ((), ()))


def _params(*sem):
    return pltpu.CompilerParams(dimension_semantics=sem, vmem_limit_bytes=VMEM_LIMIT)


def _mm_kernel(x_ref, w_ref, o_ref):
    o_ref[...] = jnp.dot(x_ref[...].astype(bf16), w_ref[...],
                         preferred_element_type=f32).astype(o_ref.dtype)


def _mm(x, w, out_dtype, tm, tn):
    m, k = x.shape
    n = w.shape[1]
    tm, tn = min(tm, m), min(tn, n)
    return pl.pallas_call(
        _mm_kernel, grid=(m // tm, n // tn),
        in_specs=[pl.BlockSpec((tm, k), lambda i, j: (i, 0)),
                  pl.BlockSpec((k, tn), lambda i, j: (0, j))],
        out_specs=pl.BlockSpec((tm, tn), lambda i, j: (i, j)),
        out_shape=jax.ShapeDtypeStruct((m, n), out_dtype),
        compiler_params=_params("parallel", "parallel"))(x, w)


def _rel_bucket(dist):
    max_exact = REL_BUCKETS // 2
    n = jnp.maximum(dist, 0)
    nf = jnp.maximum(n, 1).astype(f32)
    large = max_exact + (jnp.log(nf / max_exact) / math.log(REL_MAX_DIST / max_exact)
                         * (REL_BUCKETS - max_exact)).astype(jnp.int32)
    large = jnp.minimum(large, REL_BUCKETS - 1)
    return jnp.where(n < max_exact, n, large)


def _bias_kernel(rel_ref, bucket_ref, valid_ref, o_ref):
    bucket = bucket_ref[...]
    for h in range(N_HEADS):
        acc = jnp.zeros(bucket.shape, f32)
        for b in range(REL_BUCKETS):
            acc = jnp.where(bucket == b, rel_ref[b, h], acc)
        for variant in range(2):
            o_ref[variant, h] = jnp.where(valid_ref[variant] != 0, acc, NEG)


def _band_bias(rel_bias, max_dist, dilation):
    qi = jnp.arange(BLOCK)[:, None]
    kj = jnp.arange(2 * BLOCK)[None, :]
    dist = BLOCK + qi - kj
    bucket = _rel_bucket(dist * dilation).astype(jnp.int32)
    valid = (dist >= 0) & (dist <= max_dist)
    valid = jnp.stack([valid & (kj >= BLOCK), valid]).astype(jnp.int32)
    return pl.pallas_call(
        _bias_kernel,
        in_specs=[pl.BlockSpec(memory_space=pltpu.SMEM),
                  pl.BlockSpec(memory_space=pltpu.VMEM),
                  pl.BlockSpec(memory_space=pltpu.VMEM)],
        out_specs=pl.BlockSpec(memory_space=pltpu.VMEM),
        out_shape=jax.ShapeDtypeStruct((2, N_HEADS, BLOCK, 2 * BLOCK), f32))(rel_bias, bucket, valid)


def _band_kernel(q_ref, kp_ref, kc_ref, vp_ref, vc_ref, bias_ref, *rest, group, with_sink):
    if with_sink:
        sink_ref, o_ref = rest
    else:
        o_ref, lse_ref = rest
    scale = HEAD_DIM ** -0.5
    for h0 in range(0, N_HEADS, HEAD_BATCH):
        heads = range(h0, h0 + HEAD_BATCH)
        qs = [slice(h * HEAD_DIM, (h + 1) * HEAD_DIM) for h in heads]
        ks = [slice((h // group) * HEAD_DIM, (h // group + 1) * HEAD_DIM) for h in heads]
        kk = [jnp.concatenate([kp_ref[0, :, c], kc_ref[0, :, c]], axis=0) for c in ks]
        vv = [jnp.concatenate([vp_ref[0, :, c], vc_ref[0, :, c]], axis=0) for c in ks]
        ss = [lax.dot_general(q_ref[0, :, c], k, _NT, preferred_element_type=f32) * scale + bias_ref[0, h]
              for h, c, k in zip(heads, qs, kk)]
        ms = [s.max(-1, keepdims=True) for s in ss]
        ps = [jnp.exp(s - m) for s, m in zip(ss, ms)]
        ls = [p.sum(-1, keepdims=True) for p in ps]
        os_ = [jnp.dot(p.astype(bf16), v, preferred_element_type=f32) / l for p, v, l in zip(ps, vv, ls)]
        lses = [m + jnp.log(l) for m, l in zip(ms, ls)]
        for c, o, lse in zip(qs, os_, lses):
            if with_sink:
                o = o * jax.nn.sigmoid(lse - sink_ref[:, c])
            else:
                lse_ref[0, :, c] = jnp.broadcast_to(lse, (BLOCK, HEAD_DIM))
            o_ref[0, :, c] = o.astype(o_ref.dtype)


def _band_attention(view, bias, dil, feat, q_col, k_col, v_col, kv_width, group, sink=None):
    b, m_len, _ = view.shape
    d_q = N_HEADS * HEAD_DIM
    nb = m_len // BLOCK
    qpr, kpr = feat // d_q, feat // kv_width

    def cur(col, per_row):
        return lambda bi, r, n: (bi, n, r * per_row + col)

    def prev(col, per_row):
        return lambda bi, r, n: (bi, jnp.maximum(n - 1, 0), r * per_row + col)

    in_specs = [pl.BlockSpec((1, BLOCK, d_q), cur(q_col, qpr)),
                pl.BlockSpec((1, BLOCK, kv_width), prev(k_col, kpr)),
                pl.BlockSpec((1, BLOCK, kv_width), cur(k_col, kpr)),
                pl.BlockSpec((1, BLOCK, kv_width), prev(v_col, kpr)),
                pl.BlockSpec((1, BLOCK, kv_width), cur(v_col, kpr)),
                pl.BlockSpec((1, N_HEADS, BLOCK, 2 * BLOCK),
                             lambda bi, r, n: (jnp.minimum(n, 1), 0, 0, 0))]
    args = [view, view, view, view, view, bias]
    o_spec = pl.BlockSpec((1, BLOCK, d_q), lambda bi, r, n: (bi, n, r))
    o_shape = jax.ShapeDtypeStruct((b, m_len, dil * d_q), bf16)
    if sink is not None:
        in_specs.append(pl.BlockSpec((1, d_q), lambda bi, r, n: (0, 0)))
        args.append(sink)
        out_specs, out_shape = o_spec, o_shape
    else:
        out_specs = [o_spec, o_spec]
        out_shape = [o_shape, jax.ShapeDtypeStruct((b, m_len, dil * d_q), f32)]
    out = pl.pallas_call(
        functools.partial(_band_kernel, group=group, with_sink=sink is not None),
        grid=(b, dil, nb), in_specs=in_specs, out_specs=out_specs, out_shape=out_shape,
        compiler_params=_params("parallel", "parallel", "arbitrary"))(*args)
    if sink is not None:
        return out.reshape(b * m_len, dil * d_q)
    return out[0].reshape(b * m_len, dil * d_q), out[1].reshape(b * m_len, dil * d_q)


def _layer_norm(z, g, b):
    mu = z.mean(-1, keepdims=True)
    zc = z - mu
    var = jnp.square(zc).mean(-1, keepdims=True)
    return zc * lax.rsqrt(var + LN_EPS) * g + b


def _proj_ln_kernel(*refs, dils, alpha):
    n_pat = len(dils)
    o_refs = refs[:n_pat]
    lse_refs = refs[n_pat:2 * n_pat] if n_pat > 1 else ()
    n_in = len(o_refs) + len(lse_refs)
    w_ref, x_ref, g_ref, b_ref, out_ref = refs[n_in:n_in + 5]
    scratch = refs[n_in + 5:]
    if n_pat == 1 and len(o_refs[0].shape) == 3:
        groups, _, width = o_refs[0].shape
        y = sum(jnp.dot(o_refs[0][gi], w_ref[gi * width:(gi + 1) * width, :], preferred_element_type=f32)
                for gi in range(groups))
        out_ref[...] = _layer_norm(alpha * x_ref[...] + y, g_ref[...], b_ref[...])
        return
    if n_pat == 1:
        o = o_refs[0][...]
    else:
        kw = w_ref.shape[0]
        mix_ref = scratch[-1]

        def natural(ref, dil, scr, c):
            if dil == 1:
                return ref[:, c * LANE:(c + 1) * LANE].astype(f32)
            rows = ref.shape[0]
            for r in range(dil):
                scr[c, pl.ds(r, rows, stride=dil), :] = (
                    ref[:, r * kw + c * LANE:r * kw + (c + 1) * LANE].astype(f32))
            return scr[c]

        for c in range(kw // LANE):
            lses = [natural(r, dl, scratch[2 * i], c) for i, (r, dl) in enumerate(zip(lse_refs, dils))]
            mx = functools.reduce(jnp.maximum, lses)
            es = [jnp.exp(l - mx) for l in lses]
            num = sum(e * natural(r, dl, scratch[2 * i + 1], c)
                      for i, (e, r, dl) in enumerate(zip(es, o_refs, dils)))
            mix_ref[:, c * LANE:(c + 1) * LANE] = (num / sum(es)).astype(bf16)
        o = mix_ref[...]
    y = jnp.dot(o, w_ref[...], preferred_element_type=f32)
    out_ref[...] = _layer_norm(alpha * x_ref[...] + y, g_ref[...], b_ref[...])


def _proj_ln(os_, lses, dils, w_out, x, g, b, alpha, tm=512):
    t, d = x.shape
    tm = min(tm, t)
    kw = w_out.shape[0]
    row = lambda width: pl.BlockSpec((tm, width), lambda i: (i, 0))
    dilated = lambda dl: pl.BlockSpec((tm // dl, dl * kw), lambda i: (i, 0))
    fixed = lambda shape: pl.BlockSpec(shape, lambda i: (0, 0))
    if os_[0].ndim == 3:
        o_specs = [pl.BlockSpec((os_[0].shape[0], tm, os_[0].shape[2]), lambda i: (0, i, 0))]
    else:
        o_specs = [dilated(dl) for dl in dils]
    in_specs = (o_specs + [dilated(dl) for dl in dils[:len(lses)]]
                + [fixed((kw, d)), row(d), fixed((1, d)), fixed((1, d))])
    scratch = [pltpu.VMEM((kw // LANE, tm, LANE), f32)] * (2 * len(lses))
    if lses:
        scratch.append(pltpu.VMEM((tm, kw), bf16))
    return pl.pallas_call(
        functools.partial(_proj_ln_kernel, dils=tuple(dils), alpha=alpha),
        grid=(t // tm,), in_specs=in_specs, out_specs=row(d),
        out_shape=jax.ShapeDtypeStruct((t, d), f32), scratch_shapes=scratch,
        compiler_params=_params("parallel"))(*os_, *lses, w_out, x, g.reshape(1, d), b.reshape(1, d))


def _mm_dil_kernel(x_ref, w_ref, o_ref, res_ref, *, dil):
    n = w_ref.shape[1]
    res = jnp.dot(x_ref[...].astype(bf16), w_ref[...], preferred_element_type=f32)
    if dil == 1:
        o_ref[...] = res.astype(o_ref.dtype)
        return
    rows = o_ref.shape[0]
    for c in range(n // LANE):
        res_ref[c] = res[:, c * LANE:(c + 1) * LANE]
        for r in range(dil):
            o_ref[:, r * n + c * LANE:r * n + (c + 1) * LANE] = (
                res_ref[c, pl.ds(r, rows, stride=dil), :].astype(o_ref.dtype))


def _mm_dil(x, w, dil, tm=512):
    t, k = x.shape
    n = w.shape[1]
    tm = min(tm, t)
    return pl.pallas_call(
        functools.partial(_mm_dil_kernel, dil=dil), grid=(t // tm,),
        in_specs=[pl.BlockSpec((tm, k), lambda i: (i, 0)), pl.BlockSpec((k, n), lambda i: (0, 0))],
        out_specs=pl.BlockSpec((tm // dil, dil * n), lambda i: (i, 0)),
        out_shape=jax.ShapeDtypeStruct((t // dil, dil * n), bf16),
        scratch_shapes=[pltpu.VMEM((n // LANE, tm, LANE), f32)],
        compiler_params=_params("parallel"))(x, w)


def _rms(c, g):
    return c * lax.rsqrt(jnp.square(c).mean(-1, keepdims=True) + RMS_EPS) * g


def _mla_prep_kernel(x_ref, win_ref, qn_ref, wq_ref, kvn_ref, wkv_ref, cos_ref, sin_ref,
                     q_ref, k_ref, v_ref):
    hw = N_HEADS * LANE
    scale = (MLA_NOPE + MLA_ROPE) ** -0.5
    cos, sin = cos_ref[...], sin_ref[...]
    xw = jnp.dot(x_ref[...].astype(bf16), win_ref[...], preferred_element_type=f32)
    cq = _rms(xw[:, :MLA_Q_RANK], qn_ref[...]).astype(bf16)
    ckv = _rms(xw[:, MLA_Q_RANK:MLA_Q_RANK + MLA_KV_RANK], kvn_ref[...]).astype(bf16)
    off = MLA_Q_RANK + MLA_KV_RANK
    kr = xw[:, off:off + LANE] * cos + xw[:, off + LANE:off + 2 * LANE] * sin
    qq = jnp.dot(cq, wq_ref[...], preferred_element_type=f32)
    kv = jnp.dot(ckv, wkv_ref[...], preferred_element_type=f32)
    ones_lane = lax.broadcasted_iota(jnp.int32, (1, LANE), 1) == MLA_V
    for h in range(N_HEADS):
        blk = slice(h * LANE, (h + 1) * LANE)
        swp = slice(hw + h * LANE, hw + (h + 1) * LANE)
        grp, sub = divmod(h, MLA_GROUP)
        dst = slice(sub * LANE, (sub + 1) * LANE)
        q_ref[grp, :, dst] = ((qq[:, blk] * cos + qq[:, swp] * sin) * scale).astype(bf16)
        k_ref[grp, :, dst] = (kv[:, blk] + kr).astype(bf16)
        v_ref[grp, :, dst] = jnp.where(ones_lane, 1.0, kv[:, swp]).astype(bf16)


def _mla_prep(x, w_in, q_norm, w_uq, kv_norm, w_ukv, seq, tm=512):
    t, d = x.shape
    tm = min(tm, seq)
    hw = N_HEADS * LANE
    half = MLA_ROPE // 2
    dq = MLA_NOPE + MLA_ROPE
    kr_w = w_in[:, MLA_Q_RANK + MLA_KV_RANK:]
    zeros = lambda n: jnp.zeros((d, n), f32)
    kr_a = jnp.concatenate([zeros(MLA_NOPE), kr_w, zeros(LANE - dq)], 1)
    kr_b = jnp.concatenate([zeros(MLA_NOPE), kr_w[:, half:], kr_w[:, :half], zeros(LANE - dq)], 1)
    win = jnp.concatenate([w_in[:, :MLA_Q_RANK + MLA_KV_RANK], kr_a, kr_b], 1).astype(bf16)
    wq3 = w_uq.reshape(MLA_Q_RANK, N_HEADS, dq)
    zq = lambda n: jnp.zeros((MLA_Q_RANK, N_HEADS, n), f32)
    wq_a = jnp.concatenate([wq3, zq(LANE - dq)], 2)
    wq_b = jnp.concatenate([zq(MLA_NOPE), wq3[:, :, MLA_NOPE + half:], wq3[:, :, MLA_NOPE:MLA_NOPE + half],
                            zq(LANE - dq)], 2)
    wq = jnp.concatenate([wq_a.reshape(MLA_Q_RANK, hw), wq_b.reshape(MLA_Q_RANK, hw)], 1).astype(bf16)
    wkv3 = w_ukv.reshape(MLA_KV_RANK, N_HEADS, MLA_NOPE + MLA_V)
    zk = lambda n: jnp.zeros((MLA_KV_RANK, N_HEADS, n), f32)
    wk = jnp.concatenate([wkv3[:, :, :MLA_NOPE], zk(LANE - MLA_NOPE)], 2)
    wv = jnp.concatenate([wkv3[:, :, MLA_NOPE:], zk(LANE - MLA_V)], 2)
    wkv = jnp.concatenate([wk.reshape(MLA_KV_RANK, hw), wv.reshape(MLA_KV_RANK, hw)], 1).astype(bf16)
    freq = ROPE_THETA ** (-jnp.arange(half, dtype=f32) / half)
    ang = jnp.arange(seq)[:, None].astype(f32) * freq[None, :]
    c, s_ = jnp.cos(ang), jnp.sin(ang)
    cos_t = jnp.concatenate([jnp.ones((seq, MLA_NOPE), f32), c, c, jnp.ones((seq, LANE - dq), f32)], 1)
    sin_t = jnp.concatenate([jnp.zeros((seq, MLA_NOPE), f32), -s_, s_, jnp.zeros((seq, LANE - dq), f32)], 1)

    n_pos = seq // tm
    row = lambda width: pl.BlockSpec((tm, width), lambda i: (i, 0))
    fixed = lambda a: pl.BlockSpec(a.shape, lambda i: (0, 0))
    pos = pl.BlockSpec((tm, LANE), lambda i: (i % n_pos, 0))
    qn, kvn = q_norm.reshape(1, -1), kv_norm.reshape(1, -1)
    groups, gw = N_HEADS // MLA_GROUP, MLA_GROUP * LANE
    out_sd = jax.ShapeDtypeStruct((groups, t, gw), bf16)
    out_spec = pl.BlockSpec((groups, tm, gw), lambda i: (0, i, 0))
    return pl.pallas_call(
        _mla_prep_kernel, grid=(t // tm,),
        in_specs=[row(d), fixed(win), fixed(qn), fixed(wq), fixed(kvn), fixed(wkv), pos, pos],
        out_specs=[out_spec] * 3, out_shape=[out_sd] * 3,
        compiler_params=_params("parallel"))(x, win, qn, wq, kvn, wkv, cos_t, sin_t)


def _mla_attn_kernel(q_ref, k_ref, v_ref, o_ref, *, tq, heads):
    qi = pl.program_id(2)
    lanes = [slice(j * LANE, (j + 1) * LANE) for j in range(heads)]
    qs = [q_ref[0, :, ln] for ln in lanes]

    def update(carry, start, mask):
        ss = [lax.dot_general(q, k_ref[0, pl.ds(start, tq), ln], _NT, preferred_element_type=f32)
              for q, ln in zip(qs, lanes)]
        if mask is not None:
            ss = [jnp.where(mask, s, NEG) for s in ss]
        ms = [jnp.maximum(m, s.max(-1, keepdims=True)) for (m, _), s in zip(carry, ss)]
        ps = [jnp.exp(s - m).astype(bf16) for s, m in zip(ss, ms)]
        pvs = [jnp.dot(p, v_ref[0, pl.ds(start, tq), ln], preferred_element_type=f32)
               for p, ln in zip(ps, lanes)]
        return tuple((m_new, jnp.exp(m - m_new) * acc + pv)
                     for (m, acc), m_new, pv in zip(carry, ms, pvs))

    init = tuple((jnp.full((tq, 1), NEG, f32), jnp.zeros((tq, LANE), f32)) for _ in range(heads))
    carry = lax.fori_loop(0, qi, lambda ki, c: update(c, pl.multiple_of(ki * tq, tq), None), init)
    causal = lax.broadcasted_iota(jnp.int32, (tq, tq), 1) <= lax.broadcasted_iota(jnp.int32, (tq, tq), 0)
    carry = update(carry, pl.multiple_of(qi * tq, tq), causal)
    for j in range(heads):
        acc = carry[j][1]
        o_ref[0, :, lanes[j]] = (acc / acc[:, MLA_V:MLA_V + 1]).astype(o_ref.dtype)


def _mla_attention(q, k, v, batch, seq, tq=512):
    groups, t, gw = q.shape
    tq = min(tq, seq)
    nq = seq // tq
    qspec = pl.BlockSpec((1, tq, gw), lambda b, h, i: (h, b * nq + i, 0))
    kspec = pl.BlockSpec((1, seq, gw), lambda b, h, i: (h, b, 0))
    return pl.pallas_call(
        functools.partial(_mla_attn_kernel, tq=tq, heads=MLA_GROUP),
        grid=(batch, groups, nq),
        in_specs=[qspec, kspec, kspec], out_specs=qspec,
        out_shape=jax.ShapeDtypeStruct((groups, t, gw), bf16),
        compiler_params=_params("parallel", "parallel", "arbitrary"))(q, k, v)


def _top16_pair(s1, s2):
    w1, w2 = s1, s2
    rank2 = jnp.full(s2.shape, 99.0, f32)
    v1, v2 = [], []
    for k in range(PEER_TOPK):
        m1 = w1.max(axis=0, keepdims=True)
        m2 = w2.max(axis=0, keepdims=True)
        h1 = w1 == m1
        h2 = w2 == m2
        rank2 = jnp.where(h2, float(k), rank2)
        w1 = jnp.where(h1, -jnp.inf, w1)
        w2 = jnp.where(h2, -jnp.inf, w2)
        v1.append(m1)
        v2.append(m2)
    return v1, v2, rank2


def _stack_rows(rows):
    n = len(rows)
    rid = lax.broadcasted_iota(jnp.int32, (n, rows[0].shape[1]), 0)
    out = jnp.broadcast_to(rows[0], (n, rows[0].shape[1]))
    for i in range(1, n):
        out = jnp.where(rid == i, rows[i], out)
    return out


def _route_kernel(x_ref, wq_ref, keys_ref, rank2_ref, crow_ref, e1z_ref, e2_ref, *, tm):
    q = jnp.dot(x_ref[...].astype(bf16), wq_ref[...], preferred_element_type=f32).astype(bf16)
    half = PEER_DKEY // 2
    s1_all = lax.dot_general(keys_ref[0, 0], q[:, :half], _NT, preferred_element_type=f32)
    s2_all = lax.dot_general(keys_ref[0, 1], q[:, half:], _NT, preferred_element_type=f32)
    rid8 = lax.broadcasted_iota(jnp.int32, (8, LANE), 0)
    for c in range(tm // LANE):
        lanes = slice(c * LANE, (c + 1) * LANE)
        s1, s2 = s1_all[:, lanes], s2_all[:, lanes]
        v1, v2, rank2 = _top16_pair(s1, s2)
        v2_all = _stack_rows(v2)
        cands = [v1[0] + v2_all]
        for a in range(1, 8):
            cands.append(jnp.where(rid8 < PEER_TOPK // (a + 1), v1[a] + v2_all[:8], -jnp.inf))
        cands.append(_stack_rows(v1[8:]) + v2[0])
        cand = jnp.concatenate(cands, axis=0)
        work = cand
        for _ in range(PEER_TOPK):
            tau = work.max(axis=0, keepdims=True)
            work = jnp.where(work == tau, -jnp.inf, work)
        cmax = v1[0] + v2[0]
        z = jnp.where(cand >= tau, jnp.exp(cand - cmax), 0.0).sum(axis=0, keepdims=True)
        crow = jnp.zeros(s1.shape, f32)
        for b in range(PEER_TOPK):
            crow = crow + jnp.where(s1 + v2[b] >= tau, 1.0, 0.0)
        rank2_ref[0, 0, :, lanes] = rank2.astype(bf16)
        crow_ref[0, 0, :, lanes] = crow
        e1z_ref[0, 0, :, lanes] = jnp.exp(s1 - v1[0]) / z
        e2_ref[0, 0, :, lanes] = jnp.exp(s2 - v2[0]).astype(bf16)


def _peer_route(x, w_q, keys, tm=ROUTE_TOKENS):
    t, d = x.shape
    tm = min(tm, t)
    spec = pl.BlockSpec((1, 1, PEER_KEYS, tm), lambda i, h: (i, h, 0, 0))
    sd = lambda dt: jax.ShapeDtypeStruct((t // tm, PEER_HEADS, PEER_KEYS, tm), dt)
    return pl.pallas_call(
        functools.partial(_route_kernel, tm=tm), grid=(t // tm, PEER_HEADS),
        in_specs=[pl.BlockSpec((tm, d), lambda i, h: (i, 0)),
                  pl.BlockSpec((d, PEER_DKEY), lambda i, h: (0, h)),
                  pl.BlockSpec((1, 2, PEER_KEYS, PEER_DKEY // 2), lambda i, h: (h, 0, 0, 0))],
        out_specs=[spec] * 4, out_shape=[sd(bf16), sd(f32), sd(f32), sd(bf16)],
        compiler_params=_params("parallel", "arbitrary"))(x, w_q, keys)


def _peer_dense_kernel(x_ref, u0_ref, u1_ref, vt0_ref, vt1_ref, rank2_in, crow_ref, e1z_ref, e2_in,
                       g_ref, b_ref, o_ref, acc_ref, xt_ref, ht0, ht1, gt0, gt1, rank2_ref, e2_ref,
                       *, rows, tb, lc, mw, alpha):
    k = pl.program_id(1)
    ec = rows * PEER_KEYS

    @pl.when(k == 0)
    def _():
        acc_ref[...] = jnp.zeros_like(acc_ref)
        ht1[...] = jnp.zeros_like(ht1)
        gt0[...] = jnp.zeros_like(gt0)
        xt_ref[...] = x_ref[...].T.astype(bf16)
        rank2_ref[...] = rank2_in[0]
        e2_ref[...] = e2_in[0]

    zero = jnp.zeros((), bf16)
    kt = PEER_KEYS // BF16_ROWS

    def up(half, ht, cw):
        lanes = slice(cw * mw, (cw + 1) * mw)
        ht[:, lanes] = jnp.dot((u0_ref, u1_ref)[half][...], xt_ref[:, lanes], preferred_element_type=f32)

    n_tiles = pl.num_programs(1) - 1

    def gate(ht, tile, half, gt, cw):
        base = pl.multiple_of(jnp.clip(tile, 0, n_tiles - 1) * F32_ROWS, F32_ROWS)
        for r in range(rows):
            keys = slice(r * PEER_KEYS, (r + 1) * PEER_KEYS)
            row = slice(half * rows + r, half * rows + r + 1)
            for c in range(cw * mw // lc, (cw + 1) * mw // lc):
                lanes = slice(c * lc, (c + 1) * lc)
                w = None
                for h in range(PEER_HEADS):
                    cr = crow_ref[0, h, pl.ds(base, F32_ROWS), lanes][row]
                    ez = e1z_ref[0, h, pl.ds(base, F32_ROWS), lanes][row]
                    cr = jnp.broadcast_to(cr, (BF16_ROWS, lc)).astype(bf16)[None]
                    ez = jnp.broadcast_to(ez, (BF16_ROWS, lc)).astype(bf16)[None]
                    term = jnp.where(rank2_ref[h, :, :, lanes] < cr, e2_ref[h, :, :, lanes] * ez, zero)
                    w = term if w is None else w + term
                hv = ht[keys, lanes]
                gelu = 0.5 * hv * (1.0 + lax.erf(hv * (2.0 ** -0.5)))
                gt[r * kt:(r + 1) * kt, :, lanes] = gelu.astype(bf16).reshape(kt, BF16_ROWS, lc) * w

    def down(half, gt, cw):
        lanes = slice(cw * mw, (cw + 1) * mw)
        acc_ref[:, lanes] += jnp.dot((vt0_ref, vt1_ref)[half][0], gt[:, :, lanes].reshape(ec, mw),
                                     preferred_element_type=f32)

    for cw in range(tb // mw):
        up(0, ht0, cw)
        gate(ht1, k - 1, 1, gt1, cw)
        down(0, gt0, cw)
    for cw in range(tb // mw):
        up(1, ht1, cw)
        gate(ht0, k, 0, gt0, cw)
        down(1, gt1, cw)

    @pl.when(k == pl.num_programs(1) - 1)
    def _():
        z = alpha * x_ref[...] + acc_ref[...].T
        o_ref[...] = _layer_norm(z, g_ref[...], b_ref[...])


def _peer_dense(x, u, vt, routing, g, b, alpha, lc=128):
    t, d = x.shape
    e = u.shape[0]
    ec = DENSE_EC
    rows = ec // PEER_KEYS
    assert 2 * rows == F32_ROWS, "a pair of expert blocks must span one 8-row f32 tile of sub-key-1 rows"
    nk = e // (2 * ec)
    kt = PEER_KEYS // BF16_ROWS
    rank2, crow, e1z, e2 = routing
    nt, _, _, tb = crow.shape
    rank2, e2 = (a.reshape(nt, PEER_HEADS, kt, BF16_ROWS, tb) for a in (rank2, e2))
    full = pl.BlockSpec((1, PEER_HEADS, kt, BF16_ROWS, tb), lambda i, k: (i, 0, 0, 0, 0))
    rowtab = pl.BlockSpec((1, PEER_HEADS, PEER_KEYS, tb), lambda i, k: (i, 0, 0, 0))
    fixed = pl.BlockSpec((1, d), lambda i, k: (0, 0))
    xspec = pl.BlockSpec((tb, d), lambda i, k: (i, 0))
    xin = xspec
    gt_scr = pltpu.VMEM((ec // BF16_ROWS, BF16_ROWS, tb), bf16)
    u_blk = lambda half: pl.BlockSpec((ec, d), lambda i, k: (2 * jnp.minimum(k, nk - 1) + half, 0))
    vt_blk = lambda half: pl.BlockSpec((1, d, ec), lambda i, k: (2 * jnp.maximum(k - 1, 0) + half, 0, 0))
    return pl.pallas_call(
        functools.partial(_peer_dense_kernel, rows=rows, tb=tb, lc=lc, mw=MXU_WIDTH, alpha=alpha),
        grid=(t // tb, nk + 1),
        in_specs=[xin, u_blk(0), u_blk(1), vt_blk(0), vt_blk(1),
                  full, rowtab, rowtab, full, fixed, fixed],
        out_specs=xspec, out_shape=jax.ShapeDtypeStruct((t, d), f32),
        scratch_shapes=[pltpu.VMEM((d, tb), f32), pltpu.VMEM((d, tb), bf16),
                        pltpu.VMEM((ec, tb), f32), pltpu.VMEM((ec, tb), f32), gt_scr, gt_scr,
                        pltpu.VMEM((PEER_HEADS, kt, BF16_ROWS, tb), bf16),
                        pltpu.VMEM((PEER_HEADS, kt, BF16_ROWS, tb), bf16)],
        compiler_params=_params("parallel", "arbitrary"))(
            x, u, u, vt, vt, rank2, crow, e1z, e2, g.reshape(1, d), b.reshape(1, d))


def _swa_layer(x, batch, seq, w_in, sinks, w_out, bias, g, b, alpha):
    feat = (N_HEADS + 2 * SWA_KV_HEADS) * HEAD_DIM
    d_q = N_HEADS * HEAD_DIM
    kvw = SWA_KV_HEADS * HEAD_DIM
    proj = _mm(x, w_in.astype(bf16), bf16, 512, feat).reshape(batch, seq, feat)
    sink = jnp.repeat(sinks.astype(f32), HEAD_DIM).reshape(1, d_q)
    o = _band_attention(proj, bias, 1, feat, 0, d_q // kvw, d_q // kvw + 1, kvw,
                        N_HEADS // SWA_KV_HEADS, sink)
    return _proj_ln([o], [], [1], w_out.astype(bf16), x, g, b, alpha)


def _dil_layer(x, batch, seq, w_in, w_out, biases, g, b, alpha):
    d_q = N_HEADS * HEAD_DIM
    feat = 3 * d_q
    w_in = w_in.astype(bf16)
    outs, lses, dils = [], [], []
    for gi, (window, dil) in enumerate(DIL_PATTERNS):
        proj = _mm_dil(x, w_in[:, gi * feat:(gi + 1) * feat], dil).reshape(batch, seq // dil, dil * feat)
        o, lse = _band_attention(proj, biases[gi], dil, feat, 0, 1, 2, d_q, 1)
        outs.append(o)
        lses.append(lse)
        dils.append(dil)
    return _proj_ln(outs, lses, dils, w_out.astype(bf16), x, g, b, alpha)


def _mla_layer(x, batch, seq, w_in, q_norm, w_uq, kv_norm, w_ukv, w_out, g, b, alpha):
    q, k, v = _mla_prep(x, w_in, q_norm, w_uq, kv_norm, w_ukv, seq)
    o = _mla_attention(q, k, v, batch, seq)
    d = w_out.shape[1]
    w3 = w_out.reshape(N_HEADS, MLA_V, d)
    w_pad = jnp.concatenate([w3, jnp.zeros((N_HEADS, LANE - MLA_V, d), f32)], 1)
    return _proj_ln([o], [], [1], w_pad.reshape(N_HEADS * LANE, d).astype(bf16), x, g, b, alpha)


def _peer_layer(x, w_q, keys, u, v, g, b, alpha):
    routing = _peer_route(x, w_q.astype(bf16), keys.astype(bf16))
    e, d = v.shape
    vt = v.astype(bf16).reshape(e // DENSE_EC, DENSE_EC, d).transpose(0, 2, 1)
    return _peer_dense(x, u.astype(bf16), vt, routing, g, b, alpha)


def kernel(x, rel_bias, ln_g, ln_b, swa_w_in, swa_sinks, swa_w_out, dil_w_in, dil_w_out,
           mla_w_in, mla_q_norm, mla_w_uq, mla_kv_norm, mla_w_ukv, mla_w_out,
           peer_w_q, peer_keys, peer_u, peer_v):
    batch, seq, d = x.shape
    depth = ln_g.shape[0]
    alpha = (2 * depth) ** 0.25
    assert seq % (DIL_PATTERNS[-1][1] * BLOCK) == 0, "sequence must be a whole number of dilation segments"
    swa_bias = _band_bias(rel_bias, SWA_WINDOW - 1, 1)
    dil_bias = [_band_bias(rel_bias, window // dil, dil) for window, dil in DIL_PATTERNS]
    h = x.reshape(batch * seq, d)
    for i in range(depth):
        kind, j = i % 3, i // 3
        if kind == 0:
            h = _swa_layer(h, batch, seq, swa_w_in[j], swa_sinks[j], swa_w_out[j], swa_bias,
                           ln_g[i, 0], ln_b[i, 0], alpha)
        elif kind == 1:
            h = _dil_layer(h, batch, seq, dil_w_in[j], dil_w_out[j], dil_bias,
                           ln_g[i, 0], ln_b[i, 0], alpha)
        else:
            h = _mla_layer(h, batch, seq, mla_w_in[j], mla_q_norm[j], mla_w_uq[j], mla_kv_norm[j],
                           mla_w_ukv[j], mla_w_out[j], ln_g[i, 0], ln_b[i, 0], alpha)
        h = _peer_layer(h, peer_w_q[i], peer_keys[i], peer_u[i], peer_v[i], ln_g[i, 1], ln_b[i, 1], alpha)
    return h.reshape(batch, seq, d)
```

```python
import functools
import math

import jax
import jax.numpy as jnp
from jax import lax
from jax.experimental import pallas as pl
from jax.experimental.pallas import tpu as pltpu

f32 = jnp.float32
bf16 = jnp.bfloat16

N_HEADS = 16
HEAD_DIM = 64
BLOCK = 128
SWA_KV_HEADS = 2
SWA_WINDOW = 128
DIL_PATTERNS = ((128, 1), (512, 4), (2048, 16))
MLA_Q_RANK = 256
MLA_KV_RANK = 128
MLA_NOPE = 64
MLA_ROPE = 32
MLA_V = 64
ROPE_THETA = 10000.0
REL_BUCKETS = 32
REL_MAX_DIST = 2048
PEER_HEADS = 8
PEER_KEYS = 128
PEER_DKEY = 256
PEER_TOPK = 16
LN_EPS = 1e-5
RMS_EPS = 1e-6
NEG = -1e30

LANE = 128
BF16_ROWS = 16
F32_ROWS = 8
MXU_WIDTH = 256
MLA_GROUP = 2
ROUTE_HEADS = 2
ROUTE_TOKENS = 512
HEAD_BATCH = 16
DENSE_EC = 4 * PEER_KEYS
VMEM_LIMIT = 56 * 1024 * 1024

_NT = (((1,), (1,)), ((), ()))


def _params(*sem):
    return pltpu.CompilerParams(dimension_semantics=sem, vmem_limit_bytes=VMEM_LIMIT)


def _mm_kernel(x_ref, w_ref, o_ref):
    o_ref[...] = jnp.dot(x_ref[...].astype(bf16), w_ref[...],
                         preferred_element_type=f32).astype(o_ref.dtype)


def _mm(x, w, out_dtype, tm, tn):
    m, k = x.shape
    n = w.shape[1]
    tm, tn = min(tm, m), min(tn, n)
    return pl.pallas_call(
        _mm_kernel, grid=(m // tm, n // tn),
        in_specs=[pl.BlockSpec((tm, k), lambda i, j: (i, 0)),
                  pl.BlockSpec((k, tn), lambda i, j: (0, j))],
        out_specs=pl.BlockSpec((tm, tn), lambda i, j: (i, j)),
        out_shape=jax.ShapeDtypeStruct((m, n), out_dtype),
        compiler_params=_params("parallel", "parallel"))(x, w)


def _rel_bucket(dist):
    max_exact = REL_BUCKETS // 2
    n = jnp.maximum(dist, 0)
    nf = jnp.maximum(n, 1).astype(f32)
    large = max_exact + (jnp.log(nf / max_exact) / math.log(REL_MAX_DIST / max_exact)
                         * (REL_BUCKETS - max_exact)).astype(jnp.int32)
    large = jnp.minimum(large, REL_BUCKETS - 1)
    return jnp.where(n < max_exact, n, large)


def _bias_kernel(rel_ref, bucket_ref, valid_ref, o_ref):
    bucket = bucket_ref[...]
    for h in range(N_HEADS):
        acc = jnp.zeros(bucket.shape, f32)
        for b in range(REL_BUCKETS):
            acc = jnp.where(bucket == b, rel_ref[b, h], acc)
        for variant in range(2):
            o_ref[variant, h] = jnp.where(valid_ref[variant] != 0, acc, NEG)


def _band_bias(rel_bias, max_dist, dilation):
    qi = jnp.arange(BLOCK)[:, None]
    kj = jnp.arange(2 * BLOCK)[None, :]
    dist = BLOCK + qi - kj
    bucket = _rel_bucket(dist * dilation).astype(jnp.int32)
    valid = (dist >= 0) & (dist <= max_dist)
    valid = jnp.stack([valid & (kj >= BLOCK), valid]).astype(jnp.int32)
    return pl.pallas_call(
        _bias_kernel,
        in_specs=[pl.BlockSpec(memory_space=pltpu.SMEM),
                  pl.BlockSpec(memory_space=pltpu.VMEM),
                  pl.BlockSpec(memory_space=pltpu.VMEM)],
        out_specs=pl.BlockSpec(memory_space=pltpu.VMEM),
        out_shape=jax.ShapeDtypeStruct((2, N_HEADS, BLOCK, 2 * BLOCK), f32))(rel_bias, bucket, valid)


def _band_kernel(q_ref, kp_ref, kc_ref, vp_ref, vc_ref, bias_ref, *rest, group, with_sink):
    if with_sink:
        sink_ref, o_ref = rest
    else:
        o_ref, lse_ref = rest
    scale = HEAD_DIM ** -0.5
    for h0 in range(0, N_HEADS, HEAD_BATCH):
        heads = range(h0, h0 + HEAD_BATCH)
        qs = [slice(h * HEAD_DIM, (h + 1) * HEAD_DIM) for h in heads]
        ks = [slice((h // group) * HEAD_DIM, (h // group + 1) * HEAD_DIM) for h in heads]
        kk = [jnp.concatenate([kp_ref[0, :, c], kc_ref[0, :, c]], axis=0) for c in ks]
        vv = [jnp.concatenate([vp_ref[0, :, c], vc_ref[0, :, c]], axis=0) for c in ks]
        ss = [lax.dot_general(q_ref[0, :, c], k, _NT, preferred_element_type=f32) * scale + bias_ref[0, h]
              for h, c, k in zip(heads, qs, kk)]
        ms = [s.max(-1, keepdims=True) for s in ss]
        ps = [jnp.exp(s - m) for s, m in zip(ss, ms)]
        ls = [p.sum(-1, keepdims=True) for p in ps]
        os_ = [jnp.dot(p.astype(bf16), v, preferred_element_type=f32) / l for p, v, l in zip(ps, vv, ls)]
        lses = [m + jnp.log(l) for m, l in zip(ms, ls)]
        for c, o, lse in zip(qs, os_, lses):
            if with_sink:
                o = o * jax.nn.sigmoid(lse - sink_ref[:, c])
            else:
                lse_ref[0, :, c] = jnp.broadcast_to(lse, (BLOCK, HEAD_DIM))
            o_ref[0, :, c] = o.astype(o_ref.dtype)


def _band_attention(view, bias, dil, feat, q_col, k_col, v_col, kv_width, group, sink=None):
    b, m_len, _ = view.shape
    d_q = N_HEADS * HEAD_DIM
    nb = m_len // BLOCK
    qpr, kpr = feat // d_q, feat // kv_width

    def cur(col, per_row):
        return lambda bi, r, n: (bi, n, r * per_row + col)

    def prev(col, per_row):
        return lambda bi, r, n: (bi, jnp.maximum(n - 1, 0), r * per_row + col)

    in_specs = [pl.BlockSpec((1, BLOCK, d_q), cur(q_col, qpr)),
                pl.BlockSpec((1, BLOCK, kv_width), prev(k_col, kpr)),
                pl.BlockSpec((1, BLOCK, kv_width), cur(k_col, kpr)),
                pl.BlockSpec((1, BLOCK, kv_width), prev(v_col, kpr)),
                pl.BlockSpec((1, BLOCK, kv_width), cur(v_col, kpr)),
                pl.BlockSpec((1, N_HEADS, BLOCK, 2 * BLOCK),
                             lambda bi, r, n: (jnp.minimum(n, 1), 0, 0, 0))]
    args = [view, view, view, view, view, bias]
    o_spec = pl.BlockSpec((1, BLOCK, d_q), lambda bi, r, n: (bi, n, r))
    o_shape = jax.ShapeDtypeStruct((b, m_len, dil * d_q), bf16)
    if sink is not None:
        in_specs.append(pl.BlockSpec((1, d_q), lambda bi, r, n: (0, 0)))
        args.append(sink)
        out_specs, out_shape = o_spec, o_shape
    else:
        out_specs = [o_spec, o_spec]
        out_shape = [o_shape, jax.ShapeDtypeStruct((b, m_len, dil * d_q), f32)]
    out = pl.pallas_call(
        functools.partial(_band_kernel, group=group, with_sink=sink is not None),
        grid=(b, dil, nb), in_specs=in_specs, out_specs=out_specs, out_shape=out_shape,
        compiler_params=_params("parallel", "parallel", "arbitrary"))(*args)
    if sink is not None:
        return out.reshape(b * m_len, dil * d_q)
    return out[0].reshape(b * m_len, dil * d_q), out[1].reshape(b * m_len, dil * d_q)


def _layer_norm(z, g, b):
    mu = z.mean(-1, keepdims=True)
    zc = z - mu
    var = jnp.square(zc).mean(-1, keepdims=True)
    return zc * lax.rsqrt(var + LN_EPS) * g + b


def _proj_ln_kernel(*refs, dils, alpha):
    n_pat = len(dils)
    o_refs = refs[:n_pat]
    lse_refs = refs[n_pat:2 * n_pat] if n_pat > 1 else ()
    n_in = len(o_refs) + len(lse_refs)
    w_ref, x_ref, g_ref, b_ref, out_ref = refs[n_in:n_in + 5]
    scratch = refs[n_in + 5:]
    if n_pat == 1 and len(o_refs[0].shape) == 3:
        groups, _, width = o_refs[0].shape
        y = sum(jnp.dot(o_refs[0][gi], w_ref[gi * width:(gi + 1) * width, :], preferred_element_type=f32)
                for gi in range(groups))
        out_ref[...] = _layer_norm(alpha * x_ref[...] + y, g_ref[...], b_ref[...])
        return
    if n_pat == 1:
        o = o_refs[0][...]
    else:
        kw = w_ref.shape[0]
        mix_ref = scratch[-1]

        def natural(ref, dil, scr, c):
            if dil == 1:
                return ref[:, c * LANE:(c + 1) * LANE].astype(f32)
            rows = ref.shape[0]
            for r in range(dil):
                scr[c, pl.ds(r, rows, stride=dil), :] = (
                    ref[:, r * kw + c * LANE:r * kw + (c + 1) * LANE].astype(f32))
            return scr[c]

        for c in range(kw // LANE):
            lses = [natural(r, dl, scratch[2 * i], c) for i, (r, dl) in enumerate(zip(lse_refs, dils))]
            mx = functools.reduce(jnp.maximum, lses)
            es = [jnp.exp(l - mx) for l in lses]
            num = sum(e * natural(r, dl, scratch[2 * i + 1], c)
                      for i, (e, r, dl) in enumerate(zip(es, o_refs, dils)))
            mix_ref[:, c * LANE:(c + 1) * LANE] = (num / sum(es)).astype(bf16)
        o = mix_ref[...]
    y = jnp.dot(o, w_ref[...], preferred_element_type=f32)
    out_ref[...] = _layer_norm(alpha * x_ref[...] + y, g_ref[...], b_ref[...])


def _proj_ln(os_, lses, dils, w_out, x, g, b, alpha, tm=512):
    t, d = x.shape
    tm = min(tm, t)
    kw = w_out.shape[0]
    row = lambda width: pl.BlockSpec((tm, width), lambda i: (i, 0))
    dilated = lambda dl: pl.BlockSpec((tm // dl, dl * kw), lambda i: (i, 0))
    fixed = lambda shape: pl.BlockSpec(shape, lambda i: (0, 0))
    if os_[0].ndim == 3:
        o_specs = [pl.BlockSpec((os_[0].shape[0], tm, os_[0].shape[2]), lambda i: (0, i, 0))]
    else:
        o_specs = [dilated(dl) for dl in dils]
    in_specs = (o_specs + [dilated(dl) for dl in dils[:len(lses)]]
                + [fixed((kw, d)), row(d), fixed((1, d)), fixed((1, d))])
    scratch = [pltpu.VMEM((kw // LANE, tm, LANE), f32)] * (2 * len(lses))
    if lses:
        scratch.append(pltpu.VMEM((tm, kw), bf16))
    return pl.pallas_call(
        functools.partial(_proj_ln_kernel, dils=tuple(dils), alpha=alpha),
        grid=(t // tm,), in_specs=in_specs, out_specs=row(d),
        out_shape=jax.ShapeDtypeStruct((t, d), f32), scratch_shapes=scratch,
        compiler_params=_params("parallel"))(*os_, *lses, w_out, x, g.reshape(1, d), b.reshape(1, d))


def _mm_dil_kernel(x_ref, w_ref, o_ref, res_ref, *, dil):
    n = w_ref.shape[1]
    res = jnp.dot(x_ref[...].astype(bf16), w_ref[...], preferred_element_type=f32)
    if dil == 1:
        o_ref[...] = res.astype(o_ref.dtype)
        return
    rows = o_ref.shape[0]
    for c in range(n // LANE):
        res_ref[c] = res[:, c * LANE:(c + 1) * LANE]
        for r in range(dil):
            o_ref[:, r * n + c * LANE:r * n + (c + 1) * LANE] = (
                res_ref[c, pl.ds(r, rows, stride=dil), :].astype(o_ref.dtype))


def _mm_dil(x, w, dil, tm=512):
    t, k = x.shape
    n = w.shape[1]
    tm = min(tm, t)
    return pl.pallas_call(
        functools.partial(_mm_dil_kernel, dil=dil), grid=(t // tm,),
        in_specs=[pl.BlockSpec((tm, k), lambda i: (i, 0)), pl.BlockSpec((k, n), lambda i: (0, 0))],
        out_specs=pl.BlockSpec((tm // dil, dil * n), lambda i: (i, 0)),
        out_shape=jax.ShapeDtypeStruct((t // dil, dil * n), bf16),
        scratch_shapes=[pltpu.VMEM((n // LANE, tm, LANE), f32)],
        compiler_params=_params("parallel"))(x, w)


def _rms(c, g):
    return c * lax.rsqrt(jnp.square(c).mean(-1, keepdims=True) + RMS_EPS) * g


def _mla_prep_kernel(x_ref, win_ref, qn_ref, wq_ref, kvn_ref, wkv_ref, cos_ref, sin_ref,
                     q_ref, k_ref, v_ref):
    hw = N_HEADS * LANE
    scale = (MLA_NOPE + MLA_ROPE) ** -0.5
    cos, sin = cos_ref[...], sin_ref[...]
    xw = jnp.dot(x_ref[...].astype(bf16), win_ref[...], preferred_element_type=f32)
    cq = _rms(xw[:, :MLA_Q_RANK], qn_ref[...]).astype(bf16)
    ckv = _rms(xw[:, MLA_Q_RANK:MLA_Q_RANK + MLA_KV_RANK], kvn_ref[...]).astype(bf16)
    off = MLA_Q_RANK + MLA_KV_RANK
    kr = xw[:, off:off + LANE] * cos + xw[:, off + LANE:off + 2 * LANE] * sin
    qq = jnp.dot(cq, wq_ref[...], preferred_element_type=f32)
    kv = jnp.dot(ckv, wkv_ref[...], preferred_element_type=f32)
    ones_lane = lax.broadcasted_iota(jnp.int32, (1, LANE), 1) == MLA_V
    for h in range(N_HEADS):
        blk = slice(h * LANE, (h + 1) * LANE)
        swp = slice(hw + h * LANE, hw + (h + 1) * LANE)
        grp, sub = divmod(h, MLA_GROUP)
        dst = slice(sub * LANE, (sub + 1) * LANE)
        q_ref[grp, :, dst] = ((qq[:, blk] * cos + qq[:, swp] * sin) * scale).astype(bf16)
        k_ref[grp, :, dst] = (kv[:, blk] + kr).astype(bf16)
        v_ref[grp, :, dst] = jnp.where(ones_lane, 1.0, kv[:, swp]).astype(bf16)


def _mla_prep(x, w_in, q_norm, w_uq, kv_norm, w_ukv, seq, tm=512):
    t, d = x.shape
    tm = min(tm, seq)
    hw = N_HEADS * LANE
    half = MLA_ROPE // 2
    dq = MLA_NOPE + MLA_ROPE
    kr_w = w_in[:, MLA_Q_RANK + MLA_KV_RANK:]
    zeros = lambda n: jnp.zeros((d, n), f32)
    kr_a = jnp.concatenate([zeros(MLA_NOPE), kr_w, zeros(LANE - dq)], 1)
    kr_b = jnp.concatenate([zeros(MLA_NOPE), kr_w[:, half:], kr_w[:, :half], zeros(LANE - dq)], 1)
    win = jnp.concatenate([w_in[:, :MLA_Q_RANK + MLA_KV_RANK], kr_a, kr_b], 1).astype(bf16)
    wq3 = w_uq.reshape(MLA_Q_RANK, N_HEADS, dq)
    zq = lambda n: jnp.zeros((MLA_Q_RANK, N_HEADS, n), f32)
    wq_a = jnp.concatenate([wq3, zq(LANE - dq)], 2)
    wq_b = jnp.concatenate([zq(MLA_NOPE), wq3[:, :, MLA_NOPE + half:], wq3[:, :, MLA_NOPE:MLA_NOPE + half],
                            zq(LANE - dq)], 2)
    wq = jnp.concatenate([wq_a.reshape(MLA_Q_RANK, hw), wq_b.reshape(MLA_Q_RANK, hw)], 1).astype(bf16)
    wkv3 = w_ukv.reshape(MLA_KV_RANK, N_HEADS, MLA_NOPE + MLA_V)
    zk = lambda n: jnp.zeros((MLA_KV_RANK, N_HEADS, n), f32)
    wk = jnp.concatenate([wkv3[:, :, :MLA_NOPE], zk(LANE - MLA_NOPE)], 2)
    wv = jnp.concatenate([wkv3[:, :, MLA_NOPE:], zk(LANE - MLA_V)], 2)
    wkv = jnp.concatenate([wk.reshape(MLA_KV_RANK, hw), wv.reshape(MLA_KV_RANK, hw)], 1).astype(bf16)
    freq = ROPE_THETA ** (-jnp.arange(half, dtype=f32) / half)
    ang = jnp.arange(seq)[:, None].astype(f32) * freq[None, :]
    c, s_ = jnp.cos(ang), jnp.sin(ang)
    cos_t = jnp.concatenate([jnp.ones((seq, MLA_NOPE), f32), c, c, jnp.ones((seq, LANE - dq), f32)], 1)
    sin_t = jnp.concatenate([jnp.zeros((seq, MLA_NOPE), f32), -s_, s_, jnp.zeros((seq, LANE - dq), f32)], 1)

    n_pos = seq // tm
    row = lambda width: pl.BlockSpec((tm, width), lambda i: (i, 0))
    fixed = lambda a: pl.BlockSpec(a.shape, lambda i: (0, 0))
    pos = pl.BlockSpec((tm, LANE), lambda i: (i % n_pos, 0))
    qn, kvn = q_norm.reshape(1, -1), kv_norm.reshape(1, -1)
    groups, gw = N_HEADS // MLA_GROUP, MLA_GROUP * LANE
    out_sd = jax.ShapeDtypeStruct((groups, t, gw), bf16)
    out_spec = pl.BlockSpec((groups, tm, gw), lambda i: (0, i, 0))
    return pl.pallas_call(
        _mla_prep_kernel, grid=(t // tm,),
        in_specs=[row(d), fixed(win), fixed(qn), fixed(wq), fixed(kvn), fixed(wkv), pos, pos],
        out_specs=[out_spec] * 3, out_shape=[out_sd] * 3,
        compiler_params=_params("parallel"))(x, win, qn, wq, kvn, wkv, cos_t, sin_t)


def _mla_attn_kernel(q_ref, k_ref, v_ref, o_ref, *, tq, tk, heads):
    qi = pl.program_id(2)
    lanes = [slice(j * LANE, (j + 1) * LANE) for j in range(heads)]
    qs = [q_ref[0, :, ln] for ln in lanes]

    def update(carry, start, mask):
        ss = [lax.dot_general(q, k_ref[0, pl.ds(start, tk), ln], _NT, preferred_element_type=f32)
              for q, ln in zip(qs, lanes)]
        if mask is not None:
            ss = [jnp.where(mask, s, NEG) for s in ss]
        ms = [jnp.maximum(m, s.max(-1, keepdims=True)) for (m, _), s in zip(carry, ss)]
        ps = [jnp.exp(s - m).astype(bf16) for s, m in zip(ss, ms)]
        pvs = [jnp.dot(p, v_ref[0, pl.ds(start, tk), ln], preferred_element_type=f32)
               for p, ln in zip(ps, lanes)]
        return tuple((m_new, jnp.exp(m - m_new) * acc + pv)
                     for (m, acc), m_new, pv in zip(carry, ms, pvs))

    init = tuple((jnp.full((tq, 1), NEG, f32), jnp.zeros((tq, LANE), f32)) for _ in range(heads))
    sub = tq // tk
    carry = lax.fori_loop(0, qi * sub, lambda ki, c: update(c, pl.multiple_of(ki * tk, tk), None), init)
    row = lax.broadcasted_iota(jnp.int32, (tq, tk), 0)
    col = lax.broadcasted_iota(jnp.int32, (tq, tk), 1)
    for j in range(sub):
        carry = update(carry, pl.multiple_of(qi * tq + j * tk, tk), col + j * tk <= row)
    for j in range(heads):
        acc = carry[j][1]
        o_ref[0, :, lanes[j]] = (acc / acc[:, MLA_V:MLA_V + 1]).astype(o_ref.dtype)


def _mla_attention(q, k, v, batch, seq, tq=1024, tk=512):
    groups, t, gw = q.shape
    tq = min(tq, seq)
    nq = seq // tq
    qspec = pl.BlockSpec((1, tq, gw), lambda b, h, i: (h, b * nq + i, 0))
    kspec = pl.BlockSpec((1, seq, gw), lambda b, h, i: (h, b, 0))
    return pl.pallas_call(
        functools.partial(_mla_attn_kernel, tq=tq, tk=min(tk, tq), heads=MLA_GROUP),
        grid=(batch, groups, nq),
        in_specs=[qspec, kspec, kspec], out_specs=qspec,
        out_shape=jax.ShapeDtypeStruct((groups, t, gw), bf16),
        compiler_params=_params("parallel", "parallel", "arbitrary"))(q, k, v)


def _top16_pair(s1, s2):
    w1, w2 = s1, s2
    rank2 = jnp.full(s2.shape, 99.0, f32)
    v1, v2 = [], []
    for k in range(PEER_TOPK):
        m1 = w1.max(axis=0, keepdims=True)
        m2 = w2.max(axis=0, keepdims=True)
        h1 = w1 == m1
        h2 = w2 == m2
        rank2 = jnp.where(h2, float(k), rank2)
        w1 = jnp.where(h1, -jnp.inf, w1)
        w2 = jnp.where(h2, -jnp.inf, w2)
        v1.append(m1)
        v2.append(m2)
    return v1, v2, rank2


def _stack_rows(rows):
    n = len(rows)
    rid = lax.broadcasted_iota(jnp.int32, (n, rows[0].shape[1]), 0)
    out = jnp.broadcast_to(rows[0], (n, rows[0].shape[1]))
    for i in range(1, n):
        out = jnp.where(rid == i, rows[i], out)
    return out


def _route_kernel(x_ref, wq_ref, keys_ref, rank2_ref, crow_ref, e1z_ref, e2_ref, *, tm):
    q = jnp.dot(x_ref[...].astype(bf16), wq_ref[...], preferred_element_type=f32).astype(bf16)
    half = PEER_DKEY // 2
    scores = []
    for hh in range(ROUTE_HEADS):
        qh = q[:, hh * PEER_DKEY:(hh + 1) * PEER_DKEY]
        scores.append((lax.dot_general(keys_ref[hh, 0], qh[:, :half], _NT, preferred_element_type=f32),
                       lax.dot_general(keys_ref[hh, 1], qh[:, half:], _NT, preferred_element_type=f32)))
    rid8 = lax.broadcasted_iota(jnp.int32, (8, LANE), 0)
    for hh, c in [(hh, c) for hh in range(ROUTE_HEADS) for c in range(tm // LANE)]:
        lanes = slice(c * LANE, (c + 1) * LANE)
        s1, s2 = scores[hh][0][:, lanes], scores[hh][1][:, lanes]
        v1, v2, rank2 = _top16_pair(s1, s2)
        v2_all = _stack_rows(v2)
        cands = [v1[0] + v2_all]
        for a in range(1, 8):
            cands.append(jnp.where(rid8 < PEER_TOPK // (a + 1), v1[a] + v2_all[:8], -jnp.inf))
        cands.append(_stack_rows(v1[8:]) + v2[0])
        cand = jnp.concatenate(cands, axis=0)
        work = cand
        for _ in range(PEER_TOPK):
            tau = work.max(axis=0, keepdims=True)
            work = jnp.where(work == tau, -jnp.inf, work)
        cmax = v1[0] + v2[0]
        z = jnp.where(cand >= tau, jnp.exp(cand - cmax), 0.0).sum(axis=0, keepdims=True)
        crow = jnp.zeros(s1.shape, f32)
        for b in range(PEER_TOPK):
            crow = crow + jnp.where(s1 + v2[b] >= tau, 1.0, 0.0)
        rank2_ref[0, hh, :, lanes] = rank2.astype(bf16)
        crow_ref[0, hh, :, lanes] = crow
        e1z_ref[0, hh, :, lanes] = jnp.exp(s1 - v1[0]) / z
        e2_ref[0, hh, :, lanes] = jnp.exp(s2 - v2[0]).astype(bf16)


def _peer_route(x, w_q, keys, tm=ROUTE_TOKENS):
    t, d = x.shape
    tm = min(tm, t)
    hb = ROUTE_HEADS
    spec = pl.BlockSpec((1, hb, PEER_KEYS, tm), lambda i, h: (i, h, 0, 0))
    sd = lambda dt: jax.ShapeDtypeStruct((t // tm, PEER_HEADS, PEER_KEYS, tm), dt)
    return pl.pallas_call(
        functools.partial(_route_kernel, tm=tm), grid=(t // tm, PEER_HEADS // hb),
        in_specs=[pl.BlockSpec((tm, d), lambda i, h: (i, 0)),
                  pl.BlockSpec((d, hb * PEER_DKEY), lambda i, h: (0, h)),
                  pl.BlockSpec((hb, 2, PEER_KEYS, PEER_DKEY // 2), lambda i, h: (h, 0, 0, 0))],
        out_specs=[spec] * 4, out_shape=[sd(bf16), sd(f32), sd(f32), sd(bf16)],
        compiler_params=_params("parallel", "arbitrary"))(x, w_q, keys)


def _peer_dense_kernel(x_ref, u0_ref, u1_ref, vt0_ref, vt1_ref, rank2_in, crow_ref, e1z_ref, e2_in,
                       g_ref, b_ref, o_ref, acc_ref, xt_ref, ht0, ht1, gt0, gt1, rank2_ref, e2_ref,
                       *, rows, tb, lc, mw, alpha):
    k = pl.program_id(1)
    ec = rows * PEER_KEYS

    @pl.when(k == 0)
    def _():
        acc_ref[...] = jnp.zeros_like(acc_ref)
        ht1[...] = jnp.zeros_like(ht1)
        gt0[...] = jnp.zeros_like(gt0)
        xt_ref[...] = x_ref[...].T.astype(bf16)
        rank2_ref[...] = rank2_in[0]
        e2_ref[...] = e2_in[0]

    zero = jnp.zeros((), bf16)
    kt = PEER_KEYS // BF16_ROWS

    def up(half, ht, cw):
        lanes = slice(cw * mw, (cw + 1) * mw)
        ht[:, lanes] = jnp.dot((u0_ref, u1_ref)[half][...], xt_ref[:, lanes], preferred_element_type=f32)

    n_tiles = pl.num_programs(1) - 1

    def gate(ht, tile, half, gt, cw):
        base = pl.multiple_of(jnp.clip(tile, 0, n_tiles - 1) * F32_ROWS, F32_ROWS)
        for r in range(rows):
            keys = slice(r * PEER_KEYS, (r + 1) * PEER_KEYS)
            row = slice(half * rows + r, half * rows + r + 1)
            for c in range(cw * mw // lc, (cw + 1) * mw // lc):
                lanes = slice(c * lc, (c + 1) * lc)
                w = None
                for h in range(PEER_HEADS):
                    cr = crow_ref[0, h, pl.ds(base, F32_ROWS), lanes][row]
                    ez = e1z_ref[0, h, pl.ds(base, F32_ROWS), lanes][row]
                    cr = jnp.broadcast_to(cr, (BF16_ROWS, lc)).astype(bf16)[None]
                    ez = jnp.broadcast_to(ez, (BF16_ROWS, lc)).astype(bf16)[None]
                    term = jnp.where(rank2_ref[h, :, :, lanes] < cr, e2_ref[h, :, :, lanes] * ez, zero)
                    w = term if w is None else w + term
                hv = ht[keys, lanes]
                gelu = 0.5 * hv * (1.0 + lax.erf(hv * (2.0 ** -0.5)))
                gt[r * kt:(r + 1) * kt, :, lanes] = gelu.astype(bf16).reshape(kt, BF16_ROWS, lc) * w

    def down(half, gt, cw):
        lanes = slice(cw * mw, (cw + 1) * mw)
        acc_ref[:, lanes] += jnp.dot((vt0_ref, vt1_ref)[half][0], gt[:, :, lanes].reshape(ec, mw),
                                     preferred_element_type=f32)

    for cw in range(tb // mw):
        up(0, ht0, cw)
        gate(ht1, k - 1, 1, gt1, cw)
        down(0, gt0, cw)
    for cw in range(tb // mw):
        up(1, ht1, cw)
        gate(ht0, k, 0, gt0, cw)
        down(1, gt1, cw)

    @pl.when(k == pl.num_programs(1) - 1)
    def _():
        z = alpha * x_ref[...] + acc_ref[...].T
        o_ref[...] = _layer_norm(z, g_ref[...], b_ref[...])


def _peer_dense(x, u, vt, routing, g, b, alpha, lc=128):
    t, d = x.shape
    e = u.shape[0]
    ec = DENSE_EC
    rows = ec // PEER_KEYS
    assert 2 * rows == F32_ROWS, "a pair of expert blocks must span one 8-row f32 tile of sub-key-1 rows"
    nk = e // (2 * ec)
    kt = PEER_KEYS // BF16_ROWS
    rank2, crow, e1z, e2 = routing
    nt, _, _, tb = crow.shape
    rank2, e2 = (a.reshape(nt, PEER_HEADS, kt, BF16_ROWS, tb) for a in (rank2, e2))
    full = pl.BlockSpec((1, PEER_HEADS, kt, BF16_ROWS, tb), lambda i, k: (i, 0, 0, 0, 0))
    rowtab = pl.BlockSpec((1, PEER_HEADS, PEER_KEYS, tb), lambda i, k: (i, 0, 0, 0))
    fixed = pl.BlockSpec((1, d), lambda i, k: (0, 0))
    xspec = pl.BlockSpec((tb, d), lambda i, k: (i, 0))
    xin = xspec
    gt_scr = pltpu.VMEM((ec // BF16_ROWS, BF16_ROWS, tb), bf16)
    u_blk = lambda half: pl.BlockSpec((ec, d), lambda i, k: (2 * jnp.minimum(k, nk - 1) + half, 0))
    vt_blk = lambda half: pl.BlockSpec((1, d, ec), lambda i, k: (2 * jnp.maximum(k - 1, 0) + half, 0, 0))
    return pl.pallas_call(
        functools.partial(_peer_dense_kernel, rows=rows, tb=tb, lc=lc, mw=MXU_WIDTH, alpha=alpha),
        grid=(t // tb, nk + 1),
        in_specs=[xin, u_blk(0), u_blk(1), vt_blk(0), vt_blk(1),
                  full, rowtab, rowtab, full, fixed, fixed],
        out_specs=xspec, out_shape=jax.ShapeDtypeStruct((t, d), f32),
        scratch_shapes=[pltpu.VMEM((d, tb), f32), pltpu.VMEM((d, tb), bf16),
                        pltpu.VMEM((ec, tb), f32), pltpu.VMEM((ec, tb), f32), gt_scr, gt_scr,
                        pltpu.VMEM((PEER_HEADS, kt, BF16_ROWS, tb), bf16),
                        pltpu.VMEM((PEER_HEADS, kt, BF16_ROWS, tb), bf16)],
        compiler_params=_params("parallel", "arbitrary"))(
            x, u, u, vt, vt, rank2, crow, e1z, e2, g.reshape(1, d), b.reshape(1, d))


def _swa_layer(x, batch, seq, w_in, sinks, w_out, bias, g, b, alpha):
    feat = (N_HEADS + 2 * SWA_KV_HEADS) * HEAD_DIM
    d_q = N_HEADS * HEAD_DIM
    kvw = SWA_KV_HEADS * HEAD_DIM
    proj = _mm(x, w_in.astype(bf16), bf16, 512, feat).reshape(batch, seq, feat)
    sink = jnp.repeat(sinks.astype(f32), HEAD_DIM).reshape(1, d_q)
    o = _band_attention(proj, bias, 1, feat, 0, d_q // kvw, d_q // kvw + 1, kvw,
                        N_HEADS // SWA_KV_HEADS, sink)
    return _proj_ln([o], [], [1], w_out.astype(bf16), x, g, b, alpha)


def _dil_layer(x, batch, seq, w_in, w_out, biases, g, b, alpha):
    d_q = N_HEADS * HEAD_DIM
    feat = 3 * d_q
    w_in = w_in.astype(bf16)
    outs, lses, dils = [], [], []
    for gi, (window, dil) in enumerate(DIL_PATTERNS):
        proj = _mm_dil(x, w_in[:, gi * feat:(gi + 1) * feat], dil).reshape(batch, seq // dil, dil * feat)
        o, lse = _band_attention(proj, biases[gi], dil, feat, 0, 1, 2, d_q, 1)
        outs.append(o)
        lses.append(lse)
        dils.append(dil)
    return _proj_ln(outs, lses, dils, w_out.astype(bf16), x, g, b, alpha)


def _mla_layer(x, batch, seq, w_in, q_norm, w_uq, kv_norm, w_ukv, w_out, g, b, alpha):
    q, k, v = _mla_prep(x, w_in, q_norm, w_uq, kv_norm, w_ukv, seq)
    o = _mla_attention(q, k, v, batch, seq)
    d = w_out.shape[1]
    w3 = w_out.reshape(N_HEADS, MLA_V, d)
    w_pad = jnp.concatenate([w3, jnp.zeros((N_HEADS, LANE - MLA_V, d), f32)], 1)
    return _proj_ln([o], [], [1], w_pad.reshape(N_HEADS * LANE, d).astype(bf16), x, g, b, alpha)


def _peer_layer(x, w_q, keys, u, v, g, b, alpha):
    routing = _peer_route(x, w_q.astype(bf16), keys.astype(bf16))
    e, d = v.shape
    vt = v.astype(bf16).reshape(e // DENSE_EC, DENSE_EC, d).transpose(0, 2, 1)
    return _peer_dense(x, u.astype(bf16), vt, routing, g, b, alpha)


def kernel(x, rel_bias, ln_g, ln_b, swa_w_in, swa_sinks, swa_w_out, dil_w_in, dil_w_out,
           mla_w_in, mla_q_norm, mla_w_uq, mla_kv_norm, mla_w_ukv, mla_w_out,
           peer_w_q, peer_keys, peer_u, peer_v):
    batch, seq, d = x.shape
    depth = ln_g.shape[0]
    alpha = (2 * depth) ** 0.25
    assert seq % (DIL_PATTERNS[-1][1] * BLOCK) == 0, "sequence must be a whole number of dilation segments"
    swa_bias = _band_bias(rel_bias, SWA_WINDOW - 1, 1)
    dil_bias = [_band_bias(rel_bias, window // dil, dil) for window, dil in DIL_PATTERNS]
    h = x.reshape(batch * seq, d)
    for i in range(depth):
        kind, j = i % 3, i // 3
        if kind == 0:
            h = _swa_layer(h, batch, seq, swa_w_in[j], swa_sinks[j], swa_w_out[j], swa_bias,
                           ln_g[i, 0], ln_b[i, 0], alpha)
        elif kind == 1:
            h = _dil_layer(h, batch, seq, dil_w_in[j], dil_w_out[j], dil_bias,
                           ln_g[i, 0], ln_b[i, 0], alpha)
        else:
            h = _mla_layer(h, batch, seq, mla_w_in[j], mla_q_norm[j], mla_w_uq[j], mla_kv_norm[j],
                           mla_w_ukv[j], mla_w_out[j], ln_g[i, 0], ln_b[i, 0], alpha)
        h = _peer_layer(h, peer_w_q[i], peer_keys[i], peer_u[i], peer_v[i], ln_g[i, 1], ln_b[i, 1], alpha)
    return h.reshape(batch, seq, d)
```

```python
import functools
import math

import jax
import jax.numpy as jnp
from jax import lax
from jax.experimental import pallas as pl
from jax.experimental.pallas import tpu as pltpu

f32 = jnp.float32
bf16 = jnp.bfloat16

N_HEADS = 16
HEAD_DIM = 64
BLOCK = 128
SWA_KV_HEADS = 2
SWA_WINDOW = 128
DIL_PATTERNS = ((128, 1), (512, 4), (2048, 16))
MLA_Q_RANK = 256
MLA_KV_RANK = 128
MLA_NOPE = 64
MLA_ROPE = 32
MLA_V = 64
ROPE_THETA = 10000.0
REL_BUCKETS = 32
REL_MAX_DIST = 2048
PEER_HEADS = 8
PEER_KEYS = 128
PEER_DKEY = 256
PEER_TOPK = 16
LN_EPS = 1e-5
RMS_EPS = 1e-6
NEG = -1e30

LANE = 128
BF16_ROWS = 16
F32_ROWS = 8
MXU_WIDTH = 256
MLA_GROUP = 2
GATE_KEY_TILES = 2
ROUTE_HEADS = 2
ROUTE_TOKENS = 512
HEAD_BATCH = 16
DENSE_EC = 4 * PEER_KEYS
VMEM_LIMIT = 56 * 1024 * 1024

_NT = (((1,), (1,)), ((), ()))


def _params(*sem):
    return pltpu.CompilerParams(dimension_semantics=sem, vmem_limit_bytes=VMEM_LIMIT)


def _mm_kernel(x_ref, w_ref, o_ref):
    o_ref[...] = jnp.dot(x_ref[...].astype(bf16), w_ref[...],
                         preferred_element_type=f32).astype(o_ref.dtype)


def _mm(x, w, out_dtype, tm, tn):
    m, k = x.shape
    n = w.shape[1]
    tm, tn = min(tm, m), min(tn, n)
    return pl.pallas_call(
        _mm_kernel, grid=(m // tm, n // tn),
        in_specs=[pl.BlockSpec((tm, k), lambda i, j: (i, 0)),
                  pl.BlockSpec((k, tn), lambda i, j: (0, j))],
        out_specs=pl.BlockSpec((tm, tn), lambda i, j: (i, j)),
        out_shape=jax.ShapeDtypeStruct((m, n), out_dtype),
        compiler_params=_params("parallel", "parallel"))(x, w)


def _rel_bucket(dist):
    max_exact = REL_BUCKETS // 2
    n = jnp.maximum(dist, 0)
    nf = jnp.maximum(n, 1).astype(f32)
    large = max_exact + (jnp.log(nf / max_exact) / math.log(REL_MAX_DIST / max_exact)
                         * (REL_BUCKETS - max_exact)).astype(jnp.int32)
    large = jnp.minimum(large, REL_BUCKETS - 1)
    return jnp.where(n < max_exact, n, large)


def _bias_kernel(rel_ref, bucket_ref, valid_ref, o_ref):
    bucket = bucket_ref[...]
    for h in range(N_HEADS):
        acc = jnp.zeros(bucket.shape, f32)
        for b in range(REL_BUCKETS):
            acc = jnp.where(bucket == b, rel_ref[b, h], acc)
        for variant in range(2):
            o_ref[variant, h] = jnp.where(valid_ref[variant] != 0, acc, NEG)


def _band_bias(rel_bias, max_dist, dilation):
    qi = jnp.arange(BLOCK)[:, None]
    kj = jnp.arange(2 * BLOCK)[None, :]
    dist = BLOCK + qi - kj
    bucket = _rel_bucket(dist * dilation).astype(jnp.int32)
    valid = (dist >= 0) & (dist <= max_dist)
    valid = jnp.stack([valid & (kj >= BLOCK), valid]).astype(jnp.int32)
    return pl.pallas_call(
        _bias_kernel,
        in_specs=[pl.BlockSpec(memory_space=pltpu.SMEM),
                  pl.BlockSpec(memory_space=pltpu.VMEM),
                  pl.BlockSpec(memory_space=pltpu.VMEM)],
        out_specs=pl.BlockSpec(memory_space=pltpu.VMEM),
        out_shape=jax.ShapeDtypeStruct((2, N_HEADS, BLOCK, 2 * BLOCK), f32))(rel_bias, bucket, valid)


def _band_kernel(q_ref, kp_ref, kc_ref, vp_ref, vc_ref, bias_ref, *rest, group, with_sink):
    if with_sink:
        sink_ref, o_ref = rest
    else:
        o_ref, lse_ref = rest
    scale = HEAD_DIM ** -0.5
    for h0 in range(0, N_HEADS, HEAD_BATCH):
        heads = range(h0, h0 + HEAD_BATCH)
        qs = [slice(h * HEAD_DIM, (h + 1) * HEAD_DIM) for h in heads]
        ks = [slice((h // group) * HEAD_DIM, (h // group + 1) * HEAD_DIM) for h in heads]
        kk = [jnp.concatenate([kp_ref[0, :, c], kc_ref[0, :, c]], axis=0) for c in ks]
        vv = [jnp.concatenate([vp_ref[0, :, c], vc_ref[0, :, c]], axis=0) for c in ks]
        ss = [lax.dot_general(q_ref[0, :, c], k, _NT, preferred_element_type=f32) * scale + bias_ref[0, h]
              for h, c, k in zip(heads, qs, kk)]
        ms = [s.max(-1, keepdims=True) for s in ss]
        ps = [jnp.exp(s - m) for s, m in zip(ss, ms)]
        ls = [p.sum(-1, keepdims=True) for p in ps]
        os_ = [jnp.dot(p.astype(bf16), v, preferred_element_type=f32) / l for p, v, l in zip(ps, vv, ls)]
        lses = [m + jnp.log(l) for m, l in zip(ms, ls)]
        for c, o, lse in zip(qs, os_, lses):
            if with_sink:
                o = o * jax.nn.sigmoid(lse - sink_ref[:, c])
            else:
                lse_ref[0, :, c] = jnp.broadcast_to(lse, (BLOCK, HEAD_DIM))
            o_ref[0, :, c] = o.astype(o_ref.dtype)


def _band_attention(view, bias, dil, feat, q_col, k_col, v_col, kv_width, group, sink=None):
    b, m_len, _ = view.shape
    d_q = N_HEADS * HEAD_DIM
    nb = m_len // BLOCK
    qpr, kpr = feat // d_q, feat // kv_width

    def cur(col, per_row):
        return lambda bi, r, n: (bi, n, r * per_row + col)

    def prev(col, per_row):
        return lambda bi, r, n: (bi, jnp.maximum(n - 1, 0), r * per_row + col)

    in_specs = [pl.BlockSpec((1, BLOCK, d_q), cur(q_col, qpr)),
                pl.BlockSpec((1, BLOCK, kv_width), prev(k_col, kpr)),
                pl.BlockSpec((1, BLOCK, kv_width), cur(k_col, kpr)),
                pl.BlockSpec((1, BLOCK, kv_width), prev(v_col, kpr)),
                pl.BlockSpec((1, BLOCK, kv_width), cur(v_col, kpr)),
                pl.BlockSpec((1, N_HEADS, BLOCK, 2 * BLOCK),
                             lambda bi, r, n: (jnp.minimum(n, 1), 0, 0, 0))]
    args = [view, view, view, view, view, bias]
    o_spec = pl.BlockSpec((1, BLOCK, d_q), lambda bi, r, n: (bi, n, r))
    o_shape = jax.ShapeDtypeStruct((b, m_len, dil * d_q), bf16)
    if sink is not None:
        in_specs.append(pl.BlockSpec((1, d_q), lambda bi, r, n: (0, 0)))
        args.append(sink)
        out_specs, out_shape = o_spec, o_shape
    else:
        out_specs = [o_spec, o_spec]
        out_shape = [o_shape, jax.ShapeDtypeStruct((b, m_len, dil * d_q), f32)]
    out = pl.pallas_call(
        functools.partial(_band_kernel, group=group, with_sink=sink is not None),
        grid=(b, dil, nb), in_specs=in_specs, out_specs=out_specs, out_shape=out_shape,
        compiler_params=_params("parallel", "parallel", "arbitrary"))(*args)
    if sink is not None:
        return out.reshape(b * m_len, dil * d_q)
    return out[0].reshape(b * m_len, dil * d_q), out[1].reshape(b * m_len, dil * d_q)


def _layer_norm(z, g, b):
    mu = z.mean(-1, keepdims=True)
    zc = z - mu
    var = jnp.square(zc).mean(-1, keepdims=True)
    return zc * lax.rsqrt(var + LN_EPS) * g + b


def _proj_ln_kernel(*refs, dils, alpha):
    n_pat = len(dils)
    o_refs = refs[:n_pat]
    lse_refs = refs[n_pat:2 * n_pat] if n_pat > 1 else ()
    n_in = len(o_refs) + len(lse_refs)
    w_ref, x_ref, g_ref, b_ref, out_ref = refs[n_in:n_in + 5]
    scratch = refs[n_in + 5:]
    if n_pat == 1 and len(o_refs[0].shape) == 3:
        groups, _, width = o_refs[0].shape
        y = sum(jnp.dot(o_refs[0][gi], w_ref[gi * width:(gi + 1) * width, :], preferred_element_type=f32)
                for gi in range(groups))
        out_ref[...] = _layer_norm(alpha * x_ref[...] + y, g_ref[...], b_ref[...])
        return
    if n_pat == 1:
        o = o_refs[0][...]
    else:
        kw = w_ref.shape[0]
        mix_ref = scratch[-1]

        def natural(ref, dil, scr, c):
            if dil == 1:
                return ref[:, c * LANE:(c + 1) * LANE].astype(f32)
            rows = ref.shape[0]
            for r in range(dil):
                scr[c, pl.ds(r, rows, stride=dil), :] = (
                    ref[:, r * kw + c * LANE:r * kw + (c + 1) * LANE].astype(f32))
            return scr[c]

        for c in range(kw // LANE):
            lses = [natural(r, dl, scratch[2 * i], c) for i, (r, dl) in enumerate(zip(lse_refs, dils))]
            mx = functools.reduce(jnp.maximum, lses)
            es = [jnp.exp(l - mx) for l in lses]
            num = sum(e * natural(r, dl, scratch[2 * i + 1], c)
                      for i, (e, r, dl) in enumerate(zip(es, o_refs, dils)))
            mix_ref[:, c * LANE:(c + 1) * LANE] = (num / sum(es)).astype(bf16)
        o = mix_ref[...]
    y = jnp.dot(o, w_ref[...], preferred_element_type=f32)
    out_ref[...] = _layer_norm(alpha * x_ref[...] + y, g_ref[...], b_ref[...])


def _proj_ln(os_, lses, dils, w_out, x, g, b, alpha, tm=512):
    t, d = x.shape
    tm = min(tm, t)
    kw = w_out.shape[0]
    row = lambda width: pl.BlockSpec((tm, width), lambda i: (i, 0))
    dilated = lambda dl: pl.BlockSpec((tm // dl, dl * kw), lambda i: (i, 0))
    fixed = lambda shape: pl.BlockSpec(shape, lambda i: (0, 0))
    if os_[0].ndim == 3:
        o_specs = [pl.BlockSpec((os_[0].shape[0], tm, os_[0].shape[2]), lambda i: (0, i, 0))]
    else:
        o_specs = [dilated(dl) for dl in dils]
    in_specs = (o_specs + [dilated(dl) for dl in dils[:len(lses)]]
                + [fixed((kw, d)), row(d), fixed((1, d)), fixed((1, d))])
    scratch = [pltpu.VMEM((kw // LANE, tm, LANE), f32)] * (2 * len(lses))
    if lses:
        scratch.append(pltpu.VMEM((tm, kw), bf16))
    return pl.pallas_call(
        functools.partial(_proj_ln_kernel, dils=tuple(dils), alpha=alpha),
        grid=(t // tm,), in_specs=in_specs, out_specs=row(d),
        out_shape=jax.ShapeDtypeStruct((t, d), f32), scratch_shapes=scratch,
        compiler_params=_params("parallel"))(*os_, *lses, w_out, x, g.reshape(1, d), b.reshape(1, d))


def _mm_dil_kernel(x_ref, w_ref, o_ref, res_ref, *, dil):
    n = w_ref.shape[1]
    res = jnp.dot(x_ref[...].astype(bf16), w_ref[...], preferred_element_type=f32)
    if dil == 1:
        o_ref[...] = res.astype(o_ref.dtype)
        return
    rows = o_ref.shape[0]
    for c in range(n // LANE):
        res_ref[c] = res[:, c * LANE:(c + 1) * LANE]
        for r in range(dil):
            o_ref[:, r * n + c * LANE:r * n + (c + 1) * LANE] = (
                res_ref[c, pl.ds(r, rows, stride=dil), :].astype(o_ref.dtype))


def _mm_dil(x, w, dil, tm=512):
    t, k = x.shape
    n = w.shape[1]
    tm = min(tm, t)
    return pl.pallas_call(
        functools.partial(_mm_dil_kernel, dil=dil), grid=(t // tm,),
        in_specs=[pl.BlockSpec((tm, k), lambda i: (i, 0)), pl.BlockSpec((k, n), lambda i: (0, 0))],
        out_specs=pl.BlockSpec((tm // dil, dil * n), lambda i: (i, 0)),
        out_shape=jax.ShapeDtypeStruct((t // dil, dil * n), bf16),
        scratch_shapes=[pltpu.VMEM((n // LANE, tm, LANE), f32)],
        compiler_params=_params("parallel"))(x, w)


def _rms(c, g):
    return c * lax.rsqrt(jnp.square(c).mean(-1, keepdims=True) + RMS_EPS) * g


def _mla_prep_kernel(x_ref, win_ref, qn_ref, wq_ref, kvn_ref, wkv_ref, cos_ref, sin_ref,
                     q_ref, k_ref, v_ref):
    hw = N_HEADS * LANE
    scale = (MLA_NOPE + MLA_ROPE) ** -0.5
    cos, sin = cos_ref[...], sin_ref[...]
    xw = jnp.dot(x_ref[...].astype(bf16), win_ref[...], preferred_element_type=f32)
    cq = _rms(xw[:, :MLA_Q_RANK], qn_ref[...]).astype(bf16)
    ckv = _rms(xw[:, MLA_Q_RANK:MLA_Q_RANK + MLA_KV_RANK], kvn_ref[...]).astype(bf16)
    off = MLA_Q_RANK + MLA_KV_RANK
    kr = xw[:, off:off + LANE] * cos + xw[:, off + LANE:off + 2 * LANE] * sin
    qq = jnp.dot(cq, wq_ref[...], preferred_element_type=f32)
    kv = jnp.dot(ckv, wkv_ref[...], preferred_element_type=f32)
    ones_lane = lax.broadcasted_iota(jnp.int32, (1, LANE), 1) == MLA_V
    for h in range(N_HEADS):
        blk = slice(h * LANE, (h + 1) * LANE)
        swp = slice(hw + h * LANE, hw + (h + 1) * LANE)
        grp, sub = divmod(h, MLA_GROUP)
        dst = slice(sub * LANE, (sub + 1) * LANE)
        q_ref[grp, :, dst] = ((qq[:, blk] * cos + qq[:, swp] * sin) * scale).astype(bf16)
        k_ref[grp, :, dst] = (kv[:, blk] + kr).astype(bf16)
        v_ref[grp, :, dst] = jnp.where(ones_lane, 1.0, kv[:, swp]).astype(bf16)


def _mla_prep(x, w_in, q_norm, w_uq, kv_norm, w_ukv, seq, tm=512):
    t, d = x.shape
    tm = min(tm, seq)
    hw = N_HEADS * LANE
    half = MLA_ROPE // 2
    dq = MLA_NOPE + MLA_ROPE
    kr_w = w_in[:, MLA_Q_RANK + MLA_KV_RANK:]
    zeros = lambda n: jnp.zeros((d, n), f32)
    kr_a = jnp.concatenate([zeros(MLA_NOPE), kr_w, zeros(LANE - dq)], 1)
    kr_b = jnp.concatenate([zeros(MLA_NOPE), kr_w[:, half:], kr_w[:, :half], zeros(LANE - dq)], 1)
    win = jnp.concatenate([w_in[:, :MLA_Q_RANK + MLA_KV_RANK], kr_a, kr_b], 1).astype(bf16)
    wq3 = w_uq.reshape(MLA_Q_RANK, N_HEADS, dq)
    zq = lambda n: jnp.zeros((MLA_Q_RANK, N_HEADS, n), f32)
    wq_a = jnp.concatenate([wq3, zq(LANE - dq)], 2)
    wq_b = jnp.concatenate([zq(MLA_NOPE), wq3[:, :, MLA_NOPE + half:], wq3[:, :, MLA_NOPE:MLA_NOPE + half],
                            zq(LANE - dq)], 2)
    wq = jnp.concatenate([wq_a.reshape(MLA_Q_RANK, hw), wq_b.reshape(MLA_Q_RANK, hw)], 1).astype(bf16)
    wkv3 = w_ukv.reshape(MLA_KV_RANK, N_HEADS, MLA_NOPE + MLA_V)
    zk = lambda n: jnp.zeros((MLA_KV_RANK, N_HEADS, n), f32)
    wk = jnp.concatenate([wkv3[:, :, :MLA_NOPE], zk(LANE - MLA_NOPE)], 2)
    wv = jnp.concatenate([wkv3[:, :, MLA_NOPE:], zk(LANE - MLA_V)], 2)
    wkv = jnp.concatenate([wk.reshape(MLA_KV_RANK, hw), wv.reshape(MLA_KV_RANK, hw)], 1).astype(bf16)
    freq = ROPE_THETA ** (-jnp.arange(half, dtype=f32) / half)
    ang = jnp.arange(seq)[:, None].astype(f32) * freq[None, :]
    c, s_ = jnp.cos(ang), jnp.sin(ang)
    cos_t = jnp.concatenate([jnp.ones((seq, MLA_NOPE), f32), c, c, jnp.ones((seq, LANE - dq), f32)], 1)
    sin_t = jnp.concatenate([jnp.zeros((seq, MLA_NOPE), f32), -s_, s_, jnp.zeros((seq, LANE - dq), f32)], 1)

    n_pos = seq // tm
    row = lambda width: pl.BlockSpec((tm, width), lambda i: (i, 0))
    fixed = lambda a: pl.BlockSpec(a.shape, lambda i: (0, 0))
    pos = pl.BlockSpec((tm, LANE), lambda i: (i % n_pos, 0))
    qn, kvn = q_norm.reshape(1, -1), kv_norm.reshape(1, -1)
    groups, gw = N_HEADS // MLA_GROUP, MLA_GROUP * LANE
    out_sd = jax.ShapeDtypeStruct((groups, t, gw), bf16)
    out_spec = pl.BlockSpec((groups, tm, gw), lambda i: (0, i, 0))
    return pl.pallas_call(
        _mla_prep_kernel, grid=(t // tm,),
        in_specs=[row(d), fixed(win), fixed(qn), fixed(wq), fixed(kvn), fixed(wkv), pos, pos],
        out_specs=[out_spec] * 3, out_shape=[out_sd] * 3,
        compiler_params=_params("parallel"))(x, win, qn, wq, kvn, wkv, cos_t, sin_t)


def _mla_attn_kernel(q_ref, k_ref, v_ref, o_ref, *, tq, tk, heads):
    qi = pl.program_id(2)
    lanes = [slice(j * LANE, (j + 1) * LANE) for j in range(heads)]
    qs = [q_ref[0, :, ln] for ln in lanes]

    def update(carry, start, mask):
        ss = [lax.dot_general(q, k_ref[0, pl.ds(start, tk), ln], _NT, preferred_element_type=f32)
              for q, ln in zip(qs, lanes)]
        if mask is not None:
            ss = [jnp.where(mask, s, NEG) for s in ss]
        ms = [jnp.maximum(m, s.max(-1, keepdims=True)) for (m, _), s in zip(carry, ss)]
        ps = [jnp.exp(s - m).astype(bf16) for s, m in zip(ss, ms)]
        pvs = [jnp.dot(p, v_ref[0, pl.ds(start, tk), ln], preferred_element_type=f32)
               for p, ln in zip(ps, lanes)]
        return tuple((m_new, jnp.exp(m - m_new) * acc + pv)
                     for (m, acc), m_new, pv in zip(carry, ms, pvs))

    init = tuple((jnp.full((tq, 1), NEG, f32), jnp.zeros((tq, LANE), f32)) for _ in range(heads))
    sub = tq // tk
    carry = lax.fori_loop(0, qi * sub, lambda ki, c: update(c, pl.multiple_of(ki * tk, tk), None), init)
    row = lax.broadcasted_iota(jnp.int32, (tq, tk), 0)
    col = lax.broadcasted_iota(jnp.int32, (tq, tk), 1)
    for j in range(sub):
        carry = update(carry, pl.multiple_of(qi * tq + j * tk, tk), col + j * tk <= row)
    for j in range(heads):
        acc = carry[j][1]
        o_ref[0, :, lanes[j]] = (acc / acc[:, MLA_V:MLA_V + 1]).astype(o_ref.dtype)


def _mla_attention(q, k, v, batch, seq, tq=1024, tk=512):
    groups, t, gw = q.shape
    tq = min(tq, seq)
    nq = seq // tq
    qspec = pl.BlockSpec((1, tq, gw), lambda b, h, i: (h, b * nq + i, 0))
    kspec = pl.BlockSpec((1, seq, gw), lambda b, h, i: (h, b, 0))
    return pl.pallas_call(
        functools.partial(_mla_attn_kernel, tq=tq, tk=min(tk, tq), heads=MLA_GROUP),
        grid=(batch, groups, nq),
        in_specs=[qspec, kspec, kspec], out_specs=qspec,
        out_shape=jax.ShapeDtypeStruct((groups, t, gw), bf16),
        compiler_params=_params("parallel", "parallel", "arbitrary"))(q, k, v)


def _top16_pair(s1, s2):
    w1, w2 = s1, s2
    rank2 = jnp.full(s2.shape, 99.0, f32)
    v1, v2 = [], []
    for k in range(PEER_TOPK):
        m1 = w1.max(axis=0, keepdims=True)
        m2 = w2.max(axis=0, keepdims=True)
        h1 = w1 == m1
        h2 = w2 == m2
        rank2 = jnp.where(h2, float(k), rank2)
        w1 = jnp.where(h1, -jnp.inf, w1)
        w2 = jnp.where(h2, -jnp.inf, w2)
        v1.append(m1)
        v2.append(m2)
    return v1, v2, rank2


def _stack_rows(rows):
    n = len(rows)
    rid = lax.broadcasted_iota(jnp.int32, (n, rows[0].shape[1]), 0)
    out = jnp.broadcast_to(rows[0], (n, rows[0].shape[1]))
    for i in range(1, n):
        out = jnp.where(rid == i, rows[i], out)
    return out


def _route_kernel(x_ref, wq_ref, keys_ref, rank2_ref, crow_ref, e1z_ref, e2_ref, *, tm):
    q = jnp.dot(x_ref[...].astype(bf16), wq_ref[...], preferred_element_type=f32).astype(bf16)
    half = PEER_DKEY // 2
    scores = []
    for hh in range(ROUTE_HEADS):
        qh = q[:, hh * PEER_DKEY:(hh + 1) * PEER_DKEY]
        scores.append((lax.dot_general(keys_ref[hh, 0], qh[:, :half], _NT, preferred_element_type=f32),
                       lax.dot_general(keys_ref[hh, 1], qh[:, half:], _NT, preferred_element_type=f32)))
    rid8 = lax.broadcasted_iota(jnp.int32, (8, LANE), 0)
    for hh, c in [(hh, c) for hh in range(ROUTE_HEADS) for c in range(tm // LANE)]:
        lanes = slice(c * LANE, (c + 1) * LANE)
        s1, s2 = scores[hh][0][:, lanes], scores[hh][1][:, lanes]
        v1, v2, rank2 = _top16_pair(s1, s2)
        v2_all = _stack_rows(v2)
        cands = [v1[0] + v2_all]
        for a in range(1, 8):
            cands.append(jnp.where(rid8 < PEER_TOPK // (a + 1), v1[a] + v2_all[:8], -jnp.inf))
        cands.append(_stack_rows(v1[8:]) + v2[0])
        cand = jnp.concatenate(cands, axis=0)
        work = cand
        for _ in range(PEER_TOPK):
            tau = work.max(axis=0, keepdims=True)
            work = jnp.where(work == tau, -jnp.inf, work)
        cmax = v1[0] + v2[0]
        z = jnp.where(cand >= tau, jnp.exp(cand - cmax), 0.0).sum(axis=0, keepdims=True)
        crow = jnp.zeros(s1.shape, f32)
        for b in range(PEER_TOPK):
            crow = crow + jnp.where(s1 + v2[b] >= tau, 1.0, 0.0)
        rank2_ref[0, hh, :, lanes] = rank2.astype(bf16)
        crow_ref[0, hh, :, lanes] = crow
        e1z_ref[0, hh, :, lanes] = jnp.exp(s1 - v1[0]) / z
        e2_ref[0, hh, :, lanes] = jnp.exp(s2 - v2[0]).astype(bf16)


def _peer_route(x, w_q, keys, tm=ROUTE_TOKENS):
    t, d = x.shape
    tm = min(tm, t)
    hb = ROUTE_HEADS
    spec = pl.BlockSpec((1, hb, PEER_KEYS, tm), lambda i, h: (i, h, 0, 0))
    sd = lambda dt: jax.ShapeDtypeStruct((t // tm, PEER_HEADS, PEER_KEYS, tm), dt)
    return pl.pallas_call(
        functools.partial(_route_kernel, tm=tm), grid=(t // tm, PEER_HEADS // hb),
        in_specs=[pl.BlockSpec((tm, d), lambda i, h: (i, 0)),
                  pl.BlockSpec((d, hb * PEER_DKEY), lambda i, h: (0, h)),
                  pl.BlockSpec((hb, 2, PEER_KEYS, PEER_DKEY // 2), lambda i, h: (h, 0, 0, 0))],
        out_specs=[spec] * 4, out_shape=[sd(bf16), sd(f32), sd(f32), sd(bf16)],
        compiler_params=_params("parallel", "arbitrary"))(x, w_q, keys)


def _peer_dense_kernel(x_ref, u0_ref, u1_ref, vt0_ref, vt1_ref, rank2_in, crow_ref, e1z_ref, e2_in,
                       g_ref, b_ref, o_ref, acc_ref, xt_ref, ht0, ht1, gt0, gt1, rank2_ref, e2_ref,
                       *, rows, tb, lc, mw, alpha):
    k = pl.program_id(1)
    ec = rows * PEER_KEYS

    @pl.when(k == 0)
    def _():
        acc_ref[...] = jnp.zeros_like(acc_ref)
        ht1[...] = jnp.zeros_like(ht1)
        gt0[...] = jnp.zeros_like(gt0)
        xt_ref[...] = x_ref[...].T.astype(bf16)
        rank2_ref[...] = rank2_in[0]
        e2_ref[...] = e2_in[0]

    zero = jnp.zeros((), bf16)
    kt = PEER_KEYS // BF16_ROWS

    def up(half, ht, cw, r):
        lanes = slice(cw * mw, (cw + 1) * mw)
        keys = slice(r * PEER_KEYS, (r + 1) * PEER_KEYS)
        ht[keys, lanes] = jnp.dot((u0_ref, u1_ref)[half][keys, :], xt_ref[:, lanes],
                                  preferred_element_type=f32)

    n_tiles = pl.num_programs(1) - 1

    def gate(ht, tile, half, gt, cw, r):
        base = pl.multiple_of(jnp.clip(tile, 0, n_tiles - 1) * F32_ROWS, F32_ROWS)
        if True:
            row = slice(half * rows + r, half * rows + r + 1)
            for c in range(cw * mw // lc, (cw + 1) * mw // lc):
                lanes = slice(c * lc, (c + 1) * lc)
                def bcast(ref, h):
                    rowv = ref[0, h, pl.ds(base, F32_ROWS), lanes][row]
                    return jnp.broadcast_to(rowv, (BF16_ROWS, lc)).astype(bf16)[None]

                crs = [bcast(crow_ref, h) for h in range(PEER_HEADS)]
                ezs = [bcast(e1z_ref, h) for h in range(PEER_HEADS)]
                for ks in range(0, kt, GATE_KEY_TILES):
                    kk = slice(ks, ks + GATE_KEY_TILES)
                    w = None
                    for h in range(PEER_HEADS):
                        term = jnp.where(rank2_ref[h, kk, :, lanes] < crs[h], e2_ref[h, kk, :, lanes] * ezs[h], zero)
                        w = term if w is None else w + term
                    hv = ht[r * PEER_KEYS + ks * BF16_ROWS:r * PEER_KEYS + (ks + GATE_KEY_TILES) * BF16_ROWS, lanes]
                    gelu = 0.5 * hv * (1.0 + lax.erf(hv * (2.0 ** -0.5)))
                    gt[r * kt + ks:r * kt + ks + GATE_KEY_TILES, :, lanes] = (
                        gelu.astype(bf16).reshape(GATE_KEY_TILES, BF16_ROWS, lc) * w)

    d_rows = acc_ref.shape[0] // rows

    def down(half, gt, cw, piece):
        lanes = slice(cw * mw, (cw + 1) * mw)
        out_rows = slice(piece * d_rows, (piece + 1) * d_rows)
        acc_ref[out_rows, lanes] += jnp.dot((vt0_ref, vt1_ref)[half][0, out_rows, :],
                                            gt[:, :, lanes].reshape(ec, mw), preferred_element_type=f32)

    for cw in range(tb // mw):
        for r in range(rows):
            up(0, ht0, cw, r)
            gate(ht1, k - 1, 1, gt1, cw, r)
            down(0, gt0, cw, r)
    for cw in range(tb // mw):
        for r in range(rows):
            up(1, ht1, cw, r)
            gate(ht0, k, 0, gt0, cw, r)
            down(1, gt1, cw, r)

    @pl.when(k == pl.num_programs(1) - 1)
    def _():
        z = alpha * x_ref[...] + acc_ref[...].T
        o_ref[...] = _layer_norm(z, g_ref[...], b_ref[...])


def _peer_dense(x, u, vt, routing, g, b, alpha, lc=128):
    t, d = x.shape
    e = u.shape[0]
    ec = DENSE_EC
    rows = ec // PEER_KEYS
    assert 2 * rows == F32_ROWS, "a pair of expert blocks must span one 8-row f32 tile of sub-key-1 rows"
    nk = e // (2 * ec)
    kt = PEER_KEYS // BF16_ROWS
    rank2, crow, e1z, e2 = routing
    nt, _, _, tb = crow.shape
    rank2, e2 = (a.reshape(nt, PEER_HEADS, kt, BF16_ROWS, tb) for a in (rank2, e2))
    full = pl.BlockSpec((1, PEER_HEADS, kt, BF16_ROWS, tb), lambda i, k: (i, 0, 0, 0, 0))
    rowtab = pl.BlockSpec((1, PEER_HEADS, PEER_KEYS, tb), lambda i, k: (i, 0, 0, 0))
    fixed = pl.BlockSpec((1, d), lambda i, k: (0, 0))
    xspec = pl.BlockSpec((tb, d), lambda i, k: (i, 0))
    xin = xspec
    gt_scr = pltpu.VMEM((ec // BF16_ROWS, BF16_ROWS, tb), bf16)
    u_blk = lambda half: pl.BlockSpec((ec, d), lambda i, k: (2 * jnp.minimum(k, nk - 1) + half, 0))
    vt_blk = lambda half: pl.BlockSpec((1, d, ec), lambda i, k: (2 * jnp.maximum(k - 1, 0) + half, 0, 0))
    return pl.pallas_call(
        functools.partial(_peer_dense_kernel, rows=rows, tb=tb, lc=lc, mw=MXU_WIDTH, alpha=alpha),
        grid=(t // tb, nk + 1),
        in_specs=[xin, u_blk(0), u_blk(1), vt_blk(0), vt_blk(1),
                  full, rowtab, rowtab, full, fixed, fixed],
        out_specs=xspec, out_shape=jax.ShapeDtypeStruct((t, d), f32),
        scratch_shapes=[pltpu.VMEM((d, tb), f32), pltpu.VMEM((d, tb), bf16),
                        pltpu.VMEM((ec, tb), f32), pltpu.VMEM((ec, tb), f32), gt_scr, gt_scr,
                        pltpu.VMEM((PEER_HEADS, kt, BF16_ROWS, tb), bf16),
                        pltpu.VMEM((PEER_HEADS, kt, BF16_ROWS, tb), bf16)],
        compiler_params=_params("parallel", "arbitrary"))(
            x, u, u, vt, vt, rank2, crow, e1z, e2, g.reshape(1, d), b.reshape(1, d))


def _swa_layer(x, batch, seq, w_in, sinks, w_out, bias, g, b, alpha):
    feat = (N_HEADS + 2 * SWA_KV_HEADS) * HEAD_DIM
    d_q = N_HEADS * HEAD_DIM
    kvw = SWA_KV_HEADS * HEAD_DIM
    proj = _mm(x, w_in.astype(bf16), bf16, 512, feat).reshape(batch, seq, feat)
    sink = jnp.repeat(sinks.astype(f32), HEAD_DIM).reshape(1, d_q)
    o = _band_attention(proj, bias, 1, feat, 0, d_q // kvw, d_q // kvw + 1, kvw,
                        N_HEADS // SWA_KV_HEADS, sink)
    return _proj_ln([o], [], [1], w_out.astype(bf16), x, g, b, alpha)


def _dil_layer(x, batch, seq, w_in, w_out, biases, g, b, alpha):
    d_q = N_HEADS * HEAD_DIM
    feat = 3 * d_q
    w_in = w_in.astype(bf16)
    outs, lses, dils = [], [], []
    for gi, (window, dil) in enumerate(DIL_PATTERNS):
        proj = _mm_dil(x, w_in[:, gi * feat:(gi + 1) * feat], dil).reshape(batch, seq // dil, dil * feat)
        o, lse = _band_attention(proj, biases[gi], dil, feat, 0, 1, 2, d_q, 1)
        outs.append(o)
        lses.append(lse)
        dils.append(dil)
    return _proj_ln(outs, lses, dils, w_out.astype(bf16), x, g, b, alpha)


def _mla_layer(x, batch, seq, w_in, q_norm, w_uq, kv_norm, w_ukv, w_out, g, b, alpha):
    q, k, v = _mla_prep(x, w_in, q_norm, w_uq, kv_norm, w_ukv, seq)
    o = _mla_attention(q, k, v, batch, seq)
    d = w_out.shape[1]
    w3 = w_out.reshape(N_HEADS, MLA_V, d)
    w_pad = jnp.concatenate([w3, jnp.zeros((N_HEADS, LANE - MLA_V, d), f32)], 1)
    return _proj_ln([o], [], [1], w_pad.reshape(N_HEADS * LANE, d).astype(bf16), x, g, b, alpha)


def _peer_layer(x, w_q, keys, u, v, g, b, alpha):
    routing = _peer_route(x, w_q.astype(bf16), keys.astype(bf16))
    e, d = v.shape
    vt = v.astype(bf16).reshape(e // DENSE_EC, DENSE_EC, d).transpose(0, 2, 1)
    return _peer_dense(x, u.astype(bf16), vt, routing, g, b, alpha)


def kernel(x, rel_bias, ln_g, ln_b, swa_w_in, swa_sinks, swa_w_out, dil_w_in, dil_w_out,
           mla_w_in, mla_q_norm, mla_w_uq, mla_kv_norm, mla_w_ukv, mla_w_out,
           peer_w_q, peer_keys, peer_u, peer_v):
    batch, seq, d = x.shape
    depth = ln_g.shape[0]
    alpha = (2 * depth) ** 0.25
    assert seq % (DIL_PATTERNS[-1][1] * BLOCK) == 0, "sequence must be a whole number of dilation segments"
    swa_bias = _band_bias(rel_bias, SWA_WINDOW - 1, 1)
    dil_bias = [_band_bias(rel_bias, window // dil, dil) for window, dil in DIL_PATTERNS]
    h = x.reshape(batch * seq, d)
    for i in range(depth):
        kind, j = i % 3, i // 3
        if kind == 0:
            h = _swa_layer(h, batch, seq, swa_w_in[j], swa_sinks[j], swa_w_out[j], swa_bias,
                           ln_g[i, 0], ln_b[i, 0], alpha)
        elif kind == 1:
            h = _dil_layer(h, batch, seq, dil_w_in[j], dil_w_out[j], dil_bias,
                           ln_g[i, 0], ln_b[i, 0], alpha)
        else:
            h = _mla_layer(h, batch, seq, mla_w_in[j], mla_q_norm[j], mla_w_uq[j], mla_kv_norm[j],
                           mla_w_ukv[j], mla_w_out[j], ln_g[i, 0], ln_b[i, 0], alpha)
        h = _peer_layer(h, peer_w_q[i], peer_keys[i], peer_u[i], peer_v[i], ln_g[i, 1], ln_b[i, 1], alpha)
    return h.reshape(batch, seq, d)
```

```python
import functools
import math

import jax
import jax.numpy as jnp
from jax import lax
from jax.experimental import pallas as pl
from jax.experimental.pallas import tpu as pltpu

f32 = jnp.float32
bf16 = jnp.bfloat16

N_HEADS = 16
HEAD_DIM = 64
BLOCK = 128
SWA_KV_HEADS = 2
SWA_WINDOW = 128
DIL_PATTERNS = ((128, 1), (512, 4), (2048, 16))
MLA_Q_RANK = 256
MLA_KV_RANK = 128
MLA_NOPE = 64
MLA_ROPE = 32
MLA_V = 64
ROPE_THETA = 10000.0
REL_BUCKETS = 32
REL_MAX_DIST = 2048
PEER_HEADS = 8
PEER_KEYS = 128
PEER_DKEY = 256
PEER_TOPK = 16
LN_EPS = 1e-5
RMS_EPS = 1e-6
NEG = -1e30

LANE = 128
BF16_ROWS = 16
F32_ROWS = 8
MXU_WIDTH = 256
MLA_GROUP = 2
ROUTE_HEADS = 2
GELU_FOLD = 0.5 ** 0.5
ROUTE_TOKENS = 512
HEAD_BATCH = 16
DENSE_EC = 4 * PEER_KEYS
VMEM_LIMIT = 56 * 1024 * 1024

_NT = (((1,), (1,)), ((), ()))


def _params(*sem):
    return pltpu.CompilerParams(dimension_semantics=sem, vmem_limit_bytes=VMEM_LIMIT)


def _mm_kernel(x_ref, w_ref, o_ref):
    o_ref[...] = jnp.dot(x_ref[...].astype(bf16), w_ref[...],
                         preferred_element_type=f32).astype(o_ref.dtype)


def _mm(x, w, out_dtype, tm, tn):
    m, k = x.shape
    n = w.shape[1]
    tm, tn = min(tm, m), min(tn, n)
    return pl.pallas_call(
        _mm_kernel, grid=(m // tm, n // tn),
        in_specs=[pl.BlockSpec((tm, k), lambda i, j: (i, 0)),
                  pl.BlockSpec((k, tn), lambda i, j: (0, j))],
        out_specs=pl.BlockSpec((tm, tn), lambda i, j: (i, j)),
        out_shape=jax.ShapeDtypeStruct((m, n), out_dtype),
        compiler_params=_params("parallel", "parallel"))(x, w)


def _rel_bucket(dist):
    max_exact = REL_BUCKETS // 2
    n = jnp.maximum(dist, 0)
    nf = jnp.maximum(n, 1).astype(f32)
    large = max_exact + (jnp.log(nf / max_exact) / math.log(REL_MAX_DIST / max_exact)
                         * (REL_BUCKETS - max_exact)).astype(jnp.int32)
    large = jnp.minimum(large, REL_BUCKETS - 1)
    return jnp.where(n < max_exact, n, large)


def _bias_kernel(rel_ref, bucket_ref, valid_ref, o_ref):
    bucket = bucket_ref[...]
    for h in range(N_HEADS):
        acc = jnp.zeros(bucket.shape, f32)
        for b in range(REL_BUCKETS):
            acc = jnp.where(bucket == b, rel_ref[b, h], acc)
        for variant in range(2):
            o_ref[variant, h] = jnp.where(valid_ref[variant] != 0, acc, NEG)


def _band_bias(rel_bias, max_dist, dilation):
    qi = jnp.arange(BLOCK)[:, None]
    kj = jnp.arange(2 * BLOCK)[None, :]
    dist = BLOCK + qi - kj
    bucket = _rel_bucket(dist * dilation).astype(jnp.int32)
    valid = (dist >= 0) & (dist <= max_dist)
    valid = jnp.stack([valid & (kj >= BLOCK), valid]).astype(jnp.int32)
    return pl.pallas_call(
        _bias_kernel,
        in_specs=[pl.BlockSpec(memory_space=pltpu.SMEM),
                  pl.BlockSpec(memory_space=pltpu.VMEM),
                  pl.BlockSpec(memory_space=pltpu.VMEM)],
        out_specs=pl.BlockSpec(memory_space=pltpu.VMEM),
        out_shape=jax.ShapeDtypeStruct((2, N_HEADS, BLOCK, 2 * BLOCK), f32))(rel_bias, bucket, valid)


def _band_kernel(q_ref, kp_ref, kc_ref, vp_ref, vc_ref, bias_ref, *rest, group, with_sink):
    if with_sink:
        sink_ref, o_ref = rest
    else:
        o_ref, lse_ref = rest
    scale = HEAD_DIM ** -0.5
    for h0 in range(0, N_HEADS, HEAD_BATCH):
        heads = range(h0, h0 + HEAD_BATCH)
        qs = [slice(h * HEAD_DIM, (h + 1) * HEAD_DIM) for h in heads]
        ks = [slice((h // group) * HEAD_DIM, (h // group + 1) * HEAD_DIM) for h in heads]
        kk = [jnp.concatenate([kp_ref[0, :, c], kc_ref[0, :, c]], axis=0) for c in ks]
        vv = [jnp.concatenate([vp_ref[0, :, c], vc_ref[0, :, c]], axis=0) for c in ks]
        ss = [lax.dot_general(q_ref[0, :, c], k, _NT, preferred_element_type=f32) * scale + bias_ref[0, h]
              for h, c, k in zip(heads, qs, kk)]
        ms = [s.max(-1, keepdims=True) for s in ss]
        ps = [jnp.exp(s - m) for s, m in zip(ss, ms)]
        ls = [p.sum(-1, keepdims=True) for p in ps]
        os_ = [jnp.dot(p.astype(bf16), v, preferred_element_type=f32) / l for p, v, l in zip(ps, vv, ls)]
        lses = [m + jnp.log(l) for m, l in zip(ms, ls)]
        for c, o, lse in zip(qs, os_, lses):
            if with_sink:
                o = o * jax.nn.sigmoid(lse - sink_ref[:, c])
            else:
                lse_ref[0, :, c] = jnp.broadcast_to(lse, (BLOCK, HEAD_DIM))
            o_ref[0, :, c] = o.astype(o_ref.dtype)


def _band_attention(view, bias, dil, feat, q_col, k_col, v_col, kv_width, group, sink=None):
    b, m_len, _ = view.shape
    d_q = N_HEADS * HEAD_DIM
    nb = m_len // BLOCK
    qpr, kpr = feat // d_q, feat // kv_width

    def cur(col, per_row):
        return lambda bi, r, n: (bi, n, r * per_row + col)

    def prev(col, per_row):
        return lambda bi, r, n: (bi, jnp.maximum(n - 1, 0), r * per_row + col)

    in_specs = [pl.BlockSpec((1, BLOCK, d_q), cur(q_col, qpr)),
                pl.BlockSpec((1, BLOCK, kv_width), prev(k_col, kpr)),
                pl.BlockSpec((1, BLOCK, kv_width), cur(k_col, kpr)),
                pl.BlockSpec((1, BLOCK, kv_width), prev(v_col, kpr)),
                pl.BlockSpec((1, BLOCK, kv_width), cur(v_col, kpr)),
                pl.BlockSpec((1, N_HEADS, BLOCK, 2 * BLOCK),
                             lambda bi, r, n: (jnp.minimum(n, 1), 0, 0, 0))]
    args = [view, view, view, view, view, bias]
    o_spec = pl.BlockSpec((1, BLOCK, d_q), lambda bi, r, n: (bi, n, r))
    o_shape = jax.ShapeDtypeStruct((b, m_len, dil * d_q), bf16)
    if sink is not None:
        in_specs.append(pl.BlockSpec((1, d_q), lambda bi, r, n: (0, 0)))
        args.append(sink)
        out_specs, out_shape = o_spec, o_shape
    else:
        out_specs = [o_spec, o_spec]
        out_shape = [o_shape, jax.ShapeDtypeStruct((b, m_len, dil * d_q), f32)]
    out = pl.pallas_call(
        functools.partial(_band_kernel, group=group, with_sink=sink is not None),
        grid=(b, dil, nb), in_specs=in_specs, out_specs=out_specs, out_shape=out_shape,
        compiler_params=_params("parallel", "parallel", "arbitrary"))(*args)
    if sink is not None:
        return out.reshape(b * m_len, dil * d_q)
    return out[0].reshape(b * m_len, dil * d_q), out[1].reshape(b * m_len, dil * d_q)


def _layer_norm(z, g, b):
    mu = z.mean(-1, keepdims=True)
    zc = z - mu
    var = jnp.square(zc).mean(-1, keepdims=True)
    return zc * lax.rsqrt(var + LN_EPS) * g + b


def _proj_ln_kernel(*refs, dils, alpha):
    n_pat = len(dils)
    o_refs = refs[:n_pat]
    lse_refs = refs[n_pat:2 * n_pat] if n_pat > 1 else ()
    n_in = len(o_refs) + len(lse_refs)
    w_ref, x_ref, g_ref, b_ref, out_ref = refs[n_in:n_in + 5]
    scratch = refs[n_in + 5:]
    if n_pat == 1 and len(o_refs[0].shape) == 3:
        groups, _, width = o_refs[0].shape
        y = sum(jnp.dot(o_refs[0][gi], w_ref[gi * width:(gi + 1) * width, :], preferred_element_type=f32)
                for gi in range(groups))
        out_ref[...] = _layer_norm(alpha * x_ref[...] + y, g_ref[...], b_ref[...])
        return
    if n_pat == 1:
        o = o_refs[0][...]
    else:
        kw = w_ref.shape[0]
        mix_ref = scratch[-1]

        def natural(ref, dil, scr, c):
            if dil == 1:
                return ref[:, c * LANE:(c + 1) * LANE].astype(f32)
            rows = ref.shape[0]
            for r in range(dil):
                scr[c, pl.ds(r, rows, stride=dil), :] = (
                    ref[:, r * kw + c * LANE:r * kw + (c + 1) * LANE].astype(f32))
            return scr[c]

        for c in range(kw // LANE):
            lses = [natural(r, dl, scratch[2 * i], c) for i, (r, dl) in enumerate(zip(lse_refs, dils))]
            mx = functools.reduce(jnp.maximum, lses)
            es = [jnp.exp(l - mx) for l in lses]
            num = sum(e * natural(r, dl, scratch[2 * i + 1], c)
                      for i, (e, r, dl) in enumerate(zip(es, o_refs, dils)))
            mix_ref[:, c * LANE:(c + 1) * LANE] = (num / sum(es)).astype(bf16)
        o = mix_ref[...]
    y = jnp.dot(o, w_ref[...], preferred_element_type=f32)
    out_ref[...] = _layer_norm(alpha * x_ref[...] + y, g_ref[...], b_ref[...])


def _proj_ln(os_, lses, dils, w_out, x, g, b, alpha, tm=512):
    t, d = x.shape
    tm = min(tm, t)
    kw = w_out.shape[0]
    row = lambda width: pl.BlockSpec((tm, width), lambda i: (i, 0))
    dilated = lambda dl: pl.BlockSpec((tm // dl, dl * kw), lambda i: (i, 0))
    fixed = lambda shape: pl.BlockSpec(shape, lambda i: (0, 0))
    if os_[0].ndim == 3:
        o_specs = [pl.BlockSpec((os_[0].shape[0], tm, os_[0].shape[2]), lambda i: (0, i, 0))]
    else:
        o_specs = [dilated(dl) for dl in dils]
    in_specs = (o_specs + [dilated(dl) for dl in dils[:len(lses)]]
                + [fixed((kw, d)), row(d), fixed((1, d)), fixed((1, d))])
    scratch = [pltpu.VMEM((kw // LANE, tm, LANE), f32)] * (2 * len(lses))
    if lses:
        scratch.append(pltpu.VMEM((tm, kw), bf16))
    return pl.pallas_call(
        functools.partial(_proj_ln_kernel, dils=tuple(dils), alpha=alpha),
        grid=(t // tm,), in_specs=in_specs, out_specs=row(d),
        out_shape=jax.ShapeDtypeStruct((t, d), f32), scratch_shapes=scratch,
        compiler_params=_params("parallel"))(*os_, *lses, w_out, x, g.reshape(1, d), b.reshape(1, d))


def _mm_dil_kernel(x_ref, w_ref, o_ref, res_ref, *, dil):
    n = w_ref.shape[1]
    res = jnp.dot(x_ref[...].astype(bf16), w_ref[...], preferred_element_type=f32)
    if dil == 1:
        o_ref[...] = res.astype(o_ref.dtype)
        return
    rows = o_ref.shape[0]
    for c in range(n // LANE):
        res_ref[c] = res[:, c * LANE:(c + 1) * LANE]
        for r in range(dil):
            o_ref[:, r * n + c * LANE:r * n + (c + 1) * LANE] = (
                res_ref[c, pl.ds(r, rows, stride=dil), :].astype(o_ref.dtype))


def _mm_dil(x, w, dil, tm=512):
    t, k = x.shape
    n = w.shape[1]
    tm = min(tm, t)
    return pl.pallas_call(
        functools.partial(_mm_dil_kernel, dil=dil), grid=(t // tm,),
        in_specs=[pl.BlockSpec((tm, k), lambda i: (i, 0)), pl.BlockSpec((k, n), lambda i: (0, 0))],
        out_specs=pl.BlockSpec((tm // dil, dil * n), lambda i: (i, 0)),
        out_shape=jax.ShapeDtypeStruct((t // dil, dil * n), bf16),
        scratch_shapes=[pltpu.VMEM((n // LANE, tm, LANE), f32)],
        compiler_params=_params("parallel"))(x, w)


def _rms(c, g):
    return c * lax.rsqrt(jnp.square(c).mean(-1, keepdims=True) + RMS_EPS) * g


def _mla_prep_kernel(x_ref, win_ref, qn_ref, wq_ref, kvn_ref, wkv_ref, cos_ref, sin_ref,
                     q_ref, k_ref, v_ref):
    hw = N_HEADS * LANE
    scale = (MLA_NOPE + MLA_ROPE) ** -0.5
    cos, sin = cos_ref[...], sin_ref[...]
    xw = jnp.dot(x_ref[...].astype(bf16), win_ref[...], preferred_element_type=f32)
    cq = _rms(xw[:, :MLA_Q_RANK], qn_ref[...]).astype(bf16)
    ckv = _rms(xw[:, MLA_Q_RANK:MLA_Q_RANK + MLA_KV_RANK], kvn_ref[...]).astype(bf16)
    off = MLA_Q_RANK + MLA_KV_RANK
    kr = xw[:, off:off + LANE] * cos + xw[:, off + LANE:off + 2 * LANE] * sin
    qq = jnp.dot(cq, wq_ref[...], preferred_element_type=f32)
    kv = jnp.dot(ckv, wkv_ref[...], preferred_element_type=f32)
    ones_lane = lax.broadcasted_iota(jnp.int32, (1, LANE), 1) == MLA_V
    for h in range(N_HEADS):
        blk = slice(h * LANE, (h + 1) * LANE)
        swp = slice(hw + h * LANE, hw + (h + 1) * LANE)
        grp, sub = divmod(h, MLA_GROUP)
        dst = slice(sub * LANE, (sub + 1) * LANE)
        q_ref[grp, :, dst] = ((qq[:, blk] * cos + qq[:, swp] * sin) * scale).astype(bf16)
        k_ref[grp, :, dst] = (kv[:, blk] + kr).astype(bf16)
        v_ref[grp, :, dst] = jnp.where(ones_lane, 1.0, kv[:, swp]).astype(bf16)


def _mla_prep(x, w_in, q_norm, w_uq, kv_norm, w_ukv, seq, tm=512):
    t, d = x.shape
    tm = min(tm, seq)
    hw = N_HEADS * LANE
    half = MLA_ROPE // 2
    dq = MLA_NOPE + MLA_ROPE
    kr_w = w_in[:, MLA_Q_RANK + MLA_KV_RANK:]
    zeros = lambda n: jnp.zeros((d, n), f32)
    kr_a = jnp.concatenate([zeros(MLA_NOPE), kr_w, zeros(LANE - dq)], 1)
    kr_b = jnp.concatenate([zeros(MLA_NOPE), kr_w[:, half:], kr_w[:, :half], zeros(LANE - dq)], 1)
    win = jnp.concatenate([w_in[:, :MLA_Q_RANK + MLA_KV_RANK], kr_a, kr_b], 1).astype(bf16)
    wq3 = w_uq.reshape(MLA_Q_RANK, N_HEADS, dq)
    zq = lambda n: jnp.zeros((MLA_Q_RANK, N_HEADS, n), f32)
    wq_a = jnp.concatenate([wq3, zq(LANE - dq)], 2)
    wq_b = jnp.concatenate([zq(MLA_NOPE), wq3[:, :, MLA_NOPE + half:], wq3[:, :, MLA_NOPE:MLA_NOPE + half],
                            zq(LANE - dq)], 2)
    wq = jnp.concatenate([wq_a.reshape(MLA_Q_RANK, hw), wq_b.reshape(MLA_Q_RANK, hw)], 1).astype(bf16)
    wkv3 = w_ukv.reshape(MLA_KV_RANK, N_HEADS, MLA_NOPE + MLA_V)
    zk = lambda n: jnp.zeros((MLA_KV_RANK, N_HEADS, n), f32)
    wk = jnp.concatenate([wkv3[:, :, :MLA_NOPE], zk(LANE - MLA_NOPE)], 2)
    wv = jnp.concatenate([wkv3[:, :, MLA_NOPE:], zk(LANE - MLA_V)], 2)
    wkv = jnp.concatenate([wk.reshape(MLA_KV_RANK, hw), wv.reshape(MLA_KV_RANK, hw)], 1).astype(bf16)
    freq = ROPE_THETA ** (-jnp.arange(half, dtype=f32) / half)
    ang = jnp.arange(seq)[:, None].astype(f32) * freq[None, :]
    c, s_ = jnp.cos(ang), jnp.sin(ang)
    cos_t = jnp.concatenate([jnp.ones((seq, MLA_NOPE), f32), c, c, jnp.ones((seq, LANE - dq), f32)], 1)
    sin_t = jnp.concatenate([jnp.zeros((seq, MLA_NOPE), f32), -s_, s_, jnp.zeros((seq, LANE - dq), f32)], 1)

    n_pos = seq // tm
    row = lambda width: pl.BlockSpec((tm, width), lambda i: (i, 0))
    fixed = lambda a: pl.BlockSpec(a.shape, lambda i: (0, 0))
    pos = pl.BlockSpec((tm, LANE), lambda i: (i % n_pos, 0))
    qn, kvn = q_norm.reshape(1, -1), kv_norm.reshape(1, -1)
    groups, gw = N_HEADS // MLA_GROUP, MLA_GROUP * LANE
    out_sd = jax.ShapeDtypeStruct((groups, t, gw), bf16)
    out_spec = pl.BlockSpec((groups, tm, gw), lambda i: (0, i, 0))
    return pl.pallas_call(
        _mla_prep_kernel, grid=(t // tm,),
        in_specs=[row(d), fixed(win), fixed(qn), fixed(wq), fixed(kvn), fixed(wkv), pos, pos],
        out_specs=[out_spec] * 3, out_shape=[out_sd] * 3,
        compiler_params=_params("parallel"))(x, win, qn, wq, kvn, wkv, cos_t, sin_t)


def _mla_attn_kernel(q_ref, k_ref, v_ref, o_ref, *, tq, tk, heads):
    qi = pl.program_id(2)
    lanes = [slice(j * LANE, (j + 1) * LANE) for j in range(heads)]
    qs = [q_ref[0, :, ln] for ln in lanes]

    def update(carry, start, mask):
        ss = [lax.dot_general(q, k_ref[0, pl.ds(start, tk), ln], _NT, preferred_element_type=f32)
              for q, ln in zip(qs, lanes)]
        if mask is not None:
            ss = [jnp.where(mask, s, NEG) for s in ss]
        ms = [jnp.maximum(m, s.max(-1, keepdims=True)) for (m, _), s in zip(carry, ss)]
        ps = [jnp.exp(s - m).astype(bf16) for s, m in zip(ss, ms)]
        pvs = [jnp.dot(p, v_ref[0, pl.ds(start, tk), ln], preferred_element_type=f32)
               for p, ln in zip(ps, lanes)]
        return tuple((m_new, jnp.exp(m - m_new) * acc + pv)
                     for (m, acc), m_new, pv in zip(carry, ms, pvs))

    init = tuple((jnp.full((tq, 1), NEG, f32), jnp.zeros((tq, LANE), f32)) for _ in range(heads))
    sub = tq // tk
    carry = lax.fori_loop(0, qi * sub, lambda ki, c: update(c, pl.multiple_of(ki * tk, tk), None), init)
    row = lax.broadcasted_iota(jnp.int32, (tq, tk), 0)
    col = lax.broadcasted_iota(jnp.int32, (tq, tk), 1)
    for j in range(sub):
        carry = update(carry, pl.multiple_of(qi * tq + j * tk, tk), col + j * tk <= row)
    for j in range(heads):
        acc = carry[j][1]
        o_ref[0, :, lanes[j]] = (acc / acc[:, MLA_V:MLA_V + 1]).astype(o_ref.dtype)


def _mla_attention(q, k, v, batch, seq, tq=1024, tk=512):
    groups, t, gw = q.shape
    tq = min(tq, seq)
    nq = seq // tq
    qspec = pl.BlockSpec((1, tq, gw), lambda b, h, i: (h, b * nq + i, 0))
    kspec = pl.BlockSpec((1, seq, gw), lambda b, h, i: (h, b, 0))
    return pl.pallas_call(
        functools.partial(_mla_attn_kernel, tq=tq, tk=min(tk, tq), heads=MLA_GROUP),
        grid=(batch, groups, nq),
        in_specs=[qspec, kspec, kspec], out_specs=qspec,
        out_shape=jax.ShapeDtypeStruct((groups, t, gw), bf16),
        compiler_params=_params("parallel", "parallel", "arbitrary"))(q, k, v)


def _top16_pair(s1, s2):
    w1, w2 = s1, s2
    rank2 = jnp.full(s2.shape, 99.0, f32)
    v1, v2 = [], []
    for k in range(PEER_TOPK):
        m1 = w1.max(axis=0, keepdims=True)
        m2 = w2.max(axis=0, keepdims=True)
        h1 = w1 == m1
        h2 = w2 == m2
        rank2 = jnp.where(h2, float(k), rank2)
        w1 = jnp.where(h1, -jnp.inf, w1)
        w2 = jnp.where(h2, -jnp.inf, w2)
        v1.append(m1)
        v2.append(m2)
    return v1, v2, rank2


def _stack_rows(rows):
    n = len(rows)
    rid = lax.broadcasted_iota(jnp.int32, (n, rows[0].shape[1]), 0)
    out = jnp.broadcast_to(rows[0], (n, rows[0].shape[1]))
    for i in range(1, n):
        out = jnp.where(rid == i, rows[i], out)
    return out


def _route_kernel(x_ref, wq_ref, keys_ref, rank2_ref, crow_ref, e1z_ref, e2_ref, *, tm):
    q = jnp.dot(x_ref[...].astype(bf16), wq_ref[...], preferred_element_type=f32).astype(bf16)
    half = PEER_DKEY // 2
    scores = []
    for hh in range(ROUTE_HEADS):
        qh = q[:, hh * PEER_DKEY:(hh + 1) * PEER_DKEY]
        scores.append((lax.dot_general(keys_ref[hh, 0], qh[:, :half], _NT, preferred_element_type=f32),
                       lax.dot_general(keys_ref[hh, 1], qh[:, half:], _NT, preferred_element_type=f32)))
    rid8 = lax.broadcasted_iota(jnp.int32, (8, LANE), 0)
    for hh, c in [(hh, c) for hh in range(ROUTE_HEADS) for c in range(tm // LANE)]:
        lanes = slice(c * LANE, (c + 1) * LANE)
        s1, s2 = scores[hh][0][:, lanes], scores[hh][1][:, lanes]
        v1, v2, rank2 = _top16_pair(s1, s2)
        v2_all = _stack_rows(v2)
        cands = [v1[0] + v2_all]
        for a in range(1, 8):
            cands.append(jnp.where(rid8 < PEER_TOPK // (a + 1), v1[a] + v2_all[:8], -jnp.inf))
        cands.append(_stack_rows(v1[8:]) + v2[0])
        cand = jnp.concatenate(cands, axis=0)
        work = cand
        for _ in range(PEER_TOPK):
            tau = work.max(axis=0, keepdims=True)
            work = jnp.where(work == tau, -jnp.inf, work)
        cmax = v1[0] + v2[0]
        z = jnp.where(cand >= tau, jnp.exp(cand - cmax), 0.0).sum(axis=0, keepdims=True)
        crow = jnp.zeros(s1.shape, f32)
        for b in range(PEER_TOPK):
            crow = crow + jnp.where(s1 + v2[b] >= tau, 1.0, 0.0)
        rank2_ref[0, hh, :, lanes] = rank2
        crow_ref[0, hh, :, lanes] = crow
        e1z_ref[0, hh, :, lanes] = jnp.exp(s1 - v1[0]) * (GELU_FOLD / z)
        e2_ref[0, hh, :, lanes] = jnp.exp(s2 - v2[0])


def _peer_route(x, w_q, keys, tm=ROUTE_TOKENS):
    t, d = x.shape
    tm = min(tm, t)
    hb = ROUTE_HEADS
    spec = pl.BlockSpec((1, hb, PEER_KEYS, tm), lambda i, h: (i, h, 0, 0))
    sd = lambda dt: jax.ShapeDtypeStruct((t // tm, PEER_HEADS, PEER_KEYS, tm), dt)
    return pl.pallas_call(
        functools.partial(_route_kernel, tm=tm), grid=(t // tm, PEER_HEADS // hb),
        in_specs=[pl.BlockSpec((tm, d), lambda i, h: (i, 0)),
                  pl.BlockSpec((d, hb * PEER_DKEY), lambda i, h: (0, h)),
                  pl.BlockSpec((hb, 2, PEER_KEYS, PEER_DKEY // 2), lambda i, h: (h, 0, 0, 0))],
        out_specs=[spec] * 4, out_shape=[sd(f32)] * 4,
        compiler_params=_params("parallel", "arbitrary"))(x, w_q, keys)


def _peer_dense_kernel(x_ref, u0_ref, u1_ref, vt0_ref, vt1_ref, rank2_ref, crow_ref, e1z_ref, e2_ref,
                       g_ref, b_ref, o_ref, acc_ref, xt_ref, ht0, ht1, gt0, gt1,
                       *, rows, tb, lc, mw, alpha):
    k = pl.program_id(1)
    ec = rows * PEER_KEYS

    @pl.when(k == 0)
    def _():
        acc_ref[...] = jnp.zeros_like(acc_ref)
        ht1[...] = jnp.zeros_like(ht1)
        gt0[...] = jnp.zeros_like(gt0)
        xt_ref[...] = x_ref[...].T.astype(bf16)

    kt = PEER_KEYS // BF16_ROWS

    def up(half, ht, cw):
        lanes = slice(cw * mw, (cw + 1) * mw)
        ht[:, lanes] = jnp.dot((u0_ref, u1_ref)[half][...], xt_ref[:, lanes], preferred_element_type=f32)

    n_tiles = pl.num_programs(1) - 1

    def gate(ht, tile, half, gt, cw):
        base = pl.multiple_of(jnp.clip(tile, 0, n_tiles - 1) * F32_ROWS, F32_ROWS)
        for r in range(rows):
            keys = slice(r * PEER_KEYS, (r + 1) * PEER_KEYS)
            row = slice(half * rows + r, half * rows + r + 1)
            for c in range(cw * mw // lc, (cw + 1) * mw // lc):
                lanes = slice(c * lc, (c + 1) * lc)
                w = None
                for h in range(PEER_HEADS):
                    cr = crow_ref[0, h, pl.ds(base, F32_ROWS), lanes][row]
                    ez = e1z_ref[0, h, pl.ds(base, F32_ROWS), lanes][row]
                    term = jnp.where(rank2_ref[0, h, :, lanes] < cr, e2_ref[0, h, :, lanes] * ez, 0.0)
                    w = term if w is None else w + term
                hv = ht[keys, lanes]
                g = hv * (1.0 + lax.erf(hv)) * w
                gt[r * kt:(r + 1) * kt, :, lanes] = g.astype(bf16).reshape(kt, BF16_ROWS, lc)

    def down(half, gt, cw):
        lanes = slice(cw * mw, (cw + 1) * mw)
        acc_ref[:, lanes] += jnp.dot((vt0_ref, vt1_ref)[half][0], gt[:, :, lanes].reshape(ec, mw),
                                     preferred_element_type=f32)

    for cw in range(tb // mw):
        up(0, ht0, cw)
        gate(ht1, k - 1, 1, gt1, cw)
        down(0, gt0, cw)
    for cw in range(tb // mw):
        up(1, ht1, cw)
        gate(ht0, k, 0, gt0, cw)
        down(1, gt1, cw)

    @pl.when(k == pl.num_programs(1) - 1)
    def _():
        z = alpha * x_ref[...] + acc_ref[...].T
        o_ref[...] = _layer_norm(z, g_ref[...], b_ref[...])


def _peer_dense(x, u, vt, routing, g, b, alpha, lc=128):
    t, d = x.shape
    e = u.shape[0]
    ec = DENSE_EC
    rows = ec // PEER_KEYS
    assert 2 * rows == F32_ROWS, "a pair of expert blocks must span one 8-row f32 tile of sub-key-1 rows"
    nk = e // (2 * ec)
    kt = PEER_KEYS // BF16_ROWS
    rank2, crow, e1z, e2 = routing
    nt, _, _, tb = crow.shape
    rowtab = pl.BlockSpec((1, PEER_HEADS, PEER_KEYS, tb), lambda i, k: (i, 0, 0, 0))
    fixed = pl.BlockSpec((1, d), lambda i, k: (0, 0))
    xspec = pl.BlockSpec((tb, d), lambda i, k: (i, 0))
    xin = xspec
    gt_scr = pltpu.VMEM((ec // BF16_ROWS, BF16_ROWS, tb), bf16)
    u_blk = lambda half: pl.BlockSpec((ec, d), lambda i, k: (2 * jnp.minimum(k, nk - 1) + half, 0))
    vt_blk = lambda half: pl.BlockSpec((1, d, ec), lambda i, k: (2 * jnp.maximum(k - 1, 0) + half, 0, 0))
    return pl.pallas_call(
        functools.partial(_peer_dense_kernel, rows=rows, tb=tb, lc=lc, mw=MXU_WIDTH, alpha=alpha),
        grid=(t // tb, nk + 1),
        in_specs=[xin, u_blk(0), u_blk(1), vt_blk(0), vt_blk(1),
                  rowtab, rowtab, rowtab, rowtab, fixed, fixed],
        out_specs=xspec, out_shape=jax.ShapeDtypeStruct((t, d), f32),
        scratch_shapes=[pltpu.VMEM((d, tb), f32), pltpu.VMEM((d, tb), bf16),
                        pltpu.VMEM((ec, tb), f32), pltpu.VMEM((ec, tb), f32), gt_scr, gt_scr],
        compiler_params=_params("parallel", "arbitrary"))(
            x, u, u, vt, vt, rank2, crow, e1z, e2, g.reshape(1, d), b.reshape(1, d))


def _swa_layer(x, batch, seq, w_in, sinks, w_out, bias, g, b, alpha):
    feat = (N_HEADS + 2 * SWA_KV_HEADS) * HEAD_DIM
    d_q = N_HEADS * HEAD_DIM
    kvw = SWA_KV_HEADS * HEAD_DIM
    proj = _mm(x, w_in.astype(bf16), bf16, 512, feat).reshape(batch, seq, feat)
    sink = jnp.repeat(sinks.astype(f32), HEAD_DIM).reshape(1, d_q)
    o = _band_attention(proj, bias, 1, feat, 0, d_q // kvw, d_q // kvw + 1, kvw,
                        N_HEADS // SWA_KV_HEADS, sink)
    return _proj_ln([o], [], [1], w_out.astype(bf16), x, g, b, alpha)


def _dil_layer(x, batch, seq, w_in, w_out, biases, g, b, alpha):
    d_q = N_HEADS * HEAD_DIM
    feat = 3 * d_q
    w_in = w_in.astype(bf16)
    outs, lses, dils = [], [], []
    for gi, (window, dil) in enumerate(DIL_PATTERNS):
        proj = _mm_dil(x, w_in[:, gi * feat:(gi + 1) * feat], dil).reshape(batch, seq // dil, dil * feat)
        o, lse = _band_attention(proj, biases[gi], dil, feat, 0, 1, 2, d_q, 1)
        outs.append(o)
        lses.append(lse)
        dils.append(dil)
    return _proj_ln(outs, lses, dils, w_out.astype(bf16), x, g, b, alpha)


def _mla_layer(x, batch, seq, w_in, q_norm, w_uq, kv_norm, w_ukv, w_out, g, b, alpha):
    q, k, v = _mla_prep(x, w_in, q_norm, w_uq, kv_norm, w_ukv, seq)
    o = _mla_attention(q, k, v, batch, seq)
    d = w_out.shape[1]
    w3 = w_out.reshape(N_HEADS, MLA_V, d)
    w_pad = jnp.concatenate([w3, jnp.zeros((N_HEADS, LANE - MLA_V, d), f32)], 1)
    return _proj_ln([o], [], [1], w_pad.reshape(N_HEADS * LANE, d).astype(bf16), x, g, b, alpha)


def _peer_layer(x, w_q, keys, u, v, g, b, alpha):
    routing = _peer_route(x, w_q.astype(bf16), keys.astype(bf16))
    e, d = v.shape
    vt = v.astype(bf16).reshape(e // DENSE_EC, DENSE_EC, d).transpose(0, 2, 1)
    return _peer_dense(x, (u * GELU_FOLD).astype(bf16), vt, routing, g, b, alpha)


def kernel(x, rel_bias, ln_g, ln_b, swa_w_in, swa_sinks, swa_w_out, dil_w_in, dil_w_out,
           mla_w_in, mla_q_norm, mla_w_uq, mla_kv_norm, mla_w_ukv, mla_w_out,
           peer_w_q, peer_keys, peer_u, peer_v):
    batch, seq, d = x.shape
    depth = ln_g.shape[0]
    alpha = (2 * depth) ** 0.25
    assert seq % (DIL_PATTERNS[-1][1] * BLOCK) == 0, "sequence must be a whole number of dilation segments"
    swa_bias = _band_bias(rel_bias, SWA_WINDOW - 1, 1)
    dil_bias = [_band_bias(rel_bias, window // dil, dil) for window, dil in DIL_PATTERNS]
    h = x.reshape(batch * seq, d)
    for i in range(depth):
        kind, j = i % 3, i // 3
        if kind == 0:
            h = _swa_layer(h, batch, seq, swa_w_in[j], swa_sinks[j], swa_w_out[j], swa_bias,
                           ln_g[i, 0], ln_b[i, 0], alpha)
        elif kind == 1:
            h = _dil_layer(h, batch, seq, dil_w_in[j], dil_w_out[j], dil_bias,
                           ln_g[i, 0], ln_b[i, 0], alpha)
        else:
            h = _mla_layer(h, batch, seq, mla_w_in[j], mla_q_norm[j], mla_w_uq[j], mla_kv_norm[j],
                           mla_w_ukv[j], mla_w_out[j], ln_g[i, 0], ln_b[i, 0], alpha)
        h = _peer_layer(h, peer_w_q[i], peer_keys[i], peer_u[i], peer_v[i], ln_g[i, 1], ln_b[i, 1], alpha)
    return h.reshape(batch, seq, d)
```

```python
import functools
import math

import jax
import jax.numpy as jnp
from jax import lax
from jax.experimental import pallas as pl
from jax.experimental.pallas import tpu as pltpu

f32 = jnp.float32
bf16 = jnp.bfloat16

N_HEADS = 16
HEAD_DIM = 64
BLOCK = 128
SWA_KV_HEADS = 2
SWA_WINDOW = 128
DIL_PATTERNS = ((128, 1), (512, 4), (2048, 16))
MLA_Q_RANK = 256
MLA_KV_RANK = 128
MLA_NOPE = 64
MLA_ROPE = 32
MLA_V = 64
ROPE_THETA = 10000.0
REL_BUCKETS = 32
REL_MAX_DIST = 2048
PEER_HEADS = 8
PEER_KEYS = 128
PEER_DKEY = 256
PEER_TOPK = 16
LN_EPS = 1e-5
RMS_EPS = 1e-6
NEG = -1e30

LANE = 128
BF16_ROWS = 16
F32_ROWS = 8
MXU_WIDTH = 256
MLA_GROUP = 2
ROUTE_HEADS = 2
GELU_FOLD = 0.5 ** 0.5
ROUTE_TOKENS = 512
HEAD_BATCH = 16
DENSE_EC = 4 * PEER_KEYS
VMEM_LIMIT = 56 * 1024 * 1024

_NT = (((1,), (1,)), ((), ()))


def _params(*sem):
    return pltpu.CompilerParams(dimension_semantics=sem, vmem_limit_bytes=VMEM_LIMIT)


def _mm_kernel(x_ref, w_ref, o_ref):
    o_ref[...] = jnp.dot(x_ref[...].astype(bf16), w_ref[...],
                         preferred_element_type=f32).astype(o_ref.dtype)


def _mm(x, w, out_dtype, tm, tn):
    m, k = x.shape
    n = w.shape[1]
    tm, tn = min(tm, m), min(tn, n)
    return pl.pallas_call(
        _mm_kernel, grid=(m // tm, n // tn),
        in_specs=[pl.BlockSpec((tm, k), lambda i, j: (i, 0)),
                  pl.BlockSpec((k, tn), lambda i, j: (0, j))],
        out_specs=pl.BlockSpec((tm, tn), lambda i, j: (i, j)),
        out_shape=jax.ShapeDtypeStruct((m, n), out_dtype),
        compiler_params=_params("parallel", "parallel"))(x, w)


def _rel_bucket(dist):
    max_exact = REL_BUCKETS // 2
    n = jnp.maximum(dist, 0)
    nf = jnp.maximum(n, 1).astype(f32)
    large = max_exact + (jnp.log(nf / max_exact) / math.log(REL_MAX_DIST / max_exact)
                         * (REL_BUCKETS - max_exact)).astype(jnp.int32)
    large = jnp.minimum(large, REL_BUCKETS - 1)
    return jnp.where(n < max_exact, n, large)


def _bias_kernel(rel_ref, bucket_ref, valid_ref, o_ref):
    bucket = bucket_ref[...]
    for h in range(N_HEADS):
        acc = jnp.zeros(bucket.shape, f32)
        for b in range(REL_BUCKETS):
            acc = jnp.where(bucket == b, rel_ref[b, h], acc)
        for variant in range(2):
            o_ref[variant, h] = jnp.where(valid_ref[variant] != 0, acc, NEG)


def _band_bias(rel_bias, max_dist, dilation):
    qi = jnp.arange(BLOCK)[:, None]
    kj = jnp.arange(2 * BLOCK)[None, :]
    dist = BLOCK + qi - kj
    bucket = _rel_bucket(dist * dilation).astype(jnp.int32)
    valid = (dist >= 0) & (dist <= max_dist)
    valid = jnp.stack([valid & (kj >= BLOCK), valid]).astype(jnp.int32)
    return pl.pallas_call(
        _bias_kernel,
        in_specs=[pl.BlockSpec(memory_space=pltpu.SMEM),
                  pl.BlockSpec(memory_space=pltpu.VMEM),
                  pl.BlockSpec(memory_space=pltpu.VMEM)],
        out_specs=pl.BlockSpec(memory_space=pltpu.VMEM),
        out_shape=jax.ShapeDtypeStruct((2, N_HEADS, BLOCK, 2 * BLOCK), f32))(rel_bias, bucket, valid)


def _band_kernel(q_ref, kp_ref, kc_ref, vp_ref, vc_ref, bias_ref, *rest, group, with_sink):
    if with_sink:
        sink_ref, o_ref = rest
    else:
        o_ref, lse_ref = rest
    scale = HEAD_DIM ** -0.5
    for h0 in range(0, N_HEADS, HEAD_BATCH):
        heads = range(h0, h0 + HEAD_BATCH)
        qs = [slice(h * HEAD_DIM, (h + 1) * HEAD_DIM) for h in heads]
        ks = [slice((h // group) * HEAD_DIM, (h // group + 1) * HEAD_DIM) for h in heads]
        kk = [jnp.concatenate([kp_ref[0, :, c], kc_ref[0, :, c]], axis=0) for c in ks]
        vv = [jnp.concatenate([vp_ref[0, :, c], vc_ref[0, :, c]], axis=0) for c in ks]
        ss = [lax.dot_general(q_ref[0, :, c], k, _NT, preferred_element_type=f32) * scale + bias_ref[0, h]
              for h, c, k in zip(heads, qs, kk)]
        ms = [s.max(-1, keepdims=True) for s in ss]
        ps = [jnp.exp(s - m).astype(bf16) for s, m in zip(ss, ms)]
        ones = jnp.ones((2 * BLOCK, LANE), bf16)
        ls = [jnp.dot(p, ones, preferred_element_type=f32)[:, :HEAD_DIM] for p in ps]
        os_ = [jnp.dot(p, v, preferred_element_type=f32) / l for p, v, l in zip(ps, vv, ls)]
        lses = [m + jnp.log(l) for m, l in zip(ms, ls)]
        for c, o, lse in zip(qs, os_, lses):
            if with_sink:
                o = o * jax.nn.sigmoid(lse - sink_ref[:, c])
            else:
                lse_ref[0, :, c] = lse
            o_ref[0, :, c] = o.astype(o_ref.dtype)


def _band_attention(view, bias, dil, feat, q_col, k_col, v_col, kv_width, group, sink=None):
    b, m_len, _ = view.shape
    d_q = N_HEADS * HEAD_DIM
    nb = m_len // BLOCK
    qpr, kpr = feat // d_q, feat // kv_width

    def cur(col, per_row):
        return lambda bi, r, n: (bi, n, r * per_row + col)

    def prev(col, per_row):
        return lambda bi, r, n: (bi, jnp.maximum(n - 1, 0), r * per_row + col)

    in_specs = [pl.BlockSpec((1, BLOCK, d_q), cur(q_col, qpr)),
                pl.BlockSpec((1, BLOCK, kv_width), prev(k_col, kpr)),
                pl.BlockSpec((1, BLOCK, kv_width), cur(k_col, kpr)),
                pl.BlockSpec((1, BLOCK, kv_width), prev(v_col, kpr)),
                pl.BlockSpec((1, BLOCK, kv_width), cur(v_col, kpr)),
                pl.BlockSpec((1, N_HEADS, BLOCK, 2 * BLOCK),
                             lambda bi, r, n: (jnp.minimum(n, 1), 0, 0, 0))]
    args = [view, view, view, view, view, bias]
    o_spec = pl.BlockSpec((1, BLOCK, d_q), lambda bi, r, n: (bi, n, r))
    o_shape = jax.ShapeDtypeStruct((b, m_len, dil * d_q), bf16)
    if sink is not None:
        in_specs.append(pl.BlockSpec((1, d_q), lambda bi, r, n: (0, 0)))
        args.append(sink)
        out_specs, out_shape = o_spec, o_shape
    else:
        out_specs = [o_spec, o_spec]
        out_shape = [o_shape, jax.ShapeDtypeStruct((b, m_len, dil * d_q), f32)]
    out = pl.pallas_call(
        functools.partial(_band_kernel, group=group, with_sink=sink is not None),
        grid=(b, dil, nb), in_specs=in_specs, out_specs=out_specs, out_shape=out_shape,
        compiler_params=_params("parallel", "parallel", "arbitrary"))(*args)
    if sink is not None:
        return out.reshape(b * m_len, dil * d_q)
    return out[0].reshape(b * m_len, dil * d_q), out[1].reshape(b * m_len, dil * d_q)


def _layer_norm(z, g, b):
    mu = z.mean(-1, keepdims=True)
    zc = z - mu
    var = jnp.square(zc).mean(-1, keepdims=True)
    return zc * lax.rsqrt(var + LN_EPS) * g + b


def _proj_ln_kernel(*refs, dils, alpha):
    n_pat = len(dils)
    o_refs = refs[:n_pat]
    lse_refs = refs[n_pat:2 * n_pat] if n_pat > 1 else ()
    n_in = len(o_refs) + len(lse_refs)
    w_ref, x_ref, g_ref, b_ref, out_ref = refs[n_in:n_in + 5]
    scratch = refs[n_in + 5:]
    if n_pat == 1 and len(o_refs[0].shape) == 3:
        groups, _, width = o_refs[0].shape
        y = sum(jnp.dot(o_refs[0][gi], w_ref[gi * width:(gi + 1) * width, :], preferred_element_type=f32)
                for gi in range(groups))
        out_ref[...] = _layer_norm(alpha * x_ref[...] + y, g_ref[...], b_ref[...])
        return
    if n_pat == 1:
        o = o_refs[0][...]
    else:
        kw = w_ref.shape[0]
        mix_ref = scratch[-1]

        def natural(ref, dil, scr, c):
            if dil == 1:
                return ref[:, c * LANE:(c + 1) * LANE].astype(f32)
            rows = ref.shape[0]
            for r in range(dil):
                scr[c, pl.ds(r, rows, stride=dil), :] = (
                    ref[:, r * kw + c * LANE:r * kw + (c + 1) * LANE].astype(f32))
            return scr[c]

        for c in range(kw // LANE):
            lses = [natural(r, dl, scratch[2 * i], c) for i, (r, dl) in enumerate(zip(lse_refs, dils))]
            mx = functools.reduce(jnp.maximum, lses)
            es = [jnp.exp(l - mx) for l in lses]
            num = sum(e * natural(r, dl, scratch[2 * i + 1], c)
                      for i, (e, r, dl) in enumerate(zip(es, o_refs, dils)))
            mix_ref[:, c * LANE:(c + 1) * LANE] = (num / sum(es)).astype(bf16)
        o = mix_ref[...]
    y = jnp.dot(o, w_ref[...], preferred_element_type=f32)
    out_ref[...] = _layer_norm(alpha * x_ref[...] + y, g_ref[...], b_ref[...])


def _proj_ln(os_, lses, dils, w_out, x, g, b, alpha, tm=512):
    t, d = x.shape
    tm = min(tm, t)
    kw = w_out.shape[0]
    row = lambda width: pl.BlockSpec((tm, width), lambda i: (i, 0))
    dilated = lambda dl: pl.BlockSpec((tm // dl, dl * kw), lambda i: (i, 0))
    fixed = lambda shape: pl.BlockSpec(shape, lambda i: (0, 0))
    if os_[0].ndim == 3:
        o_specs = [pl.BlockSpec((os_[0].shape[0], tm, os_[0].shape[2]), lambda i: (0, i, 0))]
    else:
        o_specs = [dilated(dl) for dl in dils]
    in_specs = (o_specs + [dilated(dl) for dl in dils[:len(lses)]]
                + [fixed((kw, d)), row(d), fixed((1, d)), fixed((1, d))])
    scratch = [pltpu.VMEM((kw // LANE, tm, LANE), f32)] * (2 * len(lses))
    if lses:
        scratch.append(pltpu.VMEM((tm, kw), bf16))
    return pl.pallas_call(
        functools.partial(_proj_ln_kernel, dils=tuple(dils), alpha=alpha),
        grid=(t // tm,), in_specs=in_specs, out_specs=row(d),
        out_shape=jax.ShapeDtypeStruct((t, d), f32), scratch_shapes=scratch,
        compiler_params=_params("parallel"))(*os_, *lses, w_out, x, g.reshape(1, d), b.reshape(1, d))


def _mm_dil_kernel(x_ref, w_ref, o_ref, res_ref, *, dil):
    n = w_ref.shape[1]
    res = jnp.dot(x_ref[...].astype(bf16), w_ref[...], preferred_element_type=f32)
    if dil == 1:
        o_ref[...] = res.astype(o_ref.dtype)
        return
    rows = o_ref.shape[0]
    for c in range(n // LANE):
        res_ref[c] = res[:, c * LANE:(c + 1) * LANE]
        for r in range(dil):
            o_ref[:, r * n + c * LANE:r * n + (c + 1) * LANE] = (
                res_ref[c, pl.ds(r, rows, stride=dil), :].astype(o_ref.dtype))


def _mm_dil(x, w, dil, tm=512):
    t, k = x.shape
    n = w.shape[1]
    tm = min(tm, t)
    return pl.pallas_call(
        functools.partial(_mm_dil_kernel, dil=dil), grid=(t // tm,),
        in_specs=[pl.BlockSpec((tm, k), lambda i: (i, 0)), pl.BlockSpec((k, n), lambda i: (0, 0))],
        out_specs=pl.BlockSpec((tm // dil, dil * n), lambda i: (i, 0)),
        out_shape=jax.ShapeDtypeStruct((t // dil, dil * n), bf16),
        scratch_shapes=[pltpu.VMEM((n // LANE, tm, LANE), f32)],
        compiler_params=_params("parallel"))(x, w)


def _rms(c, g):
    return c * lax.rsqrt(jnp.square(c).mean(-1, keepdims=True) + RMS_EPS) * g


def _mla_prep_kernel(x_ref, win_ref, qn_ref, wq_ref, kvn_ref, wkv_ref, cos_ref, sin_ref,
                     q_ref, k_ref, v_ref):
    hw = N_HEADS * LANE
    scale = (MLA_NOPE + MLA_ROPE) ** -0.5
    cos, sin = cos_ref[...], sin_ref[...]
    xw = jnp.dot(x_ref[...].astype(bf16), win_ref[...], preferred_element_type=f32)
    cq = _rms(xw[:, :MLA_Q_RANK], qn_ref[...]).astype(bf16)
    ckv = _rms(xw[:, MLA_Q_RANK:MLA_Q_RANK + MLA_KV_RANK], kvn_ref[...]).astype(bf16)
    off = MLA_Q_RANK + MLA_KV_RANK
    kr = xw[:, off:off + LANE] * cos + xw[:, off + LANE:off + 2 * LANE] * sin
    qq = jnp.dot(cq, wq_ref[...], preferred_element_type=f32)
    kv = jnp.dot(ckv, wkv_ref[...], preferred_element_type=f32)
    ones_lane = lax.broadcasted_iota(jnp.int32, (1, LANE), 1) == MLA_V
    for h in range(N_HEADS):
        blk = slice(h * LANE, (h + 1) * LANE)
        swp = slice(hw + h * LANE, hw + (h + 1) * LANE)
        grp, sub = divmod(h, MLA_GROUP)
        dst = slice(sub * LANE, (sub + 1) * LANE)
        q_ref[grp, :, dst] = ((qq[:, blk] * cos + qq[:, swp] * sin) * scale).astype(bf16)
        k_ref[grp, :, dst] = (kv[:, blk] + kr).astype(bf16)
        v_ref[grp, :, dst] = jnp.where(ones_lane, 1.0, kv[:, swp]).astype(bf16)


def _mla_prep(x, w_in, q_norm, w_uq, kv_norm, w_ukv, seq, tm=512):
    t, d = x.shape
    tm = min(tm, seq)
    hw = N_HEADS * LANE
    half = MLA_ROPE // 2
    dq = MLA_NOPE + MLA_ROPE
    kr_w = w_in[:, MLA_Q_RANK + MLA_KV_RANK:]
    zeros = lambda n: jnp.zeros((d, n), f32)
    kr_a = jnp.concatenate([zeros(MLA_NOPE), kr_w, zeros(LANE - dq)], 1)
    kr_b = jnp.concatenate([zeros(MLA_NOPE), kr_w[:, half:], kr_w[:, :half], zeros(LANE - dq)], 1)
    win = jnp.concatenate([w_in[:, :MLA_Q_RANK + MLA_KV_RANK], kr_a, kr_b], 1).astype(bf16)
    wq3 = w_uq.reshape(MLA_Q_RANK, N_HEADS, dq)
    zq = lambda n: jnp.zeros((MLA_Q_RANK, N_HEADS, n), f32)
    wq_a = jnp.concatenate([wq3, zq(LANE - dq)], 2)
    wq_b = jnp.concatenate([zq(MLA_NOPE), wq3[:, :, MLA_NOPE + half:], wq3[:, :, MLA_NOPE:MLA_NOPE + half],
                            zq(LANE - dq)], 2)
    wq = jnp.concatenate([wq_a.reshape(MLA_Q_RANK, hw), wq_b.reshape(MLA_Q_RANK, hw)], 1).astype(bf16)
    wkv3 = w_ukv.reshape(MLA_KV_RANK, N_HEADS, MLA_NOPE + MLA_V)
    zk = lambda n: jnp.zeros((MLA_KV_RANK, N_HEADS, n), f32)
    wk = jnp.concatenate([wkv3[:, :, :MLA_NOPE], zk(LANE - MLA_NOPE)], 2)
    wv = jnp.concatenate([wkv3[:, :, MLA_NOPE:], zk(LANE - MLA_V)], 2)
    wkv = jnp.concatenate([wk.reshape(MLA_KV_RANK, hw), wv.reshape(MLA_KV_RANK, hw)], 1).astype(bf16)
    freq = ROPE_THETA ** (-jnp.arange(half, dtype=f32) / half)
    ang = jnp.arange(seq)[:, None].astype(f32) * freq[None, :]
    c, s_ = jnp.cos(ang), jnp.sin(ang)
    cos_t = jnp.concatenate([jnp.ones((seq, MLA_NOPE), f32), c, c, jnp.ones((seq, LANE - dq), f32)], 1)
    sin_t = jnp.concatenate([jnp.zeros((seq, MLA_NOPE), f32), -s_, s_, jnp.zeros((seq, LANE - dq), f32)], 1)

    n_pos = seq // tm
    row = lambda width: pl.BlockSpec((tm, width), lambda i: (i, 0))
    fixed = lambda a: pl.BlockSpec(a.shape, lambda i: (0, 0))
    pos = pl.BlockSpec((tm, LANE), lambda i: (i % n_pos, 0))
    qn, kvn = q_norm.reshape(1, -1), kv_norm.reshape(1, -1)
    groups, gw = N_HEADS // MLA_GROUP, MLA_GROUP * LANE
    out_sd = jax.ShapeDtypeStruct((groups, t, gw), bf16)
    out_spec = pl.BlockSpec((groups, tm, gw), lambda i: (0, i, 0))
    return pl.pallas_call(
        _mla_prep_kernel, grid=(t // tm,),
        in_specs=[row(d), fixed(win), fixed(qn), fixed(wq), fixed(kvn), fixed(wkv), pos, pos],
        out_specs=[out_spec] * 3, out_shape=[out_sd] * 3,
        compiler_params=_params("parallel"))(x, win, qn, wq, kvn, wkv, cos_t, sin_t)


def _mla_attn_kernel(q_ref, k_ref, v_ref, o_ref, *, tq, tk, heads):
    qi = pl.program_id(2)
    lanes = [slice(j * LANE, (j + 1) * LANE) for j in range(heads)]
    qs = [q_ref[0, :, ln] for ln in lanes]

    def update(carry, start, mask):
        ss = [lax.dot_general(q, k_ref[0, pl.ds(start, tk), ln], _NT, preferred_element_type=f32)
              for q, ln in zip(qs, lanes)]
        if mask is not None:
            ss = [jnp.where(mask, s, NEG) for s in ss]
        ms = [jnp.maximum(m, s.max(-1, keepdims=True)) for (m, _), s in zip(carry, ss)]
        ps = [jnp.exp(s - m).astype(bf16) for s, m in zip(ss, ms)]
        pvs = [jnp.dot(p, v_ref[0, pl.ds(start, tk), ln], preferred_element_type=f32)
               for p, ln in zip(ps, lanes)]
        return tuple((m_new, jnp.exp(m - m_new) * acc + pv)
                     for (m, acc), m_new, pv in zip(carry, ms, pvs))

    init = tuple((jnp.full((tq, 1), NEG, f32), jnp.zeros((tq, LANE), f32)) for _ in range(heads))
    sub = tq // tk
    carry = lax.fori_loop(0, qi * sub, lambda ki, c: update(c, pl.multiple_of(ki * tk, tk), None), init)
    row = lax.broadcasted_iota(jnp.int32, (tq, tk), 0)
    col = lax.broadcasted_iota(jnp.int32, (tq, tk), 1)
    for j in range(sub):
        carry = update(carry, pl.multiple_of(qi * tq + j * tk, tk), col + j * tk <= row)
    for j in range(heads):
        acc = carry[j][1]
        o_ref[0, :, lanes[j]] = (acc / acc[:, MLA_V:MLA_V + 1]).astype(o_ref.dtype)


def _mla_attention(q, k, v, batch, seq, tq=1024, tk=512):
    groups, t, gw = q.shape
    tq = min(tq, seq)
    nq = seq // tq
    qspec = pl.BlockSpec((1, tq, gw), lambda b, h, i: (h, b * nq + i, 0))
    kspec = pl.BlockSpec((1, seq, gw), lambda b, h, i: (h, b, 0))
    return pl.pallas_call(
        functools.partial(_mla_attn_kernel, tq=tq, tk=min(tk, tq), heads=MLA_GROUP),
        grid=(batch, groups, nq),
        in_specs=[qspec, kspec, kspec], out_specs=qspec,
        out_shape=jax.ShapeDtypeStruct((groups, t, gw), bf16),
        compiler_params=_params("parallel", "parallel", "arbitrary"))(q, k, v)


def _batcher_network(n):
    def merge(lo, hi, r):
        step = r * 2
        if step < hi - lo:
            yield from merge(lo, hi, step)
            yield from merge(lo + r, hi, step)
            yield from [(i, i + r) for i in range(lo + r, hi - r, step)]
        else:
            yield (lo, lo + r)

    def sort(lo, hi):
        if hi - lo >= 1:
            mid = lo + (hi - lo) // 2
            yield from sort(lo, mid)
            yield from sort(mid + 1, hi)
            yield from merge(lo, hi, 1)

    return tuple(sort(0, n - 1))


def _top16_values(s):
    n = s.shape[0] // F32_ROWS
    tiles = [s[j * F32_ROWS:(j + 1) * F32_ROWS] for j in range(n)]
    for i, j in _batcher_network(PEER_TOPK):
        if j < n:
            tiles[i], tiles[j] = jnp.maximum(tiles[i], tiles[j]), jnp.minimum(tiles[i], tiles[j])
    vals = []
    for k in range(PEER_TOPK):
        mx = tiles[0].max(axis=0, keepdims=True)
        vals.append(mx)
        hit = tiles[0] == mx
        for j in range(min(n, PEER_TOPK - 1 - k)):
            below = tiles[j + 1] if j + 1 < n else -jnp.inf
            tiles[j] = jnp.where(hit, below, tiles[j])
    return vals


def _top16_pair(s1, s2):
    v1, v2 = _top16_values(s1), _top16_values(s2)
    rank2 = jnp.zeros(s2.shape, f32)
    for k in range(PEER_TOPK):
        rank2 = jnp.where(s2 < v2[k], float(k + 1), rank2)
    return v1, v2, rank2


def _stack_rows(rows):
    n = len(rows)
    rid = lax.broadcasted_iota(jnp.int32, (n, rows[0].shape[1]), 0)
    out = jnp.broadcast_to(rows[0], (n, rows[0].shape[1]))
    for i in range(1, n):
        out = jnp.where(rid == i, rows[i], out)
    return out


def _route_kernel(x_ref, wq_ref, keys_ref, rank2_ref, crow_ref, e1z_ref, e2_ref, *, tm):
    q = jnp.dot(x_ref[...].astype(bf16), wq_ref[...], preferred_element_type=f32).astype(bf16)
    half = PEER_DKEY // 2
    scores = []
    for hh in range(ROUTE_HEADS):
        qh = q[:, hh * PEER_DKEY:(hh + 1) * PEER_DKEY]
        scores.append((lax.dot_general(keys_ref[hh, 0], qh[:, :half], _NT, preferred_element_type=f32),
                       lax.dot_general(keys_ref[hh, 1], qh[:, half:], _NT, preferred_element_type=f32)))
    rid8 = lax.broadcasted_iota(jnp.int32, (8, LANE), 0)
    for hh, c in [(hh, c) for hh in range(ROUTE_HEADS) for c in range(tm // LANE)]:
        lanes = slice(c * LANE, (c + 1) * LANE)
        s1, s2 = scores[hh][0][:, lanes], scores[hh][1][:, lanes]
        v1, v2, rank2 = _top16_pair(s1, s2)
        v2_all = _stack_rows(v2)
        cands = [v1[0] + v2_all]
        for a in range(1, 8):
            cands.append(jnp.where(rid8 < PEER_TOPK // (a + 1), v1[a] + v2_all[:8], -jnp.inf))
        cands.append(_stack_rows(v1[8:]) + v2[0])
        cand = jnp.concatenate(cands, axis=0)
        tau = _top16_values(cand)[-1]
        cmax = v1[0] + v2[0]
        z = jnp.where(cand >= tau, jnp.exp(cand - cmax), 0.0).sum(axis=0, keepdims=True)
        crow = jnp.zeros(s1.shape, f32)
        for b in range(PEER_TOPK):
            crow = crow + jnp.where(s1 + v2[b] >= tau, 1.0, 0.0)
        rank2_ref[0, hh, :, lanes] = rank2.astype(bf16)
        crow_ref[0, hh, :, lanes] = crow
        e1z_ref[0, hh, :, lanes] = jnp.exp(s1 - v1[0]) * (GELU_FOLD / z)
        e2_ref[0, hh, :, lanes] = jnp.exp(s2 - v2[0]).astype(bf16)


def _peer_route(x, w_q, keys, tm=ROUTE_TOKENS):
    t, d = x.shape
    tm = min(tm, t)
    hb = ROUTE_HEADS
    spec = pl.BlockSpec((1, hb, PEER_KEYS, tm), lambda i, h: (i, h, 0, 0))
    sd = lambda dt: jax.ShapeDtypeStruct((t // tm, PEER_HEADS, PEER_KEYS, tm), dt)
    return pl.pallas_call(
        functools.partial(_route_kernel, tm=tm), grid=(t // tm, PEER_HEADS // hb),
        in_specs=[pl.BlockSpec((tm, d), lambda i, h: (i, 0)),
                  pl.BlockSpec((d, hb * PEER_DKEY), lambda i, h: (0, h)),
                  pl.BlockSpec((hb, 2, PEER_KEYS, PEER_DKEY // 2), lambda i, h: (h, 0, 0, 0))],
        out_specs=[spec] * 4, out_shape=[sd(bf16), sd(f32), sd(f32), sd(bf16)],
        compiler_params=_params("parallel", "arbitrary"))(x, w_q, keys)


def _peer_dense_kernel(x_ref, u0_ref, u1_ref, vt0_ref, vt1_ref, rank2_in, crow_ref, e1z_ref, e2_in,
                       g_ref, b_ref, o_ref, acc_ref, xt_ref, ht0, ht1, gt0, gt1, rank2_ref, e2_ref,
                       *, rows, tb, lc, mw, alpha):
    k = pl.program_id(1)
    ec = rows * PEER_KEYS

    @pl.when(k == 0)
    def _():
        acc_ref[...] = jnp.zeros_like(acc_ref)
        ht1[...] = jnp.zeros_like(ht1)
        gt0[...] = jnp.zeros_like(gt0)
        xt_ref[...] = x_ref[...].T.astype(bf16)
        rank2_ref[...] = rank2_in[0]
        e2_ref[...] = e2_in[0]

    zero = jnp.zeros((), bf16)
    kt = PEER_KEYS // BF16_ROWS

    def up(half, ht, cw):
        lanes = slice(cw * mw, (cw + 1) * mw)
        ht[:, lanes] = jnp.dot((u0_ref, u1_ref)[half][...], xt_ref[:, lanes], preferred_element_type=f32)

    n_tiles = pl.num_programs(1) - 1

    def gate(ht, tile, half, gt, cw):
        base = pl.multiple_of(jnp.clip(tile, 0, n_tiles - 1) * F32_ROWS, F32_ROWS)
        for r in range(rows):
            keys = slice(r * PEER_KEYS, (r + 1) * PEER_KEYS)
            row = slice(half * rows + r, half * rows + r + 1)
            for c in range(cw * mw // lc, (cw + 1) * mw // lc):
                lanes = slice(c * lc, (c + 1) * lc)
                w = None
                for h in range(PEER_HEADS):
                    cr = crow_ref[0, h, pl.ds(base, F32_ROWS), lanes][row]
                    ez = e1z_ref[0, h, pl.ds(base, F32_ROWS), lanes][row]
                    cr = jnp.broadcast_to(cr, (BF16_ROWS, lc)).astype(bf16)[None]
                    ez = jnp.broadcast_to(ez, (BF16_ROWS, lc)).astype(bf16)[None]
                    term = jnp.where(rank2_ref[h, :, :, lanes] < cr, e2_ref[h, :, :, lanes] * ez, zero)
                    w = term if w is None else w + term
                hv = ht[keys, lanes]
                g = hv * (1.0 + lax.erf(hv))
                gt[r * kt:(r + 1) * kt, :, lanes] = g.astype(bf16).reshape(kt, BF16_ROWS, lc) * w

    def down(half, gt, cw):
        lanes = slice(cw * mw, (cw + 1) * mw)
        acc_ref[:, lanes] += jnp.dot((vt0_ref, vt1_ref)[half][0], gt[:, :, lanes].reshape(ec, mw),
                                     preferred_element_type=f32)

    for cw in range(tb // mw):
        up(0, ht0, cw)
        gate(ht1, k - 1, 1, gt1, cw)
        down(0, gt0, cw)
    for cw in range(tb // mw):
        up(1, ht1, cw)
        gate(ht0, k, 0, gt0, cw)
        down(1, gt1, cw)

    @pl.when(k == pl.num_programs(1) - 1)
    def _():
        z = alpha * x_ref[...] + acc_ref[...].T
        o_ref[...] = _layer_norm(z, g_ref[...], b_ref[...])


def _peer_dense(x, u, vt, routing, g, b, alpha, lc=128):
    t, d = x.shape
    e = u.shape[0]
    ec = DENSE_EC
    rows = ec // PEER_KEYS
    assert 2 * rows == F32_ROWS, "a pair of expert blocks must span one 8-row f32 tile of sub-key-1 rows"
    nk = e // (2 * ec)
    kt = PEER_KEYS // BF16_ROWS
    rank2, crow, e1z, e2 = routing
    nt, _, _, tb = crow.shape
    rank2, e2 = (a.reshape(nt, PEER_HEADS, kt, BF16_ROWS, tb) for a in (rank2, e2))
    full = pl.BlockSpec((1, PEER_HEADS, kt, BF16_ROWS, tb), lambda i, k: (i, 0, 0, 0, 0))
    rowtab = pl.BlockSpec((1, PEER_HEADS, PEER_KEYS, tb), lambda i, k: (i, 0, 0, 0))
    fixed = pl.BlockSpec((1, d), lambda i, k: (0, 0))
    xspec = pl.BlockSpec((tb, d), lambda i, k: (i, 0))
    xin = xspec
    gt_scr = pltpu.VMEM((ec // BF16_ROWS, BF16_ROWS, tb), bf16)
    u_blk = lambda half: pl.BlockSpec((ec, d), lambda i, k: (2 * jnp.minimum(k, nk - 1) + half, 0))
    vt_blk = lambda half: pl.BlockSpec((1, d, ec), lambda i, k: (2 * jnp.maximum(k - 1, 0) + half, 0, 0))
    return pl.pallas_call(
        functools.partial(_peer_dense_kernel, rows=rows, tb=tb, lc=lc, mw=MXU_WIDTH, alpha=alpha),
        grid=(t // tb, nk + 1),
        in_specs=[xin, u_blk(0), u_blk(1), vt_blk(0), vt_blk(1),
                  full, rowtab, rowtab, full, fixed, fixed],
        out_specs=xspec, out_shape=jax.ShapeDtypeStruct((t, d), f32),
        scratch_shapes=[pltpu.VMEM((d, tb), f32), pltpu.VMEM((d, tb), bf16),
                        pltpu.VMEM((ec, tb), f32), pltpu.VMEM((ec, tb), f32), gt_scr, gt_scr,
                        pltpu.VMEM((PEER_HEADS, kt, BF16_ROWS, tb), bf16),
                        pltpu.VMEM((PEER_HEADS, kt, BF16_ROWS, tb), bf16)],
        compiler_params=_params("parallel", "arbitrary"))(
            x, u, u, vt, vt, rank2, crow, e1z, e2, g.reshape(1, d), b.reshape(1, d))


def _swa_layer(x, batch, seq, w_in, sinks, w_out, bias, g, b, alpha):
    feat = (N_HEADS + 2 * SWA_KV_HEADS) * HEAD_DIM
    d_q = N_HEADS * HEAD_DIM
    kvw = SWA_KV_HEADS * HEAD_DIM
    proj = _mm(x, w_in.astype(bf16), bf16, 512, feat).reshape(batch, seq, feat)
    sink = jnp.repeat(sinks.astype(f32), HEAD_DIM).reshape(1, d_q)
    o = _band_attention(proj, bias, 1, feat, 0, d_q // kvw, d_q // kvw + 1, kvw,
                        N_HEADS // SWA_KV_HEADS, sink)
    return _proj_ln([o], [], [1], w_out.astype(bf16), x, g, b, alpha)


def _dil_layer(x, batch, seq, w_in, w_out, biases, g, b, alpha):
    d_q = N_HEADS * HEAD_DIM
    feat = 3 * d_q
    w_in = w_in.astype(bf16)
    outs, lses, dils = [], [], []
    for gi, (window, dil) in enumerate(DIL_PATTERNS):
        proj = _mm_dil(x, w_in[:, gi * feat:(gi + 1) * feat], dil).reshape(batch, seq // dil, dil * feat)
        o, lse = _band_attention(proj, biases[gi], dil, feat, 0, 1, 2, d_q, 1)
        outs.append(o)
        lses.append(lse)
        dils.append(dil)
    return _proj_ln(outs, lses, dils, w_out.astype(bf16), x, g, b, alpha)


def _mla_layer(x, batch, seq, w_in, q_norm, w_uq, kv_norm, w_ukv, w_out, g, b, alpha):
    q, k, v = _mla_prep(x, w_in, q_norm, w_uq, kv_norm, w_ukv, seq)
    o = _mla_attention(q, k, v, batch, seq)
    d = w_out.shape[1]
    w3 = w_out.reshape(N_HEADS, MLA_V, d)
    w_pad = jnp.concatenate([w3, jnp.zeros((N_HEADS, LANE - MLA_V, d), f32)], 1)
    return _proj_ln([o], [], [1], w_pad.reshape(N_HEADS * LANE, d).astype(bf16), x, g, b, alpha)


def _peer_layer(x, w_q, keys, u, v, g, b, alpha):
    routing = _peer_route(x, w_q.astype(bf16), keys.astype(bf16))
    e, d = v.shape
    vt = v.astype(bf16).reshape(e // DENSE_EC, DENSE_EC, d).transpose(0, 2, 1)
    return _peer_dense(x, (u * GELU_FOLD).astype(bf16), vt, routing, g, b, alpha)


def kernel(x, rel_bias, ln_g, ln_b, swa_w_in, swa_sinks, swa_w_out, dil_w_in, dil_w_out,
           mla_w_in, mla_q_norm, mla_w_uq, mla_kv_norm, mla_w_ukv, mla_w_out,
           peer_w_q, peer_keys, peer_u, peer_v):
    batch, seq, d = x.shape
    depth = ln_g.shape[0]
    alpha = (2 * depth) ** 0.25
    assert seq % (DIL_PATTERNS[-1][1] * BLOCK) == 0, "sequence must be a whole number of dilation segments"
    swa_bias = _band_bias(rel_bias, SWA_WINDOW - 1, 1)
    dil_bias = [_band_bias(rel_bias, window // dil, dil) for window, dil in DIL_PATTERNS]
    h = x.reshape(batch * seq, d)
    for i in range(depth):
        kind, j = i % 3, i // 3
        if kind == 0:
            h = _swa_layer(h, batch, seq, swa_w_in[j], swa_sinks[j], swa_w_out[j], swa_bias,
                           ln_g[i, 0], ln_b[i, 0], alpha)
        elif kind == 1:
            h = _dil_layer(h, batch, seq, dil_w_in[j], dil_w_out[j], dil_bias,
                           ln_g[i, 0], ln_b[i, 0], alpha)
        else:
            h = _mla_layer(h, batch, seq, mla_w_in[j], mla_q_norm[j], mla_w_uq[j], mla_kv_norm[j],
                           mla_w_ukv[j], mla_w_out[j], ln_g[i, 0], ln_b[i, 0], alpha)
        h = _peer_layer(h, peer_w_q[i], peer_keys[i], peer_u[i], peer_v[i], ln_g[i, 1], ln_b[i, 1], alpha)
    return h.reshape(batch, seq, d)
```

```python
import functools
import math

import jax
import jax.numpy as jnp
from jax import lax
from jax.experimental import pallas as pl
from jax.experimental.pallas import tpu as pltpu

f32 = jnp.float32
bf16 = jnp.bfloat16

N_HEADS = 16
HEAD_DIM = 64
BLOCK = 128
SWA_KV_HEADS = 2
SWA_WINDOW = 128
DIL_PATTERNS = ((128, 1), (512, 4), (2048, 16))
MLA_Q_RANK = 256
MLA_KV_RANK = 128
MLA_NOPE = 64
MLA_ROPE = 32
MLA_V = 64
ROPE_THETA = 10000.0
REL_BUCKETS = 32
REL_MAX_DIST = 2048
PEER_HEADS = 8
PEER_KEYS = 128
PEER_DKEY = 256
PEER_TOPK = 16
LN_EPS = 1e-5
RMS_EPS = 1e-6
NEG = -1e30

LANE = 128
BF16_ROWS = 16
F32_ROWS = 8
MXU_WIDTH = 256
MLA_GROUP = 2
ROUTE_HEADS = 2
GELU_FOLD = 0.5 ** 0.5
ROUTE_TOKENS = 512
HEAD_BATCH = 16
DENSE_EC = 4 * PEER_KEYS
VMEM_LIMIT = 56 * 1024 * 1024

_NT = (((1,), (1,)), ((), ()))


def _params(*sem):
    return pltpu.CompilerParams(dimension_semantics=sem, vmem_limit_bytes=VMEM_LIMIT)


def _mm_kernel(x_ref, w_ref, o_ref):
    o_ref[...] = jnp.dot(x_ref[...].astype(bf16), w_ref[...],
                         preferred_element_type=f32).astype(o_ref.dtype)


def _mm(x, w, out_dtype, tm, tn):
    m, k = x.shape
    n = w.shape[1]
    tm, tn = min(tm, m), min(tn, n)
    return pl.pallas_call(
        _mm_kernel, grid=(m // tm, n // tn),
        in_specs=[pl.BlockSpec((tm, k), lambda i, j: (i, 0)),
                  pl.BlockSpec((k, tn), lambda i, j: (0, j))],
        out_specs=pl.BlockSpec((tm, tn), lambda i, j: (i, j)),
        out_shape=jax.ShapeDtypeStruct((m, n), out_dtype),
        compiler_params=_params("parallel", "parallel"))(x, w)


def _rel_bucket(dist):
    max_exact = REL_BUCKETS // 2
    n = jnp.maximum(dist, 0)
    nf = jnp.maximum(n, 1).astype(f32)
    large = max_exact + (jnp.log(nf / max_exact) / math.log(REL_MAX_DIST / max_exact)
                         * (REL_BUCKETS - max_exact)).astype(jnp.int32)
    large = jnp.minimum(large, REL_BUCKETS - 1)
    return jnp.where(n < max_exact, n, large)


def _bias_kernel(rel_ref, bucket_ref, valid_ref, o_ref):
    bucket = bucket_ref[...]
    for h in range(N_HEADS):
        acc = jnp.zeros(bucket.shape, f32)
        for b in range(REL_BUCKETS):
            acc = jnp.where(bucket == b, rel_ref[b, h], acc)
        for variant in range(2):
            o_ref[variant, h] = jnp.where(valid_ref[variant] != 0, acc, NEG)


def _band_bias(rel_bias, max_dist, dilation):
    qi = jnp.arange(BLOCK)[:, None]
    kj = jnp.arange(2 * BLOCK)[None, :]
    dist = BLOCK + qi - kj
    bucket = _rel_bucket(dist * dilation).astype(jnp.int32)
    valid = (dist >= 0) & (dist <= max_dist)
    valid = jnp.stack([valid & (kj >= BLOCK), valid]).astype(jnp.int32)
    return pl.pallas_call(
        _bias_kernel,
        in_specs=[pl.BlockSpec(memory_space=pltpu.SMEM),
                  pl.BlockSpec(memory_space=pltpu.VMEM),
                  pl.BlockSpec(memory_space=pltpu.VMEM)],
        out_specs=pl.BlockSpec(memory_space=pltpu.VMEM),
        out_shape=jax.ShapeDtypeStruct((2, N_HEADS, BLOCK, 2 * BLOCK), f32))(rel_bias, bucket, valid)


def _band_kernel(q_ref, kp_ref, kc_ref, vp_ref, vc_ref, bias_ref, *rest, group, with_sink):
    if with_sink:
        sink_ref, o_ref = rest
    else:
        o_ref, lse_ref = rest
    scale = HEAD_DIM ** -0.5
    for h0 in range(0, N_HEADS, HEAD_BATCH):
        heads = range(h0, h0 + HEAD_BATCH)
        qs = [slice(h * HEAD_DIM, (h + 1) * HEAD_DIM) for h in heads]
        ks = [slice((h // group) * HEAD_DIM, (h // group + 1) * HEAD_DIM) for h in heads]
        kk = [jnp.concatenate([kp_ref[0, :, c], kc_ref[0, :, c]], axis=0) for c in ks]
        vv = [jnp.concatenate([vp_ref[0, :, c], vc_ref[0, :, c]], axis=0) for c in ks]
        ss = [lax.dot_general(q_ref[0, :, c], k, _NT, preferred_element_type=f32) * scale + bias_ref[0, h]
              for h, c, k in zip(heads, qs, kk)]
        ms = [s.max(-1, keepdims=True) for s in ss]
        ps = [jnp.exp(s - m).astype(bf16) for s, m in zip(ss, ms)]
        ones = jnp.ones((2 * BLOCK, LANE), bf16)
        ls = [jnp.dot(p, ones, preferred_element_type=f32)[:, :HEAD_DIM] for p in ps]
        os_ = [jnp.dot(p, v, preferred_element_type=f32) / l for p, v, l in zip(ps, vv, ls)]
        lses = [m + jnp.log(l) for m, l in zip(ms, ls)]
        for c, o, lse in zip(qs, os_, lses):
            if with_sink:
                o = o * jax.nn.sigmoid(lse - sink_ref[:, c])
            else:
                lse_ref[0, :, c] = lse
            o_ref[0, :, c] = o.astype(o_ref.dtype)


def _band_attention(view, bias, dil, feat, q_col, k_col, v_col, kv_width, group, sink=None):
    b, m_len, _ = view.shape
    d_q = N_HEADS * HEAD_DIM
    nb = m_len // BLOCK
    qpr, kpr = feat // d_q, feat // kv_width

    def cur(col, per_row):
        return lambda bi, r, n: (bi, n, r * per_row + col)

    def prev(col, per_row):
        return lambda bi, r, n: (bi, jnp.maximum(n - 1, 0), r * per_row + col)

    in_specs = [pl.BlockSpec((1, BLOCK, d_q), cur(q_col, qpr)),
                pl.BlockSpec((1, BLOCK, kv_width), prev(k_col, kpr)),
                pl.BlockSpec((1, BLOCK, kv_width), cur(k_col, kpr)),
                pl.BlockSpec((1, BLOCK, kv_width), prev(v_col, kpr)),
                pl.BlockSpec((1, BLOCK, kv_width), cur(v_col, kpr)),
                pl.BlockSpec((1, N_HEADS, BLOCK, 2 * BLOCK),
                             lambda bi, r, n: (jnp.minimum(n, 1), 0, 0, 0))]
    args = [view, view, view, view, view, bias]
    o_spec = pl.BlockSpec((1, BLOCK, d_q), lambda bi, r, n: (bi, n, r))
    o_shape = jax.ShapeDtypeStruct((b, m_len, dil * d_q), bf16)
    if sink is not None:
        in_specs.append(pl.BlockSpec((1, d_q), lambda bi, r, n: (0, 0)))
        args.append(sink)
        out_specs, out_shape = o_spec, o_shape
    else:
        out_specs = [o_spec, o_spec]
        out_shape = [o_shape, jax.ShapeDtypeStruct((b, m_len, dil * d_q), f32)]
    out = pl.pallas_call(
        functools.partial(_band_kernel, group=group, with_sink=sink is not None),
        grid=(b, dil, nb), in_specs=in_specs, out_specs=out_specs, out_shape=out_shape,
        compiler_params=_params("parallel", "parallel", "arbitrary"))(*args)
    if sink is not None:
        return out.reshape(b * m_len, dil * d_q)
    return out[0].reshape(b * m_len, dil * d_q), out[1].reshape(b * m_len, dil * d_q)


def _layer_norm(z, g, b):
    mu = z.mean(-1, keepdims=True)
    zc = z - mu
    var = jnp.square(zc).mean(-1, keepdims=True)
    return zc * lax.rsqrt(var + LN_EPS) * g + b


def _proj_ln_kernel(*refs, dils, alpha):
    n_pat = len(dils)
    o_refs = refs[:n_pat]
    lse_refs = refs[n_pat:2 * n_pat] if n_pat > 1 else ()
    n_in = len(o_refs) + len(lse_refs)
    w_ref, x_ref, g_ref, b_ref, out_ref = refs[n_in:n_in + 5]
    scratch = refs[n_in + 5:]
    if n_pat == 1 and len(o_refs[0].shape) == 3:
        groups, _, width = o_refs[0].shape
        y = sum(jnp.dot(o_refs[0][gi], w_ref[gi * width:(gi + 1) * width, :], preferred_element_type=f32)
                for gi in range(groups))
        out_ref[...] = _layer_norm(alpha * x_ref[...] + y, g_ref[...], b_ref[...])
        return
    if n_pat == 1:
        o = o_refs[0][...]
    else:
        kw = w_ref.shape[0]
        mix_ref = scratch[-1]

        def natural(ref, dil, scr, c):
            if dil == 1:
                return ref[:, c * LANE:(c + 1) * LANE].astype(f32)
            rows = ref.shape[0]
            for r in range(dil):
                scr[c, pl.ds(r, rows, stride=dil), :] = (
                    ref[:, r * kw + c * LANE:r * kw + (c + 1) * LANE].astype(f32))
            return scr[c]

        for c in range(kw // LANE):
            lses = [natural(r, dl, scratch[2 * i], c) for i, (r, dl) in enumerate(zip(lse_refs, dils))]
            mx = functools.reduce(jnp.maximum, lses)
            es = [jnp.exp(l - mx) for l in lses]
            num = sum(e * natural(r, dl, scratch[2 * i + 1], c)
                      for i, (e, r, dl) in enumerate(zip(es, o_refs, dils)))
            mix_ref[:, c * LANE:(c + 1) * LANE] = (num / sum(es)).astype(bf16)
        o = mix_ref[...]
    y = jnp.dot(o, w_ref[...], preferred_element_type=f32)
    out_ref[...] = _layer_norm(alpha * x_ref[...] + y, g_ref[...], b_ref[...])


def _proj_ln(os_, lses, dils, w_out, x, g, b, alpha, tm=512):
    t, d = x.shape
    tm = min(tm, t)
    kw = w_out.shape[0]
    row = lambda width: pl.BlockSpec((tm, width), lambda i: (i, 0))
    dilated = lambda dl: pl.BlockSpec((tm // dl, dl * kw), lambda i: (i, 0))
    fixed = lambda shape: pl.BlockSpec(shape, lambda i: (0, 0))
    if os_[0].ndim == 3:
        o_specs = [pl.BlockSpec((os_[0].shape[0], tm, os_[0].shape[2]), lambda i: (0, i, 0))]
    else:
        o_specs = [dilated(dl) for dl in dils]
    in_specs = (o_specs + [dilated(dl) for dl in dils[:len(lses)]]
                + [fixed((kw, d)), row(d), fixed((1, d)), fixed((1, d))])
    scratch = [pltpu.VMEM((kw // LANE, tm, LANE), f32)] * (2 * len(lses))
    if lses:
        scratch.append(pltpu.VMEM((tm, kw), bf16))
    return pl.pallas_call(
        functools.partial(_proj_ln_kernel, dils=tuple(dils), alpha=alpha),
        grid=(t // tm,), in_specs=in_specs, out_specs=row(d),
        out_shape=jax.ShapeDtypeStruct((t, d), f32), scratch_shapes=scratch,
        compiler_params=_params("parallel"))(*os_, *lses, w_out, x, g.reshape(1, d), b.reshape(1, d))


def _mm_dil_kernel(x_ref, w_ref, o_ref, res_ref, *, dil):
    n = w_ref.shape[1]
    res = jnp.dot(x_ref[...].astype(bf16), w_ref[...], preferred_element_type=f32)
    if dil == 1:
        o_ref[...] = res.astype(o_ref.dtype)
        return
    rows = o_ref.shape[0]
    for c in range(n // LANE):
        res_ref[c] = res[:, c * LANE:(c + 1) * LANE]
        for r in range(dil):
            o_ref[:, r * n + c * LANE:r * n + (c + 1) * LANE] = (
                res_ref[c, pl.ds(r, rows, stride=dil), :].astype(o_ref.dtype))


def _mm_dil(x, w, dil, tm=512):
    t, k = x.shape
    n = w.shape[1]
    tm = min(tm, t)
    return pl.pallas_call(
        functools.partial(_mm_dil_kernel, dil=dil), grid=(t // tm,),
        in_specs=[pl.BlockSpec((tm, k), lambda i: (i, 0)), pl.BlockSpec((k, n), lambda i: (0, 0))],
        out_specs=pl.BlockSpec((tm // dil, dil * n), lambda i: (i, 0)),
        out_shape=jax.ShapeDtypeStruct((t // dil, dil * n), bf16),
        scratch_shapes=[pltpu.VMEM((n // LANE, tm, LANE), f32)],
        compiler_params=_params("parallel"))(x, w)


def _rms(c, g):
    return c * lax.rsqrt(jnp.square(c).mean(-1, keepdims=True) + RMS_EPS) * g


def _mla_prep_kernel(x_ref, win_ref, qn_ref, wq_ref, kvn_ref, wkv_ref, cos_ref, sin_ref,
                     q_ref, k_ref, v_ref):
    hw = N_HEADS * LANE
    scale = (MLA_NOPE + MLA_ROPE) ** -0.5
    cos, sin = cos_ref[...], sin_ref[...]
    xw = jnp.dot(x_ref[...].astype(bf16), win_ref[...], preferred_element_type=f32)
    cq = _rms(xw[:, :MLA_Q_RANK], qn_ref[...]).astype(bf16)
    ckv = _rms(xw[:, MLA_Q_RANK:MLA_Q_RANK + MLA_KV_RANK], kvn_ref[...]).astype(bf16)
    off = MLA_Q_RANK + MLA_KV_RANK
    kr = xw[:, off:off + LANE] * cos + xw[:, off + LANE:off + 2 * LANE] * sin
    qq = jnp.dot(cq, wq_ref[...], preferred_element_type=f32)
    kv = jnp.dot(ckv, wkv_ref[...], preferred_element_type=f32)
    ones_lane = lax.broadcasted_iota(jnp.int32, (1, LANE), 1) == MLA_V
    for h in range(N_HEADS):
        blk = slice(h * LANE, (h + 1) * LANE)
        swp = slice(hw + h * LANE, hw + (h + 1) * LANE)
        grp, sub = divmod(h, MLA_GROUP)
        dst = slice(sub * LANE, (sub + 1) * LANE)
        q_ref[grp, :, dst] = ((qq[:, blk] * cos + qq[:, swp] * sin) * scale).astype(bf16)
        k_ref[grp, :, dst] = (kv[:, blk] + kr).astype(bf16)
        v_ref[grp, :, dst] = jnp.where(ones_lane, 1.0, kv[:, swp]).astype(bf16)


def _mla_prep(x, w_in, q_norm, w_uq, kv_norm, w_ukv, seq, tm=512):
    t, d = x.shape
    tm = min(tm, seq)
    hw = N_HEADS * LANE
    half = MLA_ROPE // 2
    dq = MLA_NOPE + MLA_ROPE
    kr_w = w_in[:, MLA_Q_RANK + MLA_KV_RANK:]
    zeros = lambda n: jnp.zeros((d, n), f32)
    kr_a = jnp.concatenate([zeros(MLA_NOPE), kr_w, zeros(LANE - dq)], 1)
    kr_b = jnp.concatenate([zeros(MLA_NOPE), kr_w[:, half:], kr_w[:, :half], zeros(LANE - dq)], 1)
    win = jnp.concatenate([w_in[:, :MLA_Q_RANK + MLA_KV_RANK], kr_a, kr_b], 1).astype(bf16)
    wq3 = w_uq.reshape(MLA_Q_RANK, N_HEADS, dq)
    zq = lambda n: jnp.zeros((MLA_Q_RANK, N_HEADS, n), f32)
    wq_a = jnp.concatenate([wq3, zq(LANE - dq)], 2)
    wq_b = jnp.concatenate([zq(MLA_NOPE), wq3[:, :, MLA_NOPE + half:], wq3[:, :, MLA_NOPE:MLA_NOPE + half],
                            zq(LANE - dq)], 2)
    wq = jnp.concatenate([wq_a.reshape(MLA_Q_RANK, hw), wq_b.reshape(MLA_Q_RANK, hw)], 1).astype(bf16)
    wkv3 = w_ukv.reshape(MLA_KV_RANK, N_HEADS, MLA_NOPE + MLA_V)
    zk = lambda n: jnp.zeros((MLA_KV_RANK, N_HEADS, n), f32)
    wk = jnp.concatenate([wkv3[:, :, :MLA_NOPE], zk(LANE - MLA_NOPE)], 2)
    wv = jnp.concatenate([wkv3[:, :, MLA_NOPE:], zk(LANE - MLA_V)], 2)
    wkv = jnp.concatenate([wk.reshape(MLA_KV_RANK, hw), wv.reshape(MLA_KV_RANK, hw)], 1).astype(bf16)
    freq = ROPE_THETA ** (-jnp.arange(half, dtype=f32) / half)
    ang = jnp.arange(seq)[:, None].astype(f32) * freq[None, :]
    c, s_ = jnp.cos(ang), jnp.sin(ang)
    cos_t = jnp.concatenate([jnp.ones((seq, MLA_NOPE), f32), c, c, jnp.ones((seq, LANE - dq), f32)], 1)
    sin_t = jnp.concatenate([jnp.zeros((seq, MLA_NOPE), f32), -s_, s_, jnp.zeros((seq, LANE - dq), f32)], 1)

    n_pos = seq // tm
    row = lambda width: pl.BlockSpec((tm, width), lambda i: (i, 0))
    fixed = lambda a: pl.BlockSpec(a.shape, lambda i: (0, 0))
    pos = pl.BlockSpec((tm, LANE), lambda i: (i % n_pos, 0))
    qn, kvn = q_norm.reshape(1, -1), kv_norm.reshape(1, -1)
    groups, gw = N_HEADS // MLA_GROUP, MLA_GROUP * LANE
    out_sd = jax.ShapeDtypeStruct((groups, t, gw), bf16)
    out_spec = pl.BlockSpec((groups, tm, gw), lambda i: (0, i, 0))
    return pl.pallas_call(
        _mla_prep_kernel, grid=(t // tm,),
        in_specs=[row(d), fixed(win), fixed(qn), fixed(wq), fixed(kvn), fixed(wkv), pos, pos],
        out_specs=[out_spec] * 3, out_shape=[out_sd] * 3,
        compiler_params=_params("parallel"))(x, win, qn, wq, kvn, wkv, cos_t, sin_t)


def _mla_attn_kernel(q_ref, k_ref, v_ref, o_ref, *, tq, tk, heads):
    qi = pl.program_id(2)
    lanes = [slice(j * LANE, (j + 1) * LANE) for j in range(heads)]
    qs = [q_ref[0, :, ln] for ln in lanes]

    def update(carry, start, mask):
        ss = [lax.dot_general(q, k_ref[0, pl.ds(start, tk), ln], _NT, preferred_element_type=f32)
              for q, ln in zip(qs, lanes)]
        if mask is not None:
            ss = [jnp.where(mask, s, NEG) for s in ss]
        ms = [jnp.maximum(m, s.max(-1, keepdims=True)) for (m, _), s in zip(carry, ss)]
        ps = [jnp.exp(s - m).astype(bf16) for s, m in zip(ss, ms)]
        pvs = [jnp.dot(p, v_ref[0, pl.ds(start, tk), ln], preferred_element_type=f32)
               for p, ln in zip(ps, lanes)]
        return tuple((m_new, jnp.exp(m - m_new) * acc + pv)
                     for (m, acc), m_new, pv in zip(carry, ms, pvs))

    init = tuple((jnp.full((tq, 1), NEG, f32), jnp.zeros((tq, LANE), f32)) for _ in range(heads))
    sub = tq // tk
    carry = lax.fori_loop(0, qi * sub, lambda ki, c: update(c, pl.multiple_of(ki * tk, tk), None), init)
    row = lax.broadcasted_iota(jnp.int32, (tq, tk), 0)
    col = lax.broadcasted_iota(jnp.int32, (tq, tk), 1)
    for j in range(sub):
        carry = update(carry, pl.multiple_of(qi * tq + j * tk, tk), col + j * tk <= row)
    for j in range(heads):
        acc = carry[j][1]
        o_ref[0, :, lanes[j]] = (acc / acc[:, MLA_V:MLA_V + 1]).astype(o_ref.dtype)


def _mla_attention(q, k, v, batch, seq, tq=1024, tk=512):
    groups, t, gw = q.shape
    tq = min(tq, seq)
    nq = seq // tq
    qspec = pl.BlockSpec((1, tq, gw), lambda b, h, i: (h, b * nq + i, 0))
    kspec = pl.BlockSpec((1, seq, gw), lambda b, h, i: (h, b, 0))
    return pl.pallas_call(
        functools.partial(_mla_attn_kernel, tq=tq, tk=min(tk, tq), heads=MLA_GROUP),
        grid=(batch, groups, nq),
        in_specs=[qspec, kspec, kspec], out_specs=qspec,
        out_shape=jax.ShapeDtypeStruct((groups, t, gw), bf16),
        compiler_params=_params("parallel", "parallel", "arbitrary"))(q, k, v)


def _batcher_network(n):
    def merge(lo, hi, r):
        step = r * 2
        if step < hi - lo:
            yield from merge(lo, hi, step)
            yield from merge(lo + r, hi, step)
            yield from [(i, i + r) for i in range(lo + r, hi - r, step)]
        else:
            yield (lo, lo + r)

    def sort(lo, hi):
        if hi - lo >= 1:
            mid = lo + (hi - lo) // 2
            yield from sort(lo, mid)
            yield from sort(mid + 1, hi)
            yield from merge(lo, hi, 1)

    return tuple(sort(0, n - 1))


def _top16_values(s):
    n = s.shape[0] // F32_ROWS
    tiles = [s[j * F32_ROWS:(j + 1) * F32_ROWS] for j in range(n)]
    for i, j in _batcher_network(PEER_TOPK):
        if j < n:
            tiles[i], tiles[j] = jnp.maximum(tiles[i], tiles[j]), jnp.minimum(tiles[i], tiles[j])
    vals = []
    for k in range(PEER_TOPK):
        mx = tiles[0].max(axis=0, keepdims=True)
        vals.append(mx)
        hit = tiles[0] == mx
        for j in range(min(n, PEER_TOPK - 1 - k)):
            below = tiles[j + 1] if j + 1 < n else -jnp.inf
            tiles[j] = jnp.where(hit, below, tiles[j])
    return vals


def _top16_pair(s1, s2):
    v1, v2 = _top16_values(s1), _top16_values(s2)
    rank2 = jnp.zeros(s2.shape, f32)
    for k in range(PEER_TOPK):
        rank2 = jnp.where(s2 < v2[k], float(k + 1), rank2)
    return v1, v2, rank2


def _stack_rows(rows):
    n = len(rows)
    rid = lax.broadcasted_iota(jnp.int32, (n, rows[0].shape[1]), 0)
    out = jnp.broadcast_to(rows[0], (n, rows[0].shape[1]))
    for i in range(1, n):
        out = jnp.where(rid == i, rows[i], out)
    return out


def _route_kernel(x_ref, wq_ref, keys_ref, rank2_ref, crow_ref, e1z_ref, e2_ref, *, tm):
    q = jnp.dot(x_ref[...].astype(bf16), wq_ref[...], preferred_element_type=f32).astype(bf16)
    half = PEER_DKEY // 2
    scores = []
    for hh in range(ROUTE_HEADS):
        qh = q[:, hh * PEER_DKEY:(hh + 1) * PEER_DKEY]
        scores.append((lax.dot_general(keys_ref[hh, 0], qh[:, :half], _NT, preferred_element_type=f32),
                       lax.dot_general(keys_ref[hh, 1], qh[:, half:], _NT, preferred_element_type=f32)))
    rid8 = lax.broadcasted_iota(jnp.int32, (8, LANE), 0)
    for hh, c in [(hh, c) for hh in range(ROUTE_HEADS) for c in range(tm // LANE)]:
        lanes = slice(c * LANE, (c + 1) * LANE)
        s1, s2 = scores[hh][0][:, lanes], scores[hh][1][:, lanes]
        v1, v2, rank2 = _top16_pair(s1, s2)
        v2_all = _stack_rows(v2)
        cands = [v1[0] + v2_all]
        for a in range(1, 8):
            cands.append(jnp.where(rid8 < PEER_TOPK // (a + 1), v1[a] + v2_all[:8], -jnp.inf))
        cands.append(_stack_rows(v1[8:]) + v2[0])
        cand = jnp.concatenate(cands, axis=0)
        tau = _top16_values(cand)[-1]
        cmax = v1[0] + v2[0]
        z = jnp.where(cand >= tau, jnp.exp(cand - cmax), 0.0).sum(axis=0, keepdims=True)
        crow = jnp.zeros(s1.shape, f32)
        for b in range(PEER_TOPK):
            crow = crow + jnp.where(s1 + v2[b] >= tau, 1.0, 0.0)
        rank2_ref[0, hh, :, lanes] = rank2.astype(bf16)
        crow_ref[0, hh, :, lanes] = crow
        e1z_ref[0, hh, :, lanes] = jnp.exp(s1 - v1[0]) * (GELU_FOLD / z)
        e2_ref[0, hh, :, lanes] = jnp.exp(s2 - v2[0]).astype(bf16)


def _peer_route(x, w_q, keys, tm=ROUTE_TOKENS):
    t, d = x.shape
    tm = min(tm, t)
    hb = ROUTE_HEADS
    spec = pl.BlockSpec((1, hb, PEER_KEYS, tm), lambda i, h: (i, h, 0, 0))
    sd = lambda dt: jax.ShapeDtypeStruct((t // tm, PEER_HEADS, PEER_KEYS, tm), dt)
    return pl.pallas_call(
        functools.partial(_route_kernel, tm=tm), grid=(t // tm, PEER_HEADS // hb),
        in_specs=[pl.BlockSpec((tm, d), lambda i, h: (i, 0)),
                  pl.BlockSpec((d, hb * PEER_DKEY), lambda i, h: (0, h)),
                  pl.BlockSpec((hb, 2, PEER_KEYS, PEER_DKEY // 2), lambda i, h: (h, 0, 0, 0))],
        out_specs=[spec] * 4, out_shape=[sd(bf16), sd(f32), sd(f32), sd(bf16)],
        compiler_params=_params("parallel", "arbitrary"))(x, w_q, keys)


def _peer_dense_kernel(x_ref, u0_ref, u1_ref, vt0_ref, vt1_ref, rank2_in, crow_ref, e1z_ref, e2_in,
                       g_ref, b_ref, o_ref, acc_ref, xt_ref, ht0, ht1, gt0, gt1, rank2_ref, e2_ref,
                       *, rows, tb, lc, mw, alpha):
    k = pl.program_id(1)
    ec = rows * PEER_KEYS

    @pl.when(k == 0)
    def _():
        acc_ref[...] = jnp.zeros_like(acc_ref)
        ht1[...] = jnp.zeros_like(ht1)
        gt0[...] = jnp.zeros_like(gt0)
        xt_ref[...] = x_ref[...].T.astype(bf16)
        rank2_ref[...] = rank2_in[0]
        e2_ref[...] = e2_in[0]

    zero = jnp.zeros((), bf16)
    kt = PEER_KEYS // BF16_ROWS

    def up(half, ht, cw):
        lanes = slice(cw * mw, (cw + 1) * mw)
        u_blk = pltpu.bitcast((u0_ref, u1_ref)[half][...], bf16)
        ht[:, lanes] = jnp.dot(u_blk, xt_ref[:, lanes], preferred_element_type=f32)

    n_tiles = pl.num_programs(1) - 1

    def gate(ht, tile, half, gt, cw):
        base = pl.multiple_of(jnp.clip(tile, 0, n_tiles - 1) * F32_ROWS, F32_ROWS)
        for r in range(rows):
            keys = slice(r * PEER_KEYS, (r + 1) * PEER_KEYS)
            row = slice(half * rows + r, half * rows + r + 1)
            for c in range(cw * mw // lc, (cw + 1) * mw // lc):
                lanes = slice(c * lc, (c + 1) * lc)
                w = None
                for h in range(PEER_HEADS):
                    cr = crow_ref[0, h, pl.ds(base, F32_ROWS), lanes][row]
                    ez = e1z_ref[0, h, pl.ds(base, F32_ROWS), lanes][row]
                    cr = jnp.broadcast_to(cr, (BF16_ROWS, lc)).astype(bf16)[None]
                    ez = jnp.broadcast_to(ez, (BF16_ROWS, lc)).astype(bf16)[None]
                    term = jnp.where(rank2_ref[h, :, :, lanes] < cr, e2_ref[h, :, :, lanes] * ez, zero)
                    w = term if w is None else w + term
                hv = ht[keys, lanes]
                g = hv * (1.0 + lax.erf(hv))
                gt[r * kt:(r + 1) * kt, :, lanes] = g.astype(bf16).reshape(kt, BF16_ROWS, lc) * w

    def down(half, gt, cw):
        lanes = slice(cw * mw, (cw + 1) * mw)
        vt_blk = pltpu.bitcast((vt0_ref, vt1_ref)[half][0], bf16)
        acc_ref[:, lanes] += jnp.dot(vt_blk, gt[:, :, lanes].reshape(ec, mw), preferred_element_type=f32)

    for cw in range(tb // mw):
        up(0, ht0, cw)
        gate(ht1, k - 1, 1, gt1, cw)
        down(0, gt0, cw)
    for cw in range(tb // mw):
        up(1, ht1, cw)
        gate(ht0, k, 0, gt0, cw)
        down(1, gt1, cw)

    @pl.when(k == pl.num_programs(1) - 1)
    def _():
        z = alpha * x_ref[...] + acc_ref[...].T
        o_ref[...] = _layer_norm(z, g_ref[...], b_ref[...])


def _peer_dense(x, u, vt, routing, g, b, alpha, lc=128):
    t, d = x.shape
    e = 2 * u.shape[0]
    ec = DENSE_EC
    rows = ec // PEER_KEYS
    assert 2 * rows == F32_ROWS, "a pair of expert blocks must span one 8-row f32 tile of sub-key-1 rows"
    nk = e // (2 * ec)
    kt = PEER_KEYS // BF16_ROWS
    rank2, crow, e1z, e2 = routing
    nt, _, _, tb = crow.shape
    rank2, e2 = (a.reshape(nt, PEER_HEADS, kt, BF16_ROWS, tb) for a in (rank2, e2))
    full = pl.BlockSpec((1, PEER_HEADS, kt, BF16_ROWS, tb), lambda i, k: (i, 0, 0, 0, 0))
    rowtab = pl.BlockSpec((1, PEER_HEADS, PEER_KEYS, tb), lambda i, k: (i, 0, 0, 0))
    fixed = pl.BlockSpec((1, d), lambda i, k: (0, 0))
    xspec = pl.BlockSpec((tb, d), lambda i, k: (i, 0))
    xin = xspec
    gt_scr = pltpu.VMEM((ec // BF16_ROWS, BF16_ROWS, tb), bf16)
    u_blk = lambda half: pl.BlockSpec((ec // 2, d), lambda i, k: (2 * jnp.minimum(k, nk - 1) + half, 0))
    vt_blk = lambda half: pl.BlockSpec((1, d // 2, ec), lambda i, k: (2 * jnp.maximum(k - 1, 0) + half, 0, 0))
    return pl.pallas_call(
        functools.partial(_peer_dense_kernel, rows=rows, tb=tb, lc=lc, mw=MXU_WIDTH, alpha=alpha),
        grid=(t // tb, nk + 1),
        in_specs=[xin, u_blk(0), u_blk(1), vt_blk(0), vt_blk(1),
                  full, rowtab, rowtab, full, fixed, fixed],
        out_specs=xspec, out_shape=jax.ShapeDtypeStruct((t, d), f32),
        scratch_shapes=[pltpu.VMEM((d, tb), f32), pltpu.VMEM((d, tb), bf16),
                        pltpu.VMEM((ec, tb), f32), pltpu.VMEM((ec, tb), f32), gt_scr, gt_scr,
                        pltpu.VMEM((PEER_HEADS, kt, BF16_ROWS, tb), bf16),
                        pltpu.VMEM((PEER_HEADS, kt, BF16_ROWS, tb), bf16)],
        compiler_params=_params("parallel", "arbitrary"))(
            x, u, u, vt, vt, rank2, crow, e1z, e2, g.reshape(1, d), b.reshape(1, d))


def _swa_layer(x, batch, seq, w_in, sinks, w_out, bias, g, b, alpha):
    feat = (N_HEADS + 2 * SWA_KV_HEADS) * HEAD_DIM
    d_q = N_HEADS * HEAD_DIM
    kvw = SWA_KV_HEADS * HEAD_DIM
    proj = _mm(x, w_in.astype(bf16), bf16, 512, feat).reshape(batch, seq, feat)
    sink = jnp.repeat(sinks.astype(f32), HEAD_DIM).reshape(1, d_q)
    o = _band_attention(proj, bias, 1, feat, 0, d_q // kvw, d_q // kvw + 1, kvw,
                        N_HEADS // SWA_KV_HEADS, sink)
    return _proj_ln([o], [], [1], w_out.astype(bf16), x, g, b, alpha)


def _dil_layer(x, batch, seq, w_in, w_out, biases, g, b, alpha):
    d_q = N_HEADS * HEAD_DIM
    feat = 3 * d_q
    w_in = w_in.astype(bf16)
    outs, lses, dils = [], [], []
    for gi, (window, dil) in enumerate(DIL_PATTERNS):
        proj = _mm_dil(x, w_in[:, gi * feat:(gi + 1) * feat], dil).reshape(batch, seq // dil, dil * feat)
        o, lse = _band_attention(proj, biases[gi], dil, feat, 0, 1, 2, d_q, 1)
        outs.append(o)
        lses.append(lse)
        dils.append(dil)
    return _proj_ln(outs, lses, dils, w_out.astype(bf16), x, g, b, alpha)


def _mla_layer(x, batch, seq, w_in, q_norm, w_uq, kv_norm, w_ukv, w_out, g, b, alpha):
    q, k, v = _mla_prep(x, w_in, q_norm, w_uq, kv_norm, w_ukv, seq)
    o = _mla_attention(q, k, v, batch, seq)
    d = w_out.shape[1]
    w3 = w_out.reshape(N_HEADS, MLA_V, d)
    w_pad = jnp.concatenate([w3, jnp.zeros((N_HEADS, LANE - MLA_V, d), f32)], 1)
    return _proj_ln([o], [], [1], w_pad.reshape(N_HEADS * LANE, d).astype(bf16), x, g, b, alpha)


def _pack_row_pairs(w):
    *lead, m2, n = w.shape
    pairs = jnp.swapaxes(w.reshape(*lead, m2 // 2, 2, n), -1, -2)
    return lax.bitcast_convert_type(pairs, jnp.uint32)


def _peer_layer(x, w_q, keys, u, v, g, b, alpha):
    routing = _peer_route(x, w_q.astype(bf16), keys.astype(bf16))
    e, d = v.shape
    vt = v.astype(bf16).reshape(e // DENSE_EC, DENSE_EC, d).transpose(0, 2, 1)
    return _peer_dense(x, _pack_row_pairs((u * GELU_FOLD).astype(bf16)), _pack_row_pairs(vt),
                       routing, g, b, alpha)


def kernel(x, rel_bias, ln_g, ln_b, swa_w_in, swa_sinks, swa_w_out, dil_w_in, dil_w_out,
           mla_w_in, mla_q_norm, mla_w_uq, mla_kv_norm, mla_w_ukv, mla_w_out,
           peer_w_q, peer_keys, peer_u, peer_v):
    batch, seq, d = x.shape
    depth = ln_g.shape[0]
    alpha = (2 * depth) ** 0.25
    assert seq % (DIL_PATTERNS[-1][1] * BLOCK) == 0, "sequence must be a whole number of dilation segments"
    swa_bias = _band_bias(rel_bias, SWA_WINDOW - 1, 1)
    dil_bias = [_band_bias(rel_bias, window // dil, dil) for window, dil in DIL_PATTERNS]
    h = x.reshape(batch * seq, d)
    for i in range(depth):
        kind, j = i % 3, i // 3
        if kind == 0:
            h = _swa_layer(h, batch, seq, swa_w_in[j], swa_sinks[j], swa_w_out[j], swa_bias,
                           ln_g[i, 0], ln_b[i, 0], alpha)
        elif kind == 1:
            h = _dil_layer(h, batch, seq, dil_w_in[j], dil_w_out[j], dil_bias,
                           ln_g[i, 0], ln_b[i, 0], alpha)
        else:
            h = _mla_layer(h, batch, seq, mla_w_in[j], mla_q_norm[j], mla_w_uq[j], mla_kv_norm[j],
                           mla_w_ukv[j], mla_w_out[j], ln_g[i, 0], ln_b[i, 0], alpha)
        h = _peer_layer(h, peer_w_q[i], peer_keys[i], peer_u[i], peer_v[i], ln_g[i, 1], ln_b[i, 1], alpha)
    return h.reshape(batch, seq, d)
```

```python
import functools
import math

import jax
import jax.numpy as jnp
from jax import lax
from jax.experimental import pallas as pl
from jax.experimental.pallas import tpu as pltpu

f32 = jnp.float32
bf16 = jnp.bfloat16

N_HEADS = 16
HEAD_DIM = 64
BLOCK = 128
SWA_KV_HEADS = 2
SWA_WINDOW = 128
DIL_PATTERNS = ((128, 1), (512, 4), (2048, 16))
MLA_Q_RANK = 256
MLA_KV_RANK = 128
MLA_NOPE = 64
MLA_ROPE = 32
MLA_V = 64
ROPE_THETA = 10000.0
REL_BUCKETS = 32
REL_MAX_DIST = 2048
PEER_HEADS = 8
PEER_KEYS = 128
PEER_DKEY = 256
PEER_TOPK = 16
LN_EPS = 1e-5
RMS_EPS = 1e-6
NEG = -1e30

LANE = 128
BF16_ROWS = 16
F32_ROWS = 8
MXU_WIDTH = 256
MLA_GROUP = 2
ROUTE_HEADS = 2
GELU_FOLD = 0.5 ** 0.5
ROUTE_TOKENS = 512
HEAD_BATCH = 16
DENSE_EC = 4 * PEER_KEYS
VMEM_LIMIT = 56 * 1024 * 1024

_NT = (((1,), (1,)), ((), ()))


def _params(*sem):
    return pltpu.CompilerParams(dimension_semantics=sem, vmem_limit_bytes=VMEM_LIMIT)


def _mm_kernel(x_ref, w_ref, o_ref):
    o_ref[...] = jnp.dot(x_ref[...].astype(bf16), w_ref[...],
                         preferred_element_type=f32).astype(o_ref.dtype)


def _mm(x, w, out_dtype, tm, tn):
    m, k = x.shape
    n = w.shape[1]
    tm, tn = min(tm, m), min(tn, n)
    return pl.pallas_call(
        _mm_kernel, grid=(m // tm, n // tn),
        in_specs=[pl.BlockSpec((tm, k), lambda i, j: (i, 0)),
                  pl.BlockSpec((k, tn), lambda i, j: (0, j))],
        out_specs=pl.BlockSpec((tm, tn), lambda i, j: (i, j)),
        out_shape=jax.ShapeDtypeStruct((m, n), out_dtype),
        compiler_params=_params("parallel", "parallel"))(x, w)


def _rel_bucket(dist):
    max_exact = REL_BUCKETS // 2
    n = jnp.maximum(dist, 0)
    nf = jnp.maximum(n, 1).astype(f32)
    large = max_exact + (jnp.log(nf / max_exact) / math.log(REL_MAX_DIST / max_exact)
                         * (REL_BUCKETS - max_exact)).astype(jnp.int32)
    large = jnp.minimum(large, REL_BUCKETS - 1)
    return jnp.where(n < max_exact, n, large)


def _bias_kernel(rel_ref, bucket_ref, valid_ref, o_ref):
    bucket = bucket_ref[...]
    for h in range(N_HEADS):
        acc = jnp.zeros(bucket.shape, f32)
        for b in range(REL_BUCKETS):
            acc = jnp.where(bucket == b, rel_ref[b, h], acc)
        for variant in range(2):
            o_ref[variant, h] = jnp.where(valid_ref[variant] != 0, acc, NEG)


def _band_bias(rel_bias, max_dist, dilation):
    qi = jnp.arange(BLOCK)[:, None]
    kj = jnp.arange(2 * BLOCK)[None, :]
    dist = BLOCK + qi - kj
    bucket = _rel_bucket(dist * dilation).astype(jnp.int32)
    valid = (dist >= 0) & (dist <= max_dist)
    valid = jnp.stack([valid & (kj >= BLOCK), valid]).astype(jnp.int32)
    return pl.pallas_call(
        _bias_kernel,
        in_specs=[pl.BlockSpec(memory_space=pltpu.SMEM),
                  pl.BlockSpec(memory_space=pltpu.VMEM),
                  pl.BlockSpec(memory_space=pltpu.VMEM)],
        out_specs=pl.BlockSpec(memory_space=pltpu.VMEM),
        out_shape=jax.ShapeDtypeStruct((2, N_HEADS, BLOCK, 2 * BLOCK), f32))(rel_bias, bucket, valid)


def _band_kernel(q_ref, kp_ref, kc_ref, vp_ref, vc_ref, bias_ref, *rest, group, with_sink):
    if with_sink:
        sink_ref, o_ref = rest
    else:
        o_ref, lse_ref = rest
    scale = HEAD_DIM ** -0.5
    for h0 in range(0, N_HEADS, HEAD_BATCH):
        heads = range(h0, h0 + HEAD_BATCH)
        qs = [slice(h * HEAD_DIM, (h + 1) * HEAD_DIM) for h in heads]
        ks = [slice((h // group) * HEAD_DIM, (h // group + 1) * HEAD_DIM) for h in heads]
        kk = [jnp.concatenate([kp_ref[0, :, c], kc_ref[0, :, c]], axis=0) for c in ks]
        vv = [jnp.concatenate([vp_ref[0, :, c], vc_ref[0, :, c]], axis=0) for c in ks]
        ss = [lax.dot_general(q_ref[0, :, c], k, _NT, preferred_element_type=f32) * scale + bias_ref[0, h]
              for h, c, k in zip(heads, qs, kk)]
        ms = [s.max(-1, keepdims=True) for s in ss]
        ps = [jnp.exp(s - m).astype(bf16) for s, m in zip(ss, ms)]
        ones = jnp.ones((2 * BLOCK, LANE), bf16)
        ls = [jnp.dot(p, ones, preferred_element_type=f32)[:, :HEAD_DIM] for p in ps]
        os_ = [jnp.dot(p, v, preferred_element_type=f32) / l for p, v, l in zip(ps, vv, ls)]
        lses = [m + jnp.log(l) for m, l in zip(ms, ls)]
        for c, o, lse in zip(qs, os_, lses):
            if with_sink:
                o = o * jax.nn.sigmoid(lse - sink_ref[:, c])
            else:
                lse_ref[0, :, c] = lse
            o_ref[0, :, c] = o.astype(o_ref.dtype)


def _band_attention(view, bias, dil, feat, q_col, k_col, v_col, kv_width, group, sink=None):
    b, m_len, _ = view.shape
    d_q = N_HEADS * HEAD_DIM
    nb = m_len // BLOCK
    qpr, kpr = feat // d_q, feat // kv_width

    def cur(col, per_row):
        return lambda bi, r, n: (bi, n, r * per_row + col)

    def prev(col, per_row):
        return lambda bi, r, n: (bi, jnp.maximum(n - 1, 0), r * per_row + col)

    in_specs = [pl.BlockSpec((1, BLOCK, d_q), cur(q_col, qpr)),
                pl.BlockSpec((1, BLOCK, kv_width), prev(k_col, kpr)),
                pl.BlockSpec((1, BLOCK, kv_width), cur(k_col, kpr)),
                pl.BlockSpec((1, BLOCK, kv_width), prev(v_col, kpr)),
                pl.BlockSpec((1, BLOCK, kv_width), cur(v_col, kpr)),
                pl.BlockSpec((1, N_HEADS, BLOCK, 2 * BLOCK),
                             lambda bi, r, n: (jnp.minimum(n, 1), 0, 0, 0))]
    args = [view, view, view, view, view, bias]
    o_spec = pl.BlockSpec((1, BLOCK, d_q), lambda bi, r, n: (bi, n, r))
    o_shape = jax.ShapeDtypeStruct((b, m_len, dil * d_q), bf16)
    if sink is not None:
        in_specs.append(pl.BlockSpec((1, d_q), lambda bi, r, n: (0, 0)))
        args.append(sink)
        out_specs, out_shape = o_spec, o_shape
    else:
        out_specs = [o_spec, o_spec]
        out_shape = [o_shape, jax.ShapeDtypeStruct((b, m_len, dil * d_q), f32)]
    out = pl.pallas_call(
        functools.partial(_band_kernel, group=group, with_sink=sink is not None),
        grid=(b, dil, nb), in_specs=in_specs, out_specs=out_specs, out_shape=out_shape,
        compiler_params=_params("parallel", "parallel", "arbitrary"))(*args)
    if sink is not None:
        return out.reshape(b * m_len, dil * d_q)
    return out[0].reshape(b * m_len, dil * d_q), out[1].reshape(b * m_len, dil * d_q)


def _layer_norm(z, g, b):
    mu = z.mean(-1, keepdims=True)
    zc = z - mu
    var = jnp.square(zc).mean(-1, keepdims=True)
    return zc * lax.rsqrt(var + LN_EPS) * g + b


def _proj_ln_kernel(*refs, dils, alpha):
    n_pat = len(dils)
    o_refs = refs[:n_pat]
    lse_refs = refs[n_pat:2 * n_pat] if n_pat > 1 else ()
    n_in = len(o_refs) + len(lse_refs)
    w_ref, x_ref, g_ref, b_ref, out_ref = refs[n_in:n_in + 5]
    scratch = refs[n_in + 5:]
    if n_pat == 1 and len(o_refs[0].shape) == 3:
        groups, _, width = o_refs[0].shape
        y = sum(jnp.dot(o_refs[0][gi], w_ref[gi * width:(gi + 1) * width, :], preferred_element_type=f32)
                for gi in range(groups))
        out_ref[...] = _layer_norm(alpha * x_ref[...] + y, g_ref[...], b_ref[...])
        return
    if n_pat == 1:
        o = o_refs[0][...]
    else:
        kw = w_ref.shape[0]
        mix_ref = scratch[-1]

        def natural(ref, dil, scr, c):
            if dil == 1:
                return ref[:, c * LANE:(c + 1) * LANE].astype(f32)
            rows = ref.shape[0]
            for r in range(dil):
                scr[c, pl.ds(r, rows, stride=dil), :] = (
                    ref[:, r * kw + c * LANE:r * kw + (c + 1) * LANE].astype(f32))
            return scr[c]

        for c in range(kw // LANE):
            lses = [natural(r, dl, scratch[2 * i], c) for i, (r, dl) in enumerate(zip(lse_refs, dils))]
            mx = functools.reduce(jnp.maximum, lses)
            es = [jnp.exp(l - mx) for l in lses]
            num = sum(e * natural(r, dl, scratch[2 * i + 1], c)
                      for i, (e, r, dl) in enumerate(zip(es, o_refs, dils)))
            mix_ref[:, c * LANE:(c + 1) * LANE] = (num / sum(es)).astype(bf16)
        o = mix_ref[...]
    y = jnp.dot(o, w_ref[...], preferred_element_type=f32)
    out_ref[...] = _layer_norm(alpha * x_ref[...] + y, g_ref[...], b_ref[...])


def _proj_ln(os_, lses, dils, w_out, x, g, b, alpha, tm=512):
    t, d = x.shape
    tm = min(tm, t)
    kw = w_out.shape[0]
    row = lambda width: pl.BlockSpec((tm, width), lambda i: (i, 0))
    dilated = lambda dl: pl.BlockSpec((tm // dl, dl * kw), lambda i: (i, 0))
    fixed = lambda shape: pl.BlockSpec(shape, lambda i: (0, 0))
    if os_[0].ndim == 3:
        o_specs = [pl.BlockSpec((os_[0].shape[0], tm, os_[0].shape[2]), lambda i: (0, i, 0))]
    else:
        o_specs = [dilated(dl) for dl in dils]
    in_specs = (o_specs + [dilated(dl) for dl in dils[:len(lses)]]
                + [fixed((kw, d)), row(d), fixed((1, d)), fixed((1, d))])
    scratch = [pltpu.VMEM((kw // LANE, tm, LANE), f32)] * (2 * len(lses))
    if lses:
        scratch.append(pltpu.VMEM((tm, kw), bf16))
    return pl.pallas_call(
        functools.partial(_proj_ln_kernel, dils=tuple(dils), alpha=alpha),
        grid=(t // tm,), in_specs=in_specs, out_specs=row(d),
        out_shape=jax.ShapeDtypeStruct((t, d), f32), scratch_shapes=scratch,
        compiler_params=_params("parallel"))(*os_, *lses, w_out, x, g.reshape(1, d), b.reshape(1, d))


def _mm_dil_kernel(x_ref, w_ref, o_ref, res_ref, *, dil):
    n = w_ref.shape[1]
    res = jnp.dot(x_ref[...].astype(bf16), w_ref[...], preferred_element_type=f32)
    if dil == 1:
        o_ref[...] = res.astype(o_ref.dtype)
        return
    rows = o_ref.shape[0]
    for c in range(n // LANE):
        res_ref[c] = res[:, c * LANE:(c + 1) * LANE]
        for r in range(dil):
            o_ref[:, r * n + c * LANE:r * n + (c + 1) * LANE] = (
                res_ref[c, pl.ds(r, rows, stride=dil), :].astype(o_ref.dtype))


def _mm_dil(x, w, dil, tm=512):
    t, k = x.shape
    n = w.shape[1]
    tm = min(tm, t)
    return pl.pallas_call(
        functools.partial(_mm_dil_kernel, dil=dil), grid=(t // tm,),
        in_specs=[pl.BlockSpec((tm, k), lambda i: (i, 0)), pl.BlockSpec((k, n), lambda i: (0, 0))],
        out_specs=pl.BlockSpec((tm // dil, dil * n), lambda i: (i, 0)),
        out_shape=jax.ShapeDtypeStruct((t // dil, dil * n), bf16),
        scratch_shapes=[pltpu.VMEM((n // LANE, tm, LANE), f32)],
        compiler_params=_params("parallel"))(x, w)


def _rms(c, g):
    return c * lax.rsqrt(jnp.square(c).mean(-1, keepdims=True) + RMS_EPS) * g


def _mla_prep_kernel(x_ref, win_ref, qn_ref, wq_ref, kvn_ref, wkv_ref, cos_ref, sin_ref,
                     q_ref, k_ref, v_ref):
    hw = N_HEADS * LANE
    scale = (MLA_NOPE + MLA_ROPE) ** -0.5
    cos, sin = cos_ref[...], sin_ref[...]
    xw = jnp.dot(x_ref[...].astype(bf16), win_ref[...], preferred_element_type=f32)
    cq = _rms(xw[:, :MLA_Q_RANK], qn_ref[...]).astype(bf16)
    ckv = _rms(xw[:, MLA_Q_RANK:MLA_Q_RANK + MLA_KV_RANK], kvn_ref[...]).astype(bf16)
    off = MLA_Q_RANK + MLA_KV_RANK
    kr = xw[:, off:off + LANE] * cos + xw[:, off + LANE:off + 2 * LANE] * sin
    qq = jnp.dot(cq, wq_ref[...], preferred_element_type=f32)
    kv = jnp.dot(ckv, wkv_ref[...], preferred_element_type=f32)
    ones_lane = lax.broadcasted_iota(jnp.int32, (1, LANE), 1) == MLA_V
    for h in range(N_HEADS):
        blk = slice(h * LANE, (h + 1) * LANE)
        swp = slice(hw + h * LANE, hw + (h + 1) * LANE)
        grp, sub = divmod(h, MLA_GROUP)
        dst = slice(sub * LANE, (sub + 1) * LANE)
        q_ref[grp, :, dst] = ((qq[:, blk] * cos + qq[:, swp] * sin) * scale).astype(bf16)
        k_ref[grp, :, dst] = (kv[:, blk] + kr).astype(bf16)
        v_ref[grp, :, dst] = jnp.where(ones_lane, 1.0, kv[:, swp]).astype(bf16)


def _mla_prep(x, w_in, q_norm, w_uq, kv_norm, w_ukv, seq, tm=512):
    t, d = x.shape
    tm = min(tm, seq)
    hw = N_HEADS * LANE
    half = MLA_ROPE // 2
    dq = MLA_NOPE + MLA_ROPE
    kr_w = w_in[:, MLA_Q_RANK + MLA_KV_RANK:]
    zeros = lambda n: jnp.zeros((d, n), f32)
    kr_a = jnp.concatenate([zeros(MLA_NOPE), kr_w, zeros(LANE - dq)], 1)
    kr_b = jnp.concatenate([zeros(MLA_NOPE), kr_w[:, half:], kr_w[:, :half], zeros(LANE - dq)], 1)
    win = jnp.concatenate([w_in[:, :MLA_Q_RANK + MLA_KV_RANK], kr_a, kr_b], 1).astype(bf16)
    wq3 = w_uq.reshape(MLA_Q_RANK, N_HEADS, dq)
    zq = lambda n: jnp.zeros((MLA_Q_RANK, N_HEADS, n), f32)
    wq_a = jnp.concatenate([wq3, zq(LANE - dq)], 2)
    wq_b = jnp.concatenate([zq(MLA_NOPE), wq3[:, :, MLA_NOPE + half:], wq3[:, :, MLA_NOPE:MLA_NOPE + half],
                            zq(LANE - dq)], 2)
    wq = jnp.concatenate([wq_a.reshape(MLA_Q_RANK, hw), wq_b.reshape(MLA_Q_RANK, hw)], 1).astype(bf16)
    wkv3 = w_ukv.reshape(MLA_KV_RANK, N_HEADS, MLA_NOPE + MLA_V)
    zk = lambda n: jnp.zeros((MLA_KV_RANK, N_HEADS, n), f32)
    wk = jnp.concatenate([wkv3[:, :, :MLA_NOPE], zk(LANE - MLA_NOPE)], 2)
    wv = jnp.concatenate([wkv3[:, :, MLA_NOPE:], zk(LANE - MLA_V)], 2)
    wkv = jnp.concatenate([wk.reshape(MLA_KV_RANK, hw), wv.reshape(MLA_KV_RANK, hw)], 1).astype(bf16)
    freq = ROPE_THETA ** (-jnp.arange(half, dtype=f32) / half)
    ang = jnp.arange(seq)[:, None].astype(f32) * freq[None, :]
    c, s_ = jnp.cos(ang), jnp.sin(ang)
    cos_t = jnp.concatenate([jnp.ones((seq, MLA_NOPE), f32), c, c, jnp.ones((seq, LANE - dq), f32)], 1)
    sin_t = jnp.concatenate([jnp.zeros((seq, MLA_NOPE), f32), -s_, s_, jnp.zeros((seq, LANE - dq), f32)], 1)

    n_pos = seq // tm
    row = lambda width: pl.BlockSpec((tm, width), lambda i: (i, 0))
    fixed = lambda a: pl.BlockSpec(a.shape, lambda i: (0, 0))
    pos = pl.BlockSpec((tm, LANE), lambda i: (i % n_pos, 0))
    qn, kvn = q_norm.reshape(1, -1), kv_norm.reshape(1, -1)
    groups, gw = N_HEADS // MLA_GROUP, MLA_GROUP * LANE
    out_sd = jax.ShapeDtypeStruct((groups, t, gw), bf16)
    out_spec = pl.BlockSpec((groups, tm, gw), lambda i: (0, i, 0))
    return pl.pallas_call(
        _mla_prep_kernel, grid=(t // tm,),
        in_specs=[row(d), fixed(win), fixed(qn), fixed(wq), fixed(kvn), fixed(wkv), pos, pos],
        out_specs=[out_spec] * 3, out_shape=[out_sd] * 3,
        compiler_params=_params("parallel"))(x, win, qn, wq, kvn, wkv, cos_t, sin_t)


def _mla_attn_kernel(q_ref, k_ref, v_ref, o_ref, *, tq, tk, heads):
    qi = pl.program_id(2)
    lanes = [slice(j * LANE, (j + 1) * LANE) for j in range(heads)]
    qs = [q_ref[0, :, ln] for ln in lanes]

    def update(carry, start, mask):
        ss = [lax.dot_general(q, k_ref[0, pl.ds(start, tk), ln], _NT, preferred_element_type=f32)
              for q, ln in zip(qs, lanes)]
        if mask is not None:
            ss = [jnp.where(mask, s, NEG) for s in ss]
        ms = [jnp.maximum(m, s.max(-1, keepdims=True)) for (m, _), s in zip(carry, ss)]
        ps = [jnp.exp(s - m).astype(bf16) for s, m in zip(ss, ms)]
        pvs = [jnp.dot(p, v_ref[0, pl.ds(start, tk), ln], preferred_element_type=f32)
               for p, ln in zip(ps, lanes)]
        return tuple((m_new, jnp.exp(m - m_new) * acc + pv)
                     for (m, acc), m_new, pv in zip(carry, ms, pvs))

    init = tuple((jnp.full((tq, 1), NEG, f32), jnp.zeros((tq, LANE), f32)) for _ in range(heads))
    sub = tq // tk
    carry = lax.fori_loop(0, qi * sub, lambda ki, c: update(c, pl.multiple_of(ki * tk, tk), None), init)
    row = lax.broadcasted_iota(jnp.int32, (tq, tk), 0)
    col = lax.broadcasted_iota(jnp.int32, (tq, tk), 1)
    for j in range(sub):
        carry = update(carry, pl.multiple_of(qi * tq + j * tk, tk), col + j * tk <= row)
    for j in range(heads):
        acc = carry[j][1]
        o_ref[0, :, lanes[j]] = (acc / acc[:, MLA_V:MLA_V + 1]).astype(o_ref.dtype)


def _mla_attention(q, k, v, batch, seq, tq=1024, tk=512):
    groups, t, gw = q.shape
    tq = min(tq, seq)
    nq = seq // tq
    qspec = pl.BlockSpec((1, tq, gw), lambda b, h, i: (h, b * nq + i, 0))
    kspec = pl.BlockSpec((1, seq, gw), lambda b, h, i: (h, b, 0))
    return pl.pallas_call(
        functools.partial(_mla_attn_kernel, tq=tq, tk=min(tk, tq), heads=MLA_GROUP),
        grid=(batch, groups, nq),
        in_specs=[qspec, kspec, kspec], out_specs=qspec,
        out_shape=jax.ShapeDtypeStruct((groups, t, gw), bf16),
        compiler_params=_params("parallel", "parallel", "arbitrary"))(q, k, v)


def _batcher_network(n):
    def merge(lo, hi, r):
        step = r * 2
        if step < hi - lo:
            yield from merge(lo, hi, step)
            yield from merge(lo + r, hi, step)
            yield from [(i, i + r) for i in range(lo + r, hi - r, step)]
        else:
            yield (lo, lo + r)

    def sort(lo, hi):
        if hi - lo >= 1:
            mid = lo + (hi - lo) // 2
            yield from sort(lo, mid)
            yield from sort(mid + 1, hi)
            yield from merge(lo, hi, 1)

    return tuple(sort(0, n - 1))


def _top16_values(s):
    n = s.shape[0] // F32_ROWS
    tiles = [s[j * F32_ROWS:(j + 1) * F32_ROWS] for j in range(n)]
    for i, j in _batcher_network(PEER_TOPK):
        if j < n:
            tiles[i], tiles[j] = jnp.maximum(tiles[i], tiles[j]), jnp.minimum(tiles[i], tiles[j])
    vals = []
    for k in range(PEER_TOPK):
        mx = tiles[0].max(axis=0, keepdims=True)
        vals.append(mx)
        hit = tiles[0] == mx
        for j in range(min(n, PEER_TOPK - 1 - k)):
            below = tiles[j + 1] if j + 1 < n else -jnp.inf
            tiles[j] = jnp.where(hit, below, tiles[j])
    return vals


def _top16_pair(s1, s2):
    v1, v2 = _top16_values(s1), _top16_values(s2)
    rank2 = jnp.zeros(s2.shape, f32)
    for k in range(PEER_TOPK):
        rank2 = jnp.where(s2 < v2[k], float(k + 1), rank2)
    return v1, v2, rank2


def _stack_rows(rows):
    n = len(rows)
    rid = lax.broadcasted_iota(jnp.int32, (n, rows[0].shape[1]), 0)
    out = jnp.broadcast_to(rows[0], (n, rows[0].shape[1]))
    for i in range(1, n):
        out = jnp.where(rid == i, rows[i], out)
    return out


def _route_kernel(x_ref, wq_ref, keys_ref, rank2_ref, crow_ref, e1z_ref, e2_ref, *, tm):
    q = jnp.dot(x_ref[...].astype(bf16), wq_ref[...], preferred_element_type=f32).astype(bf16)
    half = PEER_DKEY // 2
    scores = []
    for hh in range(ROUTE_HEADS):
        qh = q[:, hh * PEER_DKEY:(hh + 1) * PEER_DKEY]
        scores.append((lax.dot_general(keys_ref[hh, 0], qh[:, :half], _NT, preferred_element_type=f32),
                       lax.dot_general(keys_ref[hh, 1], qh[:, half:], _NT, preferred_element_type=f32)))
    rid8 = lax.broadcasted_iota(jnp.int32, (8, LANE), 0)
    for hh, c in [(hh, c) for hh in range(ROUTE_HEADS) for c in range(tm // LANE)]:
        lanes = slice(c * LANE, (c + 1) * LANE)
        s1, s2 = scores[hh][0][:, lanes], scores[hh][1][:, lanes]
        v1, v2, rank2 = _top16_pair(s1, s2)
        v2_all = _stack_rows(v2)
        cands = [v1[0] + v2_all]
        for a in range(1, 8):
            cands.append(jnp.where(rid8 < PEER_TOPK // (a + 1), v1[a] + v2_all[:8], -jnp.inf))
        cands.append(_stack_rows(v1[8:]) + v2[0])
        cand = jnp.concatenate(cands, axis=0)
        tau = _top16_values(cand)[-1]
        cmax = v1[0] + v2[0]
        z = jnp.where(cand >= tau, jnp.exp(cand - cmax), 0.0).sum(axis=0, keepdims=True)
        crow = jnp.zeros(s1.shape, f32)
        for b in range(PEER_TOPK):
            crow = crow + jnp.where(s1 + v2[b] >= tau, 1.0, 0.0)
        rank2_ref[0, hh, :, lanes] = rank2.astype(bf16)
        crow_ref[0, hh, :, lanes] = crow
        e1z_ref[0, hh, :, lanes] = jnp.exp(s1 - v1[0]) * (GELU_FOLD / z)
        e2_ref[0, hh, :, lanes] = jnp.exp(s2 - v2[0]).astype(bf16)


def _peer_route(x, w_q, keys, tm=ROUTE_TOKENS):
    t, d = x.shape
    tm = min(tm, t)
    hb = ROUTE_HEADS
    spec = pl.BlockSpec((1, hb, PEER_KEYS, tm), lambda i, h: (i, h, 0, 0))
    sd = lambda dt: jax.ShapeDtypeStruct((t // tm, PEER_HEADS, PEER_KEYS, tm), dt)
    return pl.pallas_call(
        functools.partial(_route_kernel, tm=tm), grid=(t // tm, PEER_HEADS // hb),
        in_specs=[pl.BlockSpec((tm, d), lambda i, h: (i, 0)),
                  pl.BlockSpec((d, hb * PEER_DKEY), lambda i, h: (0, h)),
                  pl.BlockSpec((hb, 2, PEER_KEYS, PEER_DKEY // 2), lambda i, h: (h, 0, 0, 0))],
        out_specs=[spec] * 4, out_shape=[sd(bf16), sd(f32), sd(f32), sd(bf16)],
        compiler_params=_params("parallel", "arbitrary"))(x, w_q, keys)


def _peer_dense_kernel(x_ref, u0_ref, u1_ref, vt0_ref, vt1_ref, rank2_in, crow_ref, e1z_ref, e2_in,
                       g_ref, b_ref, o_ref, acc_ref, xt_ref, ht0, ht1, gt0, gt1, rank2_ref, e2_ref,
                       *, rows, tb, lc, mw, alpha):
    k = pl.program_id(1)
    ec = rows * PEER_KEYS

    @pl.when(k == 0)
    def _():
        acc_ref[...] = jnp.zeros_like(acc_ref)
        ht1[...] = jnp.zeros_like(ht1)
        gt0[...] = jnp.zeros_like(gt0)
        xt_ref[...] = x_ref[...].T.astype(bf16)
        rank2_ref[...] = rank2_in[0]
        e2_ref[...] = e2_in[0]

    zero = jnp.zeros((), bf16)
    kt = PEER_KEYS // BF16_ROWS

    def up(half, ht, cw):
        lanes = slice(cw * mw, (cw + 1) * mw)
        u_blk = pltpu.bitcast((u0_ref, u1_ref)[half][...], bf16)
        ht[:, lanes] = jnp.dot(u_blk, xt_ref[:, lanes], preferred_element_type=f32)

    n_tiles = pl.num_programs(1) - 1

    def gate(ht, tile, half, gt, cw):
        base = pl.multiple_of(jnp.clip(tile, 0, n_tiles - 1) * F32_ROWS, F32_ROWS)
        for r in range(rows):
            keys = slice(r * PEER_KEYS, (r + 1) * PEER_KEYS)
            row = slice(half * rows + r, half * rows + r + 1)
            for c in range(cw * mw // lc, (cw + 1) * mw // lc):
                lanes = slice(c * lc, (c + 1) * lc)
                w = None
                for h in range(PEER_HEADS):
                    cr = crow_ref[0, h, pl.ds(base, F32_ROWS), lanes][row]
                    ez = e1z_ref[0, h, pl.ds(base, F32_ROWS), lanes][row]
                    cr = jnp.broadcast_to(cr, (BF16_ROWS, lc)).astype(bf16)[None]
                    ez = jnp.broadcast_to(ez, (BF16_ROWS, lc)).astype(bf16)[None]
                    term = jnp.where(rank2_ref[h, :, :, lanes] < cr, e2_ref[h, :, :, lanes] * ez, zero)
                    w = term if w is None else w + term
                hv = ht[keys, lanes]
                g = hv * (1.0 + lax.erf(hv))
                gt[r * kt:(r + 1) * kt, :, lanes] = g.astype(bf16).reshape(kt, BF16_ROWS, lc) * w

    def down(half, gt, cw):
        lanes = slice(cw * mw, (cw + 1) * mw)
        vt_blk = pltpu.bitcast((vt0_ref, vt1_ref)[half][0], bf16)
        acc_ref[:, lanes] += jnp.dot(vt_blk, gt[:, :, lanes].reshape(ec, mw), preferred_element_type=f32)

    for cw in range(tb // mw):
        up(0, ht0, cw)
        gate(ht1, k - 1, 1, gt1, cw)
        down(0, gt0, cw)
    for cw in range(tb // mw):
        up(1, ht1, cw)
        gate(ht0, k, 0, gt0, cw)
        down(1, gt1, cw)

    @pl.when(k == pl.num_programs(1) - 1)
    def _():
        z = alpha * x_ref[...] + acc_ref[...].T
        o_ref[...] = _layer_norm(z, g_ref[...], b_ref[...])


def _peer_dense(x, u, vt, routing, g, b, alpha, lc=128):
    t, d = x.shape
    e = 2 * u.shape[0]
    ec = DENSE_EC
    rows = ec // PEER_KEYS
    assert 2 * rows == F32_ROWS, "a pair of expert blocks must span one 8-row f32 tile of sub-key-1 rows"
    nk = e // (2 * ec)
    kt = PEER_KEYS // BF16_ROWS
    rank2, crow, e1z, e2 = routing
    nt, _, _, tb = crow.shape
    rank2, e2 = (a.reshape(nt, PEER_HEADS, kt, BF16_ROWS, tb) for a in (rank2, e2))
    full = pl.BlockSpec((1, PEER_HEADS, kt, BF16_ROWS, tb), lambda i, k: (i, 0, 0, 0, 0))
    rowtab = pl.BlockSpec((1, PEER_HEADS, PEER_KEYS, tb), lambda i, k: (i, 0, 0, 0))
    fixed = pl.BlockSpec((1, d), lambda i, k: (0, 0))
    xspec = pl.BlockSpec((tb, d), lambda i, k: (i, 0))
    xin = xspec
    gt_scr = pltpu.VMEM((ec // BF16_ROWS, BF16_ROWS, tb), bf16)
    u_blk = lambda half: pl.BlockSpec((ec // 2, d), lambda i, k: (2 * jnp.minimum(k, nk - 1) + half, 0))
    vt_blk = lambda half: pl.BlockSpec((1, d // 2, ec), lambda i, k: (2 * jnp.maximum(k - 1, 0) + half, 0, 0))
    return pl.pallas_call(
        functools.partial(_peer_dense_kernel, rows=rows, tb=tb, lc=lc, mw=MXU_WIDTH, alpha=alpha),
        grid=(t // tb, nk + 1),
        in_specs=[xin, u_blk(0), u_blk(1), vt_blk(0), vt_blk(1),
                  full, rowtab, rowtab, full, fixed, fixed],
        out_specs=xspec, out_shape=jax.ShapeDtypeStruct((t, d), f32),
        scratch_shapes=[pltpu.VMEM((d, tb), f32), pltpu.VMEM((d, tb), bf16),
                        pltpu.VMEM((ec, tb), f32), pltpu.VMEM((ec, tb), f32), gt_scr, gt_scr,
                        pltpu.VMEM((PEER_HEADS, kt, BF16_ROWS, tb), bf16),
                        pltpu.VMEM((PEER_HEADS, kt, BF16_ROWS, tb), bf16)],
        compiler_params=_params("parallel", "arbitrary"))(
            x, u, u, vt, vt, rank2, crow, e1z, e2, g.reshape(1, d), b.reshape(1, d))


def _swa_layer(x, batch, seq, w_in, sinks, w_out, bias, g, b, alpha):
    feat = (N_HEADS + 2 * SWA_KV_HEADS) * HEAD_DIM
    d_q = N_HEADS * HEAD_DIM
    kvw = SWA_KV_HEADS * HEAD_DIM
    proj = _mm(x, w_in.astype(bf16), bf16, 512, feat).reshape(batch, seq, feat)
    sink = jnp.repeat(sinks.astype(f32), HEAD_DIM).reshape(1, d_q)
    o = _band_attention(proj, bias, 1, feat, 0, d_q // kvw, d_q // kvw + 1, kvw,
                        N_HEADS // SWA_KV_HEADS, sink)
    return _proj_ln([o], [], [1], w_out.astype(bf16), x, g, b, alpha)


def _dil_layer(x, batch, seq, w_in, w_out, biases, g, b, alpha):
    d_q = N_HEADS * HEAD_DIM
    feat = 3 * d_q
    w_in = w_in.astype(bf16)
    outs, lses, dils = [], [], []
    for gi, (window, dil) in enumerate(DIL_PATTERNS):
        proj = _mm_dil(x, w_in[:, gi * feat:(gi + 1) * feat], dil).reshape(batch, seq // dil, dil * feat)
        o, lse = _band_attention(proj, biases[gi], dil, feat, 0, 1, 2, d_q, 1)
        outs.append(o)
        lses.append(lse)
        dils.append(dil)
    return _proj_ln(outs, lses, dils, w_out.astype(bf16), x, g, b, alpha)


def _mla_layer(x, batch, seq, w_in, q_norm, w_uq, kv_norm, w_ukv, w_out, g, b, alpha):
    q, k, v = _mla_prep(x, w_in, q_norm, w_uq, kv_norm, w_ukv, seq)
    o = _mla_attention(q, k, v, batch, seq)
    d = w_out.shape[1]
    w3 = w_out.reshape(N_HEADS, MLA_V, d)
    w_pad = jnp.concatenate([w3, jnp.zeros((N_HEADS, LANE - MLA_V, d), f32)], 1)
    return _proj_ln([o], [], [1], w_pad.reshape(N_HEADS * LANE, d).astype(bf16), x, g, b, alpha)


def _pack_row_pairs(w):
    bits = lax.bitcast_convert_type(w, jnp.uint16).astype(jnp.uint32)
    return bits[..., 0::2, :] | (bits[..., 1::2, :] << 16)


def _peer_layer(x, w_q, keys, u, v, g, b, alpha):
    routing = _peer_route(x, w_q.astype(bf16), keys.astype(bf16))
    e, d = v.shape
    vt = v.astype(bf16).reshape(e // DENSE_EC, DENSE_EC, d).transpose(0, 2, 1)
    return _peer_dense(x, _pack_row_pairs((u * GELU_FOLD).astype(bf16)), _pack_row_pairs(vt),
                       routing, g, b, alpha)


def kernel(x, rel_bias, ln_g, ln_b, swa_w_in, swa_sinks, swa_w_out, dil_w_in, dil_w_out,
           mla_w_in, mla_q_norm, mla_w_uq, mla_kv_norm, mla_w_ukv, mla_w_out,
           peer_w_q, peer_keys, peer_u, peer_v):
    batch, seq, d = x.shape
    depth = ln_g.shape[0]
    alpha = (2 * depth) ** 0.25
    assert seq % (DIL_PATTERNS[-1][1] * BLOCK) == 0, "sequence must be a whole number of dilation segments"
    swa_bias = _band_bias(rel_bias, SWA_WINDOW - 1, 1)
    dil_bias = [_band_bias(rel_bias, window // dil, dil) for window, dil in DIL_PATTERNS]
    h = x.reshape(batch * seq, d)
    for i in range(depth):
        kind, j = i % 3, i // 3
        if kind == 0:
            h = _swa_layer(h, batch, seq, swa_w_in[j], swa_sinks[j], swa_w_out[j], swa_bias,
                           ln_g[i, 0], ln_b[i, 0], alpha)
        elif kind == 1:
            h = _dil_layer(h, batch, seq, dil_w_in[j], dil_w_out[j], dil_bias,
                           ln_g[i, 0], ln_b[i, 0], alpha)
        else:
            h = _mla_layer(h, batch, seq, mla_w_in[j], mla_q_norm[j], mla_w_uq[j], mla_kv_norm[j],
                           mla_w_ukv[j], mla_w_out[j], ln_g[i, 0], ln_b[i, 0], alpha)
        h = _peer_layer(h, peer_w_q[i], peer_keys[i], peer_u[i], peer_v[i], ln_g[i, 1], ln_b[i, 1], alpha)
    return h.reshape(batch, seq, d)
```

```python
import functools
import math

import jax
import jax.numpy as jnp
from jax import lax
from jax.experimental import pallas as pl
from jax.experimental.pallas import tpu as pltpu

f32 = jnp.float32
bf16 = jnp.bfloat16

N_HEADS = 16
HEAD_DIM = 64
BLOCK = 128
SWA_KV_HEADS = 2
SWA_WINDOW = 128
DIL_PATTERNS = ((128, 1), (512, 4), (2048, 16))
MLA_Q_RANK = 256
MLA_KV_RANK = 128
MLA_NOPE = 64
MLA_ROPE = 32
MLA_V = 64
ROPE_THETA = 10000.0
REL_BUCKETS = 32
REL_MAX_DIST = 2048
PEER_HEADS = 8
PEER_KEYS = 128
PEER_DKEY = 256
PEER_TOPK = 16
LN_EPS = 1e-5
RMS_EPS = 1e-6
NEG = -1e30

LANE = 128
BF16_ROWS = 16
F32_ROWS = 8
MXU_WIDTH = 256
MLA_GROUP = 2
ROUTE_HEADS = 2
GELU_FOLD = 0.5 ** 0.5
ROUTE_TOKENS = 512
HEAD_BATCH = 16
DENSE_EC = 4 * PEER_KEYS
VMEM_LIMIT = 56 * 1024 * 1024

_NT = (((1,), (1,)), ((), ()))


def _params(*sem):
    return pltpu.CompilerParams(dimension_semantics=sem, vmem_limit_bytes=VMEM_LIMIT)


def _mm_kernel(x_ref, w_ref, o_ref):
    o_ref[...] = jnp.dot(x_ref[...].astype(bf16), w_ref[...],
                         preferred_element_type=f32).astype(o_ref.dtype)


def _mm(x, w, out_dtype, tm, tn):
    m, k = x.shape
    n = w.shape[1]
    tm, tn = min(tm, m), min(tn, n)
    return pl.pallas_call(
        _mm_kernel, grid=(m // tm, n // tn),
        in_specs=[pl.BlockSpec((tm, k), lambda i, j: (i, 0)),
                  pl.BlockSpec((k, tn), lambda i, j: (0, j))],
        out_specs=pl.BlockSpec((tm, tn), lambda i, j: (i, j)),
        out_shape=jax.ShapeDtypeStruct((m, n), out_dtype),
        compiler_params=_params("parallel", "parallel"))(x, w)


def _rel_bucket(dist):
    max_exact = REL_BUCKETS // 2
    n = jnp.maximum(dist, 0)
    nf = jnp.maximum(n, 1).astype(f32)
    large = max_exact + (jnp.log(nf / max_exact) / math.log(REL_MAX_DIST / max_exact)
                         * (REL_BUCKETS - max_exact)).astype(jnp.int32)
    large = jnp.minimum(large, REL_BUCKETS - 1)
    return jnp.where(n < max_exact, n, large)


def _bias_kernel(rel_ref, bucket_ref, valid_ref, o_ref):
    bucket = bucket_ref[...]
    for h in range(N_HEADS):
        acc = jnp.zeros(bucket.shape, f32)
        for b in range(REL_BUCKETS):
            acc = jnp.where(bucket == b, rel_ref[b, h], acc)
        for variant in range(2):
            o_ref[variant, h] = jnp.where(valid_ref[variant] != 0, acc, NEG)


def _band_bias(rel_bias, max_dist, dilation):
    qi = jnp.arange(BLOCK)[:, None]
    kj = jnp.arange(2 * BLOCK)[None, :]
    dist = BLOCK + qi - kj
    bucket = _rel_bucket(dist * dilation).astype(jnp.int32)
    valid = (dist >= 0) & (dist <= max_dist)
    valid = jnp.stack([valid & (kj >= BLOCK), valid]).astype(jnp.int32)
    return pl.pallas_call(
        _bias_kernel,
        in_specs=[pl.BlockSpec(memory_space=pltpu.SMEM),
                  pl.BlockSpec(memory_space=pltpu.VMEM),
                  pl.BlockSpec(memory_space=pltpu.VMEM)],
        out_specs=pl.BlockSpec(memory_space=pltpu.VMEM),
        out_shape=jax.ShapeDtypeStruct((2, N_HEADS, BLOCK, 2 * BLOCK), f32))(rel_bias, bucket, valid)


def _band_kernel(q_ref, kp_ref, kc_ref, vp_ref, vc_ref, bias_ref, *rest, group, with_sink):
    if with_sink:
        sink_ref, o_ref = rest
    else:
        o_ref, lse_ref = rest
    scale = HEAD_DIM ** -0.5
    for h0 in range(0, N_HEADS, HEAD_BATCH):
        heads = range(h0, h0 + HEAD_BATCH)
        qs = [slice(h * HEAD_DIM, (h + 1) * HEAD_DIM) for h in heads]
        ks = [slice((h // group) * HEAD_DIM, (h // group + 1) * HEAD_DIM) for h in heads]
        kk = [jnp.concatenate([kp_ref[0, :, c], kc_ref[0, :, c]], axis=0) for c in ks]
        vv = [jnp.concatenate([vp_ref[0, :, c], vc_ref[0, :, c]], axis=0) for c in ks]
        ss = [lax.dot_general(q_ref[0, :, c], k, _NT, preferred_element_type=f32) * scale + bias_ref[0, h]
              for h, c, k in zip(heads, qs, kk)]
        ms = [s.max(-1, keepdims=True) for s in ss]
        ps = [jnp.exp(s - m).astype(bf16) for s, m in zip(ss, ms)]
        ones = jnp.ones((2 * BLOCK, LANE), bf16)
        ls = [jnp.dot(p, ones, preferred_element_type=f32)[:, :HEAD_DIM] for p in ps]
        os_ = [jnp.dot(p, v, preferred_element_type=f32) / l for p, v, l in zip(ps, vv, ls)]
        lses = [m + jnp.log(l) for m, l in zip(ms, ls)]
        for c, o, lse in zip(qs, os_, lses):
            if with_sink:
                o = o * jax.nn.sigmoid(lse - sink_ref[:, c])
            else:
                lse_ref[0, :, c] = lse
            o_ref[0, :, c] = o.astype(o_ref.dtype)


def _band_attention(view, bias, dil, feat, q_col, k_col, v_col, kv_width, group, sink=None):
    b, m_len, _ = view.shape
    d_q = N_HEADS * HEAD_DIM
    nb = m_len // BLOCK
    qpr, kpr = feat // d_q, feat // kv_width

    def cur(col, per_row):
        return lambda bi, r, n: (bi, n, r * per_row + col)

    def prev(col, per_row):
        return lambda bi, r, n: (bi, jnp.maximum(n - 1, 0), r * per_row + col)

    in_specs = [pl.BlockSpec((1, BLOCK, d_q), cur(q_col, qpr)),
                pl.BlockSpec((1, BLOCK, kv_width), prev(k_col, kpr)),
                pl.BlockSpec((1, BLOCK, kv_width), cur(k_col, kpr)),
                pl.BlockSpec((1, BLOCK, kv_width), prev(v_col, kpr)),
                pl.BlockSpec((1, BLOCK, kv_width), cur(v_col, kpr)),
                pl.BlockSpec((1, N_HEADS, BLOCK, 2 * BLOCK),
                             lambda bi, r, n: (jnp.minimum(n, 1), 0, 0, 0))]
    args = [view, view, view, view, view, bias]
    o_spec = pl.BlockSpec((1, BLOCK, d_q), lambda bi, r, n: (bi, n, r))
    o_shape = jax.ShapeDtypeStruct((b, m_len, dil * d_q), bf16)
    if sink is not None:
        in_specs.append(pl.BlockSpec((1, d_q), lambda bi, r, n: (0, 0)))
        args.append(sink)
        out_specs, out_shape = o_spec, o_shape
    else:
        out_specs = [o_spec, o_spec]
        out_shape = [o_shape, jax.ShapeDtypeStruct((b, m_len, dil * d_q), f32)]
    out = pl.pallas_call(
        functools.partial(_band_kernel, group=group, with_sink=sink is not None),
        grid=(b, dil, nb), in_specs=in_specs, out_specs=out_specs, out_shape=out_shape,
        compiler_params=_params("parallel", "parallel", "arbitrary"))(*args)
    if sink is not None:
        return out.reshape(b * m_len, dil * d_q)
    return out[0].reshape(b * m_len, dil * d_q), out[1].reshape(b * m_len, dil * d_q)


def _layer_norm(z, g, b):
    mu = z.mean(-1, keepdims=True)
    zc = z - mu
    var = jnp.square(zc).mean(-1, keepdims=True)
    return zc * lax.rsqrt(var + LN_EPS) * g + b


def _proj_ln_kernel(*refs, dils, alpha):
    n_pat = len(dils)
    o_refs = refs[:n_pat]
    lse_refs = refs[n_pat:2 * n_pat] if n_pat > 1 else ()
    n_in = len(o_refs) + len(lse_refs)
    w_ref, x_ref, g_ref, b_ref, out_ref = refs[n_in:n_in + 5]
    scratch = refs[n_in + 5:]
    if n_pat == 1 and len(o_refs[0].shape) == 3:
        groups, _, width = o_refs[0].shape
        y = sum(jnp.dot(o_refs[0][gi], w_ref[gi * width:(gi + 1) * width, :], preferred_element_type=f32)
                for gi in range(groups))
        out_ref[...] = _layer_norm(alpha * x_ref[...] + y, g_ref[...], b_ref[...])
        return
    if n_pat == 1:
        o = o_refs[0][...]
    else:
        kw = w_ref.shape[0]
        mix_ref = scratch[-1]

        def natural(ref, dil, scr, c):
            if dil == 1:
                return ref[:, c * LANE:(c + 1) * LANE].astype(f32)
            rows = ref.shape[0]
            for r in range(dil):
                scr[c, pl.ds(r, rows, stride=dil), :] = (
                    ref[:, r * kw + c * LANE:r * kw + (c + 1) * LANE].astype(f32))
            return scr[c]

        for c in range(kw // LANE):
            lses = [natural(r, dl, scratch[2 * i], c) for i, (r, dl) in enumerate(zip(lse_refs, dils))]
            mx = functools.reduce(jnp.maximum, lses)
            es = [jnp.exp(l - mx) for l in lses]
            num = sum(e * natural(r, dl, scratch[2 * i + 1], c)
                      for i, (e, r, dl) in enumerate(zip(es, o_refs, dils)))
            mix_ref[:, c * LANE:(c + 1) * LANE] = (num / sum(es)).astype(bf16)
        o = mix_ref[...]
    y = jnp.dot(o, w_ref[...], preferred_element_type=f32)
    out_ref[...] = _layer_norm(alpha * x_ref[...] + y, g_ref[...], b_ref[...])


def _proj_ln(os_, lses, dils, w_out, x, g, b, alpha, tm=512):
    t, d = x.shape
    tm = min(tm, t)
    kw = w_out.shape[0]
    row = lambda width: pl.BlockSpec((tm, width), lambda i: (i, 0))
    dilated = lambda dl: pl.BlockSpec((tm // dl, dl * kw), lambda i: (i, 0))
    fixed = lambda shape: pl.BlockSpec(shape, lambda i: (0, 0))
    if os_[0].ndim == 3:
        o_specs = [pl.BlockSpec((os_[0].shape[0], tm, os_[0].shape[2]), lambda i: (0, i, 0))]
    else:
        o_specs = [dilated(dl) for dl in dils]
    in_specs = (o_specs + [dilated(dl) for dl in dils[:len(lses)]]
                + [fixed((kw, d)), row(d), fixed((1, d)), fixed((1, d))])
    scratch = [pltpu.VMEM((kw // LANE, tm, LANE), f32)] * (2 * len(lses))
    if lses:
        scratch.append(pltpu.VMEM((tm, kw), bf16))
    return pl.pallas_call(
        functools.partial(_proj_ln_kernel, dils=tuple(dils), alpha=alpha),
        grid=(t // tm,), in_specs=in_specs, out_specs=row(d),
        out_shape=jax.ShapeDtypeStruct((t, d), f32), scratch_shapes=scratch,
        compiler_params=_params("parallel"))(*os_, *lses, w_out, x, g.reshape(1, d), b.reshape(1, d))


def _mm_dil_kernel(x_ref, w_ref, o_ref, res_ref, *, dil):
    n = w_ref.shape[1]
    res = jnp.dot(x_ref[...].astype(bf16), w_ref[...], preferred_element_type=f32)
    if dil == 1:
        o_ref[...] = res.astype(o_ref.dtype)
        return
    rows = o_ref.shape[0]
    for c in range(n // LANE):
        res_ref[c] = res[:, c * LANE:(c + 1) * LANE]
        for r in range(dil):
            o_ref[:, r * n + c * LANE:r * n + (c + 1) * LANE] = (
                res_ref[c, pl.ds(r, rows, stride=dil), :].astype(o_ref.dtype))


def _mm_dil(x, w, dil, tm=512):
    t, k = x.shape
    n = w.shape[1]
    tm = min(tm, t)
    return pl.pallas_call(
        functools.partial(_mm_dil_kernel, dil=dil), grid=(t // tm,),
        in_specs=[pl.BlockSpec((tm, k), lambda i: (i, 0)), pl.BlockSpec((k, n), lambda i: (0, 0))],
        out_specs=pl.BlockSpec((tm // dil, dil * n), lambda i: (i, 0)),
        out_shape=jax.ShapeDtypeStruct((t // dil, dil * n), bf16),
        scratch_shapes=[pltpu.VMEM((n // LANE, tm, LANE), f32)],
        compiler_params=_params("parallel"))(x, w)


def _rms(c, g):
    return c * lax.rsqrt(jnp.square(c).mean(-1, keepdims=True) + RMS_EPS) * g


def _mla_prep_kernel(x_ref, win_ref, qn_ref, wq_ref, kvn_ref, wkv_ref, cos_ref, sin_ref,
                     q_ref, k_ref, v_ref):
    hw = N_HEADS * LANE
    scale = (MLA_NOPE + MLA_ROPE) ** -0.5
    cos, sin = cos_ref[...], sin_ref[...]
    xw = jnp.dot(x_ref[...].astype(bf16), win_ref[...], preferred_element_type=f32)
    cq = _rms(xw[:, :MLA_Q_RANK], qn_ref[...]).astype(bf16)
    ckv = _rms(xw[:, MLA_Q_RANK:MLA_Q_RANK + MLA_KV_RANK], kvn_ref[...]).astype(bf16)
    off = MLA_Q_RANK + MLA_KV_RANK
    kr = xw[:, off:off + LANE] * cos + xw[:, off + LANE:off + 2 * LANE] * sin
    qq = jnp.dot(cq, wq_ref[...], preferred_element_type=f32)
    kv = jnp.dot(ckv, wkv_ref[...], preferred_element_type=f32)
    ones_lane = lax.broadcasted_iota(jnp.int32, (1, LANE), 1) == MLA_V
    for h in range(N_HEADS):
        blk = slice(h * LANE, (h + 1) * LANE)
        swp = slice(hw + h * LANE, hw + (h + 1) * LANE)
        grp, sub = divmod(h, MLA_GROUP)
        dst = slice(sub * LANE, (sub + 1) * LANE)
        q_ref[grp, :, dst] = ((qq[:, blk] * cos + qq[:, swp] * sin) * scale).astype(bf16)
        k_ref[grp, :, dst] = (kv[:, blk] + kr).astype(bf16)
        v_ref[grp, :, dst] = jnp.where(ones_lane, 1.0, kv[:, swp]).astype(bf16)


def _mla_prep(x, w_in, q_norm, w_uq, kv_norm, w_ukv, seq, tm=512):
    t, d = x.shape
    tm = min(tm, seq)
    hw = N_HEADS * LANE
    half = MLA_ROPE // 2
    dq = MLA_NOPE + MLA_ROPE
    kr_w = w_in[:, MLA_Q_RANK + MLA_KV_RANK:]
    zeros = lambda n: jnp.zeros((d, n), f32)
    kr_a = jnp.concatenate([zeros(MLA_NOPE), kr_w, zeros(LANE - dq)], 1)
    kr_b = jnp.concatenate([zeros(MLA_NOPE), kr_w[:, half:], kr_w[:, :half], zeros(LANE - dq)], 1)
    win = jnp.concatenate([w_in[:, :MLA_Q_RANK + MLA_KV_RANK], kr_a, kr_b], 1).astype(bf16)
    wq3 = w_uq.reshape(MLA_Q_RANK, N_HEADS, dq)
    zq = lambda n: jnp.zeros((MLA_Q_RANK, N_HEADS, n), f32)
    wq_a = jnp.concatenate([wq3, zq(LANE - dq)], 2)
    wq_b = jnp.concatenate([zq(MLA_NOPE), wq3[:, :, MLA_NOPE + half:], wq3[:, :, MLA_NOPE:MLA_NOPE + half],
                            zq(LANE - dq)], 2)
    wq = jnp.concatenate([wq_a.reshape(MLA_Q_RANK, hw), wq_b.reshape(MLA_Q_RANK, hw)], 1).astype(bf16)
    wkv3 = w_ukv.reshape(MLA_KV_RANK, N_HEADS, MLA_NOPE + MLA_V)
    zk = lambda n: jnp.zeros((MLA_KV_RANK, N_HEADS, n), f32)
    wk = jnp.concatenate([wkv3[:, :, :MLA_NOPE], zk(LANE - MLA_NOPE)], 2)
    wv = jnp.concatenate([wkv3[:, :, MLA_NOPE:], zk(LANE - MLA_V)], 2)
    wkv = jnp.concatenate([wk.reshape(MLA_KV_RANK, hw), wv.reshape(MLA_KV_RANK, hw)], 1).astype(bf16)
    freq = ROPE_THETA ** (-jnp.arange(half, dtype=f32) / half)
    ang = jnp.arange(seq)[:, None].astype(f32) * freq[None, :]
    c, s_ = jnp.cos(ang), jnp.sin(ang)
    cos_t = jnp.concatenate([jnp.ones((seq, MLA_NOPE), f32), c, c, jnp.ones((seq, LANE - dq), f32)], 1)
    sin_t = jnp.concatenate([jnp.zeros((seq, MLA_NOPE), f32), -s_, s_, jnp.zeros((seq, LANE - dq), f32)], 1)

    n_pos = seq // tm
    row = lambda width: pl.BlockSpec((tm, width), lambda i: (i, 0))
    fixed = lambda a: pl.BlockSpec(a.shape, lambda i: (0, 0))
    pos = pl.BlockSpec((tm, LANE), lambda i: (i % n_pos, 0))
    qn, kvn = q_norm.reshape(1, -1), kv_norm.reshape(1, -1)
    groups, gw = N_HEADS // MLA_GROUP, MLA_GROUP * LANE
    out_sd = jax.ShapeDtypeStruct((groups, t, gw), bf16)
    out_spec = pl.BlockSpec((groups, tm, gw), lambda i: (0, i, 0))
    return pl.pallas_call(
        _mla_prep_kernel, grid=(t // tm,),
        in_specs=[row(d), fixed(win), fixed(qn), fixed(wq), fixed(kvn), fixed(wkv), pos, pos],
        out_specs=[out_spec] * 3, out_shape=[out_sd] * 3,
        compiler_params=_params("parallel"))(x, win, qn, wq, kvn, wkv, cos_t, sin_t)


def _mla_attn_kernel(q_ref, k_ref, v_ref, o_ref, *, tq, tk, heads):
    qi = pl.program_id(2)
    lanes = [slice(j * LANE, (j + 1) * LANE) for j in range(heads)]
    qs = [q_ref[0, :, ln] for ln in lanes]

    def update(carry, start, mask):
        ss = [lax.dot_general(q, k_ref[0, pl.ds(start, tk), ln], _NT, preferred_element_type=f32)
              for q, ln in zip(qs, lanes)]
        if mask is not None:
            ss = [jnp.where(mask, s, NEG) for s in ss]
        ms = [jnp.maximum(m, s.max(-1, keepdims=True)) for (m, _), s in zip(carry, ss)]
        ps = [jnp.exp(s - m).astype(bf16) for s, m in zip(ss, ms)]
        pvs = [jnp.dot(p, v_ref[0, pl.ds(start, tk), ln], preferred_element_type=f32)
               for p, ln in zip(ps, lanes)]
        return tuple((m_new, jnp.exp(m - m_new) * acc + pv)
                     for (m, acc), m_new, pv in zip(carry, ms, pvs))

    init = tuple((jnp.full((tq, 1), NEG, f32), jnp.zeros((tq, LANE), f32)) for _ in range(heads))
    sub = tq // tk
    carry = lax.fori_loop(0, qi * sub, lambda ki, c: update(c, pl.multiple_of(ki * tk, tk), None), init)
    row = lax.broadcasted_iota(jnp.int32, (tq, tk), 0)
    col = lax.broadcasted_iota(jnp.int32, (tq, tk), 1)
    for j in range(sub):
        carry = update(carry, pl.multiple_of(qi * tq + j * tk, tk), col + j * tk <= row)
    for j in range(heads):
        acc = carry[j][1]
        o_ref[0, :, lanes[j]] = (acc / acc[:, MLA_V:MLA_V + 1]).astype(o_ref.dtype)


def _mla_attention(q, k, v, batch, seq, tq=1024, tk=512):
    groups, t, gw = q.shape
    tq = min(tq, seq)
    nq = seq // tq
    qspec = pl.BlockSpec((1, tq, gw), lambda b, h, i: (h, b * nq + i, 0))
    kspec = pl.BlockSpec((1, seq, gw), lambda b, h, i: (h, b, 0))
    return pl.pallas_call(
        functools.partial(_mla_attn_kernel, tq=tq, tk=min(tk, tq), heads=MLA_GROUP),
        grid=(batch, groups, nq),
        in_specs=[qspec, kspec, kspec], out_specs=qspec,
        out_shape=jax.ShapeDtypeStruct((groups, t, gw), bf16),
        compiler_params=_params("parallel", "parallel", "arbitrary"))(q, k, v)


def _batcher_network(n):
    def merge(lo, hi, r):
        step = r * 2
        if step < hi - lo:
            yield from merge(lo, hi, step)
            yield from merge(lo + r, hi, step)
            yield from [(i, i + r) for i in range(lo + r, hi - r, step)]
        else:
            yield (lo, lo + r)

    def sort(lo, hi):
        if hi - lo >= 1:
            mid = lo + (hi - lo) // 2
            yield from sort(lo, mid)
            yield from sort(mid + 1, hi)
            yield from merge(lo, hi, 1)

    return tuple(sort(0, n - 1))


def _top16_values(s):
    n = s.shape[0] // F32_ROWS
    tiles = [s[j * F32_ROWS:(j + 1) * F32_ROWS] for j in range(n)]
    for i, j in _batcher_network(PEER_TOPK):
        if j < n:
            tiles[i], tiles[j] = jnp.maximum(tiles[i], tiles[j]), jnp.minimum(tiles[i], tiles[j])
    vals = []
    for k in range(PEER_TOPK):
        mx = tiles[0].max(axis=0, keepdims=True)
        vals.append(mx)
        hit = tiles[0] == mx
        for j in range(min(n, PEER_TOPK - 1 - k)):
            below = tiles[j + 1] if j + 1 < n else -jnp.inf
            tiles[j] = jnp.where(hit, below, tiles[j])
    return vals


def _top16_pair(s1, s2):
    v1, v2 = _top16_values(s1), _top16_values(s2)
    rank2 = jnp.zeros(s2.shape, f32)
    for k in range(PEER_TOPK):
        rank2 = jnp.where(s2 < v2[k], float(k + 1), rank2)
    return v1, v2, rank2


def _stack_rows(rows):
    n = len(rows)
    rid = lax.broadcasted_iota(jnp.int32, (n, rows[0].shape[1]), 0)
    out = jnp.broadcast_to(rows[0], (n, rows[0].shape[1]))
    for i in range(1, n):
        out = jnp.where(rid == i, rows[i], out)
    return out


def _route_kernel(x_ref, wq_ref, keys_ref, rank2_ref, crow_ref, e1z_ref, e2_ref, *, tm):
    q = jnp.dot(x_ref[...].astype(bf16), wq_ref[...], preferred_element_type=f32).astype(bf16)
    half = PEER_DKEY // 2
    scores = []
    for hh in range(ROUTE_HEADS):
        qh = q[:, hh * PEER_DKEY:(hh + 1) * PEER_DKEY]
        scores.append((lax.dot_general(keys_ref[hh, 0], qh[:, :half], _NT, preferred_element_type=f32),
                       lax.dot_general(keys_ref[hh, 1], qh[:, half:], _NT, preferred_element_type=f32)))
    rid8 = lax.broadcasted_iota(jnp.int32, (8, LANE), 0)
    for hh, c in [(hh, c) for hh in range(ROUTE_HEADS) for c in range(tm // LANE)]:
        lanes = slice(c * LANE, (c + 1) * LANE)
        s1, s2 = scores[hh][0][:, lanes], scores[hh][1][:, lanes]
        v1, v2, rank2 = _top16_pair(s1, s2)
        v2_all = _stack_rows(v2)
        cands = [v1[0] + v2_all]
        for a in range(1, 8):
            cands.append(jnp.where(rid8 < PEER_TOPK // (a + 1), v1[a] + v2_all[:8], -jnp.inf))
        cands.append(_stack_rows(v1[8:]) + v2[0])
        cand = jnp.concatenate(cands, axis=0)
        tau = _top16_values(cand)[-1]
        cmax = v1[0] + v2[0]
        z = jnp.where(cand >= tau, jnp.exp(cand - cmax), 0.0).sum(axis=0, keepdims=True)
        crow = jnp.zeros(s1.shape, f32)
        for b in range(PEER_TOPK):
            crow = crow + jnp.where(s1 + v2[b] >= tau, 1.0, 0.0)
        rank2_ref[0, hh, :, lanes] = rank2.astype(bf16)
        crow_ref[0, hh, :, lanes] = crow
        e1z_ref[0, hh, :, lanes] = jnp.exp(s1 - v1[0]) * (GELU_FOLD / z)
        e2_ref[0, hh, :, lanes] = jnp.exp(s2 - v2[0]).astype(bf16)


def _peer_route(x, w_q, keys, tm=ROUTE_TOKENS):
    t, d = x.shape
    tm = min(tm, t)
    hb = ROUTE_HEADS
    spec = pl.BlockSpec((1, hb, PEER_KEYS, tm), lambda i, h: (i, h, 0, 0))
    sd = lambda dt: jax.ShapeDtypeStruct((t // tm, PEER_HEADS, PEER_KEYS, tm), dt)
    return pl.pallas_call(
        functools.partial(_route_kernel, tm=tm), grid=(t // tm, PEER_HEADS // hb),
        in_specs=[pl.BlockSpec((tm, d), lambda i, h: (i, 0)),
                  pl.BlockSpec((d, hb * PEER_DKEY), lambda i, h: (0, h)),
                  pl.BlockSpec((hb, 2, PEER_KEYS, PEER_DKEY // 2), lambda i, h: (h, 0, 0, 0))],
        out_specs=[spec] * 4, out_shape=[sd(bf16), sd(f32), sd(f32), sd(bf16)],
        compiler_params=_params("parallel", "arbitrary"))(x, w_q, keys)


def _peer_dense_kernel(x_ref, u0_ref, u1_ref, vt0_ref, vt1_ref, rank2_in, crow_ref, e1z_ref, e2_in,
                       g_ref, b_ref, o_ref, acc_ref, xt_ref, ht0, ht1, gt0, gt1, rank2_ref, e2_ref,
                       *, rows, tb, lc, mw, alpha):
    k = pl.program_id(1)
    ec = rows * PEER_KEYS

    @pl.when(k == 0)
    def _():
        acc_ref[...] = jnp.zeros_like(acc_ref)
        ht1[...] = jnp.zeros_like(ht1)
        gt0[...] = jnp.zeros_like(gt0)
        xt_ref[...] = x_ref[...].T.astype(bf16)
        rank2_ref[...] = rank2_in[0]
        e2_ref[...] = e2_in[0]

    zero = jnp.zeros((), bf16)
    kt = PEER_KEYS // BF16_ROWS

    def up(half, ht, cw):
        lanes = slice(cw * mw, (cw + 1) * mw)
        u_blk = pltpu.bitcast((u0_ref, u1_ref)[half][0], bf16)
        ht[:, lanes] = jnp.dot(u_blk, xt_ref[:, lanes], preferred_element_type=f32)

    n_tiles = pl.num_programs(1) - 1

    def gate(ht, tile, half, gt, cw):
        base = pl.multiple_of(jnp.clip(tile, 0, n_tiles - 1) * F32_ROWS, F32_ROWS)
        for r in range(rows):
            keys = slice(r * PEER_KEYS, (r + 1) * PEER_KEYS)
            row = slice(half * rows + r, half * rows + r + 1)
            for c in range(cw * mw // lc, (cw + 1) * mw // lc):
                lanes = slice(c * lc, (c + 1) * lc)
                w = None
                for h in range(PEER_HEADS):
                    cr = crow_ref[0, h, pl.ds(base, F32_ROWS), lanes][row]
                    ez = e1z_ref[0, h, pl.ds(base, F32_ROWS), lanes][row]
                    cr = jnp.broadcast_to(cr, (BF16_ROWS, lc)).astype(bf16)[None]
                    ez = jnp.broadcast_to(ez, (BF16_ROWS, lc)).astype(bf16)[None]
                    term = jnp.where(rank2_ref[h, :, :, lanes] < cr, e2_ref[h, :, :, lanes] * ez, zero)
                    w = term if w is None else w + term
                hv = ht[keys, lanes]
                g = hv * (1.0 + lax.erf(hv))
                gt[r * kt:(r + 1) * kt, :, lanes] = g.astype(bf16).reshape(kt, BF16_ROWS, lc) * w

    def down(half, gt, cw):
        lanes = slice(cw * mw, (cw + 1) * mw)
        vt_blk = pltpu.bitcast((vt0_ref, vt1_ref)[half][0, 0], bf16)
        acc_ref[:, lanes] += jnp.dot(vt_blk, gt[:, :, lanes].reshape(ec, mw), preferred_element_type=f32)

    for cw in range(tb // mw):
        up(0, ht0, cw)
        gate(ht1, k - 1, 1, gt1, cw)
        down(0, gt0, cw)
    for cw in range(tb // mw):
        up(1, ht1, cw)
        gate(ht0, k, 0, gt0, cw)
        down(1, gt1, cw)

    @pl.when(k == pl.num_programs(1) - 1)
    def _():
        z = alpha * x_ref[...] + acc_ref[...].T
        o_ref[...] = _layer_norm(z, g_ref[...], b_ref[...])


def _peer_dense(x, u, vt, layer, routing, g, b, alpha, lc=128):
    t, d = x.shape
    e = 2 * u.shape[1]
    ec = DENSE_EC
    rows = ec // PEER_KEYS
    assert 2 * rows == F32_ROWS, "a pair of expert blocks must span one 8-row f32 tile of sub-key-1 rows"
    nk = e // (2 * ec)
    kt = PEER_KEYS // BF16_ROWS
    rank2, crow, e1z, e2 = routing
    nt, _, _, tb = crow.shape
    rank2, e2 = (a.reshape(nt, PEER_HEADS, kt, BF16_ROWS, tb) for a in (rank2, e2))
    full = pl.BlockSpec((1, PEER_HEADS, kt, BF16_ROWS, tb), lambda i, k: (i, 0, 0, 0, 0))
    rowtab = pl.BlockSpec((1, PEER_HEADS, PEER_KEYS, tb), lambda i, k: (i, 0, 0, 0))
    fixed = pl.BlockSpec((1, d), lambda i, k: (0, 0))
    xspec = pl.BlockSpec((tb, d), lambda i, k: (i, 0))
    xin = xspec
    gt_scr = pltpu.VMEM((ec // BF16_ROWS, BF16_ROWS, tb), bf16)
    u_blk = lambda half: pl.BlockSpec(
        (1, ec // 2, d), lambda i, k: (layer, 2 * jnp.minimum(k, nk - 1) + half, 0))
    vt_blk = lambda half: pl.BlockSpec(
        (1, 1, d // 2, ec), lambda i, k: (layer, 2 * jnp.maximum(k - 1, 0) + half, 0, 0))
    return pl.pallas_call(
        functools.partial(_peer_dense_kernel, rows=rows, tb=tb, lc=lc, mw=MXU_WIDTH, alpha=alpha),
        grid=(t // tb, nk + 1),
        in_specs=[xin, u_blk(0), u_blk(1), vt_blk(0), vt_blk(1),
                  full, rowtab, rowtab, full, fixed, fixed],
        out_specs=xspec, out_shape=jax.ShapeDtypeStruct((t, d), f32),
        scratch_shapes=[pltpu.VMEM((d, tb), f32), pltpu.VMEM((d, tb), bf16),
                        pltpu.VMEM((ec, tb), f32), pltpu.VMEM((ec, tb), f32), gt_scr, gt_scr,
                        pltpu.VMEM((PEER_HEADS, kt, BF16_ROWS, tb), bf16),
                        pltpu.VMEM((PEER_HEADS, kt, BF16_ROWS, tb), bf16)],
        compiler_params=_params("parallel", "arbitrary"))(
            x, u, u, vt, vt, rank2, crow, e1z, e2, g.reshape(1, d), b.reshape(1, d))


def _swa_layer(x, batch, seq, w_in, sinks, w_out, bias, g, b, alpha):
    feat = (N_HEADS + 2 * SWA_KV_HEADS) * HEAD_DIM
    d_q = N_HEADS * HEAD_DIM
    kvw = SWA_KV_HEADS * HEAD_DIM
    proj = _mm(x, w_in.astype(bf16), bf16, 512, feat).reshape(batch, seq, feat)
    sink = jnp.repeat(sinks.astype(f32), HEAD_DIM).reshape(1, d_q)
    o = _band_attention(proj, bias, 1, feat, 0, d_q // kvw, d_q // kvw + 1, kvw,
                        N_HEADS // SWA_KV_HEADS, sink)
    return _proj_ln([o], [], [1], w_out.astype(bf16), x, g, b, alpha)


def _dil_layer(x, batch, seq, w_in, w_out, biases, g, b, alpha):
    d_q = N_HEADS * HEAD_DIM
    feat = 3 * d_q
    w_in = w_in.astype(bf16)
    outs, lses, dils = [], [], []
    for gi, (window, dil) in enumerate(DIL_PATTERNS):
        proj = _mm_dil(x, w_in[:, gi * feat:(gi + 1) * feat], dil).reshape(batch, seq // dil, dil * feat)
        o, lse = _band_attention(proj, biases[gi], dil, feat, 0, 1, 2, d_q, 1)
        outs.append(o)
        lses.append(lse)
        dils.append(dil)
    return _proj_ln(outs, lses, dils, w_out.astype(bf16), x, g, b, alpha)


def _mla_layer(x, batch, seq, w_in, q_norm, w_uq, kv_norm, w_ukv, w_out, g, b, alpha):
    q, k, v = _mla_prep(x, w_in, q_norm, w_uq, kv_norm, w_ukv, seq)
    o = _mla_attention(q, k, v, batch, seq)
    d = w_out.shape[1]
    w3 = w_out.reshape(N_HEADS, MLA_V, d)
    w_pad = jnp.concatenate([w3, jnp.zeros((N_HEADS, LANE - MLA_V, d), f32)], 1)
    return _proj_ln([o], [], [1], w_pad.reshape(N_HEADS * LANE, d).astype(bf16), x, g, b, alpha)


def _pack_u_kernel(u_ref, o_ref):
    o_ref[0] = pltpu.bitcast((u_ref[0] * GELU_FOLD).astype(bf16), jnp.uint32)


def _pack_vt_kernel(v_ref, o_ref):
    o_ref[0, 0] = pltpu.bitcast(v_ref[0].T.astype(bf16), jnp.uint32)


def _pack_expert_weights(u, v):
    depth, e, d = u.shape
    ec = DENSE_EC
    src = pl.BlockSpec((1, ec, d), lambda l, j: (l, j, 0))
    up = pl.pallas_call(
        _pack_u_kernel, grid=(depth, e // ec), in_specs=[src],
        out_specs=pl.BlockSpec((1, ec // 2, d), lambda l, j: (l, j, 0)),
        out_shape=jax.ShapeDtypeStruct((depth, e // 2, d), jnp.uint32),
        compiler_params=_params("parallel", "parallel"))(u)
    vtp = pl.pallas_call(
        _pack_vt_kernel, grid=(depth, e // ec), in_specs=[src],
        out_specs=pl.BlockSpec((1, 1, d // 2, ec), lambda l, j: (l, j, 0, 0)),
        out_shape=jax.ShapeDtypeStruct((depth, e // ec, d // 2, ec), jnp.uint32),
        compiler_params=_params("parallel", "parallel"))(v)
    return up, vtp


def _peer_layer(x, w_q, keys, u_packed, vt_packed, layer, g, b, alpha):
    routing = _peer_route(x, w_q.astype(bf16), keys.astype(bf16))
    return _peer_dense(x, u_packed, vt_packed, layer, routing, g, b, alpha)


def kernel(x, rel_bias, ln_g, ln_b, swa_w_in, swa_sinks, swa_w_out, dil_w_in, dil_w_out,
           mla_w_in, mla_q_norm, mla_w_uq, mla_kv_norm, mla_w_ukv, mla_w_out,
           peer_w_q, peer_keys, peer_u, peer_v):
    batch, seq, d = x.shape
    depth = ln_g.shape[0]
    alpha = (2 * depth) ** 0.25
    assert seq % (DIL_PATTERNS[-1][1] * BLOCK) == 0, "sequence must be a whole number of dilation segments"
    swa_bias = _band_bias(rel_bias, SWA_WINDOW - 1, 1)
    dil_bias = [_band_bias(rel_bias, window // dil, dil) for window, dil in DIL_PATTERNS]
    u_packed, vt_packed = _pack_expert_weights(peer_u, peer_v)
    h = x.reshape(batch * seq, d)
    for i in range(depth):
        kind, j = i % 3, i // 3
        if kind == 0:
            h = _swa_layer(h, batch, seq, swa_w_in[j], swa_sinks[j], swa_w_out[j], swa_bias,
                           ln_g[i, 0], ln_b[i, 0], alpha)
        elif kind == 1:
            h = _dil_layer(h, batch, seq, dil_w_in[j], dil_w_out[j], dil_bias,
                           ln_g[i, 0], ln_b[i, 0], alpha)
        else:
            h = _mla_layer(h, batch, seq, mla_w_in[j], mla_q_norm[j], mla_w_uq[j], mla_kv_norm[j],
                           mla_w_ukv[j], mla_w_out[j], ln_g[i, 0], ln_b[i, 0], alpha)
        h = _peer_layer(h, peer_w_q[i], peer_keys[i], u_packed, vt_packed, i, ln_g[i, 1], ln_b[i, 1], alpha)
    return h.reshape(batch, seq, d)
```

```python
import functools
import math

import jax
import jax.numpy as jnp
from jax import lax
from jax.experimental import pallas as pl
from jax.experimental.pallas import tpu as pltpu

f32 = jnp.float32
bf16 = jnp.bfloat16

N_HEADS = 16
HEAD_DIM = 64
BLOCK = 128
SWA_KV_HEADS = 2
SWA_WINDOW = 128
DIL_PATTERNS = ((128, 1), (512, 4), (2048, 16))
MLA_Q_RANK = 256
MLA_KV_RANK = 128
MLA_NOPE = 64
MLA_ROPE = 32
MLA_V = 64
ROPE_THETA = 10000.0
REL_BUCKETS = 32
REL_MAX_DIST = 2048
PEER_HEADS = 8
PEER_KEYS = 128
PEER_DKEY = 256
PEER_TOPK = 16
LN_EPS = 1e-5
RMS_EPS = 1e-6
NEG = -1e30

LANE = 128
BF16_ROWS = 16
F32_ROWS = 8
MXU_WIDTH = 256
MLA_GROUP = 2
ROUTE_HEADS = 2
GELU_FOLD = 0.5 ** 0.5
ROUTE_TOKENS = 512
DENSE_EC = 4 * PEER_KEYS
VMEM_LIMIT = 56 * 1024 * 1024

_NT = (((1,), (1,)), ((), ()))


def _params(*sem):
    return pltpu.CompilerParams(dimension_semantics=sem, vmem_limit_bytes=VMEM_LIMIT)


def _mm_kernel(x_ref, w_ref, o_ref):
    o_ref[...] = jnp.dot(x_ref[...].astype(bf16), w_ref[...],
                         preferred_element_type=f32).astype(o_ref.dtype)


def _mm(x, w, out_dtype, tm, tn):
    m, k = x.shape
    n = w.shape[1]
    tm, tn = min(tm, m), min(tn, n)
    return pl.pallas_call(
        _mm_kernel, grid=(m // tm, n // tn),
        in_specs=[pl.BlockSpec((tm, k), lambda i, j: (i, 0)),
                  pl.BlockSpec((k, tn), lambda i, j: (0, j))],
        out_specs=pl.BlockSpec((tm, tn), lambda i, j: (i, j)),
        out_shape=jax.ShapeDtypeStruct((m, n), out_dtype),
        compiler_params=_params("parallel", "parallel"))(x, w)


def _rel_bucket(dist):
    max_exact = REL_BUCKETS // 2
    n = jnp.maximum(dist, 0)
    nf = jnp.maximum(n, 1).astype(f32)
    large = max_exact + (jnp.log(nf / max_exact) / math.log(REL_MAX_DIST / max_exact)
                         * (REL_BUCKETS - max_exact)).astype(jnp.int32)
    large = jnp.minimum(large, REL_BUCKETS - 1)
    return jnp.where(n < max_exact, n, large)


def _bias_kernel(rel_ref, bucket_ref, valid_ref, o_ref):
    bucket = bucket_ref[...]
    for h in range(N_HEADS):
        acc = jnp.zeros(bucket.shape, f32)
        for b in range(REL_BUCKETS):
            acc = jnp.where(bucket == b, rel_ref[b, h], acc)
        for variant in range(2):
            o_ref[variant, h] = jnp.where(valid_ref[variant] != 0, acc, NEG)


def _band_bias(rel_bias, max_dist, dilation):
    qi = jnp.arange(BLOCK)[:, None]
    kj = jnp.arange(2 * BLOCK)[None, :]
    dist = BLOCK + qi - kj
    bucket = _rel_bucket(dist * dilation).astype(jnp.int32)
    valid = (dist >= 0) & (dist <= max_dist)
    valid = jnp.stack([valid & (kj >= BLOCK), valid]).astype(jnp.int32)
    return pl.pallas_call(
        _bias_kernel,
        in_specs=[pl.BlockSpec(memory_space=pltpu.SMEM),
                  pl.BlockSpec(memory_space=pltpu.VMEM),
                  pl.BlockSpec(memory_space=pltpu.VMEM)],
        out_specs=pl.BlockSpec(memory_space=pltpu.VMEM),
        out_shape=jax.ShapeDtypeStruct((2, N_HEADS, BLOCK, 2 * BLOCK), f32))(rel_bias, bucket, valid)


def _band_kernel(q_ref, kp_ref, kc_ref, vp_ref, vc_ref, bias_ref, *rest, group, with_sink):
    if with_sink:
        sink_ref, o_ref = rest
    else:
        o_ref, lse_ref = rest
    scale = HEAD_DIM ** -0.5
    pair = LANE // HEAD_DIM
    lo = lax.broadcasted_iota(jnp.int32, (1, LANE), 1) < HEAD_DIM
    zero = jnp.zeros((), bf16)

    kv_blocks = {}

    def kv_block(blk, swapped):
        if (blk, swapped) not in kv_blocks:
            cols = slice(blk * LANE, (blk + 1) * LANE)
            k = jnp.concatenate([kp_ref[0, :, cols], kc_ref[0, :, cols]], axis=0)
            v = jnp.concatenate([vp_ref[0, :, cols], vc_ref[0, :, cols]], axis=0)
            if swapped:
                k = jnp.concatenate([k[:, HEAD_DIM:], k[:, :HEAD_DIM]], axis=1)
                v = jnp.concatenate([v[:, HEAD_DIM:], v[:, :HEAD_DIM]], axis=1)
            kv_blocks[(blk, swapped)] = (k, v)
        return kv_blocks[(blk, swapped)]

    heads = range(N_HEADS)
    qp = [q_ref[0, :, p * LANE:(p + 1) * LANE] for p in range(N_HEADS // pair)]
    qm = [jnp.where(lo if h % pair == 0 else ~lo, qp[h // pair], zero) for h in heads]
    kvs = [kv_block((h // group) // pair, (h // group) % pair != h % pair) for h in heads]
    ss = [lax.dot_general(q, kv[0], _NT, preferred_element_type=f32) * scale + bias_ref[0, h]
          for h, q, kv in zip(heads, qm, kvs)]
    ms = [s.max(-1, keepdims=True) for s in ss]
    ps = [jnp.exp(s - m) for s, m in zip(ss, ms)]
    ls = [p.sum(-1, keepdims=True) for p in ps]
    os_ = [jnp.dot(p.astype(bf16), kv[1], preferred_element_type=f32) / l for p, kv, l in zip(ps, kvs, ls)]
    lses = [jnp.broadcast_to(m + jnp.log(l), (BLOCK, LANE)) for m, l in zip(ms, ls)]
    for p in range(N_HEADS // pair):
        cols = slice(p * LANE, (p + 1) * LANE)
        o = jnp.where(lo, os_[pair * p], os_[pair * p + 1])
        lse = jnp.where(lo, lses[pair * p], lses[pair * p + 1])
        if with_sink:
            o = o * jax.nn.sigmoid(lse - sink_ref[:, cols])
        else:
            lse_ref[0, :, cols] = lse
        o_ref[0, :, cols] = o.astype(o_ref.dtype)


def _band_attention(view, bias, dil, feat, q_col, k_col, v_col, kv_width, group, sink=None):
    b, m_len, _ = view.shape
    d_q = N_HEADS * HEAD_DIM
    nb = m_len // BLOCK
    qpr, kpr = feat // d_q, feat // kv_width

    def cur(col, per_row):
        return lambda bi, r, n: (bi, n, r * per_row + col)

    def prev(col, per_row):
        return lambda bi, r, n: (bi, jnp.maximum(n - 1, 0), r * per_row + col)

    in_specs = [pl.BlockSpec((1, BLOCK, d_q), cur(q_col, qpr)),
                pl.BlockSpec((1, BLOCK, kv_width), prev(k_col, kpr)),
                pl.BlockSpec((1, BLOCK, kv_width), cur(k_col, kpr)),
                pl.BlockSpec((1, BLOCK, kv_width), prev(v_col, kpr)),
                pl.BlockSpec((1, BLOCK, kv_width), cur(v_col, kpr)),
                pl.BlockSpec((1, N_HEADS, BLOCK, 2 * BLOCK),
                             lambda bi, r, n: (jnp.minimum(n, 1), 0, 0, 0))]
    args = [view, view, view, view, view, bias]
    o_spec = pl.BlockSpec((1, BLOCK, d_q), lambda bi, r, n: (bi, n, r))
    o_shape = jax.ShapeDtypeStruct((b, m_len, dil * d_q), bf16)
    if sink is not None:
        in_specs.append(pl.BlockSpec((1, d_q), lambda bi, r, n: (0, 0)))
        args.append(sink)
        out_specs, out_shape = o_spec, o_shape
    else:
        out_specs = [o_spec, o_spec]
        out_shape = [o_shape, jax.ShapeDtypeStruct((b, m_len, dil * d_q), f32)]
    out = pl.pallas_call(
        functools.partial(_band_kernel, group=group, with_sink=sink is not None),
        grid=(b, dil, nb), in_specs=in_specs, out_specs=out_specs, out_shape=out_shape,
        compiler_params=_params("parallel", "parallel", "arbitrary"))(*args)
    if sink is not None:
        return out.reshape(b * m_len, dil * d_q)
    return out[0].reshape(b * m_len, dil * d_q), out[1].reshape(b * m_len, dil * d_q)


def _layer_norm(z, g, b):
    mu = z.mean(-1, keepdims=True)
    zc = z - mu
    var = jnp.square(zc).mean(-1, keepdims=True)
    return zc * lax.rsqrt(var + LN_EPS) * g + b


def _proj_ln_kernel(*refs, dils, alpha):
    n_pat = len(dils)
    o_refs = refs[:n_pat]
    lse_refs = refs[n_pat:2 * n_pat] if n_pat > 1 else ()
    n_in = len(o_refs) + len(lse_refs)
    w_ref, x_ref, g_ref, b_ref, out_ref = refs[n_in:n_in + 5]
    scratch = refs[n_in + 5:]
    if n_pat == 1 and len(o_refs[0].shape) == 3:
        groups, _, width = o_refs[0].shape
        y = sum(jnp.dot(o_refs[0][gi], w_ref[gi * width:(gi + 1) * width, :], preferred_element_type=f32)
                for gi in range(groups))
        out_ref[...] = _layer_norm(alpha * x_ref[...] + y, g_ref[...], b_ref[...])
        return
    if n_pat == 1:
        o = o_refs[0][...]
    else:
        kw = w_ref.shape[0]
        mix_ref = scratch[-1]

        def natural(ref, dil, scr, c):
            if dil == 1:
                return ref[:, c * LANE:(c + 1) * LANE].astype(f32)
            rows = ref.shape[0]
            for r in range(dil):
                scr[c, pl.ds(r, rows, stride=dil), :] = (
                    ref[:, r * kw + c * LANE:r * kw + (c + 1) * LANE].astype(f32))
            return scr[c]

        for c in range(kw // LANE):
            lses = [natural(r, dl, scratch[2 * i], c) for i, (r, dl) in enumerate(zip(lse_refs, dils))]
            mx = functools.reduce(jnp.maximum, lses)
            es = [jnp.exp(l - mx) for l in lses]
            num = sum(e * natural(r, dl, scratch[2 * i + 1], c)
                      for i, (e, r, dl) in enumerate(zip(es, o_refs, dils)))
            mix_ref[:, c * LANE:(c + 1) * LANE] = (num / sum(es)).astype(bf16)
        o = mix_ref[...]
    y = jnp.dot(o, w_ref[...], preferred_element_type=f32)
    out_ref[...] = _layer_norm(alpha * x_ref[...] + y, g_ref[...], b_ref[...])


def _proj_ln(os_, lses, dils, w_out, x, g, b, alpha, tm=512):
    t, d = x.shape
    tm = min(tm, t)
    kw = w_out.shape[0]
    row = lambda width: pl.BlockSpec((tm, width), lambda i: (i, 0))
    dilated = lambda dl: pl.BlockSpec((tm // dl, dl * kw), lambda i: (i, 0))
    fixed = lambda shape: pl.BlockSpec(shape, lambda i: (0, 0))
    if os_[0].ndim == 3:
        o_specs = [pl.BlockSpec((os_[0].shape[0], tm, os_[0].shape[2]), lambda i: (0, i, 0))]
    else:
        o_specs = [dilated(dl) for dl in dils]
    in_specs = (o_specs + [dilated(dl) for dl in dils[:len(lses)]]
                + [fixed((kw, d)), row(d), fixed((1, d)), fixed((1, d))])
    scratch = [pltpu.VMEM((kw // LANE, tm, LANE), f32)] * (2 * len(lses))
    if lses:
        scratch.append(pltpu.VMEM((tm, kw), bf16))
    return pl.pallas_call(
        functools.partial(_proj_ln_kernel, dils=tuple(dils), alpha=alpha),
        grid=(t // tm,), in_specs=in_specs, out_specs=row(d),
        out_shape=jax.ShapeDtypeStruct((t, d), f32), scratch_shapes=scratch,
        compiler_params=_params("parallel"))(*os_, *lses, w_out, x, g.reshape(1, d), b.reshape(1, d))


def _mm_dil_kernel(x_ref, w_ref, o_ref, res_ref, *, dil):
    n = w_ref.shape[1]
    res = jnp.dot(x_ref[...].astype(bf16), w_ref[...], preferred_element_type=f32)
    if dil == 1:
        o_ref[...] = res.astype(o_ref.dtype)
        return
    rows = o_ref.shape[0]
    for c in range(n // LANE):
        res_ref[c] = res[:, c * LANE:(c + 1) * LANE]
        for r in range(dil):
            o_ref[:, r * n + c * LANE:r * n + (c + 1) * LANE] = (
                res_ref[c, pl.ds(r, rows, stride=dil), :].astype(o_ref.dtype))


def _mm_dil(x, w, dil, tm=512):
    t, k = x.shape
    n = w.shape[1]
    tm = min(tm, t)
    return pl.pallas_call(
        functools.partial(_mm_dil_kernel, dil=dil), grid=(t // tm,),
        in_specs=[pl.BlockSpec((tm, k), lambda i: (i, 0)), pl.BlockSpec((k, n), lambda i: (0, 0))],
        out_specs=pl.BlockSpec((tm // dil, dil * n), lambda i: (i, 0)),
        out_shape=jax.ShapeDtypeStruct((t // dil, dil * n), bf16),
        scratch_shapes=[pltpu.VMEM((n // LANE, tm, LANE), f32)],
        compiler_params=_params("parallel"))(x, w)


def _rms(c, g):
    return c * lax.rsqrt(jnp.square(c).mean(-1, keepdims=True) + RMS_EPS) * g


def _mla_prep_kernel(x_ref, win_ref, qn_ref, wq_ref, kvn_ref, wkv_ref, cos_ref, sin_ref,
                     q_ref, k_ref, v_ref):
    hw = N_HEADS * LANE
    scale = (MLA_NOPE + MLA_ROPE) ** -0.5
    cos, sin = cos_ref[...], sin_ref[...]
    xw = jnp.dot(x_ref[...].astype(bf16), win_ref[...], preferred_element_type=f32)
    cq = _rms(xw[:, :MLA_Q_RANK], qn_ref[...]).astype(bf16)
    ckv = _rms(xw[:, MLA_Q_RANK:MLA_Q_RANK + MLA_KV_RANK], kvn_ref[...]).astype(bf16)
    off = MLA_Q_RANK + MLA_KV_RANK
    kr = xw[:, off:off + LANE] * cos + xw[:, off + LANE:off + 2 * LANE] * sin
    qq = jnp.dot(cq, wq_ref[...], preferred_element_type=f32)
    kv = jnp.dot(ckv, wkv_ref[...], preferred_element_type=f32)
    ones_lane = lax.broadcasted_iota(jnp.int32, (1, LANE), 1) == MLA_V
    for h in range(N_HEADS):
        blk = slice(h * LANE, (h + 1) * LANE)
        swp = slice(hw + h * LANE, hw + (h + 1) * LANE)
        grp, sub = divmod(h, MLA_GROUP)
        dst = slice(sub * LANE, (sub + 1) * LANE)
        q_ref[grp, :, dst] = ((qq[:, blk] * cos + qq[:, swp] * sin) * scale).astype(bf16)
        k_ref[grp, :, dst] = (kv[:, blk] + kr).astype(bf16)
        v_ref[grp, :, dst] = jnp.where(ones_lane, 1.0, kv[:, swp]).astype(bf16)


def _mla_prep(x, w_in, q_norm, w_uq, kv_norm, w_ukv, seq, tm=512):
    t, d = x.shape
    tm = min(tm, seq)
    hw = N_HEADS * LANE
    half = MLA_ROPE // 2
    dq = MLA_NOPE + MLA_ROPE
    kr_w = w_in[:, MLA_Q_RANK + MLA_KV_RANK:]
    zeros = lambda n: jnp.zeros((d, n), f32)
    kr_a = jnp.concatenate([zeros(MLA_NOPE), kr_w, zeros(LANE - dq)], 1)
    kr_b = jnp.concatenate([zeros(MLA_NOPE), kr_w[:, half:], kr_w[:, :half], zeros(LANE - dq)], 1)
    win = jnp.concatenate([w_in[:, :MLA_Q_RANK + MLA_KV_RANK], kr_a, kr_b], 1).astype(bf16)
    wq3 = w_uq.reshape(MLA_Q_RANK, N_HEADS, dq)
    zq = lambda n: jnp.zeros((MLA_Q_RANK, N_HEADS, n), f32)
    wq_a = jnp.concatenate([wq3, zq(LANE - dq)], 2)
    wq_b = jnp.concatenate([zq(MLA_NOPE), wq3[:, :, MLA_NOPE + half:], wq3[:, :, MLA_NOPE:MLA_NOPE + half],
                            zq(LANE - dq)], 2)
    wq = jnp.concatenate([wq_a.reshape(MLA_Q_RANK, hw), wq_b.reshape(MLA_Q_RANK, hw)], 1).astype(bf16)
    wkv3 = w_ukv.reshape(MLA_KV_RANK, N_HEADS, MLA_NOPE + MLA_V)
    zk = lambda n: jnp.zeros((MLA_KV_RANK, N_HEADS, n), f32)
    wk = jnp.concatenate([wkv3[:, :, :MLA_NOPE], zk(LANE - MLA_NOPE)], 2)
    wv = jnp.concatenate([wkv3[:, :, MLA_NOPE:], zk(LANE - MLA_V)], 2)
    wkv = jnp.concatenate([wk.reshape(MLA_KV_RANK, hw), wv.reshape(MLA_KV_RANK, hw)], 1).astype(bf16)
    freq = ROPE_THETA ** (-jnp.arange(half, dtype=f32) / half)
    ang = jnp.arange(seq)[:, None].astype(f32) * freq[None, :]
    c, s_ = jnp.cos(ang), jnp.sin(ang)
    cos_t = jnp.concatenate([jnp.ones((seq, MLA_NOPE), f32), c, c, jnp.ones((seq, LANE - dq), f32)], 1)
    sin_t = jnp.concatenate([jnp.zeros((seq, MLA_NOPE), f32), -s_, s_, jnp.zeros((seq, LANE - dq), f32)], 1)

    n_pos = seq // tm
    row = lambda width: pl.BlockSpec((tm, width), lambda i: (i, 0))
    fixed = lambda a: pl.BlockSpec(a.shape, lambda i: (0, 0))
    pos = pl.BlockSpec((tm, LANE), lambda i: (i % n_pos, 0))
    qn, kvn = q_norm.reshape(1, -1), kv_norm.reshape(1, -1)
    groups, gw = N_HEADS // MLA_GROUP, MLA_GROUP * LANE
    out_sd = jax.ShapeDtypeStruct((groups, t, gw), bf16)
    out_spec = pl.BlockSpec((groups, tm, gw), lambda i: (0, i, 0))
    return pl.pallas_call(
        _mla_prep_kernel, grid=(t // tm,),
        in_specs=[row(d), fixed(win), fixed(qn), fixed(wq), fixed(kvn), fixed(wkv), pos, pos],
        out_specs=[out_spec] * 3, out_shape=[out_sd] * 3,
        compiler_params=_params("parallel"))(x, win, qn, wq, kvn, wkv, cos_t, sin_t)


def _mla_attn_kernel(q_ref, k_ref, v_ref, o_ref, *, tq, tk, heads):
    qi = pl.program_id(2)
    lanes = [slice(j * LANE, (j + 1) * LANE) for j in range(heads)]
    qs = [q_ref[0, :, ln] for ln in lanes]

    def update(carry, start, mask):
        ss = [lax.dot_general(q, k_ref[0, pl.ds(start, tk), ln], _NT, preferred_element_type=f32)
              for q, ln in zip(qs, lanes)]
        if mask is not None:
            ss = [jnp.where(mask, s, NEG) for s in ss]
        ms = [jnp.maximum(m, s.max(-1, keepdims=True)) for (m, _), s in zip(carry, ss)]
        ps = [jnp.exp(s - m).astype(bf16) for s, m in zip(ss, ms)]
        pvs = [jnp.dot(p, v_ref[0, pl.ds(start, tk), ln], preferred_element_type=f32)
               for p, ln in zip(ps, lanes)]
        return tuple((m_new, jnp.exp(m - m_new) * acc + pv)
                     for (m, acc), m_new, pv in zip(carry, ms, pvs))

    init = tuple((jnp.full((tq, 1), NEG, f32), jnp.zeros((tq, LANE), f32)) for _ in range(heads))
    sub = tq // tk
    carry = lax.fori_loop(0, qi * sub, lambda ki, c: update(c, pl.multiple_of(ki * tk, tk), None), init)
    row = lax.broadcasted_iota(jnp.int32, (tq, tk), 0)
    col = lax.broadcasted_iota(jnp.int32, (tq, tk), 1)
    for j in range(sub):
        carry = update(carry, pl.multiple_of(qi * tq + j * tk, tk), col + j * tk <= row)
    for j in range(heads):
        acc = carry[j][1]
        o_ref[0, :, lanes[j]] = (acc / acc[:, MLA_V:MLA_V + 1]).astype(o_ref.dtype)


def _mla_attention(q, k, v, batch, seq, tq=1024, tk=512):
    groups, t, gw = q.shape
    tq = min(tq, seq)
    nq = seq // tq
    qspec = pl.BlockSpec((1, tq, gw), lambda b, h, i: (h, b * nq + i, 0))
    kspec = pl.BlockSpec((1, seq, gw), lambda b, h, i: (h, b, 0))
    return pl.pallas_call(
        functools.partial(_mla_attn_kernel, tq=tq, tk=min(tk, tq), heads=MLA_GROUP),
        grid=(batch, groups, nq),
        in_specs=[qspec, kspec, kspec], out_specs=qspec,
        out_shape=jax.ShapeDtypeStruct((groups, t, gw), bf16),
        compiler_params=_params("parallel", "parallel", "arbitrary"))(q, k, v)


def _batcher_network(n):
    def merge(lo, hi, r):
        step = r * 2
        if step < hi - lo:
            yield from merge(lo, hi, step)
            yield from merge(lo + r, hi, step)
            yield from [(i, i + r) for i in range(lo + r, hi - r, step)]
        else:
            yield (lo, lo + r)

    def sort(lo, hi):
        if hi - lo >= 1:
            mid = lo + (hi - lo) // 2
            yield from sort(lo, mid)
            yield from sort(mid + 1, hi)
            yield from merge(lo, hi, 1)

    return tuple(sort(0, n - 1))


def _top16_values(s):
    n = s.shape[0] // F32_ROWS
    tiles = [s[j * F32_ROWS:(j + 1) * F32_ROWS] for j in range(n)]
    for i, j in _batcher_network(PEER_TOPK):
        if j < n:
            tiles[i], tiles[j] = jnp.maximum(tiles[i], tiles[j]), jnp.minimum(tiles[i], tiles[j])
    vals = []
    for k in range(PEER_TOPK):
        mx = tiles[0].max(axis=0, keepdims=True)
        vals.append(mx)
        hit = tiles[0] == mx
        for j in range(min(n, PEER_TOPK - 1 - k)):
            below = tiles[j + 1] if j + 1 < n else -jnp.inf
            tiles[j] = jnp.where(hit, below, tiles[j])
    return vals


def _top16_pair(s1, s2):
    v1, v2 = _top16_values(s1), _top16_values(s2)
    rank2 = jnp.zeros(s2.shape, f32)
    for k in range(PEER_TOPK):
        rank2 = jnp.where(s2 < v2[k], float(k + 1), rank2)
    return v1, v2, rank2


def _stack_rows(rows):
    n = len(rows)
    rid = lax.broadcasted_iota(jnp.int32, (n, rows[0].shape[1]), 0)
    out = jnp.broadcast_to(rows[0], (n, rows[0].shape[1]))
    for i in range(1, n):
        out = jnp.where(rid == i, rows[i], out)
    return out


def _route_kernel(x_ref, wq_ref, keys_ref, rank2_ref, crow_ref, e1z_ref, e2_ref, *, tm):
    q = jnp.dot(x_ref[...].astype(bf16), wq_ref[...], preferred_element_type=f32).astype(bf16)
    half = PEER_DKEY // 2
    scores = []
    for hh in range(ROUTE_HEADS):
        qh = q[:, hh * PEER_DKEY:(hh + 1) * PEER_DKEY]
        scores.append((lax.dot_general(keys_ref[hh, 0], qh[:, :half], _NT, preferred_element_type=f32),
                       lax.dot_general(keys_ref[hh, 1], qh[:, half:], _NT, preferred_element_type=f32)))
    rid8 = lax.broadcasted_iota(jnp.int32, (8, LANE), 0)
    for hh, c in [(hh, c) for hh in range(ROUTE_HEADS) for c in range(tm // LANE)]:
        lanes = slice(c * LANE, (c + 1) * LANE)
        s1, s2 = scores[hh][0][:, lanes], scores[hh][1][:, lanes]
        v1, v2, rank2 = _top16_pair(s1, s2)
        v2_all = _stack_rows(v2)
        cands = [v1[0] + v2_all]
        for a in range(1, 8):
            cands.append(jnp.where(rid8 < PEER_TOPK // (a + 1), v1[a] + v2_all[:8], -jnp.inf))
        cands.append(_stack_rows(v1[8:]) + v2[0])
        cand = jnp.concatenate(cands, axis=0)
        tau = _top16_values(cand)[-1]
        cmax = v1[0] + v2[0]
        z = jnp.where(cand >= tau, jnp.exp(cand - cmax), 0.0).sum(axis=0, keepdims=True)
        crow = jnp.zeros(s1.shape, f32)
        for b in range(PEER_TOPK):
            crow = crow + jnp.where(s1 + v2[b] >= tau, 1.0, 0.0)
        rank2_ref[0, hh, :, lanes] = rank2.astype(bf16)
        crow_ref[0, hh, :, lanes] = crow
        e1z_ref[0, hh, :, lanes] = jnp.exp(s1 - v1[0]) * (GELU_FOLD / z)
        e2_ref[0, hh, :, lanes] = jnp.exp(s2 - v2[0]).astype(bf16)


def _peer_route(x, w_q, keys, tm=ROUTE_TOKENS):
    t, d = x.shape
    tm = min(tm, t)
    hb = ROUTE_HEADS
    spec = pl.BlockSpec((1, hb, PEER_KEYS, tm), lambda i, h: (i, h, 0, 0))
    sd = lambda dt: jax.ShapeDtypeStruct((t // tm, PEER_HEADS, PEER_KEYS, tm), dt)
    return pl.pallas_call(
        functools.partial(_route_kernel, tm=tm), grid=(t // tm, PEER_HEADS // hb),
        in_specs=[pl.BlockSpec((tm, d), lambda i, h: (i, 0)),
                  pl.BlockSpec((d, hb * PEER_DKEY), lambda i, h: (0, h)),
                  pl.BlockSpec((hb, 2, PEER_KEYS, PEER_DKEY // 2), lambda i, h: (h, 0, 0, 0))],
        out_specs=[spec] * 4, out_shape=[sd(bf16), sd(f32), sd(f32), sd(bf16)],
        compiler_params=_params("parallel", "arbitrary"))(x, w_q, keys)


def _peer_dense_kernel(x_ref, u0_ref, u1_ref, vt0_ref, vt1_ref, rank2_in, crow_ref, e1z_ref, e2_in,
                       g_ref, b_ref, o_ref, acc_ref, xt_ref, ht0, ht1, gt0, gt1, rank2_ref, e2_ref,
                       *, rows, tb, lc, mw, alpha):
    k = pl.program_id(1)
    ec = rows * PEER_KEYS

    @pl.when(k == 0)
    def _():
        acc_ref[...] = jnp.zeros_like(acc_ref)
        ht1[...] = jnp.zeros_like(ht1)
        gt0[...] = jnp.zeros_like(gt0)
        xt_ref[...] = x_ref[...].T.astype(bf16)
        rank2_ref[...] = rank2_in[0]
        e2_ref[...] = e2_in[0]

    zero = jnp.zeros((), bf16)
    kt = PEER_KEYS // BF16_ROWS

    def up(half, ht, cw):
        lanes = slice(cw * mw, (cw + 1) * mw)
        u_blk = pltpu.bitcast((u0_ref, u1_ref)[half][0], bf16)
        ht[:, lanes] = jnp.dot(u_blk, xt_ref[:, lanes], preferred_element_type=f32)

    n_tiles = pl.num_programs(1) - 1

    def gate(ht, tile, half, gt, cw):
        base = pl.multiple_of(jnp.clip(tile, 0, n_tiles - 1) * F32_ROWS, F32_ROWS)
        for r in range(rows):
            keys = slice(r * PEER_KEYS, (r + 1) * PEER_KEYS)
            row = slice(half * rows + r, half * rows + r + 1)
            for c in range(cw * mw // lc, (cw + 1) * mw // lc):
                lanes = slice(c * lc, (c + 1) * lc)
                w = None
                for h in range(PEER_HEADS):
                    cr = crow_ref[0, h, pl.ds(base, F32_ROWS), lanes][row]
                    ez = e1z_ref[0, h, pl.ds(base, F32_ROWS), lanes][row]
                    cr = jnp.broadcast_to(cr, (BF16_ROWS, lc)).astype(bf16)[None]
                    ez = jnp.broadcast_to(ez, (BF16_ROWS, lc)).astype(bf16)[None]
                    term = jnp.where(rank2_ref[h, :, :, lanes] < cr, e2_ref[h, :, :, lanes] * ez, zero)
                    w = term if w is None else w + term
                hv = ht[keys, lanes]
                g = hv * (1.0 + lax.erf(hv))
                gt[r * kt:(r + 1) * kt, :, lanes] = g.astype(bf16).reshape(kt, BF16_ROWS, lc) * w

    def down(half, gt, cw):
        lanes = slice(cw * mw, (cw + 1) * mw)
        vt_blk = pltpu.bitcast((vt0_ref, vt1_ref)[half][0, 0], bf16)
        acc_ref[:, lanes] += jnp.dot(vt_blk, gt[:, :, lanes].reshape(ec, mw), preferred_element_type=f32)

    for cw in range(tb // mw):
        up(0, ht0, cw)
        gate(ht1, k - 1, 1, gt1, cw)
        down(0, gt0, cw)
    for cw in range(tb // mw):
        up(1, ht1, cw)
        gate(ht0, k, 0, gt0, cw)
        down(1, gt1, cw)

    @pl.when(k == pl.num_programs(1) - 1)
    def _():
        z = alpha * x_ref[...] + acc_ref[...].T
        o_ref[...] = _layer_norm(z, g_ref[...], b_ref[...])


def _peer_dense(x, u, vt, layer, routing, g, b, alpha, lc=128):
    t, d = x.shape
    e = 2 * u.shape[1]
    ec = DENSE_EC
    rows = ec // PEER_KEYS
    assert 2 * rows == F32_ROWS, "a pair of expert blocks must span one 8-row f32 tile of sub-key-1 rows"
    nk = e // (2 * ec)
    kt = PEER_KEYS // BF16_ROWS
    rank2, crow, e1z, e2 = routing
    nt, _, _, tb = crow.shape
    rank2, e2 = (a.reshape(nt, PEER_HEADS, kt, BF16_ROWS, tb) for a in (rank2, e2))
    full = pl.BlockSpec((1, PEER_HEADS, kt, BF16_ROWS, tb), lambda i, k: (i, 0, 0, 0, 0))
    rowtab = pl.BlockSpec((1, PEER_HEADS, PEER_KEYS, tb), lambda i, k: (i, 0, 0, 0))
    fixed = pl.BlockSpec((1, d), lambda i, k: (0, 0))
    xspec = pl.BlockSpec((tb, d), lambda i, k: (i, 0))
    xin = xspec
    gt_scr = pltpu.VMEM((ec // BF16_ROWS, BF16_ROWS, tb), bf16)
    u_blk = lambda half: pl.BlockSpec(
        (1, ec // 2, d), lambda i, k: (layer, 2 * jnp.minimum(k, nk - 1) + half, 0))
    vt_blk = lambda half: pl.BlockSpec(
        (1, 1, d // 2, ec), lambda i, k: (layer, 2 * jnp.maximum(k - 1, 0) + half, 0, 0))
    return pl.pallas_call(
        functools.partial(_peer_dense_kernel, rows=rows, tb=tb, lc=lc, mw=MXU_WIDTH, alpha=alpha),
        grid=(t // tb, nk + 1),
        in_specs=[xin, u_blk(0), u_blk(1), vt_blk(0), vt_blk(1),
                  full, rowtab, rowtab, full, fixed, fixed],
        out_specs=xspec, out_shape=jax.ShapeDtypeStruct((t, d), f32),
        scratch_shapes=[pltpu.VMEM((d, tb), f32), pltpu.VMEM((d, tb), bf16),
                        pltpu.VMEM((ec, tb), f32), pltpu.VMEM((ec, tb), f32), gt_scr, gt_scr,
                        pltpu.VMEM((PEER_HEADS, kt, BF16_ROWS, tb), bf16),
                        pltpu.VMEM((PEER_HEADS, kt, BF16_ROWS, tb), bf16)],
        compiler_params=_params("parallel", "arbitrary"))(
            x, u, u, vt, vt, rank2, crow, e1z, e2, g.reshape(1, d), b.reshape(1, d))


def _swa_layer(x, batch, seq, w_in, sinks, w_out, bias, g, b, alpha):
    feat = (N_HEADS + 2 * SWA_KV_HEADS) * HEAD_DIM
    d_q = N_HEADS * HEAD_DIM
    kvw = SWA_KV_HEADS * HEAD_DIM
    proj = _mm(x, w_in.astype(bf16), bf16, 512, feat).reshape(batch, seq, feat)
    sink = jnp.repeat(sinks.astype(f32), HEAD_DIM).reshape(1, d_q)
    o = _band_attention(proj, bias, 1, feat, 0, d_q // kvw, d_q // kvw + 1, kvw,
                        N_HEADS // SWA_KV_HEADS, sink)
    return _proj_ln([o], [], [1], w_out.astype(bf16), x, g, b, alpha)


def _dil_layer(x, batch, seq, w_in, w_out, biases, g, b, alpha):
    d_q = N_HEADS * HEAD_DIM
    feat = 3 * d_q
    w_in = w_in.astype(bf16)
    outs, lses, dils = [], [], []
    for gi, (window, dil) in enumerate(DIL_PATTERNS):
        proj = _mm_dil(x, w_in[:, gi * feat:(gi + 1) * feat], dil).reshape(batch, seq // dil, dil * feat)
        o, lse = _band_attention(proj, biases[gi], dil, feat, 0, 1, 2, d_q, 1)
        outs.append(o)
        lses.append(lse)
        dils.append(dil)
    return _proj_ln(outs, lses, dils, w_out.astype(bf16), x, g, b, alpha)


def _mla_layer(x, batch, seq, w_in, q_norm, w_uq, kv_norm, w_ukv, w_out, g, b, alpha):
    q, k, v = _mla_prep(x, w_in, q_norm, w_uq, kv_norm, w_ukv, seq)
    o = _mla_attention(q, k, v, batch, seq)
    d = w_out.shape[1]
    w3 = w_out.reshape(N_HEADS, MLA_V, d)
    w_pad = jnp.concatenate([w3, jnp.zeros((N_HEADS, LANE - MLA_V, d), f32)], 1)
    return _proj_ln([o], [], [1], w_pad.reshape(N_HEADS * LANE, d).astype(bf16), x, g, b, alpha)


def _pack_u_kernel(u_ref, o_ref):
    o_ref[0] = pltpu.bitcast((u_ref[0] * GELU_FOLD).astype(bf16), jnp.uint32)


def _pack_vt_kernel(v_ref, o_ref):
    o_ref[0, 0] = pltpu.bitcast(v_ref[0].T.astype(bf16), jnp.uint32)


def _pack_expert_weights(u, v):
    depth, e, d = u.shape
    ec = DENSE_EC
    src = pl.BlockSpec((1, ec, d), lambda l, j: (l, j, 0))
    up = pl.pallas_call(
        _pack_u_kernel, grid=(depth, e // ec), in_specs=[src],
        out_specs=pl.BlockSpec((1, ec // 2, d), lambda l, j: (l, j, 0)),
        out_shape=jax.ShapeDtypeStruct((depth, e // 2, d), jnp.uint32),
        compiler_params=_params("parallel", "parallel"))(u)
    vtp = pl.pallas_call(
        _pack_vt_kernel, grid=(depth, e // ec), in_specs=[src],
        out_specs=pl.BlockSpec((1, 1, d // 2, ec), lambda l, j: (l, j, 0, 0)),
        out_shape=jax.ShapeDtypeStruct((depth, e // ec, d // 2, ec), jnp.uint32),
        compiler_params=_params("parallel", "parallel"))(v)
    return up, vtp


def _peer_layer(x, w_q, keys, u_packed, vt_packed, layer, g, b, alpha):
    routing = _peer_route(x, w_q.astype(bf16), keys.astype(bf16))
    return _peer_dense(x, u_packed, vt_packed, layer, routing, g, b, alpha)


def kernel(x, rel_bias, ln_g, ln_b, swa_w_in, swa_sinks, swa_w_out, dil_w_in, dil_w_out,
           mla_w_in, mla_q_norm, mla_w_uq, mla_kv_norm, mla_w_ukv, mla_w_out,
           peer_w_q, peer_keys, peer_u, peer_v):
    batch, seq, d = x.shape
    depth = ln_g.shape[0]
    alpha = (2 * depth) ** 0.25
    assert seq % (DIL_PATTERNS[-1][1] * BLOCK) == 0, "sequence must be a whole number of dilation segments"
    swa_bias = _band_bias(rel_bias, SWA_WINDOW - 1, 1)
    dil_bias = [_band_bias(rel_bias, window // dil, dil) for window, dil in DIL_PATTERNS]
    u_packed, vt_packed = _pack_expert_weights(peer_u, peer_v)
    h = x.reshape(batch * seq, d)
    for i in range(depth):
        kind, j = i % 3, i // 3
        if kind == 0:
            h = _swa_layer(h, batch, seq, swa_w_in[j], swa_sinks[j], swa_w_out[j], swa_bias,
                           ln_g[i, 0], ln_b[i, 0], alpha)
        elif kind == 1:
            h = _dil_layer(h, batch, seq, dil_w_in[j], dil_w_out[j], dil_bias,
                           ln_g[i, 0], ln_b[i, 0], alpha)
        else:
            h = _mla_layer(h, batch, seq, mla_w_in[j], mla_q_norm[j], mla_w_uq[j], mla_kv_norm[j],
                           mla_w_ukv[j], mla_w_out[j], ln_g[i, 0], ln_b[i, 0], alpha)
        h = _peer_layer(h, peer_w_q[i], peer_keys[i], u_packed, vt_packed, i, ln_g[i, 1], ln_b[i, 1], alpha)
    return h.reshape(batch, seq, d)
```

```python
import functools
import math

import jax
import jax.numpy as jnp
from jax import lax
from jax.experimental import pallas as pl
from jax.experimental.pallas import tpu as pltpu

f32 = jnp.float32
bf16 = jnp.bfloat16

N_HEADS = 16
HEAD_DIM = 64
BLOCK = 128
SWA_KV_HEADS = 2
SWA_WINDOW = 128
DIL_PATTERNS = ((128, 1), (512, 4), (2048, 16))
MLA_Q_RANK = 256
MLA_KV_RANK = 128
MLA_NOPE = 64
MLA_ROPE = 32
MLA_V = 64
ROPE_THETA = 10000.0
REL_BUCKETS = 32
REL_MAX_DIST = 2048
PEER_HEADS = 8
PEER_KEYS = 128
PEER_DKEY = 256
PEER_TOPK = 16
LN_EPS = 1e-5
RMS_EPS = 1e-6
NEG = -1e30

LANE = 128
BF16_ROWS = 16
F32_ROWS = 8
MXU_WIDTH = 256
MLA_GROUP = 2
ROUTE_HEADS = 4
GELU_FOLD = 0.5 ** 0.5
ROUTE_TOKENS = 512
DENSE_EC = 4 * PEER_KEYS
VMEM_LIMIT = 56 * 1024 * 1024

_NT = (((1,), (1,)), ((), ()))


def _params(*sem):
    return pltpu.CompilerParams(dimension_semantics=sem, vmem_limit_bytes=VMEM_LIMIT)


def _mm_kernel(x_ref, w_ref, o_ref):
    o_ref[...] = jnp.dot(x_ref[...].astype(bf16), w_ref[...],
                         preferred_element_type=f32).astype(o_ref.dtype)


def _mm(x, w, out_dtype, tm, tn):
    m, k = x.shape
    n = w.shape[1]
    tm, tn = min(tm, m), min(tn, n)
    return pl.pallas_call(
        _mm_kernel, grid=(m // tm, n // tn),
        in_specs=[pl.BlockSpec((tm, k), lambda i, j: (i, 0)),
                  pl.BlockSpec((k, tn), lambda i, j: (0, j))],
        out_specs=pl.BlockSpec((tm, tn), lambda i, j: (i, j)),
        out_shape=jax.ShapeDtypeStruct((m, n), out_dtype),
        compiler_params=_params("parallel", "parallel"))(x, w)


def _rel_bucket(dist):
    max_exact = REL_BUCKETS // 2
    n = jnp.maximum(dist, 0)
    nf = jnp.maximum(n, 1).astype(f32)
    large = max_exact + (jnp.log(nf / max_exact) / math.log(REL_MAX_DIST / max_exact)
                         * (REL_BUCKETS - max_exact)).astype(jnp.int32)
    large = jnp.minimum(large, REL_BUCKETS - 1)
    return jnp.where(n < max_exact, n, large)


def _bias_kernel(rel_ref, bucket_ref, valid_ref, o_ref):
    bucket = bucket_ref[...]
    for h in range(N_HEADS):
        acc = jnp.zeros(bucket.shape, f32)
        for b in range(REL_BUCKETS):
            acc = jnp.where(bucket == b, rel_ref[b, h], acc)
        for variant in range(2):
            o_ref[variant, h] = jnp.where(valid_ref[variant] != 0, acc, NEG)


def _band_bias(rel_bias, max_dist, dilation):
    qi = jnp.arange(BLOCK)[:, None]
    kj = jnp.arange(2 * BLOCK)[None, :]
    dist = BLOCK + qi - kj
    bucket = _rel_bucket(dist * dilation).astype(jnp.int32)
    valid = (dist >= 0) & (dist <= max_dist)
    valid = jnp.stack([valid & (kj >= BLOCK), valid]).astype(jnp.int32)
    return pl.pallas_call(
        _bias_kernel,
        in_specs=[pl.BlockSpec(memory_space=pltpu.SMEM),
                  pl.BlockSpec(memory_space=pltpu.VMEM),
                  pl.BlockSpec(memory_space=pltpu.VMEM)],
        out_specs=pl.BlockSpec(memory_space=pltpu.VMEM),
        out_shape=jax.ShapeDtypeStruct((2, N_HEADS, BLOCK, 2 * BLOCK), f32))(rel_bias, bucket, valid)


def _band_kernel(q_ref, kp_ref, kc_ref, vp_ref, vc_ref, bias_ref, *rest, group, with_sink):
    if with_sink:
        sink_ref, o_ref = rest
    else:
        o_ref, lse_ref = rest
    scale = HEAD_DIM ** -0.5
    pair = LANE // HEAD_DIM
    lo = lax.broadcasted_iota(jnp.int32, (1, LANE), 1) < HEAD_DIM
    zero = jnp.zeros((), bf16)

    kv_blocks = {}

    def kv_block(blk, swapped):
        if (blk, swapped) not in kv_blocks:
            cols = slice(blk * LANE, (blk + 1) * LANE)
            k = jnp.concatenate([kp_ref[0, :, cols], kc_ref[0, :, cols]], axis=0)
            v = jnp.concatenate([vp_ref[0, :, cols], vc_ref[0, :, cols]], axis=0)
            if swapped:
                k = jnp.concatenate([k[:, HEAD_DIM:], k[:, :HEAD_DIM]], axis=1)
                v = jnp.concatenate([v[:, HEAD_DIM:], v[:, :HEAD_DIM]], axis=1)
            kv_blocks[(blk, swapped)] = (k, v)
        return kv_blocks[(blk, swapped)]

    heads = range(N_HEADS)
    qp = [q_ref[0, :, p * LANE:(p + 1) * LANE] for p in range(N_HEADS // pair)]
    qm = [jnp.where(lo if h % pair == 0 else ~lo, qp[h // pair], zero) for h in heads]
    kvs = [kv_block((h // group) // pair, (h // group) % pair != h % pair) for h in heads]
    ss = [lax.dot_general(q, kv[0], _NT, preferred_element_type=f32) * scale + bias_ref[0, h]
          for h, q, kv in zip(heads, qm, kvs)]
    ms = [s.max(-1, keepdims=True) for s in ss]
    ps = [jnp.exp(s - m) for s, m in zip(ss, ms)]
    ls = [p.sum(-1, keepdims=True) for p in ps]
    os_ = [jnp.dot(p.astype(bf16), kv[1], preferred_element_type=f32) / l for p, kv, l in zip(ps, kvs, ls)]
    lses = [jnp.broadcast_to(m + jnp.log(l), (BLOCK, LANE)) for m, l in zip(ms, ls)]
    for p in range(N_HEADS // pair):
        cols = slice(p * LANE, (p + 1) * LANE)
        o = jnp.where(lo, os_[pair * p], os_[pair * p + 1])
        lse = jnp.where(lo, lses[pair * p], lses[pair * p + 1])
        if with_sink:
            o = o * jax.nn.sigmoid(lse - sink_ref[:, cols])
        else:
            lse_ref[0, :, cols] = lse
        o_ref[0, :, cols] = o.astype(o_ref.dtype)


def _band_attention(view, bias, dil, feat, q_col, k_col, v_col, kv_width, group, sink=None):
    b, m_len, _ = view.shape
    d_q = N_HEADS * HEAD_DIM
    nb = m_len // BLOCK
    qpr, kpr = feat // d_q, feat // kv_width

    def cur(col, per_row):
        return lambda bi, r, n: (bi, n, r * per_row + col)

    def prev(col, per_row):
        return lambda bi, r, n: (bi, jnp.maximum(n - 1, 0), r * per_row + col)

    in_specs = [pl.BlockSpec((1, BLOCK, d_q), cur(q_col, qpr)),
                pl.BlockSpec((1, BLOCK, kv_width), prev(k_col, kpr)),
                pl.BlockSpec((1, BLOCK, kv_width), cur(k_col, kpr)),
                pl.BlockSpec((1, BLOCK, kv_width), prev(v_col, kpr)),
                pl.BlockSpec((1, BLOCK, kv_width), cur(v_col, kpr)),
                pl.BlockSpec((1, N_HEADS, BLOCK, 2 * BLOCK),
                             lambda bi, r, n: (jnp.minimum(n, 1), 0, 0, 0))]
    args = [view, view, view, view, view, bias]
    o_spec = pl.BlockSpec((1, BLOCK, d_q), lambda bi, r, n: (bi, n, r))
    o_shape = jax.ShapeDtypeStruct((b, m_len, dil * d_q), bf16)
    if sink is not None:
        in_specs.append(pl.BlockSpec((1, d_q), lambda bi, r, n: (0, 0)))
        args.append(sink)
        out_specs, out_shape = o_spec, o_shape
    else:
        out_specs = [o_spec, o_spec]
        out_shape = [o_shape, jax.ShapeDtypeStruct((b, m_len, dil * d_q), f32)]
    out = pl.pallas_call(
        functools.partial(_band_kernel, group=group, with_sink=sink is not None),
        grid=(b, dil, nb), in_specs=in_specs, out_specs=out_specs, out_shape=out_shape,
        compiler_params=_params("parallel", "parallel", "arbitrary"))(*args)
    if sink is not None:
        return out.reshape(b * m_len, dil * d_q)
    return out[0].reshape(b * m_len, dil * d_q), out[1].reshape(b * m_len, dil * d_q)


def _layer_norm(z, g, b):
    mu = z.mean(-1, keepdims=True)
    zc = z - mu
    var = jnp.square(zc).mean(-1, keepdims=True)
    return zc * lax.rsqrt(var + LN_EPS) * g + b


def _proj_ln_kernel(*refs, dils, alpha):
    n_pat = len(dils)
    o_refs = refs[:n_pat]
    lse_refs = refs[n_pat:2 * n_pat] if n_pat > 1 else ()
    n_in = len(o_refs) + len(lse_refs)
    w_ref, x_ref, g_ref, b_ref, out_ref = refs[n_in:n_in + 5]
    scratch = refs[n_in + 5:]
    if n_pat == 1 and len(o_refs[0].shape) == 3:
        groups, _, width = o_refs[0].shape
        y = sum(jnp.dot(o_refs[0][gi], w_ref[gi * width:(gi + 1) * width, :], preferred_element_type=f32)
                for gi in range(groups))
        out_ref[...] = _layer_norm(alpha * x_ref[...] + y, g_ref[...], b_ref[...])
        return
    if n_pat == 1:
        o = o_refs[0][...]
    else:
        kw = w_ref.shape[0]
        mix_ref = scratch[-1]

        def natural(ref, dil, scr, c):
            if dil == 1:
                return ref[:, c * LANE:(c + 1) * LANE].astype(f32)
            rows = ref.shape[0]
            for r in range(dil):
                scr[c, pl.ds(r, rows, stride=dil), :] = (
                    ref[:, r * kw + c * LANE:r * kw + (c + 1) * LANE].astype(f32))
            return scr[c]

        for c in range(kw // LANE):
            lses = [natural(r, dl, scratch[2 * i], c) for i, (r, dl) in enumerate(zip(lse_refs, dils))]
            mx = functools.reduce(jnp.maximum, lses)
            es = [jnp.exp(l - mx) for l in lses]
            num = sum(e * natural(r, dl, scratch[2 * i + 1], c)
                      for i, (e, r, dl) in enumerate(zip(es, o_refs, dils)))
            mix_ref[:, c * LANE:(c + 1) * LANE] = (num / sum(es)).astype(bf16)
        o = mix_ref[...]
    y = jnp.dot(o, w_ref[...], preferred_element_type=f32)
    out_ref[...] = _layer_norm(alpha * x_ref[...] + y, g_ref[...], b_ref[...])


def _proj_ln(os_, lses, dils, w_out, x, g, b, alpha, tm=512):
    t, d = x.shape
    tm = min(tm, t)
    kw = w_out.shape[0]
    row = lambda width: pl.BlockSpec((tm, width), lambda i: (i, 0))
    dilated = lambda dl: pl.BlockSpec((tm // dl, dl * kw), lambda i: (i, 0))
    fixed = lambda shape: pl.BlockSpec(shape, lambda i: (0, 0))
    if os_[0].ndim == 3:
        o_specs = [pl.BlockSpec((os_[0].shape[0], tm, os_[0].shape[2]), lambda i: (0, i, 0))]
    else:
        o_specs = [dilated(dl) for dl in dils]
    in_specs = (o_specs + [dilated(dl) for dl in dils[:len(lses)]]
                + [fixed((kw, d)), row(d), fixed((1, d)), fixed((1, d))])
    scratch = [pltpu.VMEM((kw // LANE, tm, LANE), f32)] * (2 * len(lses))
    if lses:
        scratch.append(pltpu.VMEM((tm, kw), bf16))
    return pl.pallas_call(
        functools.partial(_proj_ln_kernel, dils=tuple(dils), alpha=alpha),
        grid=(t // tm,), in_specs=in_specs, out_specs=row(d),
        out_shape=jax.ShapeDtypeStruct((t, d), f32), scratch_shapes=scratch,
        compiler_params=_params("parallel"))(*os_, *lses, w_out, x, g.reshape(1, d), b.reshape(1, d))


def _mm_dil_kernel(x_ref, w_ref, o_ref, res_ref, *, dil):
    n = w_ref.shape[1]
    res = jnp.dot(x_ref[...].astype(bf16), w_ref[...], preferred_element_type=f32)
    if dil == 1:
        o_ref[...] = res.astype(o_ref.dtype)
        return
    rows = o_ref.shape[0]
    for c in range(n // LANE):
        res_ref[c] = res[:, c * LANE:(c + 1) * LANE]
        for r in range(dil):
            o_ref[:, r * n + c * LANE:r * n + (c + 1) * LANE] = (
                res_ref[c, pl.ds(r, rows, stride=dil), :].astype(o_ref.dtype))


def _mm_dil(x, w, dil, tm=512):
    t, k = x.shape
    n = w.shape[1]
    tm = min(tm, t)
    return pl.pallas_call(
        functools.partial(_mm_dil_kernel, dil=dil), grid=(t // tm,),
        in_specs=[pl.BlockSpec((tm, k), lambda i: (i, 0)), pl.BlockSpec((k, n), lambda i: (0, 0))],
        out_specs=pl.BlockSpec((tm // dil, dil * n), lambda i: (i, 0)),
        out_shape=jax.ShapeDtypeStruct((t // dil, dil * n), bf16),
        scratch_shapes=[pltpu.VMEM((n // LANE, tm, LANE), f32)],
        compiler_params=_params("parallel"))(x, w)


def _rms(c, g):
    return c * lax.rsqrt(jnp.square(c).mean(-1, keepdims=True) + RMS_EPS) * g


def _mla_prep_kernel(x_ref, win_ref, qn_ref, wq_ref, kvn_ref, wkv_ref, cos_ref, sin_ref,
                     q_ref, k_ref, v_ref):
    hw = N_HEADS * LANE
    scale = (MLA_NOPE + MLA_ROPE) ** -0.5
    cos, sin = cos_ref[...], sin_ref[...]
    xw = jnp.dot(x_ref[...].astype(bf16), win_ref[...], preferred_element_type=f32)
    cq = _rms(xw[:, :MLA_Q_RANK], qn_ref[...]).astype(bf16)
    ckv = _rms(xw[:, MLA_Q_RANK:MLA_Q_RANK + MLA_KV_RANK], kvn_ref[...]).astype(bf16)
    off = MLA_Q_RANK + MLA_KV_RANK
    kr = xw[:, off:off + LANE] * cos + xw[:, off + LANE:off + 2 * LANE] * sin
    qq = jnp.dot(cq, wq_ref[...], preferred_element_type=f32)
    kv = jnp.dot(ckv, wkv_ref[...], preferred_element_type=f32)
    ones_lane = lax.broadcasted_iota(jnp.int32, (1, LANE), 1) == MLA_V
    for h in range(N_HEADS):
        blk = slice(h * LANE, (h + 1) * LANE)
        swp = slice(hw + h * LANE, hw + (h + 1) * LANE)
        grp, sub = divmod(h, MLA_GROUP)
        dst = slice(sub * LANE, (sub + 1) * LANE)
        q_ref[grp, :, dst] = ((qq[:, blk] * cos + qq[:, swp] * sin) * scale).astype(bf16)
        k_ref[grp, :, dst] = (kv[:, blk] + kr).astype(bf16)
        v_ref[grp, :, dst] = jnp.where(ones_lane, 1.0, kv[:, swp]).astype(bf16)


def _mla_prep(x, w_in, q_norm, w_uq, kv_norm, w_ukv, seq, tm=512):
    t, d = x.shape
    tm = min(tm, seq)
    hw = N_HEADS * LANE
    half = MLA_ROPE // 2
    dq = MLA_NOPE + MLA_ROPE
    kr_w = w_in[:, MLA_Q_RANK + MLA_KV_RANK:]
    zeros = lambda n: jnp.zeros((d, n), f32)
    kr_a = jnp.concatenate([zeros(MLA_NOPE), kr_w, zeros(LANE - dq)], 1)
    kr_b = jnp.concatenate([zeros(MLA_NOPE), kr_w[:, half:], kr_w[:, :half], zeros(LANE - dq)], 1)
    win = jnp.concatenate([w_in[:, :MLA_Q_RANK + MLA_KV_RANK], kr_a, kr_b], 1).astype(bf16)
    wq3 = w_uq.reshape(MLA_Q_RANK, N_HEADS, dq)
    zq = lambda n: jnp.zeros((MLA_Q_RANK, N_HEADS, n), f32)
    wq_a = jnp.concatenate([wq3, zq(LANE - dq)], 2)
    wq_b = jnp.concatenate([zq(MLA_NOPE), wq3[:, :, MLA_NOPE + half:], wq3[:, :, MLA_NOPE:MLA_NOPE + half],
                            zq(LANE - dq)], 2)
    wq = jnp.concatenate([wq_a.reshape(MLA_Q_RANK, hw), wq_b.reshape(MLA_Q_RANK, hw)], 1).astype(bf16)
    wkv3 = w_ukv.reshape(MLA_KV_RANK, N_HEADS, MLA_NOPE + MLA_V)
    zk = lambda n: jnp.zeros((MLA_KV_RANK, N_HEADS, n), f32)
    wk = jnp.concatenate([wkv3[:, :, :MLA_NOPE], zk(LANE - MLA_NOPE)], 2)
    wv = jnp.concatenate([wkv3[:, :, MLA_NOPE:], zk(LANE - MLA_V)], 2)
    wkv = jnp.concatenate([wk.reshape(MLA_KV_RANK, hw), wv.reshape(MLA_KV_RANK, hw)], 1).astype(bf16)
    freq = ROPE_THETA ** (-jnp.arange(half, dtype=f32) / half)
    ang = jnp.arange(seq)[:, None].astype(f32) * freq[None, :]
    c, s_ = jnp.cos(ang), jnp.sin(ang)
    cos_t = jnp.concatenate([jnp.ones((seq, MLA_NOPE), f32), c, c, jnp.ones((seq, LANE - dq), f32)], 1)
    sin_t = jnp.concatenate([jnp.zeros((seq, MLA_NOPE), f32), -s_, s_, jnp.zeros((seq, LANE - dq), f32)], 1)

    n_pos = seq // tm
    row = lambda width: pl.BlockSpec((tm, width), lambda i: (i, 0))
    fixed = lambda a: pl.BlockSpec(a.shape, lambda i: (0, 0))
    pos = pl.BlockSpec((tm, LANE), lambda i: (i % n_pos, 0))
    qn, kvn = q_norm.reshape(1, -1), kv_norm.reshape(1, -1)
    groups, gw = N_HEADS // MLA_GROUP, MLA_GROUP * LANE
    out_sd = jax.ShapeDtypeStruct((groups, t, gw), bf16)
    out_spec = pl.BlockSpec((groups, tm, gw), lambda i: (0, i, 0))
    return pl.pallas_call(
        _mla_prep_kernel, grid=(t // tm,),
        in_specs=[row(d), fixed(win), fixed(qn), fixed(wq), fixed(kvn), fixed(wkv), pos, pos],
        out_specs=[out_spec] * 3, out_shape=[out_sd] * 3,
        compiler_params=_params("parallel"))(x, win, qn, wq, kvn, wkv, cos_t, sin_t)


def _mla_attn_kernel(q_ref, k_ref, v_ref, o_ref, *, tq, tk, heads):
    qi = pl.program_id(2)
    lanes = [slice(j * LANE, (j + 1) * LANE) for j in range(heads)]
    qs = [q_ref[0, :, ln] for ln in lanes]

    def update(carry, start, mask):
        ss = [lax.dot_general(q, k_ref[0, pl.ds(start, tk), ln], _NT, preferred_element_type=f32)
              for q, ln in zip(qs, lanes)]
        if mask is not None:
            ss = [jnp.where(mask, s, NEG) for s in ss]
        ms = [jnp.maximum(m, s.max(-1, keepdims=True)) for (m, _), s in zip(carry, ss)]
        ps = [jnp.exp(s - m).astype(bf16) for s, m in zip(ss, ms)]
        pvs = [jnp.dot(p, v_ref[0, pl.ds(start, tk), ln], preferred_element_type=f32)
               for p, ln in zip(ps, lanes)]
        return tuple((m_new, jnp.exp(m - m_new) * acc + pv)
                     for (m, acc), m_new, pv in zip(carry, ms, pvs))

    init = tuple((jnp.full((tq, 1), NEG, f32), jnp.zeros((tq, LANE), f32)) for _ in range(heads))
    sub = tq // tk
    carry = lax.fori_loop(0, qi * sub, lambda ki, c: update(c, pl.multiple_of(ki * tk, tk), None), init)
    row = lax.broadcasted_iota(jnp.int32, (tq, tk), 0)
    col = lax.broadcasted_iota(jnp.int32, (tq, tk), 1)
    for j in range(sub):
        carry = update(carry, pl.multiple_of(qi * tq + j * tk, tk), col + j * tk <= row)
    for j in range(heads):
        acc = carry[j][1]
        o_ref[0, :, lanes[j]] = (acc / acc[:, MLA_V:MLA_V + 1]).astype(o_ref.dtype)


def _mla_attention(q, k, v, batch, seq, tq=1024, tk=512):
    groups, t, gw = q.shape
    tq = min(tq, seq)
    nq = seq // tq
    qspec = pl.BlockSpec((1, tq, gw), lambda b, h, i: (h, b * nq + i, 0))
    kspec = pl.BlockSpec((1, seq, gw), lambda b, h, i: (h, b, 0))
    return pl.pallas_call(
        functools.partial(_mla_attn_kernel, tq=tq, tk=min(tk, tq), heads=MLA_GROUP),
        grid=(batch, groups, nq),
        in_specs=[qspec, kspec, kspec], out_specs=qspec,
        out_shape=jax.ShapeDtypeStruct((groups, t, gw), bf16),
        compiler_params=_params("parallel", "parallel", "arbitrary"))(q, k, v)


def _batcher_network(n):
    def merge(lo, hi, r):
        step = r * 2
        if step < hi - lo:
            yield from merge(lo, hi, step)
            yield from merge(lo + r, hi, step)
            yield from [(i, i + r) for i in range(lo + r, hi - r, step)]
        else:
            yield (lo, lo + r)

    def sort(lo, hi):
        if hi - lo >= 1:
            mid = lo + (hi - lo) // 2
            yield from sort(lo, mid)
            yield from sort(mid + 1, hi)
            yield from merge(lo, hi, 1)

    return tuple(sort(0, n - 1))


def _top16_values(s):
    n = s.shape[0] // F32_ROWS
    tiles = [s[j * F32_ROWS:(j + 1) * F32_ROWS] for j in range(n)]
    for i, j in _batcher_network(PEER_TOPK):
        if j < n:
            tiles[i], tiles[j] = jnp.maximum(tiles[i], tiles[j]), jnp.minimum(tiles[i], tiles[j])
    vals = []
    for k in range(PEER_TOPK):
        mx = tiles[0].max(axis=0, keepdims=True)
        vals.append(mx)
        hit = tiles[0] == mx
        for j in range(min(n, PEER_TOPK - 1 - k)):
            below = tiles[j + 1] if j + 1 < n else -jnp.inf
            tiles[j] = jnp.where(hit, below, tiles[j])
    return vals


def _top16_pair(s1, s2):
    v1, v2 = _top16_values(s1), _top16_values(s2)
    rank2 = jnp.zeros(s2.shape, f32)
    for k in range(PEER_TOPK):
        rank2 = jnp.where(s2 < v2[k], float(k + 1), rank2)
    return v1, v2, rank2


def _stack_rows(rows):
    n = len(rows)
    rid = lax.broadcasted_iota(jnp.int32, (n, rows[0].shape[1]), 0)
    out = jnp.broadcast_to(rows[0], (n, rows[0].shape[1]))
    for i in range(1, n):
        out = jnp.where(rid == i, rows[i], out)
    return out


def _route_kernel(x_ref, wq_ref, keys_ref, rank2_ref, crow_ref, e1z_ref, e2_ref, *, tm):
    q = jnp.dot(x_ref[...].astype(bf16), wq_ref[...], preferred_element_type=f32).astype(bf16)
    half = PEER_DKEY // 2
    scores = []
    for hh in range(ROUTE_HEADS):
        qh = q[:, hh * PEER_DKEY:(hh + 1) * PEER_DKEY]
        scores.append((lax.dot_general(keys_ref[hh, 0], qh[:, :half], _NT, preferred_element_type=f32),
                       lax.dot_general(keys_ref[hh, 1], qh[:, half:], _NT, preferred_element_type=f32)))
    rid8 = lax.broadcasted_iota(jnp.int32, (8, LANE), 0)
    for hh, c in [(hh, c) for hh in range(ROUTE_HEADS) for c in range(tm // LANE)]:
        lanes = slice(c * LANE, (c + 1) * LANE)
        s1, s2 = scores[hh][0][:, lanes], scores[hh][1][:, lanes]
        v1, v2, rank2 = _top16_pair(s1, s2)
        v2_all = _stack_rows(v2)
        cands = [v1[0] + v2_all]
        for a in range(1, 8):
            cands.append(jnp.where(rid8 < PEER_TOPK // (a + 1), v1[a] + v2_all[:8], -jnp.inf))
        cands.append(_stack_rows(v1[8:]) + v2[0])
        cand = jnp.concatenate(cands, axis=0)
        tau = _top16_values(cand)[-1]
        cmax = v1[0] + v2[0]
        z = jnp.where(cand >= tau, jnp.exp(cand - cmax), 0.0).sum(axis=0, keepdims=True)
        crow = jnp.zeros(s1.shape, f32)
        for b in range(PEER_TOPK):
            crow = crow + jnp.where(s1 + v2[b] >= tau, 1.0, 0.0)
        rank2_ref[0, hh, :, lanes] = rank2.astype(bf16)
        crow_ref[0, hh, :, lanes] = crow
        e1z_ref[0, hh, :, lanes] = jnp.exp(s1 - v1[0]) * (GELU_FOLD / z)
        e2_ref[0, hh, :, lanes] = jnp.exp(s2 - v2[0]).astype(bf16)


def _peer_route(x, w_q, keys, tm=ROUTE_TOKENS):
    t, d = x.shape
    tm = min(tm, t)
    hb = ROUTE_HEADS
    spec = pl.BlockSpec((1, hb, PEER_KEYS, tm), lambda i, h: (i, h, 0, 0))
    sd = lambda dt: jax.ShapeDtypeStruct((t // tm, PEER_HEADS, PEER_KEYS, tm), dt)
    return pl.pallas_call(
        functools.partial(_route_kernel, tm=tm), grid=(t // tm, PEER_HEADS // hb),
        in_specs=[pl.BlockSpec((tm, d), lambda i, h: (i, 0)),
                  pl.BlockSpec((d, hb * PEER_DKEY), lambda i, h: (0, h)),
                  pl.BlockSpec((hb, 2, PEER_KEYS, PEER_DKEY // 2), lambda i, h: (h, 0, 0, 0))],
        out_specs=[spec] * 4, out_shape=[sd(bf16), sd(f32), sd(f32), sd(bf16)],
        compiler_params=_params("parallel", "arbitrary"))(x, w_q, keys)


def _peer_dense_kernel(x_ref, u0_ref, u1_ref, vt0_ref, vt1_ref, rank2_in, crow_ref, e1z_ref, e2_in,
                       g_ref, b_ref, o_ref, acc_ref, xt_ref, ht0, ht1, gt0, gt1, rank2_ref, e2_ref,
                       *, rows, tb, lc, mw, alpha):
    k = pl.program_id(1)
    ec = rows * PEER_KEYS

    @pl.when(k == 0)
    def _():
        acc_ref[...] = jnp.zeros_like(acc_ref)
        ht1[...] = jnp.zeros_like(ht1)
        gt0[...] = jnp.zeros_like(gt0)
        xt_ref[...] = x_ref[...].T.astype(bf16)
        rank2_ref[...] = rank2_in[0]
        e2_ref[...] = e2_in[0]

    zero = jnp.zeros((), bf16)
    kt = PEER_KEYS // BF16_ROWS

    def up(half, ht, cw):
        lanes = slice(cw * mw, (cw + 1) * mw)
        u_blk = pltpu.bitcast((u0_ref, u1_ref)[half][0], bf16)
        ht[:, lanes] = jnp.dot(u_blk, xt_ref[:, lanes], preferred_element_type=f32)

    n_tiles = pl.num_programs(1) - 1

    def gate(ht, tile, half, gt, cw):
        base = pl.multiple_of(jnp.clip(tile, 0, n_tiles - 1) * F32_ROWS, F32_ROWS)
        for r in range(rows):
            keys = slice(r * PEER_KEYS, (r + 1) * PEER_KEYS)
            row = slice(half * rows + r, half * rows + r + 1)
            for c in range(cw * mw // lc, (cw + 1) * mw // lc):
                lanes = slice(c * lc, (c + 1) * lc)
                w = None
                for h in range(PEER_HEADS):
                    cr = crow_ref[0, h, pl.ds(base, F32_ROWS), lanes][row]
                    ez = e1z_ref[0, h, pl.ds(base, F32_ROWS), lanes][row]
                    cr = jnp.broadcast_to(cr, (BF16_ROWS, lc)).astype(bf16)[None]
                    ez = jnp.broadcast_to(ez, (BF16_ROWS, lc)).astype(bf16)[None]
                    term = jnp.where(rank2_ref[h, :, :, lanes] < cr, e2_ref[h, :, :, lanes] * ez, zero)
                    w = term if w is None else w + term
                hv = ht[keys, lanes]
                g = hv * (1.0 + lax.erf(hv))
                gt[r * kt:(r + 1) * kt, :, lanes] = g.astype(bf16).reshape(kt, BF16_ROWS, lc) * w

    def down(half, gt, cw):
        lanes = slice(cw * mw, (cw + 1) * mw)
        vt_blk = pltpu.bitcast((vt0_ref, vt1_ref)[half][0, 0], bf16)
        acc_ref[:, lanes] += jnp.dot(vt_blk, gt[:, :, lanes].reshape(ec, mw), preferred_element_type=f32)

    for cw in range(tb // mw):
        up(0, ht0, cw)
        gate(ht1, k - 1, 1, gt1, cw)
        down(0, gt0, cw)
    for cw in range(tb // mw):
        up(1, ht1, cw)
        gate(ht0, k, 0, gt0, cw)
        down(1, gt1, cw)

    @pl.when(k == pl.num_programs(1) - 1)
    def _():
        z = alpha * x_ref[...] + acc_ref[...].T
        o_ref[...] = _layer_norm(z, g_ref[...], b_ref[...])


def _peer_dense(x, u, vt, layer, routing, g, b, alpha, lc=128):
    t, d = x.shape
    e = 2 * u.shape[1]
    ec = DENSE_EC
    rows = ec // PEER_KEYS
    assert 2 * rows == F32_ROWS, "a pair of expert blocks must span one 8-row f32 tile of sub-key-1 rows"
    nk = e // (2 * ec)
    kt = PEER_KEYS // BF16_ROWS
    rank2, crow, e1z, e2 = routing
    nt, _, _, tb = crow.shape
    rank2, e2 = (a.reshape(nt, PEER_HEADS, kt, BF16_ROWS, tb) for a in (rank2, e2))
    full = pl.BlockSpec((1, PEER_HEADS, kt, BF16_ROWS, tb), lambda i, k: (i, 0, 0, 0, 0))
    rowtab = pl.BlockSpec((1, PEER_HEADS, PEER_KEYS, tb), lambda i, k: (i, 0, 0, 0))
    fixed = pl.BlockSpec((1, d), lambda i, k: (0, 0))
    xspec = pl.BlockSpec((tb, d), lambda i, k: (i, 0))
    xin = xspec
    gt_scr = pltpu.VMEM((ec // BF16_ROWS, BF16_ROWS, tb), bf16)
    u_blk = lambda half: pl.BlockSpec(
        (1, ec // 2, d), lambda i, k: (layer, 2 * jnp.minimum(k, nk - 1) + half, 0))
    vt_blk = lambda half: pl.BlockSpec(
        (1, 1, d // 2, ec), lambda i, k: (layer, 2 * jnp.maximum(k - 1, 0) + half, 0, 0))
    return pl.pallas_call(
        functools.partial(_peer_dense_kernel, rows=rows, tb=tb, lc=lc, mw=MXU_WIDTH, alpha=alpha),
        grid=(t // tb, nk + 1),
        in_specs=[xin, u_blk(0), u_blk(1), vt_blk(0), vt_blk(1),
                  full, rowtab, rowtab, full, fixed, fixed],
        out_specs=xspec, out_shape=jax.ShapeDtypeStruct((t, d), f32),
        scratch_shapes=[pltpu.VMEM((d, tb), f32), pltpu.VMEM((d, tb), bf16),
                        pltpu.VMEM((ec, tb), f32), pltpu.VMEM((ec, tb), f32), gt_scr, gt_scr,
                        pltpu.VMEM((PEER_HEADS, kt, BF16_ROWS, tb), bf16),
                        pltpu.VMEM((PEER_HEADS, kt, BF16_ROWS, tb), bf16)],
        compiler_params=_params("parallel", "arbitrary"))(
            x, u, u, vt, vt, rank2, crow, e1z, e2, g.reshape(1, d), b.reshape(1, d))


def _swa_layer(x, batch, seq, w_in, sinks, w_out, bias, g, b, alpha):
    feat = (N_HEADS + 2 * SWA_KV_HEADS) * HEAD_DIM
    d_q = N_HEADS * HEAD_DIM
    kvw = SWA_KV_HEADS * HEAD_DIM
    proj = _mm(x, w_in.astype(bf16), bf16, 512, feat).reshape(batch, seq, feat)
    sink = jnp.repeat(sinks.astype(f32), HEAD_DIM).reshape(1, d_q)
    o = _band_attention(proj, bias, 1, feat, 0, d_q // kvw, d_q // kvw + 1, kvw,
                        N_HEADS // SWA_KV_HEADS, sink)
    return _proj_ln([o], [], [1], w_out.astype(bf16), x, g, b, alpha)


def _dil_layer(x, batch, seq, w_in, w_out, biases, g, b, alpha):
    d_q = N_HEADS * HEAD_DIM
    feat = 3 * d_q
    w_in = w_in.astype(bf16)
    outs, lses, dils = [], [], []
    for gi, (window, dil) in enumerate(DIL_PATTERNS):
        proj = _mm_dil(x, w_in[:, gi * feat:(gi + 1) * feat], dil).reshape(batch, seq // dil, dil * feat)
        o, lse = _band_attention(proj, biases[gi], dil, feat, 0, 1, 2, d_q, 1)
        outs.append(o)
        lses.append(lse)
        dils.append(dil)
    return _proj_ln(outs, lses, dils, w_out.astype(bf16), x, g, b, alpha)


def _mla_layer(x, batch, seq, w_in, q_norm, w_uq, kv_norm, w_ukv, w_out, g, b, alpha):
    q, k, v = _mla_prep(x, w_in, q_norm, w_uq, kv_norm, w_ukv, seq)
    o = _mla_attention(q, k, v, batch, seq)
    d = w_out.shape[1]
    w3 = w_out.reshape(N_HEADS, MLA_V, d)
    w_pad = jnp.concatenate([w3, jnp.zeros((N_HEADS, LANE - MLA_V, d), f32)], 1)
    return _proj_ln([o], [], [1], w_pad.reshape(N_HEADS * LANE, d).astype(bf16), x, g, b, alpha)


def _pack_u_kernel(u_ref, o_ref):
    o_ref[0] = pltpu.bitcast((u_ref[0] * GELU_FOLD).astype(bf16), jnp.uint32)


def _pack_vt_kernel(v_ref, o_ref):
    o_ref[0, 0] = pltpu.bitcast(v_ref[0].T.astype(bf16), jnp.uint32)


def _pack_expert_weights(u, v):
    depth, e, d = u.shape
    ec = DENSE_EC
    src = pl.BlockSpec((1, ec, d), lambda l, j: (l, j, 0))
    up = pl.pallas_call(
        _pack_u_kernel, grid=(depth, e // ec), in_specs=[src],
        out_specs=pl.BlockSpec((1, ec // 2, d), lambda l, j: (l, j, 0)),
        out_shape=jax.ShapeDtypeStruct((depth, e // 2, d), jnp.uint32),
        compiler_params=_params("parallel", "parallel"))(u)
    vtp = pl.pallas_call(
        _pack_vt_kernel, grid=(depth, e // ec), in_specs=[src],
        out_specs=pl.BlockSpec((1, 1, d // 2, ec), lambda l, j: (l, j, 0, 0)),
        out_shape=jax.ShapeDtypeStruct((depth, e // ec, d // 2, ec), jnp.uint32),
        compiler_params=_params("parallel", "parallel"))(v)
    return up, vtp


def _peer_layer(x, w_q, keys, u_packed, vt_packed, layer, g, b, alpha):
    routing = _peer_route(x, w_q.astype(bf16), keys.astype(bf16))
    return _peer_dense(x, u_packed, vt_packed, layer, routing, g, b, alpha)


def kernel(x, rel_bias, ln_g, ln_b, swa_w_in, swa_sinks, swa_w_out, dil_w_in, dil_w_out,
           mla_w_in, mla_q_norm, mla_w_uq, mla_kv_norm, mla_w_ukv, mla_w_out,
           peer_w_q, peer_keys, peer_u, peer_v):
    batch, seq, d = x.shape
    depth = ln_g.shape[0]
    alpha = (2 * depth) ** 0.25
    assert seq % (DIL_PATTERNS[-1][1] * BLOCK) == 0, "sequence must be a whole number of dilation segments"
    swa_bias = _band_bias(rel_bias, SWA_WINDOW - 1, 1)
    dil_bias = [_band_bias(rel_bias, window // dil, dil) for window, dil in DIL_PATTERNS]
    u_packed, vt_packed = _pack_expert_weights(peer_u, peer_v)
    h = x.reshape(batch * seq, d)
    for i in range(depth):
        kind, j = i % 3, i // 3
        if kind == 0:
            h = _swa_layer(h, batch, seq, swa_w_in[j], swa_sinks[j], swa_w_out[j], swa_bias,
                           ln_g[i, 0], ln_b[i, 0], alpha)
        elif kind == 1:
            h = _dil_layer(h, batch, seq, dil_w_in[j], dil_w_out[j], dil_bias,
                           ln_g[i, 0], ln_b[i, 0], alpha)
        else:
            h = _mla_layer(h, batch, seq, mla_w_in[j], mla_q_norm[j], mla_w_uq[j], mla_kv_norm[j],
                           mla_w_ukv[j], mla_w_out[j], ln_g[i, 0], ln_b[i, 0], alpha)
        h = _peer_layer(h, peer_w_q[i], peer_keys[i], u_packed, vt_packed, i, ln_g[i, 1], ln_b[i, 1], alpha)
    return h.reshape(batch, seq, d)
```

```python
import functools
import math

import jax
import jax.numpy as jnp
from jax import lax
from jax.experimental import pallas as pl
from jax.experimental.pallas import tpu as pltpu

f32 = jnp.float32
bf16 = jnp.bfloat16

N_HEADS = 16
HEAD_DIM = 64
BLOCK = 128
SWA_KV_HEADS = 2
SWA_WINDOW = 128
DIL_PATTERNS = ((128, 1), (512, 4), (2048, 16))
MLA_Q_RANK = 256
MLA_KV_RANK = 128
MLA_NOPE = 64
MLA_ROPE = 32
MLA_V = 64
ROPE_THETA = 10000.0
REL_BUCKETS = 32
REL_MAX_DIST = 2048
PEER_HEADS = 8
PEER_KEYS = 128
PEER_DKEY = 256
PEER_TOPK = 16
LN_EPS = 1e-5
RMS_EPS = 1e-6
NEG = -1e30

LANE = 128
BF16_ROWS = 16
F32_ROWS = 8
MXU_WIDTH = 256
MLA_GROUP = 2
ROUTE_HEADS = 4
GELU_FOLD = 0.5 ** 0.5
ROUTE_TOKENS = 512
DENSE_EC = 4 * PEER_KEYS
VMEM_LIMIT = 56 * 1024 * 1024

_NT = (((1,), (1,)), ((), ()))


def _params(*sem):
    return pltpu.CompilerParams(dimension_semantics=sem, vmem_limit_bytes=VMEM_LIMIT)


def _mm_kernel(x_ref, w_ref, o_ref):
    o_ref[...] = jnp.dot(x_ref[...].astype(bf16), w_ref[...],
                         preferred_element_type=f32).astype(o_ref.dtype)


def _mm(x, w, out_dtype, tm, tn):
    m, k = x.shape
    n = w.shape[1]
    tm, tn = min(tm, m), min(tn, n)
    return pl.pallas_call(
        _mm_kernel, grid=(m // tm, n // tn),
        in_specs=[pl.BlockSpec((tm, k), lambda i, j: (i, 0)),
                  pl.BlockSpec((k, tn), lambda i, j: (0, j))],
        out_specs=pl.BlockSpec((tm, tn), lambda i, j: (i, j)),
        out_shape=jax.ShapeDtypeStruct((m, n), out_dtype),
        compiler_params=_params("parallel", "parallel"))(x, w)


def _rel_bucket(dist):
    max_exact = REL_BUCKETS // 2
    n = jnp.maximum(dist, 0)
    nf = jnp.maximum(n, 1).astype(f32)
    large = max_exact + (jnp.log(nf / max_exact) / math.log(REL_MAX_DIST / max_exact)
                         * (REL_BUCKETS - max_exact)).astype(jnp.int32)
    large = jnp.minimum(large, REL_BUCKETS - 1)
    return jnp.where(n < max_exact, n, large)


def _bias_kernel(rel_ref, bucket_ref, valid_ref, o_ref):
    bucket = bucket_ref[...]
    for h in range(N_HEADS):
        acc = jnp.zeros(bucket.shape, f32)
        for b in range(REL_BUCKETS):
            acc = jnp.where(bucket == b, rel_ref[b, h], acc)
        for variant in range(2):
            o_ref[variant, h] = jnp.where(valid_ref[variant] != 0, acc, NEG)


def _band_bias(rel_bias, max_dist, dilation):
    qi = jnp.arange(BLOCK)[:, None]
    kj = jnp.arange(2 * BLOCK)[None, :]
    dist = BLOCK + qi - kj
    bucket = _rel_bucket(dist * dilation).astype(jnp.int32)
    valid = (dist >= 0) & (dist <= max_dist)
    valid = jnp.stack([valid & (kj >= BLOCK), valid]).astype(jnp.int32)
    return pl.pallas_call(
        _bias_kernel,
        in_specs=[pl.BlockSpec(memory_space=pltpu.SMEM),
                  pl.BlockSpec(memory_space=pltpu.VMEM),
                  pl.BlockSpec(memory_space=pltpu.VMEM)],
        out_specs=pl.BlockSpec(memory_space=pltpu.VMEM),
        out_shape=jax.ShapeDtypeStruct((2, N_HEADS, BLOCK, 2 * BLOCK), f32))(rel_bias, bucket, valid)


def _band_kernel(q_ref, kp_ref, kc_ref, vp_ref, vc_ref, bias_ref, *rest, group, with_sink):
    if with_sink:
        sink_ref, o_ref = rest
    else:
        o_ref, lse_ref = rest
    scale = HEAD_DIM ** -0.5
    pair = LANE // HEAD_DIM
    lo = lax.broadcasted_iota(jnp.int32, (1, LANE), 1) < HEAD_DIM
    zero = jnp.zeros((), bf16)

    kv_blocks = {}

    def kv_block(blk, swapped):
        if (blk, swapped) not in kv_blocks:
            cols = slice(blk * LANE, (blk + 1) * LANE)
            k = jnp.concatenate([kp_ref[0, :, cols], kc_ref[0, :, cols]], axis=0)
            v = jnp.concatenate([vp_ref[0, :, cols], vc_ref[0, :, cols]], axis=0)
            if swapped:
                k = jnp.concatenate([k[:, HEAD_DIM:], k[:, :HEAD_DIM]], axis=1)
                v = jnp.concatenate([v[:, HEAD_DIM:], v[:, :HEAD_DIM]], axis=1)
            kv_blocks[(blk, swapped)] = (k, v)
        return kv_blocks[(blk, swapped)]

    heads = range(N_HEADS)
    qp = [q_ref[0, :, p * LANE:(p + 1) * LANE] for p in range(N_HEADS // pair)]
    qm = [jnp.where(lo if h % pair == 0 else ~lo, qp[h // pair], zero) for h in heads]
    kvs = [kv_block((h // group) // pair, (h // group) % pair != h % pair) for h in heads]
    ss = [lax.dot_general(q, kv[0], _NT, preferred_element_type=f32) * scale + bias_ref[0, h]
          for h, q, kv in zip(heads, qm, kvs)]
    ms = [s.max(-1, keepdims=True) for s in ss]
    ps = [jnp.exp(s - m) for s, m in zip(ss, ms)]
    ls = [p.sum(-1, keepdims=True) for p in ps]
    os_ = [jnp.dot(p.astype(bf16), kv[1], preferred_element_type=f32) / l for p, kv, l in zip(ps, kvs, ls)]
    lses = [jnp.broadcast_to(m + jnp.log(l), (BLOCK, LANE)) for m, l in zip(ms, ls)]
    for p in range(N_HEADS // pair):
        cols = slice(p * LANE, (p + 1) * LANE)
        o = jnp.where(lo, os_[pair * p], os_[pair * p + 1])
        lse = jnp.where(lo, lses[pair * p], lses[pair * p + 1])
        if with_sink:
            o = o * jax.nn.sigmoid(lse - sink_ref[:, cols])
        else:
            lse_ref[0, :, cols] = lse
        o_ref[0, :, cols] = o.astype(o_ref.dtype)


def _band_attention(view, bias, dil, feat, q_col, k_col, v_col, kv_width, group, sink=None):
    b, m_len, _ = view.shape
    d_q = N_HEADS * HEAD_DIM
    nb = m_len // BLOCK
    qpr, kpr = feat // d_q, feat // kv_width

    def cur(col, per_row):
        return lambda bi, r, n: (bi, n, r * per_row + col)

    def prev(col, per_row):
        return lambda bi, r, n: (bi, jnp.maximum(n - 1, 0), r * per_row + col)

    in_specs = [pl.BlockSpec((1, BLOCK, d_q), cur(q_col, qpr)),
                pl.BlockSpec((1, BLOCK, kv_width), prev(k_col, kpr)),
                pl.BlockSpec((1, BLOCK, kv_width), cur(k_col, kpr)),
                pl.BlockSpec((1, BLOCK, kv_width), prev(v_col, kpr)),
                pl.BlockSpec((1, BLOCK, kv_width), cur(v_col, kpr)),
                pl.BlockSpec((1, N_HEADS, BLOCK, 2 * BLOCK),
                             lambda bi, r, n: (jnp.minimum(n, 1), 0, 0, 0))]
    args = [view, view, view, view, view, bias]
    o_spec = pl.BlockSpec((1, BLOCK, d_q), lambda bi, r, n: (bi, n, r))
    o_shape = jax.ShapeDtypeStruct((b, m_len, dil * d_q), bf16)
    if sink is not None:
        in_specs.append(pl.BlockSpec((1, d_q), lambda bi, r, n: (0, 0)))
        args.append(sink)
        out_specs, out_shape = o_spec, o_shape
    else:
        out_specs = [o_spec, o_spec]
        out_shape = [o_shape, jax.ShapeDtypeStruct((b, m_len, dil * d_q), f32)]
    out = pl.pallas_call(
        functools.partial(_band_kernel, group=group, with_sink=sink is not None),
        grid=(b, dil, nb), in_specs=in_specs, out_specs=out_specs, out_shape=out_shape,
        compiler_params=_params("parallel", "parallel", "arbitrary"))(*args)
    if sink is not None:
        return out.reshape(b * m_len, dil * d_q)
    return out[0].reshape(b * m_len, dil * d_q), out[1].reshape(b * m_len, dil * d_q)


def _layer_norm(z, g, b):
    mu = z.mean(-1, keepdims=True)
    zc = z - mu
    var = jnp.square(zc).mean(-1, keepdims=True)
    return zc * lax.rsqrt(var + LN_EPS) * g + b


def _proj_ln_kernel(*refs, dils, alpha):
    n_pat = len(dils)
    o_refs = refs[:n_pat]
    lse_refs = refs[n_pat:2 * n_pat] if n_pat > 1 else ()
    n_in = len(o_refs) + len(lse_refs)
    w_ref, x_ref, g_ref, b_ref, out_ref = refs[n_in:n_in + 5]
    scratch = refs[n_in + 5:]
    if n_pat == 1 and len(o_refs[0].shape) == 3:
        groups, _, width = o_refs[0].shape
        y = sum(jnp.dot(o_refs[0][gi], w_ref[gi * width:(gi + 1) * width, :], preferred_element_type=f32)
                for gi in range(groups))
        out_ref[...] = _layer_norm(alpha * x_ref[...] + y, g_ref[...], b_ref[...])
        return
    if n_pat == 1:
        o = o_refs[0][...]
    else:
        kw = w_ref.shape[0]
        mix_ref = scratch[-1]

        def natural(ref, dil, scr, c):
            if dil == 1:
                return ref[:, c * LANE:(c + 1) * LANE].astype(f32)
            rows = ref.shape[0]
            for r in range(dil):
                scr[c, pl.ds(r, rows, stride=dil), :] = (
                    ref[:, r * kw + c * LANE:r * kw + (c + 1) * LANE].astype(f32))
            return scr[c]

        for c in range(kw // LANE):
            lses = [natural(r, dl, scratch[2 * i], c) for i, (r, dl) in enumerate(zip(lse_refs, dils))]
            mx = functools.reduce(jnp.maximum, lses)
            es = [jnp.exp(l - mx) for l in lses]
            num = sum(e * natural(r, dl, scratch[2 * i + 1], c)
                      for i, (e, r, dl) in enumerate(zip(es, o_refs, dils)))
            mix_ref[:, c * LANE:(c + 1) * LANE] = (num / sum(es)).astype(bf16)
        o = mix_ref[...]
    y = jnp.dot(o, w_ref[...], preferred_element_type=f32)
    out_ref[...] = _layer_norm(alpha * x_ref[...] + y, g_ref[...], b_ref[...])


def _proj_ln(os_, lses, dils, w_out, x, g, b, alpha, tm=512):
    t, d = x.shape
    tm = min(tm, t)
    kw = w_out.shape[0]
    row = lambda width: pl.BlockSpec((tm, width), lambda i: (i, 0))
    dilated = lambda dl: pl.BlockSpec((tm // dl, dl * kw), lambda i: (i, 0))
    fixed = lambda shape: pl.BlockSpec(shape, lambda i: (0, 0))
    if os_[0].ndim == 3:
        o_specs = [pl.BlockSpec((os_[0].shape[0], tm, os_[0].shape[2]), lambda i: (0, i, 0))]
    else:
        o_specs = [dilated(dl) for dl in dils]
    in_specs = (o_specs + [dilated(dl) for dl in dils[:len(lses)]]
                + [fixed((kw, d)), row(d), fixed((1, d)), fixed((1, d))])
    scratch = [pltpu.VMEM((kw // LANE, tm, LANE), f32)] * (2 * len(lses))
    if lses:
        scratch.append(pltpu.VMEM((tm, kw), bf16))
    return pl.pallas_call(
        functools.partial(_proj_ln_kernel, dils=tuple(dils), alpha=alpha),
        grid=(t // tm,), in_specs=in_specs, out_specs=row(d),
        out_shape=jax.ShapeDtypeStruct((t, d), f32), scratch_shapes=scratch,
        compiler_params=_params("parallel"))(*os_, *lses, w_out, x, g.reshape(1, d), b.reshape(1, d))


def _mm_dil_kernel(x_ref, w_ref, o_ref, res_ref, *, dil):
    n = w_ref.shape[1]
    res = jnp.dot(x_ref[...].astype(bf16), w_ref[...], preferred_element_type=f32)
    if dil == 1:
        o_ref[...] = res.astype(o_ref.dtype)
        return
    rows = o_ref.shape[0]
    for c in range(n // LANE):
        res_ref[c] = res[:, c * LANE:(c + 1) * LANE]
        for r in range(dil):
            o_ref[:, r * n + c * LANE:r * n + (c + 1) * LANE] = (
                res_ref[c, pl.ds(r, rows, stride=dil), :].astype(o_ref.dtype))


def _mm_dil(x, w, dil, tm=512):
    t, k = x.shape
    n = w.shape[1]
    tm = min(tm, t)
    return pl.pallas_call(
        functools.partial(_mm_dil_kernel, dil=dil), grid=(t // tm,),
        in_specs=[pl.BlockSpec((tm, k), lambda i: (i, 0)), pl.BlockSpec((k, n), lambda i: (0, 0))],
        out_specs=pl.BlockSpec((tm // dil, dil * n), lambda i: (i, 0)),
        out_shape=jax.ShapeDtypeStruct((t // dil, dil * n), bf16),
        scratch_shapes=[pltpu.VMEM((n // LANE, tm, LANE), f32)],
        compiler_params=_params("parallel"))(x, w)


def _rms(c, g):
    return c * lax.rsqrt(jnp.square(c).mean(-1, keepdims=True) + RMS_EPS) * g


def _mla_prep_kernel(x_ref, win_ref, qn_ref, wq_ref, kvn_ref, wkv_ref, cos_ref, sin_ref,
                     q_ref, k_ref, v_ref):
    hw = N_HEADS * LANE
    scale = (MLA_NOPE + MLA_ROPE) ** -0.5
    cos, sin = cos_ref[...], sin_ref[...]
    xw = jnp.dot(x_ref[...].astype(bf16), win_ref[...], preferred_element_type=f32)
    cq = _rms(xw[:, :MLA_Q_RANK], qn_ref[...]).astype(bf16)
    ckv = _rms(xw[:, MLA_Q_RANK:MLA_Q_RANK + MLA_KV_RANK], kvn_ref[...]).astype(bf16)
    off = MLA_Q_RANK + MLA_KV_RANK
    kr = xw[:, off:off + LANE] * cos + xw[:, off + LANE:off + 2 * LANE] * sin
    qq = jnp.dot(cq, wq_ref[...], preferred_element_type=f32)
    kv = jnp.dot(ckv, wkv_ref[...], preferred_element_type=f32)
    ones_lane = lax.broadcasted_iota(jnp.int32, (1, LANE), 1) == MLA_V
    for h in range(N_HEADS):
        blk = slice(h * LANE, (h + 1) * LANE)
        swp = slice(hw + h * LANE, hw + (h + 1) * LANE)
        grp, sub = divmod(h, MLA_GROUP)
        dst = slice(sub * LANE, (sub + 1) * LANE)
        q_ref[grp, :, dst] = ((qq[:, blk] * cos + qq[:, swp] * sin) * scale).astype(bf16)
        k_ref[grp, :, dst] = (kv[:, blk] + kr).astype(bf16)
        v_ref[grp, :, dst] = jnp.where(ones_lane, 1.0, kv[:, swp]).astype(bf16)


def _mla_prep(x, w_in, q_norm, w_uq, kv_norm, w_ukv, seq, tm=512):
    t, d = x.shape
    tm = min(tm, seq)
    hw = N_HEADS * LANE
    half = MLA_ROPE // 2
    dq = MLA_NOPE + MLA_ROPE
    kr_w = w_in[:, MLA_Q_RANK + MLA_KV_RANK:]
    zeros = lambda n: jnp.zeros((d, n), f32)
    kr_a = jnp.concatenate([zeros(MLA_NOPE), kr_w, zeros(LANE - dq)], 1)
    kr_b = jnp.concatenate([zeros(MLA_NOPE), kr_w[:, half:], kr_w[:, :half], zeros(LANE - dq)], 1)
    win = jnp.concatenate([w_in[:, :MLA_Q_RANK + MLA_KV_RANK], kr_a, kr_b], 1).astype(bf16)
    wq3 = w_uq.reshape(MLA_Q_RANK, N_HEADS, dq)
    zq = lambda n: jnp.zeros((MLA_Q_RANK, N_HEADS, n), f32)
    wq_a = jnp.concatenate([wq3, zq(LANE - dq)], 2)
    wq_b = jnp.concatenate([zq(MLA_NOPE), wq3[:, :, MLA_NOPE + half:], wq3[:, :, MLA_NOPE:MLA_NOPE + half],
                            zq(LANE - dq)], 2)
    wq = jnp.concatenate([wq_a.reshape(MLA_Q_RANK, hw), wq_b.reshape(MLA_Q_RANK, hw)], 1).astype(bf16)
    wkv3 = w_ukv.reshape(MLA_KV_RANK, N_HEADS, MLA_NOPE + MLA_V)
    zk = lambda n: jnp.zeros((MLA_KV_RANK, N_HEADS, n), f32)
    wk = jnp.concatenate([wkv3[:, :, :MLA_NOPE], zk(LANE - MLA_NOPE)], 2)
    wv = jnp.concatenate([wkv3[:, :, MLA_NOPE:], zk(LANE - MLA_V)], 2)
    wkv = jnp.concatenate([wk.reshape(MLA_KV_RANK, hw), wv.reshape(MLA_KV_RANK, hw)], 1).astype(bf16)
    freq = ROPE_THETA ** (-jnp.arange(half, dtype=f32) / half)
    ang = jnp.arange(seq)[:, None].astype(f32) * freq[None, :]
    c, s_ = jnp.cos(ang), jnp.sin(ang)
    cos_t = jnp.concatenate([jnp.ones((seq, MLA_NOPE), f32), c, c, jnp.ones((seq, LANE - dq), f32)], 1)
    sin_t = jnp.concatenate([jnp.zeros((seq, MLA_NOPE), f32), -s_, s_, jnp.zeros((seq, LANE - dq), f32)], 1)

    n_pos = seq // tm
    row = lambda width: pl.BlockSpec((tm, width), lambda i: (i, 0))
    fixed = lambda a: pl.BlockSpec(a.shape, lambda i: (0, 0))
    pos = pl.BlockSpec((tm, LANE), lambda i: (i % n_pos, 0))
    qn, kvn = q_norm.reshape(1, -1), kv_norm.reshape(1, -1)
    groups, gw = N_HEADS // MLA_GROUP, MLA_GROUP * LANE
    out_sd = jax.ShapeDtypeStruct((groups, t, gw), bf16)
    out_spec = pl.BlockSpec((groups, tm, gw), lambda i: (0, i, 0))
    return pl.pallas_call(
        _mla_prep_kernel, grid=(t // tm,),
        in_specs=[row(d), fixed(win), fixed(qn), fixed(wq), fixed(kvn), fixed(wkv), pos, pos],
        out_specs=[out_spec] * 3, out_shape=[out_sd] * 3,
        compiler_params=_params("parallel"))(x, win, qn, wq, kvn, wkv, cos_t, sin_t)


def _mla_attn_kernel(q_ref, k_ref, v_ref, o_ref, *, tq, tk, heads):
    qi = pl.program_id(2)
    lanes = [slice(j * LANE, (j + 1) * LANE) for j in range(heads)]
    qs = [q_ref[0, :, ln] for ln in lanes]

    def update(carry, start, mask, r0=0):
        ss = [lax.dot_general(q[r0:], k_ref[0, pl.ds(start, tk), ln], _NT, preferred_element_type=f32)
              for q, ln in zip(qs, lanes)]
        if mask is not None:
            ss = [jnp.where(mask, s, NEG) for s in ss]
        ms = [jnp.maximum(m, s.max(-1, keepdims=True)) for (m, _), s in zip(carry, ss)]
        ps = [jnp.exp(s - m).astype(bf16) for s, m in zip(ss, ms)]
        pvs = [jnp.dot(p, v_ref[0, pl.ds(start, tk), ln], preferred_element_type=f32)
               for p, ln in zip(ps, lanes)]
        return tuple((m_new, jnp.exp(m - m_new) * acc + pv)
                     for (m, acc), m_new, pv in zip(carry, ms, pvs))

    init = tuple((jnp.full((tq, 1), NEG, f32), jnp.zeros((tq, LANE), f32)) for _ in range(heads))
    sub = tq // tk
    carry = lax.fori_loop(0, qi * sub, lambda ki, c: update(c, pl.multiple_of(ki * tk, tk), None), init)
    for j in range(sub):
        r0 = j * tk
        row = lax.broadcasted_iota(jnp.int32, (tq - r0, tk), 0)
        col = lax.broadcasted_iota(jnp.int32, (tq - r0, tk), 1)
        part = update(tuple((m[r0:], acc[r0:]) for m, acc in carry),
                      pl.multiple_of(qi * tq + r0, tk), col <= row, r0)
        carry = tuple((jnp.concatenate([m[:r0], pm], axis=0), jnp.concatenate([acc[:r0], pa], axis=0))
                      for (m, acc), (pm, pa) in zip(carry, part)) if r0 else part
    for j in range(heads):
        acc = carry[j][1]
        o_ref[0, :, lanes[j]] = (acc / acc[:, MLA_V:MLA_V + 1]).astype(o_ref.dtype)


def _mla_attention(q, k, v, batch, seq, tq=1024, tk=512):
    groups, t, gw = q.shape
    tq = min(tq, seq)
    nq = seq // tq
    qspec = pl.BlockSpec((1, tq, gw), lambda b, h, i: (h, b * nq + i, 0))
    kspec = pl.BlockSpec((1, seq, gw), lambda b, h, i: (h, b, 0))
    return pl.pallas_call(
        functools.partial(_mla_attn_kernel, tq=tq, tk=min(tk, tq), heads=MLA_GROUP),
        grid=(batch, groups, nq),
        in_specs=[qspec, kspec, kspec], out_specs=qspec,
        out_shape=jax.ShapeDtypeStruct((groups, t, gw), bf16),
        compiler_params=_params("parallel", "parallel", "arbitrary"))(q, k, v)


def _batcher_network(n):
    def merge(lo, hi, r):
        step = r * 2
        if step < hi - lo:
            yield from merge(lo, hi, step)
            yield from merge(lo + r, hi, step)
            yield from [(i, i + r) for i in range(lo + r, hi - r, step)]
        else:
            yield (lo, lo + r)

    def sort(lo, hi):
        if hi - lo >= 1:
            mid = lo + (hi - lo) // 2
            yield from sort(lo, mid)
            yield from sort(mid + 1, hi)
            yield from merge(lo, hi, 1)

    return tuple(sort(0, n - 1))


def _top16_values(s):
    n = s.shape[0] // F32_ROWS
    tiles = [s[j * F32_ROWS:(j + 1) * F32_ROWS] for j in range(n)]
    for i, j in _batcher_network(PEER_TOPK):
        if j < n:
            tiles[i], tiles[j] = jnp.maximum(tiles[i], tiles[j]), jnp.minimum(tiles[i], tiles[j])
    vals = []
    for k in range(PEER_TOPK):
        mx = tiles[0].max(axis=0, keepdims=True)
        vals.append(mx)
        hit = tiles[0] == mx
        for j in range(min(n, PEER_TOPK - 1 - k)):
            below = tiles[j + 1] if j + 1 < n else -jnp.inf
            tiles[j] = jnp.where(hit, below, tiles[j])
    return vals


def _top16_pair(s1, s2):
    v1, v2 = _top16_values(s1), _top16_values(s2)
    rank2 = jnp.zeros(s2.shape, f32)
    for k in range(PEER_TOPK):
        rank2 = jnp.where(s2 < v2[k], float(k + 1), rank2)
    return v1, v2, rank2


def _stack_rows(rows):
    n = len(rows)
    rid = lax.broadcasted_iota(jnp.int32, (n, rows[0].shape[1]), 0)
    out = jnp.broadcast_to(rows[0], (n, rows[0].shape[1]))
    for i in range(1, n):
        out = jnp.where(rid == i, rows[i], out)
    return out


def _route_kernel(x_ref, wq_ref, keys_ref, rank2_ref, crow_ref, e1z_ref, e2_ref, *, tm):
    q = jnp.dot(x_ref[...].astype(bf16), wq_ref[...], preferred_element_type=f32).astype(bf16)
    half = PEER_DKEY // 2
    scores = []
    for hh in range(ROUTE_HEADS):
        qh = q[:, hh * PEER_DKEY:(hh + 1) * PEER_DKEY]
        scores.append((lax.dot_general(keys_ref[hh, 0], qh[:, :half], _NT, preferred_element_type=f32),
                       lax.dot_general(keys_ref[hh, 1], qh[:, half:], _NT, preferred_element_type=f32)))
    rid8 = lax.broadcasted_iota(jnp.int32, (8, LANE), 0)
    for hh, c in [(hh, c) for hh in range(ROUTE_HEADS) for c in range(tm // LANE)]:
        lanes = slice(c * LANE, (c + 1) * LANE)
        s1, s2 = scores[hh][0][:, lanes], scores[hh][1][:, lanes]
        v1, v2, rank2 = _top16_pair(s1, s2)
        v2_all = _stack_rows(v2)
        cands = [v1[0] + v2_all]
        for a in range(1, 8):
            cands.append(jnp.where(rid8 < PEER_TOPK // (a + 1), v1[a] + v2_all[:8], -jnp.inf))
        cands.append(_stack_rows(v1[8:]) + v2[0])
        cand = jnp.concatenate(cands, axis=0)
        tau = _top16_values(cand)[-1]
        cmax = v1[0] + v2[0]
        z = jnp.where(cand >= tau, jnp.exp(cand - cmax), 0.0).sum(axis=0, keepdims=True)
        crow = jnp.zeros(s1.shape, f32)
        for b in range(PEER_TOPK):
            crow = crow + jnp.where(s1 + v2[b] >= tau, 1.0, 0.0)
        rank2_ref[0, hh, :, lanes] = rank2.astype(bf16)
        crow_ref[0, hh, :, lanes] = crow
        e1z_ref[0, hh, :, lanes] = jnp.exp(s1 - v1[0]) * (GELU_FOLD / z)
        e2_ref[0, hh, :, lanes] = jnp.exp(s2 - v2[0]).astype(bf16)


def _peer_route(x, w_q, keys, tm=ROUTE_TOKENS):
    t, d = x.shape
    tm = min(tm, t)
    hb = ROUTE_HEADS
    spec = pl.BlockSpec((1, hb, PEER_KEYS, tm), lambda i, h: (i, h, 0, 0))
    sd = lambda dt: jax.ShapeDtypeStruct((t // tm, PEER_HEADS, PEER_KEYS, tm), dt)
    return pl.pallas_call(
        functools.partial(_route_kernel, tm=tm), grid=(t // tm, PEER_HEADS // hb),
        in_specs=[pl.BlockSpec((tm, d), lambda i, h: (i, 0)),
                  pl.BlockSpec((d, hb * PEER_DKEY), lambda i, h: (0, h)),
                  pl.BlockSpec((hb, 2, PEER_KEYS, PEER_DKEY // 2), lambda i, h: (h, 0, 0, 0))],
        out_specs=[spec] * 4, out_shape=[sd(bf16), sd(f32), sd(f32), sd(bf16)],
        compiler_params=_params("parallel", "arbitrary"))(x, w_q, keys)


def _peer_dense_kernel(x_ref, u0_ref, u1_ref, vt0_ref, vt1_ref, rank2_in, crow_ref, e1z_ref, e2_in,
                       g_ref, b_ref, o_ref, acc_ref, xt_ref, ht0, ht1, gt0, gt1, rank2_ref, e2_ref,
                       *, rows, tb, lc, mw, alpha):
    k = pl.program_id(1)
    last = pl.num_programs(1) - 1
    ec = rows * PEER_KEYS
    zero = jnp.zeros((), bf16)
    kt = PEER_KEYS // BF16_ROWS

    def up(half, ht, cw):
        lanes = slice(cw * mw, (cw + 1) * mw)
        u_blk = pltpu.bitcast((u0_ref, u1_ref)[half][0], bf16)
        ht[:, lanes] = jnp.dot(u_blk, xt_ref[:, lanes], preferred_element_type=f32)

    n_tiles = last

    def gate(ht, tile, half, gt, cw):
        base = pl.multiple_of(jnp.clip(tile, 0, n_tiles - 1) * F32_ROWS, F32_ROWS)
        for r in range(rows):
            keys = slice(r * PEER_KEYS, (r + 1) * PEER_KEYS)
            row = slice(half * rows + r, half * rows + r + 1)
            for c in range(cw * mw // lc, (cw + 1) * mw // lc):
                lanes = slice(c * lc, (c + 1) * lc)
                w = None
                for h in range(PEER_HEADS):
                    cr = crow_ref[0, h, pl.ds(base, F32_ROWS), lanes][row]
                    ez = e1z_ref[0, h, pl.ds(base, F32_ROWS), lanes][row]
                    cr = jnp.broadcast_to(cr, (BF16_ROWS, lc)).astype(bf16)[None]
                    ez = jnp.broadcast_to(ez, (BF16_ROWS, lc)).astype(bf16)[None]
                    term = jnp.where(rank2_ref[h, :, :, lanes] < cr, e2_ref[h, :, :, lanes] * ez, zero)
                    w = term if w is None else w + term
                hv = ht[keys, lanes]
                g = hv * (1.0 + lax.erf(hv))
                gt[r * kt:(r + 1) * kt, :, lanes] = g.astype(bf16).reshape(kt, BF16_ROWS, lc) * w

    def down(half, gt, cw):
        lanes = slice(cw * mw, (cw + 1) * mw)
        vt_blk = pltpu.bitcast((vt0_ref, vt1_ref)[half][0, 0], bf16)
        acc_ref[:, lanes] += jnp.dot(vt_blk, gt[:, :, lanes].reshape(ec, mw), preferred_element_type=f32)

    chunks = range(tb // mw)

    @pl.when(k == 0)
    def _():
        acc_ref[...] = jnp.zeros_like(acc_ref)
        xt_ref[...] = x_ref[...].T.astype(bf16)
        rank2_ref[...] = rank2_in[0]
        e2_ref[...] = e2_in[0]
        for cw in chunks:
            up(0, ht0, cw)
            up(1, ht1, cw)
        for cw in chunks:
            gate(ht0, k, 0, gt0, cw)

    @pl.when(jnp.logical_and(k > 0, k < last))
    def _():
        for cw in chunks:
            up(0, ht0, cw)
            gate(ht1, k - 1, 1, gt1, cw)
            down(0, gt0, cw)
        for cw in chunks:
            up(1, ht1, cw)
            gate(ht0, k, 0, gt0, cw)
            down(1, gt1, cw)

    @pl.when(k == last)
    def _():
        for cw in chunks:
            gate(ht1, k - 1, 1, gt1, cw)
            down(0, gt0, cw)
        for cw in chunks:
            down(1, gt1, cw)
        z = alpha * x_ref[...] + acc_ref[...].T
        o_ref[...] = _layer_norm(z, g_ref[...], b_ref[...])


def _peer_dense(x, u, vt, layer, routing, g, b, alpha, lc=128):
    t, d = x.shape
    e = 2 * u.shape[1]
    ec = DENSE_EC
    rows = ec // PEER_KEYS
    assert 2 * rows == F32_ROWS, "a pair of expert blocks must span one 8-row f32 tile of sub-key-1 rows"
    nk = e // (2 * ec)
    kt = PEER_KEYS // BF16_ROWS
    rank2, crow, e1z, e2 = routing
    nt, _, _, tb = crow.shape
    rank2, e2 = (a.reshape(nt, PEER_HEADS, kt, BF16_ROWS, tb) for a in (rank2, e2))
    full = pl.BlockSpec((1, PEER_HEADS, kt, BF16_ROWS, tb), lambda i, k: (i, 0, 0, 0, 0))
    rowtab = pl.BlockSpec((1, PEER_HEADS, PEER_KEYS, tb), lambda i, k: (i, 0, 0, 0))
    fixed = pl.BlockSpec((1, d), lambda i, k: (0, 0))
    xspec = pl.BlockSpec((tb, d), lambda i, k: (i, 0))
    xin = xspec
    gt_scr = pltpu.VMEM((ec // BF16_ROWS, BF16_ROWS, tb), bf16)
    u_blk = lambda half: pl.BlockSpec(
        (1, ec // 2, d), lambda i, k: (layer, 2 * jnp.minimum(k, nk - 1) + half, 0))
    vt_blk = lambda half: pl.BlockSpec(
        (1, 1, d // 2, ec), lambda i, k: (layer, 2 * jnp.maximum(k - 1, 0) + half, 0, 0))
    return pl.pallas_call(
        functools.partial(_peer_dense_kernel, rows=rows, tb=tb, lc=lc, mw=MXU_WIDTH, alpha=alpha),
        grid=(t // tb, nk + 1),
        in_specs=[xin, u_blk(0), u_blk(1), vt_blk(0), vt_blk(1),
                  full, rowtab, rowtab, full, fixed, fixed],
        out_specs=xspec, out_shape=jax.ShapeDtypeStruct((t, d), f32),
        scratch_shapes=[pltpu.VMEM((d, tb), f32), pltpu.VMEM((d, tb), bf16),
                        pltpu.VMEM((ec, tb), f32), pltpu.VMEM((ec, tb), f32), gt_scr, gt_scr,
                        pltpu.VMEM((PEER_HEADS, kt, BF16_ROWS, tb), bf16),
                        pltpu.VMEM((PEER_HEADS, kt, BF16_ROWS, tb), bf16)],
        compiler_params=_params("parallel", "arbitrary"))(
            x, u, u, vt, vt, rank2, crow, e1z, e2, g.reshape(1, d), b.reshape(1, d))


def _swa_layer(x, batch, seq, w_in, sinks, w_out, bias, g, b, alpha):
    feat = (N_HEADS + 2 * SWA_KV_HEADS) * HEAD_DIM
    d_q = N_HEADS * HEAD_DIM
    kvw = SWA_KV_HEADS * HEAD_DIM
    proj = _mm(x, w_in.astype(bf16), bf16, 512, feat).reshape(batch, seq, feat)
    sink = jnp.repeat(sinks.astype(f32), HEAD_DIM).reshape(1, d_q)
    o = _band_attention(proj, bias, 1, feat, 0, d_q // kvw, d_q // kvw + 1, kvw,
                        N_HEADS // SWA_KV_HEADS, sink)
    return _proj_ln([o], [], [1], w_out.astype(bf16), x, g, b, alpha)


def _dil_layer(x, batch, seq, w_in, w_out, biases, g, b, alpha):
    d_q = N_HEADS * HEAD_DIM
    feat = 3 * d_q
    w_in = w_in.astype(bf16)
    outs, lses, dils = [], [], []
    for gi, (window, dil) in enumerate(DIL_PATTERNS):
        proj = _mm_dil(x, w_in[:, gi * feat:(gi + 1) * feat], dil).reshape(batch, seq // dil, dil * feat)
        o, lse = _band_attention(proj, biases[gi], dil, feat, 0, 1, 2, d_q, 1)
        outs.append(o)
        lses.append(lse)
        dils.append(dil)
    return _proj_ln(outs, lses, dils, w_out.astype(bf16), x, g, b, alpha)


def _mla_layer(x, batch, seq, w_in, q_norm, w_uq, kv_norm, w_ukv, w_out, g, b, alpha):
    q, k, v = _mla_prep(x, w_in, q_norm, w_uq, kv_norm, w_ukv, seq)
    o = _mla_attention(q, k, v, batch, seq)
    d = w_out.shape[1]
    w3 = w_out.reshape(N_HEADS, MLA_V, d)
    w_pad = jnp.concatenate([w3, jnp.zeros((N_HEADS, LANE - MLA_V, d), f32)], 1)
    return _proj_ln([o], [], [1], w_pad.reshape(N_HEADS * LANE, d).astype(bf16), x, g, b, alpha)


def _pack_u_kernel(u_ref, o_ref):
    o_ref[0] = pltpu.bitcast((u_ref[0] * GELU_FOLD).astype(bf16), jnp.uint32)


def _pack_vt_kernel(v_ref, o_ref):
    o_ref[0, 0] = pltpu.bitcast(v_ref[0].T.astype(bf16), jnp.uint32)


def _pack_expert_weights(u, v):
    depth, e, d = u.shape
    ec = DENSE_EC
    src = pl.BlockSpec((1, ec, d), lambda l, j: (l, j, 0))
    up = pl.pallas_call(
        _pack_u_kernel, grid=(depth, e // ec), in_specs=[src],
        out_specs=pl.BlockSpec((1, ec // 2, d), lambda l, j: (l, j, 0)),
        out_shape=jax.ShapeDtypeStruct((depth, e // 2, d), jnp.uint32),
        compiler_params=_params("parallel", "parallel"))(u)
    vtp = pl.pallas_call(
        _pack_vt_kernel, grid=(depth, e // ec), in_specs=[src],
        out_specs=pl.BlockSpec((1, 1, d // 2, ec), lambda l, j: (l, j, 0, 0)),
        out_shape=jax.ShapeDtypeStruct((depth, e // ec, d // 2, ec), jnp.uint32),
        compiler_params=_params("parallel", "parallel"))(v)
    return up, vtp


def _peer_layer(x, w_q, keys, u_packed, vt_packed, layer, g, b, alpha):
    routing = _peer_route(x, w_q.astype(bf16), keys.astype(bf16))
    return _peer_dense(x, u_packed, vt_packed, layer, routing, g, b, alpha)


def kernel(x, rel_bias, ln_g, ln_b, swa_w_in, swa_sinks, swa_w_out, dil_w_in, dil_w_out,
           mla_w_in, mla_q_norm, mla_w_uq, mla_kv_norm, mla_w_ukv, mla_w_out,
           peer_w_q, peer_keys, peer_u, peer_v):
    batch, seq, d = x.shape
    depth = ln_g.shape[0]
    alpha = (2 * depth) ** 0.25
    assert seq % (DIL_PATTERNS[-1][1] * BLOCK) == 0, "sequence must be a whole number of dilation segments"
    swa_bias = _band_bias(rel_bias, SWA_WINDOW - 1, 1)
    dil_bias = [_band_bias(rel_bias, window // dil, dil) for window, dil in DIL_PATTERNS]
    u_packed, vt_packed = _pack_expert_weights(peer_u, peer_v)
    h = x.reshape(batch * seq, d)
    for i in range(depth):
        kind, j = i % 3, i // 3
        if kind == 0:
            h = _swa_layer(h, batch, seq, swa_w_in[j], swa_sinks[j], swa_w_out[j], swa_bias,
                           ln_g[i, 0], ln_b[i, 0], alpha)
        elif kind == 1:
            h = _dil_layer(h, batch, seq, dil_w_in[j], dil_w_out[j], dil_bias,
                           ln_g[i, 0], ln_b[i, 0], alpha)
        else:
            h = _mla_layer(h, batch, seq, mla_w_in[j], mla_q_norm[j], mla_w_uq[j], mla_kv_norm[j],
                           mla_w_ukv[j], mla_w_out[j], ln_g[i, 0], ln_b[i, 0], alpha)
        h = _peer_layer(h, peer_w_q[i], peer_keys[i], u_packed, vt_packed, i, ln_g[i, 1], ln_b[i, 1], alpha)
    return h.reshape(batch, seq, d)
```

```python
import functools
import math

import jax
import jax.numpy as jnp
from jax import lax
from jax.experimental import pallas as pl
from jax.experimental.pallas import tpu as pltpu

f32 = jnp.float32
bf16 = jnp.bfloat16

N_HEADS = 16
HEAD_DIM = 64
BLOCK = 128
SWA_KV_HEADS = 2
SWA_WINDOW = 128
DIL_PATTERNS = ((128, 1), (512, 4), (2048, 16))
MLA_Q_RANK = 256
MLA_KV_RANK = 128
MLA_NOPE = 64
MLA_ROPE = 32
MLA_V = 64
ROPE_THETA = 10000.0
REL_BUCKETS = 32
REL_MAX_DIST = 2048
PEER_HEADS = 8
PEER_KEYS = 128
PEER_DKEY = 256
PEER_TOPK = 16
LN_EPS = 1e-5
RMS_EPS = 1e-6
NEG = -1e30

LANE = 128
BF16_ROWS = 16
F32_ROWS = 8
MXU_WIDTH = 256
MLA_GROUP = 2
ROUTE_HEADS = 4
GELU_FOLD = 0.5 ** 0.5
ROUTE_TOKENS = 512
DENSE_EC = 4 * PEER_KEYS
VMEM_LIMIT = 56 * 1024 * 1024

_NT = (((1,), (1,)), ((), ()))


def _params(*sem):
    return pltpu.CompilerParams(dimension_semantics=sem, vmem_limit_bytes=VMEM_LIMIT)


def _mm_kernel(x_ref, w_ref, o_ref):
    o_ref[...] = jnp.dot(x_ref[...].astype(bf16), w_ref[...],
                         preferred_element_type=f32).astype(o_ref.dtype)


def _mm(x, w, out_dtype, tm, tn):
    m, k = x.shape
    n = w.shape[1]
    tm, tn = min(tm, m), min(tn, n)
    return pl.pallas_call(
        _mm_kernel, grid=(m // tm, n // tn),
        in_specs=[pl.BlockSpec((tm, k), lambda i, j: (i, 0)),
                  pl.BlockSpec((k, tn), lambda i, j: (0, j))],
        out_specs=pl.BlockSpec((tm, tn), lambda i, j: (i, j)),
        out_shape=jax.ShapeDtypeStruct((m, n), out_dtype),
        compiler_params=_params("parallel", "parallel"))(x, w)


def _rel_bucket(dist):
    max_exact = REL_BUCKETS // 2
    n = jnp.maximum(dist, 0)
    nf = jnp.maximum(n, 1).astype(f32)
    large = max_exact + (jnp.log(nf / max_exact) / math.log(REL_MAX_DIST / max_exact)
                         * (REL_BUCKETS - max_exact)).astype(jnp.int32)
    large = jnp.minimum(large, REL_BUCKETS - 1)
    return jnp.where(n < max_exact, n, large)


def _bias_kernel(rel_ref, bucket_ref, valid_ref, o_ref):
    bucket = bucket_ref[...]
    for h in range(N_HEADS):
        acc = jnp.zeros(bucket.shape, f32)
        for b in range(REL_BUCKETS):
            acc = jnp.where(bucket == b, rel_ref[b, h], acc)
        for variant in range(2):
            o_ref[variant, h] = jnp.where(valid_ref[variant] != 0, acc, NEG)


def _band_bias(rel_bias, max_dist, dilation):
    qi = jnp.arange(BLOCK)[:, None]
    kj = jnp.arange(2 * BLOCK)[None, :]
    dist = BLOCK + qi - kj
    bucket = _rel_bucket(dist * dilation).astype(jnp.int32)
    valid = (dist >= 0) & (dist <= max_dist)
    valid = jnp.stack([valid & (kj >= BLOCK), valid]).astype(jnp.int32)
    return pl.pallas_call(
        _bias_kernel,
        in_specs=[pl.BlockSpec(memory_space=pltpu.SMEM),
                  pl.BlockSpec(memory_space=pltpu.VMEM),
                  pl.BlockSpec(memory_space=pltpu.VMEM)],
        out_specs=pl.BlockSpec(memory_space=pltpu.VMEM),
        out_shape=jax.ShapeDtypeStruct((2, N_HEADS, BLOCK, 2 * BLOCK), f32))(rel_bias, bucket, valid)


def _band_kernel(q_ref, kp_ref, kc_ref, vp_ref, vc_ref, bias_ref, *rest, group, with_sink):
    if with_sink:
        sink_ref, o_ref = rest
    else:
        o_ref, lse_ref = rest
    scale = HEAD_DIM ** -0.5
    pair = LANE // HEAD_DIM
    lo = lax.broadcasted_iota(jnp.int32, (1, LANE), 1) < HEAD_DIM
    zero = jnp.zeros((), bf16)

    kv_blocks = {}

    def kv_block(blk, swapped):
        if (blk, swapped) not in kv_blocks:
            cols = slice(blk * LANE, (blk + 1) * LANE)
            k = jnp.concatenate([kp_ref[0, :, cols], kc_ref[0, :, cols]], axis=0)
            v = jnp.concatenate([vp_ref[0, :, cols], vc_ref[0, :, cols]], axis=0)
            if swapped:
                k = jnp.concatenate([k[:, HEAD_DIM:], k[:, :HEAD_DIM]], axis=1)
                v = jnp.concatenate([v[:, HEAD_DIM:], v[:, :HEAD_DIM]], axis=1)
            kv_blocks[(blk, swapped)] = (k, v)
        return kv_blocks[(blk, swapped)]

    heads = range(N_HEADS)
    qp = [q_ref[0, :, p * LANE:(p + 1) * LANE] for p in range(N_HEADS // pair)]
    qm = [jnp.where(lo if h % pair == 0 else ~lo, qp[h // pair], zero) for h in heads]
    kvs = [kv_block((h // group) // pair, (h // group) % pair != h % pair) for h in heads]
    ss = [lax.dot_general(q, kv[0], _NT, preferred_element_type=f32) * scale + bias_ref[0, h]
          for h, q, kv in zip(heads, qm, kvs)]
    ms = [s.max(-1, keepdims=True) for s in ss]
    ps = [jnp.exp(s - m) for s, m in zip(ss, ms)]
    ls = [p.sum(-1, keepdims=True) for p in ps]
    os_ = [jnp.dot(p.astype(bf16), kv[1], preferred_element_type=f32) / l for p, kv, l in zip(ps, kvs, ls)]
    lses = [jnp.broadcast_to(m + jnp.log(l), (BLOCK, LANE)) for m, l in zip(ms, ls)]
    for p in range(N_HEADS // pair):
        cols = slice(p * LANE, (p + 1) * LANE)
        o = jnp.where(lo, os_[pair * p], os_[pair * p + 1])
        lse = jnp.where(lo, lses[pair * p], lses[pair * p + 1])
        if with_sink:
            o = o * jax.nn.sigmoid(lse - sink_ref[:, cols])
        else:
            lse_ref[0, :, cols] = lse
        o_ref[0, :, cols] = o.astype(o_ref.dtype)


def _band_attention(view, bias, dil, feat, q_col, k_col, v_col, kv_width, group, sink=None):
    b, m_len, _ = view.shape
    d_q = N_HEADS * HEAD_DIM
    nb = m_len // BLOCK
    qpr, kpr = feat // d_q, feat // kv_width

    def cur(col, per_row):
        return lambda bi, r, n: (bi, n, r * per_row + col)

    def prev(col, per_row):
        return lambda bi, r, n: (bi, jnp.maximum(n - 1, 0), r * per_row + col)

    in_specs = [pl.BlockSpec((1, BLOCK, d_q), cur(q_col, qpr)),
                pl.BlockSpec((1, BLOCK, kv_width), prev(k_col, kpr)),
                pl.BlockSpec((1, BLOCK, kv_width), cur(k_col, kpr)),
                pl.BlockSpec((1, BLOCK, kv_width), prev(v_col, kpr)),
                pl.BlockSpec((1, BLOCK, kv_width), cur(v_col, kpr)),
                pl.BlockSpec((1, N_HEADS, BLOCK, 2 * BLOCK),
                             lambda bi, r, n: (jnp.minimum(n, 1), 0, 0, 0))]
    args = [view, view, view, view, view, bias]
    o_spec = pl.BlockSpec((1, BLOCK, d_q), lambda bi, r, n: (bi, n, r))
    o_shape = jax.ShapeDtypeStruct((b, m_len, dil * d_q), bf16)
    if sink is not None:
        in_specs.append(pl.BlockSpec((1, d_q), lambda bi, r, n: (0, 0)))
        args.append(sink)
        out_specs, out_shape = o_spec, o_shape
    else:
        out_specs = [o_spec, o_spec]
        out_shape = [o_shape, jax.ShapeDtypeStruct((b, m_len, dil * d_q), f32)]
    out = pl.pallas_call(
        functools.partial(_band_kernel, group=group, with_sink=sink is not None),
        grid=(b, dil, nb), in_specs=in_specs, out_specs=out_specs, out_shape=out_shape,
        compiler_params=_params("parallel", "parallel", "arbitrary"))(*args)
    if sink is not None:
        return out.reshape(b * m_len, dil * d_q)
    return out[0].reshape(b * m_len, dil * d_q), out[1].reshape(b * m_len, dil * d_q)


def _layer_norm(z, g, b):
    mu = z.mean(-1, keepdims=True)
    zc = z - mu
    var = jnp.square(zc).mean(-1, keepdims=True)
    return zc * lax.rsqrt(var + LN_EPS) * g + b


def _proj_ln_kernel(*refs, dils, alpha):
    n_pat = len(dils)
    o_refs = refs[:n_pat]
    lse_refs = refs[n_pat:2 * n_pat] if n_pat > 1 else ()
    n_in = len(o_refs) + len(lse_refs)
    w_ref, x_ref, g_ref, b_ref, out_ref = refs[n_in:n_in + 5]
    scratch = refs[n_in + 5:]
    if n_pat == 1 and len(o_refs[0].shape) == 3:
        groups, _, width = o_refs[0].shape
        y = sum(jnp.dot(o_refs[0][gi], w_ref[gi * width:(gi + 1) * width, :], preferred_element_type=f32)
                for gi in range(groups))
        out_ref[...] = _layer_norm(alpha * x_ref[...] + y, g_ref[...], b_ref[...])
        return
    if n_pat == 1:
        o = o_refs[0][...]
    else:
        kw = w_ref.shape[0]
        mix_ref = scratch[-1]

        def natural(ref, dil, scr, c):
            if dil == 1:
                return ref[:, c * LANE:(c + 1) * LANE].astype(f32)
            rows = ref.shape[0]
            for r in range(dil):
                scr[c, pl.ds(r, rows, stride=dil), :] = (
                    ref[:, r * kw + c * LANE:r * kw + (c + 1) * LANE].astype(f32))
            return scr[c]

        for c in range(kw // LANE):
            lses = [natural(r, dl, scratch[2 * i], c) for i, (r, dl) in enumerate(zip(lse_refs, dils))]
            mx = functools.reduce(jnp.maximum, lses)
            es = [jnp.exp(l - mx) for l in lses]
            num = sum(e * natural(r, dl, scratch[2 * i + 1], c)
                      for i, (e, r, dl) in enumerate(zip(es, o_refs, dils)))
            mix_ref[:, c * LANE:(c + 1) * LANE] = (num / sum(es)).astype(bf16)
        o = mix_ref[...]
    y = jnp.dot(o, w_ref[...], preferred_element_type=f32)
    out_ref[...] = _layer_norm(alpha * x_ref[...] + y, g_ref[...], b_ref[...])


def _proj_ln(os_, lses, dils, w_out, x, g, b, alpha, tm=512):
    t, d = x.shape
    tm = min(tm, t)
    kw = w_out.shape[0]
    row = lambda width: pl.BlockSpec((tm, width), lambda i: (i, 0))
    dilated = lambda dl: pl.BlockSpec((tm // dl, dl * kw), lambda i: (i, 0))
    fixed = lambda shape: pl.BlockSpec(shape, lambda i: (0, 0))
    if os_[0].ndim == 3:
        o_specs = [pl.BlockSpec((os_[0].shape[0], tm, os_[0].shape[2]), lambda i: (0, i, 0))]
    else:
        o_specs = [dilated(dl) for dl in dils]
    in_specs = (o_specs + [dilated(dl) for dl in dils[:len(lses)]]
                + [fixed((kw, d)), row(d), fixed((1, d)), fixed((1, d))])
    scratch = [pltpu.VMEM((kw // LANE, tm, LANE), f32)] * (2 * len(lses))
    if lses:
        scratch.append(pltpu.VMEM((tm, kw), bf16))
    return pl.pallas_call(
        functools.partial(_proj_ln_kernel, dils=tuple(dils), alpha=alpha),
        grid=(t // tm,), in_specs=in_specs, out_specs=row(d),
        out_shape=jax.ShapeDtypeStruct((t, d), f32), scratch_shapes=scratch,
        compiler_params=_params("parallel"))(*os_, *lses, w_out, x, g.reshape(1, d), b.reshape(1, d))


def _mm_dil_kernel(x_ref, w_ref, o_ref, res_ref, *, dil):
    n = w_ref.shape[1]
    res = jnp.dot(x_ref[...].astype(bf16), w_ref[...], preferred_element_type=f32)
    if dil == 1:
        o_ref[...] = res.astype(o_ref.dtype)
        return
    rows = o_ref.shape[0]
    for c in range(n // LANE):
        res_ref[c] = res[:, c * LANE:(c + 1) * LANE]
        for r in range(dil):
            o_ref[:, r * n + c * LANE:r * n + (c + 1) * LANE] = (
                res_ref[c, pl.ds(r, rows, stride=dil), :].astype(o_ref.dtype))


def _mm_dil(x, w, dil, tm=512):
    t, k = x.shape
    n = w.shape[1]
    tm = min(tm, t)
    return pl.pallas_call(
        functools.partial(_mm_dil_kernel, dil=dil), grid=(t // tm,),
        in_specs=[pl.BlockSpec((tm, k), lambda i: (i, 0)), pl.BlockSpec((k, n), lambda i: (0, 0))],
        out_specs=pl.BlockSpec((tm // dil, dil * n), lambda i: (i, 0)),
        out_shape=jax.ShapeDtypeStruct((t // dil, dil * n), bf16),
        scratch_shapes=[pltpu.VMEM((n // LANE, tm, LANE), f32)],
        compiler_params=_params("parallel"))(x, w)


def _rms(c, g):
    return c * lax.rsqrt(jnp.square(c).mean(-1, keepdims=True) + RMS_EPS) * g


def _mla_prep_kernel(x_ref, win_ref, qn_ref, wq_ref, kvn_ref, wkv_ref, cos_ref, sin_ref,
                     q_ref, k_ref, v_ref):
    hw = N_HEADS * LANE
    scale = (MLA_NOPE + MLA_ROPE) ** -0.5
    cos, sin = cos_ref[...], sin_ref[...]
    xw = jnp.dot(x_ref[...].astype(bf16), win_ref[...], preferred_element_type=f32)
    cq = _rms(xw[:, :MLA_Q_RANK], qn_ref[...]).astype(bf16)
    ckv = _rms(xw[:, MLA_Q_RANK:MLA_Q_RANK + MLA_KV_RANK], kvn_ref[...]).astype(bf16)
    off = MLA_Q_RANK + MLA_KV_RANK
    kr = xw[:, off:off + LANE] * cos + xw[:, off + LANE:off + 2 * LANE] * sin
    qq = jnp.dot(cq, wq_ref[...], preferred_element_type=f32)
    kv = jnp.dot(ckv, wkv_ref[...], preferred_element_type=f32)
    ones_lane = lax.broadcasted_iota(jnp.int32, (1, LANE), 1) == MLA_V
    for h in range(N_HEADS):
        blk = slice(h * LANE, (h + 1) * LANE)
        swp = slice(hw + h * LANE, hw + (h + 1) * LANE)
        grp, sub = divmod(h, MLA_GROUP)
        dst = slice(sub * LANE, (sub + 1) * LANE)
        q_ref[grp, :, dst] = ((qq[:, blk] * cos + qq[:, swp] * sin) * scale).astype(bf16)
        k_ref[grp, :, dst] = (kv[:, blk] + kr).astype(bf16)
        v_ref[grp, :, dst] = jnp.where(ones_lane, 1.0, kv[:, swp]).astype(bf16)


def _mla_prep(x, w_in, q_norm, w_uq, kv_norm, w_ukv, seq, tm=512):
    t, d = x.shape
    tm = min(tm, seq)
    hw = N_HEADS * LANE
    half = MLA_ROPE // 2
    dq = MLA_NOPE + MLA_ROPE
    kr_w = w_in[:, MLA_Q_RANK + MLA_KV_RANK:]
    zeros = lambda n: jnp.zeros((d, n), f32)
    kr_a = jnp.concatenate([zeros(MLA_NOPE), kr_w, zeros(LANE - dq)], 1)
    kr_b = jnp.concatenate([zeros(MLA_NOPE), kr_w[:, half:], kr_w[:, :half], zeros(LANE - dq)], 1)
    win = jnp.concatenate([w_in[:, :MLA_Q_RANK + MLA_KV_RANK], kr_a, kr_b], 1).astype(bf16)
    wq3 = w_uq.reshape(MLA_Q_RANK, N_HEADS, dq)
    zq = lambda n: jnp.zeros((MLA_Q_RANK, N_HEADS, n), f32)
    wq_a = jnp.concatenate([wq3, zq(LANE - dq)], 2)
    wq_b = jnp.concatenate([zq(MLA_NOPE), wq3[:, :, MLA_NOPE + half:], wq3[:, :, MLA_NOPE:MLA_NOPE + half],
                            zq(LANE - dq)], 2)
    wq = jnp.concatenate([wq_a.reshape(MLA_Q_RANK, hw), wq_b.reshape(MLA_Q_RANK, hw)], 1).astype(bf16)
    wkv3 = w_ukv.reshape(MLA_KV_RANK, N_HEADS, MLA_NOPE + MLA_V)
    zk = lambda n: jnp.zeros((MLA_KV_RANK, N_HEADS, n), f32)
    wk = jnp.concatenate([wkv3[:, :, :MLA_NOPE], zk(LANE - MLA_NOPE)], 2)
    wv = jnp.concatenate([wkv3[:, :, MLA_NOPE:], zk(LANE - MLA_V)], 2)
    wkv = jnp.concatenate([wk.reshape(MLA_KV_RANK, hw), wv.reshape(MLA_KV_RANK, hw)], 1).astype(bf16)
    freq = ROPE_THETA ** (-jnp.arange(half, dtype=f32) / half)
    ang = jnp.arange(seq)[:, None].astype(f32) * freq[None, :]
    c, s_ = jnp.cos(ang), jnp.sin(ang)
    cos_t = jnp.concatenate([jnp.ones((seq, MLA_NOPE), f32), c, c, jnp.ones((seq, LANE - dq), f32)], 1)
    sin_t = jnp.concatenate([jnp.zeros((seq, MLA_NOPE), f32), -s_, s_, jnp.zeros((seq, LANE - dq), f32)], 1)

    n_pos = seq // tm
    row = lambda width: pl.BlockSpec((tm, width), lambda i: (i, 0))
    fixed = lambda a: pl.BlockSpec(a.shape, lambda i: (0, 0))
    pos = pl.BlockSpec((tm, LANE), lambda i: (i % n_pos, 0))
    qn, kvn = q_norm.reshape(1, -1), kv_norm.reshape(1, -1)
    groups, gw = N_HEADS // MLA_GROUP, MLA_GROUP * LANE
    out_sd = jax.ShapeDtypeStruct((groups, t, gw), bf16)
    out_spec = pl.BlockSpec((groups, tm, gw), lambda i: (0, i, 0))
    return pl.pallas_call(
        _mla_prep_kernel, grid=(t // tm,),
        in_specs=[row(d), fixed(win), fixed(qn), fixed(wq), fixed(kvn), fixed(wkv), pos, pos],
        out_specs=[out_spec] * 3, out_shape=[out_sd] * 3,
        compiler_params=_params("parallel"))(x, win, qn, wq, kvn, wkv, cos_t, sin_t)


def _mla_attn_kernel(q_ref, k_ref, v_ref, o_ref, *, tq, tk, heads):
    qi = pl.program_id(2)
    lanes = [slice(j * LANE, (j + 1) * LANE) for j in range(heads)]
    qs = [q_ref[0, :, ln] for ln in lanes]

    def update(carry, start, mask, r0=0):
        ss = [lax.dot_general(q[r0:], k_ref[0, pl.ds(start, tk), ln], _NT, preferred_element_type=f32)
              for q, ln in zip(qs, lanes)]
        if mask is not None:
            ss = [jnp.where(mask, s, NEG) for s in ss]
        ms = [jnp.maximum(m, s.max(-1, keepdims=True)) for (m, _), s in zip(carry, ss)]
        ps = [jnp.exp(s - m).astype(bf16) for s, m in zip(ss, ms)]
        pvs = [jnp.dot(p, v_ref[0, pl.ds(start, tk), ln], preferred_element_type=f32)
               for p, ln in zip(ps, lanes)]
        return tuple((m_new, jnp.exp(m - m_new) * acc + pv)
                     for (m, acc), m_new, pv in zip(carry, ms, pvs))

    init = tuple((jnp.full((tq, 1), NEG, f32), jnp.zeros((tq, LANE), f32)) for _ in range(heads))
    sub = tq // tk
    carry = lax.fori_loop(0, qi * sub, lambda ki, c: update(c, pl.multiple_of(ki * tk, tk), None), init)
    for j in range(sub):
        r0 = j * tk
        row = lax.broadcasted_iota(jnp.int32, (tq - r0, tk), 0)
        col = lax.broadcasted_iota(jnp.int32, (tq - r0, tk), 1)
        part = update(tuple((m[r0:], acc[r0:]) for m, acc in carry),
                      pl.multiple_of(qi * tq + r0, tk), col <= row, r0)
        carry = tuple((jnp.concatenate([m[:r0], pm], axis=0), jnp.concatenate([acc[:r0], pa], axis=0))
                      for (m, acc), (pm, pa) in zip(carry, part)) if r0 else part
    for j in range(heads):
        acc = carry[j][1]
        o_ref[0, :, lanes[j]] = (acc / acc[:, MLA_V:MLA_V + 1]).astype(o_ref.dtype)


def _mla_attention(q, k, v, batch, seq, tq=1024, tk=512):
    groups, t, gw = q.shape
    tq = min(tq, seq)
    nq = seq // tq
    qspec = pl.BlockSpec((1, tq, gw), lambda b, h, i: (h, b * nq + i, 0))
    kspec = pl.BlockSpec((1, seq, gw), lambda b, h, i: (h, b, 0))
    return pl.pallas_call(
        functools.partial(_mla_attn_kernel, tq=tq, tk=min(tk, tq), heads=MLA_GROUP),
        grid=(batch, groups, nq),
        in_specs=[qspec, kspec, kspec], out_specs=qspec,
        out_shape=jax.ShapeDtypeStruct((groups, t, gw), bf16),
        compiler_params=_params("parallel", "parallel", "arbitrary"))(q, k, v)


def _batcher_network(n):
    def merge(lo, hi, r):
        step = r * 2
        if step < hi - lo:
            yield from merge(lo, hi, step)
            yield from merge(lo + r, hi, step)
            yield from [(i, i + r) for i in range(lo + r, hi - r, step)]
        else:
            yield (lo, lo + r)

    def sort(lo, hi):
        if hi - lo >= 1:
            mid = lo + (hi - lo) // 2
            yield from sort(lo, mid)
            yield from sort(mid + 1, hi)
            yield from merge(lo, hi, 1)

    return tuple(sort(0, n - 1))


def _top16_values(s):
    n = s.shape[0] // F32_ROWS
    tiles = [s[j * F32_ROWS:(j + 1) * F32_ROWS] for j in range(n)]
    for i, j in _batcher_network(PEER_TOPK):
        if j < n:
            tiles[i], tiles[j] = jnp.maximum(tiles[i], tiles[j]), jnp.minimum(tiles[i], tiles[j])
    vals = []
    for k in range(PEER_TOPK):
        mx = tiles[0].max(axis=0, keepdims=True)
        vals.append(mx)
        hit = tiles[0] == mx
        for j in range(min(n, PEER_TOPK - 1 - k)):
            below = tiles[j + 1] if j + 1 < n else -jnp.inf
            tiles[j] = jnp.where(hit, below, tiles[j])
    return vals


def _top16_pair(s1, s2):
    v1, v2 = _top16_values(s1), _top16_values(s2)
    rank2 = jnp.zeros(s2.shape, f32)
    for k in range(PEER_TOPK):
        rank2 = jnp.where(s2 < v2[k], float(k + 1), rank2)
    return v1, v2, rank2


def _stack_rows(rows):
    n = len(rows)
    rid = lax.broadcasted_iota(jnp.int32, (n, rows[0].shape[1]), 0)
    out = jnp.broadcast_to(rows[0], (n, rows[0].shape[1]))
    for i in range(1, n):
        out = jnp.where(rid == i, rows[i], out)
    return out


def _route_kernel(x_ref, wq_ref, keys_ref, rank2_ref, crow_ref, e1z_ref, e2_ref, *, tm):
    q = jnp.dot(x_ref[...].astype(bf16), wq_ref[...], preferred_element_type=f32).astype(bf16)
    half = PEER_DKEY // 2
    scores = []
    for hh in range(ROUTE_HEADS):
        qh = q[:, hh * PEER_DKEY:(hh + 1) * PEER_DKEY]
        scores.append((lax.dot_general(keys_ref[hh, 0], qh[:, :half], _NT, preferred_element_type=f32),
                       lax.dot_general(keys_ref[hh, 1], qh[:, half:], _NT, preferred_element_type=f32)))
    rid8 = lax.broadcasted_iota(jnp.int32, (8, LANE), 0)
    for hh, c in [(hh, c) for hh in range(ROUTE_HEADS) for c in range(tm // LANE)]:
        lanes = slice(c * LANE, (c + 1) * LANE)
        s1, s2 = scores[hh][0][:, lanes], scores[hh][1][:, lanes]
        v1, v2, rank2 = _top16_pair(s1, s2)
        v2_all = _stack_rows(v2)
        cands = [v1[0] + v2_all]
        for a in range(1, 8):
            cands.append(jnp.where(rid8 < PEER_TOPK // (a + 1), v1[a] + v2_all[:8], -jnp.inf))
        cands.append(_stack_rows(v1[8:]) + v2[0])
        cand = jnp.concatenate(cands, axis=0)
        tau = _top16_values(cand)[-1]
        cmax = v1[0] + v2[0]
        z = jnp.where(cand >= tau, jnp.exp(cand - cmax), 0.0).sum(axis=0, keepdims=True)
        v1_all = _stack_rows(v1)
        counts = jnp.zeros(v1_all.shape, f32)
        for b in range(PEER_TOPK):
            counts = counts + jnp.where(v1_all + v2[b] >= tau, 1.0, 0.0)
        crow = jnp.zeros(s1.shape, f32)
        for a in range(PEER_TOPK):
            crow = jnp.where(s1 == v1[a], counts[a:a + 1], crow)
        rank2_ref[0, hh, :, lanes] = rank2.astype(bf16)
        crow_ref[0, hh, :, lanes] = crow
        e1z_ref[0, hh, :, lanes] = jnp.exp(s1 - v1[0]) * (GELU_FOLD / z)
        e2_ref[0, hh, :, lanes] = jnp.exp(s2 - v2[0]).astype(bf16)


def _peer_route(x, w_q, keys, tm=ROUTE_TOKENS):
    t, d = x.shape
    tm = min(tm, t)
    hb = ROUTE_HEADS
    spec = pl.BlockSpec((1, hb, PEER_KEYS, tm), lambda i, h: (i, h, 0, 0))
    sd = lambda dt: jax.ShapeDtypeStruct((t // tm, PEER_HEADS, PEER_KEYS, tm), dt)
    return pl.pallas_call(
        functools.partial(_route_kernel, tm=tm), grid=(t // tm, PEER_HEADS // hb),
        in_specs=[pl.BlockSpec((tm, d), lambda i, h: (i, 0)),
                  pl.BlockSpec((d, hb * PEER_DKEY), lambda i, h: (0, h)),
                  pl.BlockSpec((hb, 2, PEER_KEYS, PEER_DKEY // 2), lambda i, h: (h, 0, 0, 0))],
        out_specs=[spec] * 4, out_shape=[sd(bf16), sd(f32), sd(f32), sd(bf16)],
        compiler_params=_params("parallel", "arbitrary"))(x, w_q, keys)


def _peer_dense_kernel(x_ref, u0_ref, u1_ref, vt0_ref, vt1_ref, rank2_in, crow_ref, e1z_ref, e2_in,
                       g_ref, b_ref, o_ref, acc_ref, xt_ref, ht0, ht1, gt0, gt1, rank2_ref, e2_ref,
                       *, rows, tb, lc, mw, alpha):
    k = pl.program_id(1)
    last = pl.num_programs(1) - 1
    ec = rows * PEER_KEYS
    zero = jnp.zeros((), bf16)
    kt = PEER_KEYS // BF16_ROWS

    def up(half, ht, cw):
        lanes = slice(cw * mw, (cw + 1) * mw)
        u_blk = pltpu.bitcast((u0_ref, u1_ref)[half][0], bf16)
        ht[:, lanes] = jnp.dot(u_blk, xt_ref[:, lanes], preferred_element_type=f32)

    n_tiles = last

    def gate(ht, tile, half, gt, cw):
        base = pl.multiple_of(jnp.clip(tile, 0, n_tiles - 1) * F32_ROWS, F32_ROWS)
        for r in range(rows):
            keys = slice(r * PEER_KEYS, (r + 1) * PEER_KEYS)
            row = slice(half * rows + r, half * rows + r + 1)
            for c in range(cw * mw // lc, (cw + 1) * mw // lc):
                lanes = slice(c * lc, (c + 1) * lc)
                w = None
                for h in range(PEER_HEADS):
                    cr = crow_ref[0, h, pl.ds(base, F32_ROWS), lanes][row]
                    ez = e1z_ref[0, h, pl.ds(base, F32_ROWS), lanes][row]
                    cr = jnp.broadcast_to(cr, (BF16_ROWS, lc)).astype(bf16)[None]
                    ez = jnp.broadcast_to(ez, (BF16_ROWS, lc)).astype(bf16)[None]
                    term = jnp.where(rank2_ref[h, :, :, lanes] < cr, e2_ref[h, :, :, lanes] * ez, zero)
                    w = term if w is None else w + term
                hv = ht[keys, lanes]
                g = hv * (1.0 + lax.erf(hv))
                gt[r * kt:(r + 1) * kt, :, lanes] = g.astype(bf16).reshape(kt, BF16_ROWS, lc) * w

    def down(half, gt, cw):
        lanes = slice(cw * mw, (cw + 1) * mw)
        vt_blk = pltpu.bitcast((vt0_ref, vt1_ref)[half][0, 0], bf16)
        acc_ref[:, lanes] += jnp.dot(vt_blk, gt[:, :, lanes].reshape(ec, mw), preferred_element_type=f32)

    chunks = range(tb // mw)

    @pl.when(k == 0)
    def _():
        acc_ref[...] = jnp.zeros_like(acc_ref)
        xt_ref[...] = x_ref[...].T.astype(bf16)
        rank2_ref[...] = rank2_in[0]
        e2_ref[...] = e2_in[0]
        for cw in chunks:
            up(0, ht0, cw)
            up(1, ht1, cw)
        for cw in chunks:
            gate(ht0, k, 0, gt0, cw)

    @pl.when(jnp.logical_and(k > 0, k < last))
    def _():
        for cw in chunks:
            up(0, ht0, cw)
            gate(ht1, k - 1, 1, gt1, cw)
            down(0, gt0, cw)
        for cw in chunks:
            up(1, ht1, cw)
            gate(ht0, k, 0, gt0, cw)
            down(1, gt1, cw)

    @pl.when(k == last)
    def _():
        for cw in chunks:
            gate(ht1, k - 1, 1, gt1, cw)
            down(0, gt0, cw)
        for cw in chunks:
            down(1, gt1, cw)
        z = alpha * x_ref[...] + acc_ref[...].T
        o_ref[...] = _layer_norm(z, g_ref[...], b_ref[...])


def _peer_dense(x, u, vt, layer, routing, g, b, alpha, lc=128):
    t, d = x.shape
    e = 2 * u.shape[1]
    ec = DENSE_EC
    rows = ec // PEER_KEYS
    assert 2 * rows == F32_ROWS, "a pair of expert blocks must span one 8-row f32 tile of sub-key-1 rows"
    nk = e // (2 * ec)
    kt = PEER_KEYS // BF16_ROWS
    rank2, crow, e1z, e2 = routing
    nt, _, _, tb = crow.shape
    rank2, e2 = (a.reshape(nt, PEER_HEADS, kt, BF16_ROWS, tb) for a in (rank2, e2))
    full = pl.BlockSpec((1, PEER_HEADS, kt, BF16_ROWS, tb), lambda i, k: (i, 0, 0, 0, 0))
    rowtab = pl.BlockSpec((1, PEER_HEADS, PEER_KEYS, tb), lambda i, k: (i, 0, 0, 0))
    fixed = pl.BlockSpec((1, d), lambda i, k: (0, 0))
    xspec = pl.BlockSpec((tb, d), lambda i, k: (i, 0))
    gt_scr = pltpu.VMEM((ec // BF16_ROWS, BF16_ROWS, tb), bf16)
    u_blk = lambda half: pl.BlockSpec(
        (1, ec // 2, d), lambda i, k: (layer, 2 * jnp.minimum(k, nk - 1) + half, 0))
    vt_blk = lambda half: pl.BlockSpec(
        (1, 1, d // 2, ec), lambda i, k: (layer, 2 * jnp.maximum(k - 1, 0) + half, 0, 0))
    return pl.pallas_call(
        functools.partial(_peer_dense_kernel, rows=rows, tb=tb, lc=lc, mw=MXU_WIDTH, alpha=alpha),
        grid=(t // tb, nk + 1),
        in_specs=[xspec, u_blk(0), u_blk(1), vt_blk(0), vt_blk(1),
                  full, rowtab, rowtab, full, fixed, fixed],
        out_specs=xspec, out_shape=jax.ShapeDtypeStruct((t, d), f32),
        scratch_shapes=[pltpu.VMEM((d, tb), f32), pltpu.VMEM((d, tb), bf16),
                        pltpu.VMEM((ec, tb), f32), pltpu.VMEM((ec, tb), f32), gt_scr, gt_scr,
                        pltpu.VMEM((PEER_HEADS, kt, BF16_ROWS, tb), bf16),
                        pltpu.VMEM((PEER_HEADS, kt, BF16_ROWS, tb), bf16)],
        compiler_params=_params("parallel", "arbitrary"))(
            x, u, u, vt, vt, rank2, crow, e1z, e2, g.reshape(1, d), b.reshape(1, d))


def _swa_layer(x, batch, seq, w_in, sinks, w_out, bias, g, b, alpha):
    feat = (N_HEADS + 2 * SWA_KV_HEADS) * HEAD_DIM
    d_q = N_HEADS * HEAD_DIM
    kvw = SWA_KV_HEADS * HEAD_DIM
    proj = _mm(x, w_in.astype(bf16), bf16, 512, feat).reshape(batch, seq, feat)
    sink = jnp.repeat(sinks.astype(f32), HEAD_DIM).reshape(1, d_q)
    o = _band_attention(proj, bias, 1, feat, 0, d_q // kvw, d_q // kvw + 1, kvw,
                        N_HEADS // SWA_KV_HEADS, sink)
    return _proj_ln([o], [], [1], w_out.astype(bf16), x, g, b, alpha)


def _dil_layer(x, batch, seq, w_in, w_out, biases, g, b, alpha):
    d_q = N_HEADS * HEAD_DIM
    feat = 3 * d_q
    w_in = w_in.astype(bf16)
    outs, lses, dils = [], [], []
    for gi, (window, dil) in enumerate(DIL_PATTERNS):
        proj = _mm_dil(x, w_in[:, gi * feat:(gi + 1) * feat], dil).reshape(batch, seq // dil, dil * feat)
        o, lse = _band_attention(proj, biases[gi], dil, feat, 0, 1, 2, d_q, 1)
        outs.append(o)
        lses.append(lse)
        dils.append(dil)
    return _proj_ln(outs, lses, dils, w_out.astype(bf16), x, g, b, alpha)


def _mla_layer(x, batch, seq, w_in, q_norm, w_uq, kv_norm, w_ukv, w_out, g, b, alpha):
    q, k, v = _mla_prep(x, w_in, q_norm, w_uq, kv_norm, w_ukv, seq)
    o = _mla_attention(q, k, v, batch, seq)
    d = w_out.shape[1]
    w3 = w_out.reshape(N_HEADS, MLA_V, d)
    w_pad = jnp.concatenate([w3, jnp.zeros((N_HEADS, LANE - MLA_V, d), f32)], 1)
    return _proj_ln([o], [], [1], w_pad.reshape(N_HEADS * LANE, d).astype(bf16), x, g, b, alpha)


def _pack_u_kernel(u_ref, o_ref):
    o_ref[0] = pltpu.bitcast((u_ref[0] * GELU_FOLD).astype(bf16), jnp.uint32)


def _pack_vt_kernel(v_ref, o_ref):
    o_ref[0, 0] = pltpu.bitcast(v_ref[0].T.astype(bf16), jnp.uint32)


def _pack_expert_weights(u, v):
    depth, e, d = u.shape
    ec = DENSE_EC
    src = pl.BlockSpec((1, ec, d), lambda l, j: (l, j, 0))
    up = pl.pallas_call(
        _pack_u_kernel, grid=(depth, e // ec), in_specs=[src],
        out_specs=pl.BlockSpec((1, ec // 2, d), lambda l, j: (l, j, 0)),
        out_shape=jax.ShapeDtypeStruct((depth, e // 2, d), jnp.uint32),
        compiler_params=_params("parallel", "parallel"))(u)
    vtp = pl.pallas_call(
        _pack_vt_kernel, grid=(depth, e // ec), in_specs=[src],
        out_specs=pl.BlockSpec((1, 1, d // 2, ec), lambda l, j: (l, j, 0, 0)),
        out_shape=jax.ShapeDtypeStruct((depth, e // ec, d // 2, ec), jnp.uint32),
        compiler_params=_params("parallel", "parallel"))(v)
    return up, vtp


def _peer_layer(x, w_q, keys, u_packed, vt_packed, layer, g, b, alpha):
    routing = _peer_route(x, w_q.astype(bf16), keys.astype(bf16))
    return _peer_dense(x, u_packed, vt_packed, layer, routing, g, b, alpha)


def kernel(x, rel_bias, ln_g, ln_b, swa_w_in, swa_sinks, swa_w_out, dil_w_in, dil_w_out,
           mla_w_in, mla_q_norm, mla_w_uq, mla_kv_norm, mla_w_ukv, mla_w_out,
           peer_w_q, peer_keys, peer_u, peer_v):
    batch, seq, d = x.shape
    depth = ln_g.shape[0]
    alpha = (2 * depth) ** 0.25
    assert seq % (DIL_PATTERNS[-1][1] * BLOCK) == 0, "sequence must be a whole number of dilation segments"
    swa_bias = _band_bias(rel_bias, SWA_WINDOW - 1, 1)
    dil_bias = [_band_bias(rel_bias, window // dil, dil) for window, dil in DIL_PATTERNS]
    u_packed, vt_packed = _pack_expert_weights(peer_u, peer_v)
    h = x.reshape(batch * seq, d)
    for i in range(depth):
        kind, j = i % 3, i // 3
        if kind == 0:
            h = _swa_layer(h, batch, seq, swa_w_in[j], swa_sinks[j], swa_w_out[j], swa_bias,
                           ln_g[i, 0], ln_b[i, 0], alpha)
        elif kind == 1:
            h = _dil_layer(h, batch, seq, dil_w_in[j], dil_w_out[j], dil_bias,
                           ln_g[i, 0], ln_b[i, 0], alpha)
        else:
            h = _mla_layer(h, batch, seq, mla_w_in[j], mla_q_norm[j], mla_w_uq[j], mla_kv_norm[j],
                           mla_w_ukv[j], mla_w_out[j], ln_g[i, 0], ln_b[i, 0], alpha)
        h = _peer_layer(h, peer_w_q[i], peer_keys[i], u_packed, vt_packed, i, ln_g[i, 1], ln_b[i, 1], alpha)
    return h.reshape(batch, seq, d)
```

```python
import functools
import math

import jax
import jax.numpy as jnp
from jax import lax
from jax.experimental import pallas as pl
from jax.experimental.pallas import tpu as pltpu

f32 = jnp.float32
bf16 = jnp.bfloat16

N_HEADS = 16
HEAD_DIM = 64
BLOCK = 128
SWA_KV_HEADS = 2
SWA_WINDOW = 128
DIL_PATTERNS = ((128, 1), (512, 4), (2048, 16))
MLA_Q_RANK = 256
MLA_KV_RANK = 128
MLA_NOPE = 64
MLA_ROPE = 32
MLA_V = 64
ROPE_THETA = 10000.0
REL_BUCKETS = 32
REL_MAX_DIST = 2048
PEER_HEADS = 8
PEER_KEYS = 128
PEER_DKEY = 256
PEER_TOPK = 16
LN_EPS = 1e-5
RMS_EPS = 1e-6
NEG = -1e30

LANE = 128
BF16_ROWS = 16
F32_ROWS = 8
MXU_WIDTH = 256
MLA_GROUP = 2
ROUTE_HEADS = 4
GELU_FOLD = 0.5 ** 0.5
ROUTE_TOKENS = 512
DENSE_EC = 4 * PEER_KEYS
VMEM_LIMIT = 56 * 1024 * 1024

_NT = (((1,), (1,)), ((), ()))


def _params(*sem):
    return pltpu.CompilerParams(dimension_semantics=sem, vmem_limit_bytes=VMEM_LIMIT)


def _mm_kernel(x_ref, w_ref, o_ref):
    o_ref[...] = jnp.dot(x_ref[...].astype(bf16), w_ref[...],
                         preferred_element_type=f32).astype(o_ref.dtype)


def _mm(x, w, out_dtype, tm, tn):
    m, k = x.shape
    n = w.shape[1]
    tm, tn = min(tm, m), min(tn, n)
    return pl.pallas_call(
        _mm_kernel, grid=(m // tm, n // tn),
        in_specs=[pl.BlockSpec((tm, k), lambda i, j: (i, 0)),
                  pl.BlockSpec((k, tn), lambda i, j: (0, j))],
        out_specs=pl.BlockSpec((tm, tn), lambda i, j: (i, j)),
        out_shape=jax.ShapeDtypeStruct((m, n), out_dtype),
        compiler_params=_params("parallel", "parallel"))(x, w)


def _rel_bucket(dist):
    max_exact = REL_BUCKETS // 2
    n = jnp.maximum(dist, 0)
    nf = jnp.maximum(n, 1).astype(f32)
    large = max_exact + (jnp.log(nf / max_exact) / math.log(REL_MAX_DIST / max_exact)
                         * (REL_BUCKETS - max_exact)).astype(jnp.int32)
    large = jnp.minimum(large, REL_BUCKETS - 1)
    return jnp.where(n < max_exact, n, large)


def _bias_kernel(rel_ref, bucket_ref, valid_ref, o_ref):
    bucket = bucket_ref[...]
    for h in range(N_HEADS):
        acc = jnp.zeros(bucket.shape, f32)
        for b in range(REL_BUCKETS):
            acc = jnp.where(bucket == b, rel_ref[b, h], acc)
        for variant in range(2):
            o_ref[variant, h] = jnp.where(valid_ref[variant] != 0, acc, NEG)


def _band_bias(rel_bias, max_dist, dilation):
    qi = jnp.arange(BLOCK)[:, None]
    kj = jnp.arange(2 * BLOCK)[None, :]
    dist = BLOCK + qi - kj
    bucket = _rel_bucket(dist * dilation).astype(jnp.int32)
    valid = (dist >= 0) & (dist <= max_dist)
    valid = jnp.stack([valid & (kj >= BLOCK), valid]).astype(jnp.int32)
    return pl.pallas_call(
        _bias_kernel,
        in_specs=[pl.BlockSpec(memory_space=pltpu.SMEM),
                  pl.BlockSpec(memory_space=pltpu.VMEM),
                  pl.BlockSpec(memory_space=pltpu.VMEM)],
        out_specs=pl.BlockSpec(memory_space=pltpu.VMEM),
        out_shape=jax.ShapeDtypeStruct((2, N_HEADS, BLOCK, 2 * BLOCK), f32))(rel_bias, bucket, valid)


def _band_kernel(q_ref, kp_ref, kc_ref, vp_ref, vc_ref, bias_ref, *rest, group, with_sink):
    if with_sink:
        sink_ref, o_ref = rest
    else:
        o_ref, lse_ref = rest
    scale = HEAD_DIM ** -0.5
    pair = LANE // HEAD_DIM
    lo = lax.broadcasted_iota(jnp.int32, (1, LANE), 1) < HEAD_DIM
    zero = jnp.zeros((), bf16)

    kv_blocks = {}

    def kv_block(blk, swapped):
        if (blk, swapped) not in kv_blocks:
            cols = slice(blk * LANE, (blk + 1) * LANE)
            k = jnp.concatenate([kp_ref[0, :, cols], kc_ref[0, :, cols]], axis=0)
            v = jnp.concatenate([vp_ref[0, :, cols], vc_ref[0, :, cols]], axis=0)
            if swapped:
                k = jnp.concatenate([k[:, HEAD_DIM:], k[:, :HEAD_DIM]], axis=1)
                v = jnp.concatenate([v[:, HEAD_DIM:], v[:, :HEAD_DIM]], axis=1)
            kv_blocks[(blk, swapped)] = (k, v)
        return kv_blocks[(blk, swapped)]

    heads = range(N_HEADS)
    qp = [q_ref[0, :, p * LANE:(p + 1) * LANE] for p in range(N_HEADS // pair)]
    qm = [jnp.where(lo if h % pair == 0 else ~lo, qp[h // pair], zero) for h in heads]
    kvs = [kv_block((h // group) // pair, (h // group) % pair != h % pair) for h in heads]
    ss = [lax.dot_general(q, kv[0], _NT, preferred_element_type=f32) * scale + bias_ref[0, h]
          for h, q, kv in zip(heads, qm, kvs)]
    ms = [s.max(-1, keepdims=True) for s in ss]
    ps = [jnp.exp(s - m) for s, m in zip(ss, ms)]
    ls = [p.sum(-1, keepdims=True) for p in ps]
    os_ = [jnp.dot(p.astype(bf16), kv[1], preferred_element_type=f32) / l for p, kv, l in zip(ps, kvs, ls)]
    lses = [jnp.broadcast_to(m + jnp.log(l), (BLOCK, LANE)) for m, l in zip(ms, ls)]
    for p in range(N_HEADS // pair):
        cols = slice(p * LANE, (p + 1) * LANE)
        o = jnp.where(lo, os_[pair * p], os_[pair * p + 1])
        lse = jnp.where(lo, lses[pair * p], lses[pair * p + 1])
        if with_sink:
            o = o * jax.nn.sigmoid(lse - sink_ref[:, cols])
        else:
            lse_ref[0, :, cols] = lse
        o_ref[0, :, cols] = o.astype(o_ref.dtype)


def _band_attention(view, bias, dil, feat, q_col, k_col, v_col, kv_width, group, sink=None):
    b, m_len, _ = view.shape
    d_q = N_HEADS * HEAD_DIM
    nb = m_len // BLOCK
    qpr, kpr = feat // d_q, feat // kv_width

    def cur(col, per_row):
        return lambda bi, r, n: (bi, n, r * per_row + col)

    def prev(col, per_row):
        return lambda bi, r, n: (bi, jnp.maximum(n - 1, 0), r * per_row + col)

    in_specs = [pl.BlockSpec((1, BLOCK, d_q), cur(q_col, qpr)),
                pl.BlockSpec((1, BLOCK, kv_width), prev(k_col, kpr)),
                pl.BlockSpec((1, BLOCK, kv_width), cur(k_col, kpr)),
                pl.BlockSpec((1, BLOCK, kv_width), prev(v_col, kpr)),
                pl.BlockSpec((1, BLOCK, kv_width), cur(v_col, kpr)),
                pl.BlockSpec((1, N_HEADS, BLOCK, 2 * BLOCK),
                             lambda bi, r, n: (jnp.minimum(n, 1), 0, 0, 0))]
    args = [view, view, view, view, view, bias]
    o_spec = pl.BlockSpec((1, BLOCK, d_q), lambda bi, r, n: (bi, n, r))
    o_shape = jax.ShapeDtypeStruct((b, m_len, dil * d_q), bf16)
    if sink is not None:
        in_specs.append(pl.BlockSpec((1, d_q), lambda bi, r, n: (0, 0)))
        args.append(sink)
        out_specs, out_shape = o_spec, o_shape
    else:
        out_specs = [o_spec, o_spec]
        out_shape = [o_shape, jax.ShapeDtypeStruct((b, m_len, dil * d_q), f32)]
    out = pl.pallas_call(
        functools.partial(_band_kernel, group=group, with_sink=sink is not None),
        grid=(b, dil, nb), in_specs=in_specs, out_specs=out_specs, out_shape=out_shape,
        compiler_params=_params("parallel", "parallel", "arbitrary"))(*args)
    if sink is not None:
        return out.reshape(b * m_len, dil * d_q)
    return out[0].reshape(b * m_len, dil * d_q), out[1].reshape(b * m_len, dil * d_q)


def _layer_norm(z, g, b):
    mu = z.mean(-1, keepdims=True)
    zc = z - mu
    var = jnp.square(zc).mean(-1, keepdims=True)
    return zc * lax.rsqrt(var + LN_EPS) * g + b


def _proj_ln_kernel(*refs, dils, alpha):
    n_pat = len(dils)
    o_refs = refs[:n_pat]
    lse_refs = refs[n_pat:2 * n_pat] if n_pat > 1 else ()
    n_in = len(o_refs) + len(lse_refs)
    w_ref, x_ref, g_ref, b_ref, out_ref = refs[n_in:n_in + 5]
    scratch = refs[n_in + 5:]
    if n_pat == 1 and len(o_refs[0].shape) == 3:
        groups, _, width = o_refs[0].shape
        y = sum(jnp.dot(o_refs[0][gi], w_ref[gi * width:(gi + 1) * width, :], preferred_element_type=f32)
                for gi in range(groups))
        out_ref[...] = _layer_norm(alpha * x_ref[...] + y, g_ref[...], b_ref[...])
        return
    if n_pat == 1:
        o = o_refs[0][...]
    else:
        kw = w_ref.shape[0]
        mix_ref = scratch[-1]

        def natural(ref, dil, scr, c):
            if dil == 1:
                return ref[:, c * LANE:(c + 1) * LANE].astype(f32)
            rows = ref.shape[0]
            for r in range(dil):
                scr[c, pl.ds(r, rows, stride=dil), :] = (
                    ref[:, r * kw + c * LANE:r * kw + (c + 1) * LANE].astype(f32))
            return scr[c]

        for c in range(kw // LANE):
            lses = [natural(r, dl, scratch[2 * i], c) for i, (r, dl) in enumerate(zip(lse_refs, dils))]
            mx = functools.reduce(jnp.maximum, lses)
            es = [jnp.exp(l - mx) for l in lses]
            num = sum(e * natural(r, dl, scratch[2 * i + 1], c)
                      for i, (e, r, dl) in enumerate(zip(es, o_refs, dils)))
            mix_ref[:, c * LANE:(c + 1) * LANE] = (num / sum(es)).astype(bf16)
        o = mix_ref[...]
    y = jnp.dot(o, w_ref[...], preferred_element_type=f32)
    out_ref[...] = _layer_norm(alpha * x_ref[...] + y, g_ref[...], b_ref[...])


def _proj_ln(os_, lses, dils, w_out, x, g, b, alpha, tm=512):
    t, d = x.shape
    tm = min(tm, t)
    kw = w_out.shape[0]
    row = lambda width: pl.BlockSpec((tm, width), lambda i: (i, 0))
    dilated = lambda dl: pl.BlockSpec((tm // dl, dl * kw), lambda i: (i, 0))
    fixed = lambda shape: pl.BlockSpec(shape, lambda i: (0, 0))
    if os_[0].ndim == 3:
        o_specs = [pl.BlockSpec((os_[0].shape[0], tm, os_[0].shape[2]), lambda i: (0, i, 0))]
    else:
        o_specs = [dilated(dl) for dl in dils]
    in_specs = (o_specs + [dilated(dl) for dl in dils[:len(lses)]]
                + [fixed((kw, d)), row(d), fixed((1, d)), fixed((1, d))])
    scratch = [pltpu.VMEM((kw // LANE, tm, LANE), f32)] * (2 * len(lses))
    if lses:
        scratch.append(pltpu.VMEM((tm, kw), bf16))
    return pl.pallas_call(
        functools.partial(_proj_ln_kernel, dils=tuple(dils), alpha=alpha),
        grid=(t // tm,), in_specs=in_specs, out_specs=row(d),
        out_shape=jax.ShapeDtypeStruct((t, d), f32), scratch_shapes=scratch,
        compiler_params=_params("parallel"))(*os_, *lses, w_out, x, g.reshape(1, d), b.reshape(1, d))


def _mm_dil_kernel(x_ref, w_ref, o_ref, res_ref, *, dil):
    n = w_ref.shape[1]
    res = jnp.dot(x_ref[...].astype(bf16), w_ref[...], preferred_element_type=f32)
    if dil == 1:
        o_ref[...] = res.astype(o_ref.dtype)
        return
    rows = o_ref.shape[0]
    for c in range(n // LANE):
        res_ref[c] = res[:, c * LANE:(c + 1) * LANE]
        for r in range(dil):
            o_ref[:, r * n + c * LANE:r * n + (c + 1) * LANE] = (
                res_ref[c, pl.ds(r, rows, stride=dil), :].astype(o_ref.dtype))


def _mm_dil(x, w, dil, tm=512):
    t, k = x.shape
    n = w.shape[1]
    tm = min(tm, t)
    return pl.pallas_call(
        functools.partial(_mm_dil_kernel, dil=dil), grid=(t // tm,),
        in_specs=[pl.BlockSpec((tm, k), lambda i: (i, 0)), pl.BlockSpec((k, n), lambda i: (0, 0))],
        out_specs=pl.BlockSpec((tm // dil, dil * n), lambda i: (i, 0)),
        out_shape=jax.ShapeDtypeStruct((t // dil, dil * n), bf16),
        scratch_shapes=[pltpu.VMEM((n // LANE, tm, LANE), f32)],
        compiler_params=_params("parallel"))(x, w)


def _rms(c, g):
    return c * lax.rsqrt(jnp.square(c).mean(-1, keepdims=True) + RMS_EPS) * g


def _mla_prep_kernel(x_ref, win_ref, qn_ref, wq_ref, kvn_ref, wkv_ref, cos_ref, sin_ref,
                     q_ref, k_ref, v_ref):
    hw = N_HEADS * LANE
    scale = (MLA_NOPE + MLA_ROPE) ** -0.5
    cos, sin = cos_ref[...], sin_ref[...]
    xw = jnp.dot(x_ref[...].astype(bf16), win_ref[...], preferred_element_type=f32)
    cq = _rms(xw[:, :MLA_Q_RANK], qn_ref[...]).astype(bf16)
    ckv = _rms(xw[:, MLA_Q_RANK:MLA_Q_RANK + MLA_KV_RANK], kvn_ref[...]).astype(bf16)
    off = MLA_Q_RANK + MLA_KV_RANK
    kr = xw[:, off:off + LANE] * cos + xw[:, off + LANE:off + 2 * LANE] * sin
    qq = jnp.dot(cq, wq_ref[...], preferred_element_type=f32)
    kv = jnp.dot(ckv, wkv_ref[...], preferred_element_type=f32)
    ones_lane = lax.broadcasted_iota(jnp.int32, (1, LANE), 1) == MLA_V
    for h in range(N_HEADS):
        blk = slice(h * LANE, (h + 1) * LANE)
        swp = slice(hw + h * LANE, hw + (h + 1) * LANE)
        grp, sub = divmod(h, MLA_GROUP)
        dst = slice(sub * LANE, (sub + 1) * LANE)
        q_ref[grp, :, dst] = ((qq[:, blk] * cos + qq[:, swp] * sin) * scale).astype(bf16)
        k_ref[grp, :, dst] = (kv[:, blk] + kr).astype(bf16)
        v_ref[grp, :, dst] = jnp.where(ones_lane, 1.0, kv[:, swp]).astype(bf16)


def _mla_prep(x, w_in, q_norm, w_uq, kv_norm, w_ukv, seq, tm=512):
    t, d = x.shape
    tm = min(tm, seq)
    hw = N_HEADS * LANE
    half = MLA_ROPE // 2
    dq = MLA_NOPE + MLA_ROPE
    kr_w = w_in[:, MLA_Q_RANK + MLA_KV_RANK:]
    zeros = lambda n: jnp.zeros((d, n), f32)
    kr_a = jnp.concatenate([zeros(MLA_NOPE), kr_w, zeros(LANE - dq)], 1)
    kr_b = jnp.concatenate([zeros(MLA_NOPE), kr_w[:, half:], kr_w[:, :half], zeros(LANE - dq)], 1)
    win = jnp.concatenate([w_in[:, :MLA_Q_RANK + MLA_KV_RANK], kr_a, kr_b], 1).astype(bf16)
    wq3 = w_uq.reshape(MLA_Q_RANK, N_HEADS, dq)
    zq = lambda n: jnp.zeros((MLA_Q_RANK, N_HEADS, n), f32)
    wq_a = jnp.concatenate([wq3, zq(LANE - dq)], 2)
    wq_b = jnp.concatenate([zq(MLA_NOPE), wq3[:, :, MLA_NOPE + half:], wq3[:, :, MLA_NOPE:MLA_NOPE + half],
                            zq(LANE - dq)], 2)
    wq = jnp.concatenate([wq_a.reshape(MLA_Q_RANK, hw), wq_b.reshape(MLA_Q_RANK, hw)], 1).astype(bf16)
    wkv3 = w_ukv.reshape(MLA_KV_RANK, N_HEADS, MLA_NOPE + MLA_V)
    zk = lambda n: jnp.zeros((MLA_KV_RANK, N_HEADS, n), f32)
    wk = jnp.concatenate([wkv3[:, :, :MLA_NOPE], zk(LANE - MLA_NOPE)], 2)
    wv = jnp.concatenate([wkv3[:, :, MLA_NOPE:], zk(LANE - MLA_V)], 2)
    wkv = jnp.concatenate([wk.reshape(MLA_KV_RANK, hw), wv.reshape(MLA_KV_RANK, hw)], 1).astype(bf16)
    freq = ROPE_THETA ** (-jnp.arange(half, dtype=f32) / half)
    ang = jnp.arange(seq)[:, None].astype(f32) * freq[None, :]
    c, s_ = jnp.cos(ang), jnp.sin(ang)
    cos_t = jnp.concatenate([jnp.ones((seq, MLA_NOPE), f32), c, c, jnp.ones((seq, LANE - dq), f32)], 1)
    sin_t = jnp.concatenate([jnp.zeros((seq, MLA_NOPE), f32), -s_, s_, jnp.zeros((seq, LANE - dq), f32)], 1)

    n_pos = seq // tm
    row = lambda width: pl.BlockSpec((tm, width), lambda i: (i, 0))
    fixed = lambda a: pl.BlockSpec(a.shape, lambda i: (0, 0))
    pos = pl.BlockSpec((tm, LANE), lambda i: (i % n_pos, 0))
    qn, kvn = q_norm.reshape(1, -1), kv_norm.reshape(1, -1)
    groups, gw = N_HEADS // MLA_GROUP, MLA_GROUP * LANE
    out_sd = jax.ShapeDtypeStruct((groups, t, gw), bf16)
    out_spec = pl.BlockSpec((groups, tm, gw), lambda i: (0, i, 0))
    return pl.pallas_call(
        _mla_prep_kernel, grid=(t // tm,),
        in_specs=[row(d), fixed(win), fixed(qn), fixed(wq), fixed(kvn), fixed(wkv), pos, pos],
        out_specs=[out_spec] * 3, out_shape=[out_sd] * 3,
        compiler_params=_params("parallel"))(x, win, qn, wq, kvn, wkv, cos_t, sin_t)


def _mla_attn_kernel(q_ref, k_ref, v_ref, o_ref, *, tq, tk, heads):
    qi = pl.program_id(2)
    lanes = [slice(j * LANE, (j + 1) * LANE) for j in range(heads)]
    qs = [q_ref[0, :, ln] for ln in lanes]

    def update(carry, start, mask, r0=0):
        ss = [lax.dot_general(q[r0:], k_ref[0, pl.ds(start, tk), ln], _NT, preferred_element_type=f32)
              for q, ln in zip(qs, lanes)]
        if mask is not None:
            ss = [jnp.where(mask, s, NEG) for s in ss]
        ms = [jnp.maximum(m, s.max(-1, keepdims=True)) for (m, _), s in zip(carry, ss)]
        ps = [jnp.exp(s - m).astype(bf16) for s, m in zip(ss, ms)]
        pvs = [jnp.dot(p, v_ref[0, pl.ds(start, tk), ln], preferred_element_type=f32)
               for p, ln in zip(ps, lanes)]
        return tuple((m_new, jnp.exp(m - m_new) * acc + pv)
                     for (m, acc), m_new, pv in zip(carry, ms, pvs))

    init = tuple((jnp.full((tq, 1), NEG, f32), jnp.zeros((tq, LANE), f32)) for _ in range(heads))
    sub = tq // tk
    def full_tile(i, c):
        for j in range(sub):
            c = update(c, pl.multiple_of(i * tq + j * tk, tk), None)
        return c

    carry = lax.fori_loop(0, qi, full_tile, init)
    for j in range(sub):
        r0 = j * tk
        row = lax.broadcasted_iota(jnp.int32, (tq - r0, tk), 0)
        col = lax.broadcasted_iota(jnp.int32, (tq - r0, tk), 1)
        part = update(tuple((m[r0:], acc[r0:]) for m, acc in carry),
                      pl.multiple_of(qi * tq + r0, tk), col <= row, r0)
        carry = tuple((jnp.concatenate([m[:r0], pm], axis=0), jnp.concatenate([acc[:r0], pa], axis=0))
                      for (m, acc), (pm, pa) in zip(carry, part)) if r0 else part
    for j in range(heads):
        acc = carry[j][1]
        o_ref[0, :, lanes[j]] = (acc / acc[:, MLA_V:MLA_V + 1]).astype(o_ref.dtype)


def _mla_attention(q, k, v, batch, seq, tq=1024, tk=512):
    groups, t, gw = q.shape
    tq = min(tq, seq)
    nq = seq // tq
    qspec = pl.BlockSpec((1, tq, gw), lambda b, h, i: (h, b * nq + i, 0))
    kspec = pl.BlockSpec((1, seq, gw), lambda b, h, i: (h, b, 0))
    return pl.pallas_call(
        functools.partial(_mla_attn_kernel, tq=tq, tk=min(tk, tq), heads=MLA_GROUP),
        grid=(batch, groups, nq),
        in_specs=[qspec, kspec, kspec], out_specs=qspec,
        out_shape=jax.ShapeDtypeStruct((groups, t, gw), bf16),
        compiler_params=_params("parallel", "parallel", "arbitrary"))(q, k, v)


def _batcher_network(n):
    def merge(lo, hi, r):
        step = r * 2
        if step < hi - lo:
            yield from merge(lo, hi, step)
            yield from merge(lo + r, hi, step)
            yield from [(i, i + r) for i in range(lo + r, hi - r, step)]
        else:
            yield (lo, lo + r)

    def sort(lo, hi):
        if hi - lo >= 1:
            mid = lo + (hi - lo) // 2
            yield from sort(lo, mid)
            yield from sort(mid + 1, hi)
            yield from merge(lo, hi, 1)

    return tuple(sort(0, n - 1))


def _top16_values(s):
    n = s.shape[0] // F32_ROWS
    tiles = [s[j * F32_ROWS:(j + 1) * F32_ROWS] for j in range(n)]
    for i, j in _batcher_network(PEER_TOPK):
        if j < n:
            tiles[i], tiles[j] = jnp.maximum(tiles[i], tiles[j]), jnp.minimum(tiles[i], tiles[j])
    vals = []
    for k in range(PEER_TOPK):
        mx = tiles[0].max(axis=0, keepdims=True)
        vals.append(mx)
        hit = tiles[0] == mx
        for j in range(min(n, PEER_TOPK - 1 - k)):
            below = tiles[j + 1] if j + 1 < n else -jnp.inf
            tiles[j] = jnp.where(hit, below, tiles[j])
    return vals


def _top16_pair(s1, s2):
    v1, v2 = _top16_values(s1), _top16_values(s2)
    rank2 = jnp.zeros(s2.shape, f32)
    for k in range(PEER_TOPK):
        rank2 = jnp.where(s2 < v2[k], float(k + 1), rank2)
    return v1, v2, rank2


def _stack_rows(rows):
    n = len(rows)
    rid = lax.broadcasted_iota(jnp.int32, (n, rows[0].shape[1]), 0)
    out = jnp.broadcast_to(rows[0], (n, rows[0].shape[1]))
    for i in range(1, n):
        out = jnp.where(rid == i, rows[i], out)
    return out


def _route_kernel(x_ref, wq_ref, keys_ref, rank2_ref, crow_ref, e1z_ref, e2_ref, *, tm):
    q = jnp.dot(x_ref[...].astype(bf16), wq_ref[...], preferred_element_type=f32).astype(bf16)
    half = PEER_DKEY // 2
    scores = []
    for hh in range(ROUTE_HEADS):
        qh = q[:, hh * PEER_DKEY:(hh + 1) * PEER_DKEY]
        scores.append((lax.dot_general(keys_ref[hh, 0], qh[:, :half], _NT, preferred_element_type=f32),
                       lax.dot_general(keys_ref[hh, 1], qh[:, half:], _NT, preferred_element_type=f32)))
    rid8 = lax.broadcasted_iota(jnp.int32, (8, LANE), 0)
    for hh, c in [(hh, c) for hh in range(ROUTE_HEADS) for c in range(tm // LANE)]:
        lanes = slice(c * LANE, (c + 1) * LANE)
        s1, s2 = scores[hh][0][:, lanes], scores[hh][1][:, lanes]
        v1, v2, rank2 = _top16_pair(s1, s2)
        v2_all = _stack_rows(v2)
        cands = [v1[0] + v2_all]
        for a in range(1, 8):
            cands.append(jnp.where(rid8 < PEER_TOPK // (a + 1), v1[a] + v2_all[:8], -jnp.inf))
        cands.append(_stack_rows(v1[8:]) + v2[0])
        cand = jnp.concatenate(cands, axis=0)
        tau = _top16_values(cand)[-1]
        cmax = v1[0] + v2[0]
        z = jnp.where(cand >= tau, jnp.exp(cand - cmax), 0.0).sum(axis=0, keepdims=True)
        v1_all = _stack_rows(v1)
        counts = jnp.zeros(v1_all.shape, f32)
        for b in range(PEER_TOPK):
            counts = counts + jnp.where(v1_all + v2[b] >= tau, 1.0, 0.0)
        crow = jnp.zeros(s1.shape, f32)
        for a in range(PEER_TOPK):
            crow = jnp.where(s1 == v1[a], counts[a:a + 1], crow)
        rank2_ref[0, hh, :, lanes] = rank2.astype(bf16)
        crow_ref[0, hh, :, lanes] = crow
        e1z_ref[0, hh, :, lanes] = jnp.exp(s1 - v1[0]) * (GELU_FOLD / z)
        e2_ref[0, hh, :, lanes] = jnp.exp(s2 - v2[0]).astype(bf16)


def _peer_route(x, w_q, keys, tm=ROUTE_TOKENS):
    t, d = x.shape
    tm = min(tm, t)
    hb = ROUTE_HEADS
    spec = pl.BlockSpec((1, hb, PEER_KEYS, tm), lambda i, h: (i, h, 0, 0))
    sd = lambda dt: jax.ShapeDtypeStruct((t // tm, PEER_HEADS, PEER_KEYS, tm), dt)
    return pl.pallas_call(
        functools.partial(_route_kernel, tm=tm), grid=(t // tm, PEER_HEADS // hb),
        in_specs=[pl.BlockSpec((tm, d), lambda i, h: (i, 0)),
                  pl.BlockSpec((d, hb * PEER_DKEY), lambda i, h: (0, h)),
                  pl.BlockSpec((hb, 2, PEER_KEYS, PEER_DKEY // 2), lambda i, h: (h, 0, 0, 0))],
        out_specs=[spec] * 4, out_shape=[sd(bf16), sd(f32), sd(f32), sd(bf16)],
        compiler_params=_params("parallel", "arbitrary"))(x, w_q, keys)


def _peer_dense_kernel(x_ref, u0_ref, u1_ref, vt0_ref, vt1_ref, rank2_in, crow_ref, e1z_ref, e2_in,
                       g_ref, b_ref, o_ref, acc_ref, xt_ref, ht0, ht1, gt0, gt1, rank2_ref, e2_ref,
                       *, rows, tb, lc, mw, alpha):
    k = pl.program_id(1)
    last = pl.num_programs(1) - 1
    ec = rows * PEER_KEYS
    zero = jnp.zeros((), bf16)
    kt = PEER_KEYS // BF16_ROWS

    def up(half, ht, cw):
        lanes = slice(cw * mw, (cw + 1) * mw)
        u_blk = pltpu.bitcast((u0_ref, u1_ref)[half][0], bf16)
        ht[:, lanes] = jnp.dot(u_blk, xt_ref[:, lanes], preferred_element_type=f32)

    n_tiles = last

    def gate(ht, tile, half, gt, cw):
        base = pl.multiple_of(jnp.clip(tile, 0, n_tiles - 1) * F32_ROWS, F32_ROWS)
        for r in range(rows):
            keys = slice(r * PEER_KEYS, (r + 1) * PEER_KEYS)
            row = slice(half * rows + r, half * rows + r + 1)
            for c in range(cw * mw // lc, (cw + 1) * mw // lc):
                lanes = slice(c * lc, (c + 1) * lc)
                w = None
                for h in range(PEER_HEADS):
                    cr = crow_ref[0, h, pl.ds(base, F32_ROWS), lanes][row]
                    ez = e1z_ref[0, h, pl.ds(base, F32_ROWS), lanes][row]
                    cr = jnp.broadcast_to(cr, (BF16_ROWS, lc)).astype(bf16)[None]
                    ez = jnp.broadcast_to(ez, (BF16_ROWS, lc)).astype(bf16)[None]
                    term = jnp.where(rank2_ref[h, :, :, lanes] < cr, e2_ref[h, :, :, lanes] * ez, zero)
                    w = term if w is None else w + term
                hv = ht[keys, lanes]
                g = hv * (1.0 + lax.erf(hv))
                gt[r * kt:(r + 1) * kt, :, lanes] = g.astype(bf16).reshape(kt, BF16_ROWS, lc) * w

    def down(half, gt, cw):
        lanes = slice(cw * mw, (cw + 1) * mw)
        vt_blk = pltpu.bitcast((vt0_ref, vt1_ref)[half][0, 0], bf16)
        acc_ref[:, lanes] += jnp.dot(vt_blk, gt[:, :, lanes].reshape(ec, mw), preferred_element_type=f32)

    chunks = range(tb // mw)

    @pl.when(k == 0)
    def _():
        acc_ref[...] = jnp.zeros_like(acc_ref)
        xt_ref[...] = x_ref[...].T.astype(bf16)
        rank2_ref[...] = rank2_in[0]
        e2_ref[...] = e2_in[0]
        for cw in chunks:
            up(0, ht0, cw)
            up(1, ht1, cw)
        for cw in chunks:
            gate(ht0, k, 0, gt0, cw)

    @pl.when(jnp.logical_and(k > 0, k < last))
    def _():
        for cw in chunks:
            up(0, ht0, cw)
            gate(ht1, k - 1, 1, gt1, cw)
            down(0, gt0, cw)
        for cw in chunks:
            up(1, ht1, cw)
            gate(ht0, k, 0, gt0, cw)
            down(1, gt1, cw)

    @pl.when(k == last)
    def _():
        for cw in chunks:
            gate(ht1, k - 1, 1, gt1, cw)
            down(0, gt0, cw)
        for cw in chunks:
            down(1, gt1, cw)
        z = alpha * x_ref[...] + acc_ref[...].T
        o_ref[...] = _layer_norm(z, g_ref[...], b_ref[...])


def _peer_dense(x, u, vt, layer, routing, g, b, alpha, lc=128):
    t, d = x.shape
    e = 2 * u.shape[1]
    ec = DENSE_EC
    rows = ec // PEER_KEYS
    assert 2 * rows == F32_ROWS, "a pair of expert blocks must span one 8-row f32 tile of sub-key-1 rows"
    nk = e // (2 * ec)
    kt = PEER_KEYS // BF16_ROWS
    rank2, crow, e1z, e2 = routing
    nt, _, _, tb = crow.shape
    rank2, e2 = (a.reshape(nt, PEER_HEADS, kt, BF16_ROWS, tb) for a in (rank2, e2))
    full = pl.BlockSpec((1, PEER_HEADS, kt, BF16_ROWS, tb), lambda i, k: (i, 0, 0, 0, 0))
    rowtab = pl.BlockSpec((1, PEER_HEADS, PEER_KEYS, tb), lambda i, k: (i, 0, 0, 0))
    fixed = pl.BlockSpec((1, d), lambda i, k: (0, 0))
    xspec = pl.BlockSpec((tb, d), lambda i, k: (i, 0))
    gt_scr = pltpu.VMEM((ec // BF16_ROWS, BF16_ROWS, tb), bf16)
    u_blk = lambda half: pl.BlockSpec(
        (1, ec // 2, d), lambda i, k: (layer, 2 * jnp.minimum(k, nk - 1) + half, 0))
    vt_blk = lambda half: pl.BlockSpec(
        (1, 1, d // 2, ec), lambda i, k: (layer, 2 * jnp.maximum(k - 1, 0) + half, 0, 0))
    return pl.pallas_call(
        functools.partial(_peer_dense_kernel, rows=rows, tb=tb, lc=lc, mw=MXU_WIDTH, alpha=alpha),
        grid=(t // tb, nk + 1),
        in_specs=[xspec, u_blk(0), u_blk(1), vt_blk(0), vt_blk(1),
                  full, rowtab, rowtab, full, fixed, fixed],
        out_specs=xspec, out_shape=jax.ShapeDtypeStruct((t, d), f32),
        scratch_shapes=[pltpu.VMEM((d, tb), f32), pltpu.VMEM((d, tb), bf16),
                        pltpu.VMEM((ec, tb), f32), pltpu.VMEM((ec, tb), f32), gt_scr, gt_scr,
                        pltpu.VMEM((PEER_HEADS, kt, BF16_ROWS, tb), bf16),
                        pltpu.VMEM((PEER_HEADS, kt, BF16_ROWS, tb), bf16)],
        compiler_params=_params("parallel", "arbitrary"))(
            x, u, u, vt, vt, rank2, crow, e1z, e2, g.reshape(1, d), b.reshape(1, d))


def _swa_layer(x, batch, seq, w_in, sinks, w_out, bias, g, b, alpha):
    feat = (N_HEADS + 2 * SWA_KV_HEADS) * HEAD_DIM
    d_q = N_HEADS * HEAD_DIM
    kvw = SWA_KV_HEADS * HEAD_DIM
    proj = _mm(x, w_in.astype(bf16), bf16, 512, feat).reshape(batch, seq, feat)
    sink = jnp.repeat(sinks.astype(f32), HEAD_DIM).reshape(1, d_q)
    o = _band_attention(proj, bias, 1, feat, 0, d_q // kvw, d_q // kvw + 1, kvw,
                        N_HEADS // SWA_KV_HEADS, sink)
    return _proj_ln([o], [], [1], w_out.astype(bf16), x, g, b, alpha)


def _dil_layer(x, batch, seq, w_in, w_out, biases, g, b, alpha):
    d_q = N_HEADS * HEAD_DIM
    feat = 3 * d_q
    w_in = w_in.astype(bf16)
    outs, lses, dils = [], [], []
    for gi, (window, dil) in enumerate(DIL_PATTERNS):
        proj = _mm_dil(x, w_in[:, gi * feat:(gi + 1) * feat], dil).reshape(batch, seq // dil, dil * feat)
        o, lse = _band_attention(proj, biases[gi], dil, feat, 0, 1, 2, d_q, 1)
        outs.append(o)
        lses.append(lse)
        dils.append(dil)
    return _proj_ln(outs, lses, dils, w_out.astype(bf16), x, g, b, alpha)


def _mla_layer(x, batch, seq, w_in, q_norm, w_uq, kv_norm, w_ukv, w_out, g, b, alpha):
    q, k, v = _mla_prep(x, w_in, q_norm, w_uq, kv_norm, w_ukv, seq)
    o = _mla_attention(q, k, v, batch, seq)
    d = w_out.shape[1]
    w3 = w_out.reshape(N_HEADS, MLA_V, d)
    w_pad = jnp.concatenate([w3, jnp.zeros((N_HEADS, LANE - MLA_V, d), f32)], 1)
    return _proj_ln([o], [], [1], w_pad.reshape(N_HEADS * LANE, d).astype(bf16), x, g, b, alpha)


def _pack_u_kernel(u_ref, o_ref):
    o_ref[0] = pltpu.bitcast((u_ref[0] * GELU_FOLD).astype(bf16), jnp.uint32)


def _pack_vt_kernel(v_ref, o_ref):
    o_ref[0, 0] = pltpu.bitcast(v_ref[0].T.astype(bf16), jnp.uint32)


def _pack_expert_weights(u, v):
    depth, e, d = u.shape
    ec = DENSE_EC
    src = pl.BlockSpec((1, ec, d), lambda l, j: (l, j, 0))
    up = pl.pallas_call(
        _pack_u_kernel, grid=(depth, e // ec), in_specs=[src],
        out_specs=pl.BlockSpec((1, ec // 2, d), lambda l, j: (l, j, 0)),
        out_shape=jax.ShapeDtypeStruct((depth, e // 2, d), jnp.uint32),
        compiler_params=_params("parallel", "parallel"))(u)
    vtp = pl.pallas_call(
        _pack_vt_kernel, grid=(depth, e // ec), in_specs=[src],
        out_specs=pl.BlockSpec((1, 1, d // 2, ec), lambda l, j: (l, j, 0, 0)),
        out_shape=jax.ShapeDtypeStruct((depth, e // ec, d // 2, ec), jnp.uint32),
        compiler_params=_params("parallel", "parallel"))(v)
    return up, vtp


def _peer_layer(x, w_q, keys, u_packed, vt_packed, layer, g, b, alpha):
    routing = _peer_route(x, w_q.astype(bf16), keys.astype(bf16))
    return _peer_dense(x, u_packed, vt_packed, layer, routing, g, b, alpha)


def kernel(x, rel_bias, ln_g, ln_b, swa_w_in, swa_sinks, swa_w_out, dil_w_in, dil_w_out,
           mla_w_in, mla_q_norm, mla_w_uq, mla_kv_norm, mla_w_ukv, mla_w_out,
           peer_w_q, peer_keys, peer_u, peer_v):
    batch, seq, d = x.shape
    depth = ln_g.shape[0]
    alpha = (2 * depth) ** 0.25
    assert seq % (DIL_PATTERNS[-1][1] * BLOCK) == 0, "sequence must be a whole number of dilation segments"
    swa_bias = _band_bias(rel_bias, SWA_WINDOW - 1, 1)
    dil_bias = [_band_bias(rel_bias, window // dil, dil) for window, dil in DIL_PATTERNS]
    u_packed, vt_packed = _pack_expert_weights(peer_u, peer_v)
    h = x.reshape(batch * seq, d)
    for i in range(depth):
        kind, j = i % 3, i // 3
        if kind == 0:
            h = _swa_layer(h, batch, seq, swa_w_in[j], swa_sinks[j], swa_w_out[j], swa_bias,
                           ln_g[i, 0], ln_b[i, 0], alpha)
        elif kind == 1:
            h = _dil_layer(h, batch, seq, dil_w_in[j], dil_w_out[j], dil_bias,
                           ln_g[i, 0], ln_b[i, 0], alpha)
        else:
            h = _mla_layer(h, batch, seq, mla_w_in[j], mla_q_norm[j], mla_w_uq[j], mla_kv_norm[j],
                           mla_w_ukv[j], mla_w_out[j], ln_g[i, 0], ln_b[i, 0], alpha)
        h = _peer_layer(h, peer_w_q[i], peer_keys[i], u_packed, vt_packed, i, ln_g[i, 1], ln_b[i, 1], alpha)
    return h.reshape(batch, seq, d)
```

```python
import functools
import math

import jax
import jax.numpy as jnp
from jax import lax
from jax.experimental import pallas as pl
from jax.experimental.pallas import tpu as pltpu

f32 = jnp.float32
bf16 = jnp.bfloat16

N_HEADS = 16
HEAD_DIM = 64
BLOCK = 128
SWA_KV_HEADS = 2
SWA_WINDOW = 128
DIL_PATTERNS = ((128, 1), (512, 4), (2048, 16))
MLA_Q_RANK = 256
MLA_KV_RANK = 128
MLA_NOPE = 64
MLA_ROPE = 32
MLA_V = 64
ROPE_THETA = 10000.0
REL_BUCKETS = 32
REL_MAX_DIST = 2048
PEER_HEADS = 8
PEER_KEYS = 128
PEER_DKEY = 256
PEER_TOPK = 16
LN_EPS = 1e-5
RMS_EPS = 1e-6
NEG = -1e30

LANE = 128
BF16_ROWS = 16
F32_ROWS = 8
MXU_WIDTH = 256
MLA_GROUP = 2
ROUTE_HEADS = 4
GELU_FOLD = 0.5 ** 0.5
ROUTE_TOKENS = 512
DENSE_EC = 4 * PEER_KEYS
VMEM_LIMIT = 56 * 1024 * 1024

_NT = (((1,), (1,)), ((), ()))


def _params(*sem):
    return pltpu.CompilerParams(dimension_semantics=sem, vmem_limit_bytes=VMEM_LIMIT)


def _mm_kernel(x_ref, w_ref, o_ref):
    o_ref[...] = jnp.dot(x_ref[...].astype(bf16), w_ref[...],
                         preferred_element_type=f32).astype(o_ref.dtype)


def _mm(x, w, out_dtype, tm, tn):
    m, k = x.shape
    n = w.shape[1]
    tm, tn = min(tm, m), min(tn, n)
    return pl.pallas_call(
        _mm_kernel, grid=(m // tm, n // tn),
        in_specs=[pl.BlockSpec((tm, k), lambda i, j: (i, 0)),
                  pl.BlockSpec((k, tn), lambda i, j: (0, j))],
        out_specs=pl.BlockSpec((tm, tn), lambda i, j: (i, j)),
        out_shape=jax.ShapeDtypeStruct((m, n), out_dtype),
        compiler_params=_params("parallel", "parallel"))(x, w)


def _rel_bucket(dist):
    max_exact = REL_BUCKETS // 2
    n = jnp.maximum(dist, 0)
    nf = jnp.maximum(n, 1).astype(f32)
    large = max_exact + (jnp.log(nf / max_exact) / math.log(REL_MAX_DIST / max_exact)
                         * (REL_BUCKETS - max_exact)).astype(jnp.int32)
    large = jnp.minimum(large, REL_BUCKETS - 1)
    return jnp.where(n < max_exact, n, large)


def _bias_kernel(rel_ref, bucket_ref, valid_ref, o_ref):
    bucket = bucket_ref[...]
    for h in range(N_HEADS):
        acc = jnp.zeros(bucket.shape, f32)
        for b in range(REL_BUCKETS):
            acc = jnp.where(bucket == b, rel_ref[b, h], acc)
        for variant in range(2):
            o_ref[variant, h] = jnp.where(valid_ref[variant] != 0, acc, NEG)


def _band_bias(rel_bias, max_dist, dilation):
    qi = jnp.arange(BLOCK)[:, None]
    kj = jnp.arange(2 * BLOCK)[None, :]
    dist = BLOCK + qi - kj
    bucket = _rel_bucket(dist * dilation).astype(jnp.int32)
    valid = (dist >= 0) & (dist <= max_dist)
    valid = jnp.stack([valid & (kj >= BLOCK), valid]).astype(jnp.int32)
    return pl.pallas_call(
        _bias_kernel,
        in_specs=[pl.BlockSpec(memory_space=pltpu.SMEM),
                  pl.BlockSpec(memory_space=pltpu.VMEM),
                  pl.BlockSpec(memory_space=pltpu.VMEM)],
        out_specs=pl.BlockSpec(memory_space=pltpu.VMEM),
        out_shape=jax.ShapeDtypeStruct((2, N_HEADS, BLOCK, 2 * BLOCK), f32))(rel_bias, bucket, valid)


def _band_kernel(q_ref, kp_ref, kc_ref, vp_ref, vc_ref, bias_ref, *rest, group, with_sink):
    if with_sink:
        sink_ref, o_ref = rest
    else:
        o_ref, lse_ref = rest
    scale = HEAD_DIM ** -0.5
    pair = LANE // HEAD_DIM
    lo = lax.broadcasted_iota(jnp.int32, (1, LANE), 1) < HEAD_DIM
    zero = jnp.zeros((), bf16)

    kv_blocks = {}

    def kv_block(blk, swapped):
        if (blk, swapped) not in kv_blocks:
            cols = slice(blk * LANE, (blk + 1) * LANE)
            k = jnp.concatenate([kp_ref[0, :, cols], kc_ref[0, :, cols]], axis=0)
            v = jnp.concatenate([vp_ref[0, :, cols], vc_ref[0, :, cols]], axis=0)
            if swapped:
                k = jnp.concatenate([k[:, HEAD_DIM:], k[:, :HEAD_DIM]], axis=1)
                v = jnp.concatenate([v[:, HEAD_DIM:], v[:, :HEAD_DIM]], axis=1)
            kv_blocks[(blk, swapped)] = (k, v)
        return kv_blocks[(blk, swapped)]

    heads = range(N_HEADS)
    qp = [q_ref[0, :, p * LANE:(p + 1) * LANE] for p in range(N_HEADS // pair)]
    qm = [jnp.where(lo if h % pair == 0 else ~lo, qp[h // pair], zero) for h in heads]
    kvs = [kv_block((h // group) // pair, (h // group) % pair != h % pair) for h in heads]
    ss = [lax.dot_general(q, kv[0], _NT, preferred_element_type=f32) * scale + bias_ref[0, h]
          for h, q, kv in zip(heads, qm, kvs)]
    ms = [s.max(-1, keepdims=True) for s in ss]
    ps = [jnp.exp(s - m) for s, m in zip(ss, ms)]
    ls = [p.sum(-1, keepdims=True) for p in ps]
    os_ = [jnp.dot(p.astype(bf16), kv[1], preferred_element_type=f32) / l for p, kv, l in zip(ps, kvs, ls)]
    lses = [jnp.broadcast_to(m + jnp.log(l), (BLOCK, LANE)) for m, l in zip(ms, ls)]
    for p in range(N_HEADS // pair):
        cols = slice(p * LANE, (p + 1) * LANE)
        o = jnp.where(lo, os_[pair * p], os_[pair * p + 1])
        lse = jnp.where(lo, lses[pair * p], lses[pair * p + 1])
        if with_sink:
            o = o * jax.nn.sigmoid(lse - sink_ref[:, cols])
        else:
            lse_ref[0, :, cols] = lse
        o_ref[0, :, cols] = o.astype(o_ref.dtype)


def _band_attention(view, bias, dil, feat, q_col, k_col, v_col, kv_width, group, sink=None):
    b, m_len, _ = view.shape
    d_q = N_HEADS * HEAD_DIM
    nb = m_len // BLOCK
    qpr, kpr = feat // d_q, feat // kv_width

    def cur(col, per_row):
        return lambda bi, r, n: (bi, n, r * per_row + col)

    def prev(col, per_row):
        return lambda bi, r, n: (bi, jnp.maximum(n - 1, 0), r * per_row + col)

    in_specs = [pl.BlockSpec((1, BLOCK, d_q), cur(q_col, qpr)),
                pl.BlockSpec((1, BLOCK, kv_width), prev(k_col, kpr)),
                pl.BlockSpec((1, BLOCK, kv_width), cur(k_col, kpr)),
                pl.BlockSpec((1, BLOCK, kv_width), prev(v_col, kpr)),
                pl.BlockSpec((1, BLOCK, kv_width), cur(v_col, kpr)),
                pl.BlockSpec((1, N_HEADS, BLOCK, 2 * BLOCK),
                             lambda bi, r, n: (jnp.minimum(n, 1), 0, 0, 0))]
    args = [view, view, view, view, view, bias]
    o_spec = pl.BlockSpec((1, BLOCK, d_q), lambda bi, r, n: (bi, n, r))
    o_shape = jax.ShapeDtypeStruct((b, m_len, dil * d_q), bf16)
    if sink is not None:
        in_specs.append(pl.BlockSpec((1, d_q), lambda bi, r, n: (0, 0)))
        args.append(sink)
        out_specs, out_shape = o_spec, o_shape
    else:
        out_specs = [o_spec, o_spec]
        out_shape = [o_shape, jax.ShapeDtypeStruct((b, m_len, dil * d_q), f32)]
    out = pl.pallas_call(
        functools.partial(_band_kernel, group=group, with_sink=sink is not None),
        grid=(b, dil, nb), in_specs=in_specs, out_specs=out_specs, out_shape=out_shape,
        compiler_params=_params("parallel", "parallel", "arbitrary"))(*args)
    if sink is not None:
        return out.reshape(b * m_len, dil * d_q)
    return out[0].reshape(b * m_len, dil * d_q), out[1].reshape(b * m_len, dil * d_q)


def _layer_norm(z, g, b):
    mu = z.mean(-1, keepdims=True)
    zc = z - mu
    var = jnp.square(zc).mean(-1, keepdims=True)
    return zc * lax.rsqrt(var + LN_EPS) * g + b


def _proj_ln_kernel(*refs, dils, alpha):
    n_pat = len(dils)
    o_refs = refs[:n_pat]
    lse_refs = refs[n_pat:2 * n_pat] if n_pat > 1 else ()
    n_in = len(o_refs) + len(lse_refs)
    w_ref, x_ref, g_ref, b_ref, out_ref = refs[n_in:n_in + 5]
    scratch = refs[n_in + 5:]
    if n_pat == 1 and len(o_refs[0].shape) == 3:
        groups, _, width = o_refs[0].shape
        y = sum(jnp.dot(o_refs[0][gi], w_ref[gi * width:(gi + 1) * width, :], preferred_element_type=f32)
                for gi in range(groups))
        out_ref[...] = _layer_norm(alpha * x_ref[...] + y, g_ref[...], b_ref[...])
        return
    if n_pat == 1:
        o = o_refs[0][...]
    else:
        kw = w_ref.shape[0]
        mix_ref = scratch[-1]

        def natural(ref, dil, scr, c):
            if dil == 1:
                return ref[:, c * LANE:(c + 1) * LANE].astype(f32)
            rows = ref.shape[0]
            for r in range(dil):
                scr[c, pl.ds(r, rows, stride=dil), :] = (
                    ref[:, r * kw + c * LANE:r * kw + (c + 1) * LANE].astype(f32))
            return scr[c]

        for c in range(kw // LANE):
            lses = [natural(r, dl, scratch[2 * i], c) for i, (r, dl) in enumerate(zip(lse_refs, dils))]
            mx = functools.reduce(jnp.maximum, lses)
            es = [jnp.exp(l - mx) for l in lses]
            num = sum(e * natural(r, dl, scratch[2 * i + 1], c)
                      for i, (e, r, dl) in enumerate(zip(es, o_refs, dils)))
            mix_ref[:, c * LANE:(c + 1) * LANE] = (num / sum(es)).astype(bf16)
        o = mix_ref[...]
    y = jnp.dot(o, w_ref[...], preferred_element_type=f32)
    out_ref[...] = _layer_norm(alpha * x_ref[...] + y, g_ref[...], b_ref[...])


def _proj_ln(os_, lses, dils, w_out, x, g, b, alpha, tm=512):
    t, d = x.shape
    tm = min(tm, t)
    kw = w_out.shape[0]
    row = lambda width: pl.BlockSpec((tm, width), lambda i: (i, 0))
    dilated = lambda dl: pl.BlockSpec((tm // dl, dl * kw), lambda i: (i, 0))
    fixed = lambda shape: pl.BlockSpec(shape, lambda i: (0, 0))
    if os_[0].ndim == 3:
        o_specs = [pl.BlockSpec((os_[0].shape[0], tm, os_[0].shape[2]), lambda i: (0, i, 0))]
    else:
        o_specs = [dilated(dl) for dl in dils]
    in_specs = (o_specs + [dilated(dl) for dl in dils[:len(lses)]]
                + [fixed((kw, d)), row(d), fixed((1, d)), fixed((1, d))])
    scratch = [pltpu.VMEM((kw // LANE, tm, LANE), f32)] * (2 * len(lses))
    if lses:
        scratch.append(pltpu.VMEM((tm, kw), bf16))
    return pl.pallas_call(
        functools.partial(_proj_ln_kernel, dils=tuple(dils), alpha=alpha),
        grid=(t // tm,), in_specs=in_specs, out_specs=row(d),
        out_shape=jax.ShapeDtypeStruct((t, d), f32), scratch_shapes=scratch,
        compiler_params=_params("parallel"))(*os_, *lses, w_out, x, g.reshape(1, d), b.reshape(1, d))


def _mm_dil_kernel(x_ref, w_ref, o_ref, res_ref, *, dil):
    n = w_ref.shape[1]
    res = jnp.dot(x_ref[...].astype(bf16), w_ref[...], preferred_element_type=f32)
    if dil == 1:
        o_ref[...] = res.astype(o_ref.dtype)
        return
    rows = o_ref.shape[0]
    for c in range(n // LANE):
        res_ref[c] = res[:, c * LANE:(c + 1) * LANE]
        for r in range(dil):
            o_ref[:, r * n + c * LANE:r * n + (c + 1) * LANE] = (
                res_ref[c, pl.ds(r, rows, stride=dil), :].astype(o_ref.dtype))


def _mm_dil(x, w, dil, tm=512):
    t, k = x.shape
    n = w.shape[1]
    tm = min(tm, t)
    return pl.pallas_call(
        functools.partial(_mm_dil_kernel, dil=dil), grid=(t // tm,),
        in_specs=[pl.BlockSpec((tm, k), lambda i: (i, 0)), pl.BlockSpec((k, n), lambda i: (0, 0))],
        out_specs=pl.BlockSpec((tm // dil, dil * n), lambda i: (i, 0)),
        out_shape=jax.ShapeDtypeStruct((t // dil, dil * n), bf16),
        scratch_shapes=[pltpu.VMEM((n // LANE, tm, LANE), f32)],
        compiler_params=_params("parallel"))(x, w)


def _rms(c, g):
    return c * lax.rsqrt(jnp.square(c).mean(-1, keepdims=True) + RMS_EPS) * g


def _mla_prep_kernel(x_ref, win_ref, qn_ref, wq_ref, kvn_ref, wkv_ref, cos_ref, sin_ref,
                     q_ref, k_ref, v_ref):
    hw = N_HEADS * LANE
    scale = (MLA_NOPE + MLA_ROPE) ** -0.5
    cos, sin = cos_ref[...], sin_ref[...]
    xw = jnp.dot(x_ref[...].astype(bf16), win_ref[...], preferred_element_type=f32)
    cq = _rms(xw[:, :MLA_Q_RANK], qn_ref[...]).astype(bf16)
    ckv = _rms(xw[:, MLA_Q_RANK:MLA_Q_RANK + MLA_KV_RANK], kvn_ref[...]).astype(bf16)
    off = MLA_Q_RANK + MLA_KV_RANK
    kr = xw[:, off:off + LANE] * cos + xw[:, off + LANE:off + 2 * LANE] * sin
    qq = jnp.dot(cq, wq_ref[...], preferred_element_type=f32)
    kv = jnp.dot(ckv, wkv_ref[...], preferred_element_type=f32)
    ones_lane = lax.broadcasted_iota(jnp.int32, (1, LANE), 1) == MLA_V
    for h in range(N_HEADS):
        blk = slice(h * LANE, (h + 1) * LANE)
        swp = slice(hw + h * LANE, hw + (h + 1) * LANE)
        grp, sub = divmod(h, MLA_GROUP)
        dst = slice(sub * LANE, (sub + 1) * LANE)
        q_ref[grp, :, dst] = ((qq[:, blk] * cos + qq[:, swp] * sin) * scale).astype(bf16)
        k_ref[grp, :, dst] = (kv[:, blk] + kr).astype(bf16)
        v_ref[grp, :, dst] = jnp.where(ones_lane, 1.0, kv[:, swp]).astype(bf16)


def _mla_prep(x, w_in, q_norm, w_uq, kv_norm, w_ukv, seq, tm=512):
    t, d = x.shape
    tm = min(tm, seq)
    hw = N_HEADS * LANE
    half = MLA_ROPE // 2
    dq = MLA_NOPE + MLA_ROPE
    kr_w = w_in[:, MLA_Q_RANK + MLA_KV_RANK:]
    zeros = lambda n: jnp.zeros((d, n), f32)
    kr_a = jnp.concatenate([zeros(MLA_NOPE), kr_w, zeros(LANE - dq)], 1)
    kr_b = jnp.concatenate([zeros(MLA_NOPE), kr_w[:, half:], kr_w[:, :half], zeros(LANE - dq)], 1)
    win = jnp.concatenate([w_in[:, :MLA_Q_RANK + MLA_KV_RANK], kr_a, kr_b], 1).astype(bf16)
    wq3 = w_uq.reshape(MLA_Q_RANK, N_HEADS, dq)
    zq = lambda n: jnp.zeros((MLA_Q_RANK, N_HEADS, n), f32)
    wq_a = jnp.concatenate([wq3, zq(LANE - dq)], 2)
    wq_b = jnp.concatenate([zq(MLA_NOPE), wq3[:, :, MLA_NOPE + half:], wq3[:, :, MLA_NOPE:MLA_NOPE + half],
                            zq(LANE - dq)], 2)
    wq = jnp.concatenate([wq_a.reshape(MLA_Q_RANK, hw), wq_b.reshape(MLA_Q_RANK, hw)], 1).astype(bf16)
    wkv3 = w_ukv.reshape(MLA_KV_RANK, N_HEADS, MLA_NOPE + MLA_V)
    zk = lambda n: jnp.zeros((MLA_KV_RANK, N_HEADS, n), f32)
    wk = jnp.concatenate([wkv3[:, :, :MLA_NOPE], zk(LANE - MLA_NOPE)], 2)
    wv = jnp.concatenate([wkv3[:, :, MLA_NOPE:], zk(LANE - MLA_V)], 2)
    wkv = jnp.concatenate([wk.reshape(MLA_KV_RANK, hw), wv.reshape(MLA_KV_RANK, hw)], 1).astype(bf16)
    freq = ROPE_THETA ** (-jnp.arange(half, dtype=f32) / half)
    ang = jnp.arange(seq)[:, None].astype(f32) * freq[None, :]
    c, s_ = jnp.cos(ang), jnp.sin(ang)
    cos_t = jnp.concatenate([jnp.ones((seq, MLA_NOPE), f32), c, c, jnp.ones((seq, LANE - dq), f32)], 1)
    sin_t = jnp.concatenate([jnp.zeros((seq, MLA_NOPE), f32), -s_, s_, jnp.zeros((seq, LANE - dq), f32)], 1)

    n_pos = seq // tm
    row = lambda width: pl.BlockSpec((tm, width), lambda i: (i, 0))
    fixed = lambda a: pl.BlockSpec(a.shape, lambda i: (0, 0))
    pos = pl.BlockSpec((tm, LANE), lambda i: (i % n_pos, 0))
    qn, kvn = q_norm.reshape(1, -1), kv_norm.reshape(1, -1)
    groups, gw = N_HEADS // MLA_GROUP, MLA_GROUP * LANE
    out_sd = jax.ShapeDtypeStruct((groups, t, gw), bf16)
    out_spec = pl.BlockSpec((groups, tm, gw), lambda i: (0, i, 0))
    return pl.pallas_call(
        _mla_prep_kernel, grid=(t // tm,),
        in_specs=[row(d), fixed(win), fixed(qn), fixed(wq), fixed(kvn), fixed(wkv), pos, pos],
        out_specs=[out_spec] * 3, out_shape=[out_sd] * 3,
        compiler_params=_params("parallel"))(x, win, qn, wq, kvn, wkv, cos_t, sin_t)


def _mla_attn_kernel(q_ref, k_ref, v_ref, o_ref, *, tq, tk, heads):
    qi = pl.program_id(2)
    lanes = [slice(j * LANE, (j + 1) * LANE) for j in range(heads)]
    qs = [q_ref[0, :, ln] for ln in lanes]

    def update(carry, start, mask, r0=0):
        ss = [lax.dot_general(q[r0:], k_ref[0, pl.ds(start, tk), ln], _NT, preferred_element_type=f32)
              for q, ln in zip(qs, lanes)]
        if mask is not None:
            ss = [jnp.where(mask, s, NEG) for s in ss]
        ms = [jnp.maximum(m, s.max(-1, keepdims=True)) for (m, _), s in zip(carry, ss)]
        ps = [jnp.exp(s - m).astype(bf16) for s, m in zip(ss, ms)]
        pvs = [jnp.dot(p, v_ref[0, pl.ds(start, tk), ln], preferred_element_type=f32)
               for p, ln in zip(ps, lanes)]
        return tuple((m_new, jnp.exp(m - m_new) * acc + pv)
                     for (m, acc), m_new, pv in zip(carry, ms, pvs))

    init = tuple((jnp.full((tq, 1), NEG, f32), jnp.zeros((tq, LANE), f32)) for _ in range(heads))
    sub = tq // tk
    def full_tile(i, c):
        for j in range(sub):
            c = update(c, pl.multiple_of(i * tq + j * tk, tk), None)
        return c

    carry = lax.fori_loop(0, qi, full_tile, init)
    for j in range(sub):
        r0 = j * tk
        row = lax.broadcasted_iota(jnp.int32, (tq - r0, tk), 0)
        col = lax.broadcasted_iota(jnp.int32, (tq - r0, tk), 1)
        part = update(tuple((m[r0:], acc[r0:]) for m, acc in carry),
                      pl.multiple_of(qi * tq + r0, tk), col <= row, r0)
        carry = tuple((jnp.concatenate([m[:r0], pm], axis=0), jnp.concatenate([acc[:r0], pa], axis=0))
                      for (m, acc), (pm, pa) in zip(carry, part)) if r0 else part
    for j in range(heads):
        acc = carry[j][1]
        o_ref[0, :, lanes[j]] = (acc / acc[:, MLA_V:MLA_V + 1]).astype(o_ref.dtype)


def _mla_attention(q, k, v, batch, seq, tq=1024, tk=512):
    groups, t, gw = q.shape
    tq = min(tq, seq)
    nq = seq // tq
    qspec = pl.BlockSpec((1, tq, gw), lambda b, h, i: (h, b * nq + i, 0))
    kspec = pl.BlockSpec((1, seq, gw), lambda b, h, i: (h, b, 0))
    return pl.pallas_call(
        functools.partial(_mla_attn_kernel, tq=tq, tk=min(tk, tq), heads=MLA_GROUP),
        grid=(batch, groups, nq),
        in_specs=[qspec, kspec, kspec], out_specs=qspec,
        out_shape=jax.ShapeDtypeStruct((groups, t, gw), bf16),
        compiler_params=_params("parallel", "parallel", "arbitrary"))(q, k, v)


def _batcher_network(n):
    def merge(lo, hi, r):
        step = r * 2
        if step < hi - lo:
            yield from merge(lo, hi, step)
            yield from merge(lo + r, hi, step)
            yield from [(i, i + r) for i in range(lo + r, hi - r, step)]
        else:
            yield (lo, lo + r)

    def sort(lo, hi):
        if hi - lo >= 1:
            mid = lo + (hi - lo) // 2
            yield from sort(lo, mid)
            yield from sort(mid + 1, hi)
            yield from merge(lo, hi, 1)

    return tuple(sort(0, n - 1))


def _top16_values(s):
    n = s.shape[0] // F32_ROWS
    tiles = [s[j * F32_ROWS:(j + 1) * F32_ROWS] for j in range(n)]
    for i, j in _batcher_network(PEER_TOPK):
        if j < n:
            tiles[i], tiles[j] = jnp.maximum(tiles[i], tiles[j]), jnp.minimum(tiles[i], tiles[j])
    vals = []
    for k in range(PEER_TOPK):
        mx = tiles[0].max(axis=0, keepdims=True)
        vals.append(mx)
        hit = tiles[0] == mx
        for j in range(min(n, PEER_TOPK - 1 - k)):
            below = tiles[j + 1] if j + 1 < n else -jnp.inf
            tiles[j] = jnp.where(hit, below, tiles[j])
    return vals


def _top16_pair(s1, s2):
    v1, v2 = _top16_values(s1), _top16_values(s2)
    rank2 = jnp.zeros(s2.shape, f32)
    for k in range(PEER_TOPK):
        rank2 = jnp.where(s2 < v2[k], float(k + 1), rank2)
    return v1, v2, rank2


def _stack_rows(rows):
    n = len(rows)
    rid = lax.broadcasted_iota(jnp.int32, (n, rows[0].shape[1]), 0)
    out = jnp.broadcast_to(rows[0], (n, rows[0].shape[1]))
    for i in range(1, n):
        out = jnp.where(rid == i, rows[i], out)
    return out


def _route_kernel(x_ref, wq_ref, keys_ref, rank2_ref, crow_ref, e1z_ref, e2_ref, xt_ref, *, tm):
    @pl.when(pl.program_id(1) == 0)
    def _():
        xt_ref[0] = pltpu.bitcast(x_ref[...].T.astype(bf16), jnp.uint32)

    q = jnp.dot(x_ref[...].astype(bf16), wq_ref[...], preferred_element_type=f32).astype(bf16)
    half = PEER_DKEY // 2
    scores = []
    for hh in range(ROUTE_HEADS):
        qh = q[:, hh * PEER_DKEY:(hh + 1) * PEER_DKEY]
        scores.append((lax.dot_general(keys_ref[hh, 0], qh[:, :half], _NT, preferred_element_type=f32),
                       lax.dot_general(keys_ref[hh, 1], qh[:, half:], _NT, preferred_element_type=f32)))
    rid8 = lax.broadcasted_iota(jnp.int32, (8, LANE), 0)
    for hh, c in [(hh, c) for hh in range(ROUTE_HEADS) for c in range(tm // LANE)]:
        lanes = slice(c * LANE, (c + 1) * LANE)
        s1, s2 = scores[hh][0][:, lanes], scores[hh][1][:, lanes]
        v1, v2, rank2 = _top16_pair(s1, s2)
        v2_all = _stack_rows(v2)
        cands = [v1[0] + v2_all]
        for a in range(1, 8):
            cands.append(jnp.where(rid8 < PEER_TOPK // (a + 1), v1[a] + v2_all[:8], -jnp.inf))
        cands.append(_stack_rows(v1[8:]) + v2[0])
        cand = jnp.concatenate(cands, axis=0)
        tau = _top16_values(cand)[-1]
        cmax = v1[0] + v2[0]
        z = jnp.where(cand >= tau, jnp.exp(cand - cmax), 0.0).sum(axis=0, keepdims=True)
        v1_all = _stack_rows(v1)
        counts = jnp.zeros(v1_all.shape, f32)
        for b in range(PEER_TOPK):
            counts = counts + jnp.where(v1_all + v2[b] >= tau, 1.0, 0.0)
        crow = jnp.zeros(s1.shape, f32)
        for a in range(PEER_TOPK):
            crow = jnp.where(s1 == v1[a], counts[a:a + 1], crow)
        rank2_ref[0, hh, :, lanes] = rank2.astype(bf16)
        crow_ref[0, hh, :, lanes] = crow
        e1z_ref[0, hh, :, lanes] = jnp.exp(s1 - v1[0]) * (GELU_FOLD / z)
        e2_ref[0, hh, :, lanes] = jnp.exp(s2 - v2[0]).astype(bf16)


def _peer_route(x, w_q, keys, tm=ROUTE_TOKENS):
    t, d = x.shape
    tm = min(tm, t)
    hb = ROUTE_HEADS
    spec = pl.BlockSpec((1, hb, PEER_KEYS, tm), lambda i, h: (i, h, 0, 0))
    sd = lambda dt: jax.ShapeDtypeStruct((t // tm, PEER_HEADS, PEER_KEYS, tm), dt)
    return pl.pallas_call(
        functools.partial(_route_kernel, tm=tm), grid=(t // tm, PEER_HEADS // hb),
        in_specs=[pl.BlockSpec((tm, d), lambda i, h: (i, 0)),
                  pl.BlockSpec((d, hb * PEER_DKEY), lambda i, h: (0, h)),
                  pl.BlockSpec((hb, 2, PEER_KEYS, PEER_DKEY // 2), lambda i, h: (h, 0, 0, 0))],
        out_specs=[spec] * 4 + [pl.BlockSpec((1, d // 2, tm), lambda i, h: (i, 0, 0))],
        out_shape=[sd(bf16), sd(f32), sd(f32), sd(bf16),
                   jax.ShapeDtypeStruct((t // tm, d // 2, tm), jnp.uint32)],
        compiler_params=_params("parallel", "arbitrary"))(x, w_q, keys)


def _peer_dense_kernel(x_ref, u0_ref, u1_ref, vt0_ref, vt1_ref, rank2_in, crow_ref, e1z_ref, e2_in, xtp_ref,
                       g_ref, b_ref, o_ref, acc_ref, xt_ref, ht0, ht1, gt0, gt1, rank2_ref, e2_ref,
                       *, rows, tb, lc, mw, alpha):
    k = pl.program_id(1)
    last = pl.num_programs(1) - 1
    ec = rows * PEER_KEYS
    zero = jnp.zeros((), bf16)
    kt = PEER_KEYS // BF16_ROWS

    def up(half, ht, cw):
        lanes = slice(cw * mw, (cw + 1) * mw)
        u_blk = pltpu.bitcast((u0_ref, u1_ref)[half][0], bf16)
        ht[:, lanes] = jnp.dot(u_blk, xt_ref[:, lanes], preferred_element_type=f32)

    n_tiles = last

    def gate(ht, tile, half, gt, cw):
        base = pl.multiple_of(jnp.clip(tile, 0, n_tiles - 1) * F32_ROWS, F32_ROWS)
        for r in range(rows):
            keys = slice(r * PEER_KEYS, (r + 1) * PEER_KEYS)
            row = slice(half * rows + r, half * rows + r + 1)
            for c in range(cw * mw // lc, (cw + 1) * mw // lc):
                lanes = slice(c * lc, (c + 1) * lc)
                w = None
                for h in range(PEER_HEADS):
                    cr = crow_ref[0, h, pl.ds(base, F32_ROWS), lanes][row]
                    ez = e1z_ref[0, h, pl.ds(base, F32_ROWS), lanes][row]
                    cr = jnp.broadcast_to(cr, (BF16_ROWS, lc)).astype(bf16)[None]
                    ez = jnp.broadcast_to(ez, (BF16_ROWS, lc)).astype(bf16)[None]
                    term = jnp.where(rank2_ref[h, :, :, lanes] < cr, e2_ref[h, :, :, lanes] * ez, zero)
                    w = term if w is None else w + term
                hv = ht[keys, lanes]
                g = hv * (1.0 + lax.erf(hv))
                gt[r * kt:(r + 1) * kt, :, lanes] = g.astype(bf16).reshape(kt, BF16_ROWS, lc) * w

    def down(half, gt, cw):
        lanes = slice(cw * mw, (cw + 1) * mw)
        vt_blk = pltpu.bitcast((vt0_ref, vt1_ref)[half][0, 0], bf16)
        acc_ref[:, lanes] += jnp.dot(vt_blk, gt[:, :, lanes].reshape(ec, mw), preferred_element_type=f32)

    chunks = range(tb // mw)

    @pl.when(k == 0)
    def _():
        acc_ref[...] = jnp.zeros_like(acc_ref)
        xt_ref[...] = pltpu.bitcast(xtp_ref[0], bf16)
        rank2_ref[...] = rank2_in[0]
        e2_ref[...] = e2_in[0]
        for cw in chunks:
            up(0, ht0, cw)
            up(1, ht1, cw)
        for cw in chunks:
            gate(ht0, k, 0, gt0, cw)

    @pl.when(jnp.logical_and(k > 0, k < last))
    def _():
        for cw in chunks:
            up(0, ht0, cw)
            gate(ht1, k - 1, 1, gt1, cw)
            down(0, gt0, cw)
        for cw in chunks:
            up(1, ht1, cw)
            gate(ht0, k, 0, gt0, cw)
            down(1, gt1, cw)

    @pl.when(k == last)
    def _():
        for cw in chunks:
            gate(ht1, k - 1, 1, gt1, cw)
            down(0, gt0, cw)
        for cw in chunks:
            down(1, gt1, cw)
        z = alpha * x_ref[...] + acc_ref[...].T
        o_ref[...] = _layer_norm(z, g_ref[...], b_ref[...])


def _peer_dense(x, u, vt, layer, routing, g, b, alpha, lc=128):
    t, d = x.shape
    e = 2 * u.shape[1]
    ec = DENSE_EC
    rows = ec // PEER_KEYS
    assert 2 * rows == F32_ROWS, "a pair of expert blocks must span one 8-row f32 tile of sub-key-1 rows"
    nk = e // (2 * ec)
    kt = PEER_KEYS // BF16_ROWS
    rank2, crow, e1z, e2, xtp = routing
    nt, _, _, tb = crow.shape
    rank2, e2 = (a.reshape(nt, PEER_HEADS, kt, BF16_ROWS, tb) for a in (rank2, e2))
    full = pl.BlockSpec((1, PEER_HEADS, kt, BF16_ROWS, tb), lambda i, k: (i, 0, 0, 0, 0))
    rowtab = pl.BlockSpec((1, PEER_HEADS, PEER_KEYS, tb), lambda i, k: (i, 0, 0, 0))
    fixed = pl.BlockSpec((1, d), lambda i, k: (0, 0))
    xspec = pl.BlockSpec((tb, d), lambda i, k: (i, 0))
    gt_scr = pltpu.VMEM((ec // BF16_ROWS, BF16_ROWS, tb), bf16)
    u_blk = lambda half: pl.BlockSpec(
        (1, ec // 2, d), lambda i, k: (layer, 2 * jnp.minimum(k, nk - 1) + half, 0))
    vt_blk = lambda half: pl.BlockSpec(
        (1, 1, d // 2, ec), lambda i, k: (layer, 2 * jnp.maximum(k - 1, 0) + half, 0, 0))
    return pl.pallas_call(
        functools.partial(_peer_dense_kernel, rows=rows, tb=tb, lc=lc, mw=MXU_WIDTH, alpha=alpha),
        grid=(t // tb, nk + 1),
        in_specs=[xspec, u_blk(0), u_blk(1), vt_blk(0), vt_blk(1),
                  full, rowtab, rowtab, full,
                  pl.BlockSpec((1, d // 2, tb), lambda i, k: (i, 0, 0)), fixed, fixed],
        out_specs=xspec, out_shape=jax.ShapeDtypeStruct((t, d), f32),
        scratch_shapes=[pltpu.VMEM((d, tb), f32), pltpu.VMEM((d, tb), bf16),
                        pltpu.VMEM((ec, tb), f32), pltpu.VMEM((ec, tb), f32), gt_scr, gt_scr,
                        pltpu.VMEM((PEER_HEADS, kt, BF16_ROWS, tb), bf16),
                        pltpu.VMEM((PEER_HEADS, kt, BF16_ROWS, tb), bf16)],
        compiler_params=_params("parallel", "arbitrary"))(
            x, u, u, vt, vt, rank2, crow, e1z, e2, xtp, g.reshape(1, d), b.reshape(1, d))


def _swa_layer(x, batch, seq, w_in, sinks, w_out, bias, g, b, alpha):
    feat = (N_HEADS + 2 * SWA_KV_HEADS) * HEAD_DIM
    d_q = N_HEADS * HEAD_DIM
    kvw = SWA_KV_HEADS * HEAD_DIM
    proj = _mm(x, w_in.astype(bf16), bf16, 512, feat).reshape(batch, seq, feat)
    sink = jnp.repeat(sinks.astype(f32), HEAD_DIM).reshape(1, d_q)
    o = _band_attention(proj, bias, 1, feat, 0, d_q // kvw, d_q // kvw + 1, kvw,
                        N_HEADS // SWA_KV_HEADS, sink)
    return _proj_ln([o], [], [1], w_out.astype(bf16), x, g, b, alpha)


def _dil_layer(x, batch, seq, w_in, w_out, biases, g, b, alpha):
    d_q = N_HEADS * HEAD_DIM
    feat = 3 * d_q
    w_in = w_in.astype(bf16)
    outs, lses, dils = [], [], []
    for gi, (window, dil) in enumerate(DIL_PATTERNS):
        proj = _mm_dil(x, w_in[:, gi * feat:(gi + 1) * feat], dil).reshape(batch, seq // dil, dil * feat)
        o, lse = _band_attention(proj, biases[gi], dil, feat, 0, 1, 2, d_q, 1)
        outs.append(o)
        lses.append(lse)
        dils.append(dil)
    return _proj_ln(outs, lses, dils, w_out.astype(bf16), x, g, b, alpha)


def _mla_layer(x, batch, seq, w_in, q_norm, w_uq, kv_norm, w_ukv, w_out, g, b, alpha):
    q, k, v = _mla_prep(x, w_in, q_norm, w_uq, kv_norm, w_ukv, seq)
    o = _mla_attention(q, k, v, batch, seq)
    d = w_out.shape[1]
    w3 = w_out.reshape(N_HEADS, MLA_V, d)
    w_pad = jnp.concatenate([w3, jnp.zeros((N_HEADS, LANE - MLA_V, d), f32)], 1)
    return _proj_ln([o], [], [1], w_pad.reshape(N_HEADS * LANE, d).astype(bf16), x, g, b, alpha)


def _pack_u_kernel(u_ref, o_ref):
    o_ref[0] = pltpu.bitcast((u_ref[0] * GELU_FOLD).astype(bf16), jnp.uint32)


def _pack_vt_kernel(v_ref, o_ref):
    o_ref[0, 0] = pltpu.bitcast(v_ref[0].T.astype(bf16), jnp.uint32)


def _pack_expert_weights(u, v):
    depth, e, d = u.shape
    ec = DENSE_EC
    src = pl.BlockSpec((1, ec, d), lambda l, j: (l, j, 0))
    up = pl.pallas_call(
        _pack_u_kernel, grid=(depth, e // ec), in_specs=[src],
        out_specs=pl.BlockSpec((1, ec // 2, d), lambda l, j: (l, j, 0)),
        out_shape=jax.ShapeDtypeStruct((depth, e // 2, d), jnp.uint32),
        compiler_params=_params("parallel", "parallel"))(u)
    vtp = pl.pallas_call(
        _pack_vt_kernel, grid=(depth, e // ec), in_specs=[src],
        out_specs=pl.BlockSpec((1, 1, d // 2, ec), lambda l, j: (l, j, 0, 0)),
        out_shape=jax.ShapeDtypeStruct((depth, e // ec, d // 2, ec), jnp.uint32),
        compiler_params=_params("parallel", "parallel"))(v)
    return up, vtp


def _peer_layer(x, w_q, keys, u_packed, vt_packed, layer, g, b, alpha):
    routing = _peer_route(x, w_q.astype(bf16), keys.astype(bf16))
    return _peer_dense(x, u_packed, vt_packed, layer, routing, g, b, alpha)


def kernel(x, rel_bias, ln_g, ln_b, swa_w_in, swa_sinks, swa_w_out, dil_w_in, dil_w_out,
           mla_w_in, mla_q_norm, mla_w_uq, mla_kv_norm, mla_w_ukv, mla_w_out,
           peer_w_q, peer_keys, peer_u, peer_v):
    batch, seq, d = x.shape
    depth = ln_g.shape[0]
    alpha = (2 * depth) ** 0.25
    assert seq % (DIL_PATTERNS[-1][1] * BLOCK) == 0, "sequence must be a whole number of dilation segments"
    swa_bias = _band_bias(rel_bias, SWA_WINDOW - 1, 1)
    dil_bias = [_band_bias(rel_bias, window // dil, dil) for window, dil in DIL_PATTERNS]
    u_packed, vt_packed = _pack_expert_weights(peer_u, peer_v)
    h = x.reshape(batch * seq, d)
    for i in range(depth):
        kind, j = i % 3, i // 3
        if kind == 0:
            h = _swa_layer(h, batch, seq, swa_w_in[j], swa_sinks[j], swa_w_out[j], swa_bias,
                           ln_g[i, 0], ln_b[i, 0], alpha)
        elif kind == 1:
            h = _dil_layer(h, batch, seq, dil_w_in[j], dil_w_out[j], dil_bias,
                           ln_g[i, 0], ln_b[i, 0], alpha)
        else:
            h = _mla_layer(h, batch, seq, mla_w_in[j], mla_q_norm[j], mla_w_uq[j], mla_kv_norm[j],
                           mla_w_ukv[j], mla_w_out[j], ln_g[i, 0], ln_b[i, 0], alpha)
        h = _peer_layer(h, peer_w_q[i], peer_keys[i], u_packed, vt_packed, i, ln_g[i, 1], ln_b[i, 1], alpha)
    return h.reshape(batch, seq, d)
```
